```python
import math, functools
import jax, jax.numpy as jnp
from jax import lax
import numpy as np

D_MODEL = 4096
BATCH = 4
SEQ = 4096
DEPTH = 2
DEC_BATCH = 8
DEC_SEQ = 16
PAST_LEN = 4096

CHUNK = 64
HEAD_DIM = 128
N_HEADS_A = D_MODEL // HEAD_DIM
N_HEADS_B = D_MODEL // HEAD_DIM
N_KV_B = N_HEADS_B // 4
G_B = N_HEADS_B // N_KV_B
N_PREV_CHUNKS = 8
BAND = (N_PREV_CHUNKS + 1) * CHUNK
REL_CLIP = 128
QBLOCK = 128
PEER_HEADS = 8
PEER_KEYS = 128
PEER_EXPERTS = PEER_KEYS * PEER_KEYS
PEER_TOPK = 16
PEER_DKEY = 256
PEER_DHALF = PEER_DKEY // 2
PEER_BLOCK = 64
PLE_DIM = 256
N_A = DEPTH // 2
N_B = DEPTH - N_A
EPS = 1e-6

kernel_name = "yoco_stickbreak_chunkband_peer_stream_step"


def rmsnorm(x, g):
    xf = x.astype(jnp.float32)
    y = xf * lax.rsqrt(jnp.mean(xf * xf, axis=-1, keepdims=True) + EPS)
    return (y * g.astype(jnp.float32)).astype(x.dtype)


def stick_breaking_attention(q, k, v, q_pos, k_pos):
    b, tq, h, hd = q.shape
    nblk = tq // QBLOCK if tq % QBLOCK == 0 else 1
    qb = q.reshape(b, nblk, tq // nblk, h, hd).swapaxes(0, 1)
    pb = q_pos.reshape(nblk, tq // nblk)
    scale = hd ** -0.5

    def one(args):
        qs, ps = args
        z = jnp.einsum('bqhd,bkhd->bhqk', qs, k).astype(jnp.float32) * scale
        mask = k_pos[None, :] < ps[:, None]
        log_keep = jnp.where(mask, jax.nn.log_sigmoid(-z), 0.0)
        incl = lax.cumsum(log_keep, axis=3, reverse=True)
        after = jnp.concatenate([incl[..., 1:], jnp.zeros_like(incl[..., :1])], axis=-1)
        w = jnp.where(mask, jnp.exp(jax.nn.log_sigmoid(z) + after), 0.0)
        return jnp.einsum('bhqk,bkhd->bqhd', w.astype(v.dtype), v)

    o = lax.map(one, (qb, pb))
    return o.swapaxes(0, 1).reshape(b, tq, h, hd)


def build_band(k, v, past_k, past_v):
    b, t = k.shape[0], k.shape[1]
    if past_k is None:
        nc = t // CHUNK

        def gather(a):
            ac = a.reshape(b, nc, CHUNK, a.shape[2], a.shape[3])
            ap = jnp.pad(ac, ((0, 0), (N_PREV_CHUNKS, 0), (0, 0), (0, 0), (0, 0)))
            return jnp.concatenate([ap[:, s:s + nc] for s in range(N_PREV_CHUNKS + 1)], axis=2)

        kb, vb = gather(k), gather(v)
        band_chunk = jnp.arange(nc)[:, None] - N_PREV_CHUNKS + (jnp.arange(BAND) // CHUNK)[None, :]
        valid = band_chunk >= 0
        n_before, cq = N_PREV_CHUNKS * CHUNK, CHUNK
    else:
        kb = jnp.concatenate([past_k, k], axis=1)[:, None]
        vb = jnp.concatenate([past_v, v], axis=1)[:, None]
        n_before, cq = past_k.shape[1], t
        valid = jnp.ones((1, n_before + t), dtype=bool)
    rel = jnp.arange(cq)[:, None] + n_before - jnp.arange(kb.shape[2])[None, :]
    return kb, vb, rel, valid


def band_attention(q, kb, vb, rel, valid, rel_table):
    b, t, h, hd = q.shape
    nq, cq, L = valid.shape[0], rel.shape[0], rel.shape[1]
    qc = q.reshape(b, nq, cq, N_KV_B, G_B, hd)
    bias = rel_table[jnp.clip(rel, -REL_CLIP, REL_CLIP) + REL_CLIP].astype(jnp.float32)
    bias = bias.reshape(cq, L, N_KV_B, G_B).transpose(2, 3, 0, 1)
    mask = valid[:, None, None, None, :]
    scale = hd ** -0.5

    def one(args):
        qs, ks, vs = args
        s = jnp.einsum('nqhgd,nkhd->nhgqk', qs, ks).astype(jnp.float32) * scale + bias
        s = jnp.where(mask, s, -jnp.inf)
        pr = jax.nn.softmax(s, axis=-1).astype(vs.dtype)
        return jnp.einsum('nhgqk,nkhd->nqhgd', pr, vs)

    o = lax.map(one, (qc, kb, vb))
    return o.reshape(b, t, h * hd)


def peer(xn, w_q, sub_keys, u_tab, v_tab):
    shp = xn.shape
    x2 = xn.reshape(-1, shp[-1])
    n_tok = x2.shape[0]
    pad = (-n_tok) % PEER_BLOCK
    xb = jnp.pad(x2, ((0, pad), (0, 0))).reshape(-1, PEER_BLOCK, shp[-1])

    def one(xt):
        q = (xt @ w_q).reshape(PEER_BLOCK, PEER_HEADS, 2, PEER_DHALF)
        s = jnp.einsum('thcd,hcnd->thcn', q, sub_keys).astype(jnp.float32)
        sv, si = lax.top_k(s, PEER_TOPK)
        cand = (sv[:, :, 0, :, None] + sv[:, :, 1, None, :]).reshape(PEER_BLOCK, PEER_HEADS, -1)
        cidx = (si[:, :, 0, :, None] * PEER_KEYS + si[:, :, 1, None, :]).reshape(PEER_BLOCK, PEER_HEADS, -1)
        fv, fi = lax.top_k(cand, PEER_TOPK)
        eidx = jnp.take_along_axis(cidx, fi, axis=-1).reshape(PEER_BLOCK, PEER_HEADS * PEER_TOPK)
        g = jax.nn.softmax(fv, axis=-1).reshape(PEER_BLOCK, PEER_HEADS * PEER_TOPK)
        u_sel = jnp.take(u_tab, eidx, axis=0)
        v_sel = jnp.take(v_tab, eidx, axis=0)
        act = jax.nn.gelu(jnp.einsum('td,ted->te', xt, u_sel).astype(jnp.float32))
        w = (g * act).astype(xt.dtype)
        return jnp.einsum('te,ted->td', w, v_sel)

    out = lax.map(one, xb).reshape(-1, shp[-1])[:n_tok]
    return out.reshape(shp)


def per_layer_embedding(h, p, g, w_gate, w_proj):
    gate = jax.nn.sigmoid((rmsnorm(h, g) @ w_gate).astype(jnp.float32))
    return (gate * (p @ w_proj).astype(jnp.float32)).astype(h.dtype)


def trunk(x, pe, past, prm):
    b, t, _ = x.shape
    p_len = 0 if past is None else past[0].shape[2]
    q_pos = p_len + jnp.arange(t)
    h = x
    a_k_rows, a_v_rows = [], []
    band = None
    kb_new = vb_new = None
    for i in range(DEPTH):
        hn = rmsnorm(h, prm['g_mix'][i])
        if i < N_A:
            qkv = (hn @ prm['w_qkv_a'][i]).reshape(b, t, 3, N_HEADS_A, HEAD_DIM)
            q, k, v = qkv[:, :, 0], qkv[:, :, 1], qkv[:, :, 2]
            a_k_rows.append(k)
            a_v_rows.append(v)
            if past is None:
                k_all, v_all, k_pos = k, v, q_pos
            else:
                k_all = jnp.concatenate([past[0][i], k], axis=1)
                v_all = jnp.concatenate([past[1][i], v], axis=1)
                k_pos = jnp.arange(p_len + t)
            o = stick_breaking_attention(q, k_all, v_all, q_pos, k_pos)
            h = h + o.reshape(b, t, -1) @ prm['w_o_a'][i]
        else:
            j = i - N_A
            if band is None:
                kvn = rmsnorm(h, prm['g_kv'])
                kv = (kvn @ prm['w_kv_b']).reshape(b, t, 2, N_KV_B, HEAD_DIM)
                kb_new, vb_new = kv[:, :, 0], kv[:, :, 1]
                band = build_band(kb_new, vb_new,
                                  None if past is None else past[2],
                                  None if past is None else past[3])
            q = (hn @ prm['w_q_b'][j]).reshape(b, t, N_HEADS_B, HEAD_DIM)
            o = band_attention(q, band[0], band[1], band[2], band[3], prm['rel_bias_b'][j])
            h = h + o @ prm['w_o_b'][j]
        h = h + peer(rmsnorm(h, prm['g_ffn'][i]), prm['w_q_peer'][i], prm['peer_sub_keys'][i],
                     prm['peer_u'][i], prm['peer_v'][i])
        h = h + per_layer_embedding(h, pe[i], prm['g_ple'][i], prm['w_ple_gate'][i], prm['w_ple_proj'][i])
    y = rmsnorm(h, prm['g_final'])
    if past is None:
        keep = min(N_PREV_CHUNKS * CHUNK, t)
        b_k, b_v = kb_new[:, t - keep:], vb_new[:, t - keep:]
    else:
        b_k, b_v = kb_new, vb_new
    return y, jnp.stack(a_k_rows), jnp.stack(a_v_rows), b_k, b_v


def setup_inputs(seed: int = 0) -> dict:
    key = jax.random.key(seed)
    ks = jax.random.split(key, 32)
    f32 = jnp.float32
    d = D_MODEL
    w_band = min(N_PREV_CHUNKS * CHUNK, PAST_LEN)

    def nrm(k, shape, s):
        return jax.random.normal(k, shape, f32) * s

    def gain(k, shape):
        return 1.0 + 0.05 * jax.random.normal(k, shape, f32)

    return {
        'x_prompt': nrm(ks[0], (BATCH, SEQ, d), 1.0),
        'x_sample': nrm(ks[1], (DEC_BATCH, DEC_SEQ, d), 1.0),
        'cache_a_k': nrm(ks[2], (N_A, DEC_BATCH, PAST_LEN, N_HEADS_A, HEAD_DIM), 1.0),
        'cache_a_v': nrm(ks[3], (N_A, DEC_BATCH, PAST_LEN, N_HEADS_A, HEAD_DIM), 1.0),
        'cache_b_k': nrm(ks[4], (DEC_BATCH, w_band, N_KV_B, HEAD_DIM), 1.0),
        'cache_b_v': nrm(ks[5], (DEC_BATCH, w_band, N_KV_B, HEAD_DIM), 1.0),
        'p_prompt': nrm(ks[6], (DEPTH, BATCH, SEQ, PLE_DIM), 1.0),
        'p_sample': nrm(ks[7], (DEPTH, DEC_BATCH, DEC_SEQ, PLE_DIM), 1.0),
        'g_mix': gain(ks[8], (DEPTH, d)),
        'w_qkv_a': nrm(ks[9], (N_A, d, 3 * N_HEADS_A * HEAD_DIM), d ** -0.5),
        'w_o_a': nrm(ks[10], (N_A, N_HEADS_A * HEAD_DIM, d), (N_HEADS_A * HEAD_DIM) ** -0.5),
        'g_kv': gain(ks[11], (d,)),
        'w_kv_b': nrm(ks[12], (d, 2 * N_KV_B * HEAD_DIM), d ** -0.5),
        'w_q_b': nrm(ks[13], (N_B, d, N_HEADS_B * HEAD_DIM), d ** -0.5),
        'rel_bias_b': nrm(ks[14], (N_B, 2 * REL_CLIP + 1, N_HEADS_B), 0.1),
        'w_o_b': nrm(ks[15], (N_B, N_HEADS_B * HEAD_DIM, d), (N_HEADS_B * HEAD_DIM) ** -0.5),
        'g_ffn': gain(ks[16], (DEPTH, d)),
        'w_q_peer': nrm(ks[17], (DEPTH, d, PEER_HEADS * PEER_DKEY), d ** -0.5),
        'peer_sub_keys': nrm(ks[18], (DEPTH, PEER_HEADS, 2, PEER_KEYS, PEER_DHALF), PEER_DHALF ** -0.5),
        'peer_u': nrm(ks[19], (DEPTH, PEER_EXPERTS, d), d ** -0.5),
        'peer_v': nrm(ks[20], (DEPTH, PEER_EXPERTS, d), PEER_HEADS ** -0.5),
        'g_ple': gain(ks[21], (DEPTH, d)),
        'w_ple_gate': nrm(ks[22], (DEPTH, d, d), d ** -0.5),
        'w_ple_proj': nrm(ks[23], (DEPTH, PLE_DIM, d), PLE_DIM ** -0.5),
        'g_final': gain(ks[24], (d,)),
    }


def reference(x_prompt, x_sample, cache_a_k, cache_a_v, cache_b_k, cache_b_v, p_prompt, p_sample,
              g_mix, w_qkv_a, w_o_a, g_kv, w_kv_b, w_q_b, rel_bias_b, w_o_b, g_ffn, w_q_peer,
              peer_sub_keys, peer_u, peer_v, g_ple, w_ple_gate, w_ple_proj, g_final):
    prm = dict(g_mix=g_mix, w_qkv_a=w_qkv_a, w_o_a=w_o_a, g_kv=g_kv, w_kv_b=w_kv_b, w_q_b=w_q_b,
               rel_bias_b=rel_bias_b, w_o_b=w_o_b, g_ffn=g_ffn, w_q_peer=w_q_peer,
               peer_sub_keys=peer_sub_keys, peer_u=peer_u, peer_v=peer_v, g_ple=g_ple,
               w_ple_gate=w_ple_gate, w_ple_proj=w_ple_proj, g_final=g_final)
    y_prompt, ak_p, av_p, bk_p, bv_p = trunk(x_prompt, p_prompt, None, prm)
    y_sample, ak_s, av_s, bk_s, bv_s = trunk(x_sample, p_sample,
                                             (cache_a_k, cache_a_v, cache_b_k, cache_b_v), prm)
    return (y_prompt, y_sample, ak_p, av_p, bk_p, bv_p, ak_s, av_s, bk_s, bv_s)
```

```python
import functools

import jax
import jax.numpy as jnp
from jax import lax
from jax.experimental import pallas as pl
from jax.experimental.pallas import tpu as pltpu

F32 = jnp.float32
BF16 = jnp.bfloat16

EPS = 1e-6
HEAD_DIM = 128
CHUNK = 64
N_PREV_CHUNKS = 8
N_KV_B = 8
G_B = 4
REL_CLIP = 128
PEER_HEADS = 8
PEER_KEYS = 128
PEER_TOPK = 16
SCALE = HEAD_DIM ** -0.5
NEG_INF = float("-inf")

VMEM_LIMIT_BYTES = 56 * 1024 * 1024
LANES = 128

_NT = (((1,), (1,)), ((), ()))
_TN = (((0,), (0,)), ((), ()))


def _params(*sem):
    return pltpu.CompilerParams(dimension_semantics=sem, vmem_limit_bytes=VMEM_LIMIT_BYTES)


def _norm_rows(x, g):
    ms = jnp.mean(x * x, axis=-1, keepdims=True)
    return x * lax.rsqrt(ms + EPS) * g


def _sigmoid(x):
    return 1.0 / (1.0 + jnp.exp(-x))


def _mm_kernel(*refs, norm, res, ple):
    it = iter(refs)
    x_ref = next(it)
    g_ref = next(it) if norm else None
    w_ref = next(it)
    res_ref = next(it) if res else None
    p_ref = next(it) if ple else None
    wp_ref = next(it) if ple else None
    o_ref = next(it)
    xn_ref = next(it) if norm else None

    if norm:
        @pl.when(pl.program_id(1) == 0)
        def _():
            xn_ref[...] = _norm_rows(x_ref[...], g_ref[...]).astype(BF16)
        x = xn_ref[...]
    else:
        x = x_ref[...]
    acc = jnp.dot(x, w_ref[...], preferred_element_type=F32)
    if ple:
        proj = jnp.dot(p_ref[...].astype(BF16), wp_ref[...], preferred_element_type=F32)
        acc = _sigmoid(acc) * proj
    if res:
        acc = acc + res_ref[...]
    o_ref[...] = acc.astype(o_ref.dtype)


def _matmul(x, w, *, gain=None, residual=None, ple=None, out_dtype=F32, tn=512):
    m, k = x.shape
    n = w.shape[1]
    tm = min(512, m)
    tn = min(tn, n)
    norm = gain is not None
    in_specs = [pl.BlockSpec((tm, k), lambda i, j: (i, 0))]
    args = [x]
    if norm:
        in_specs.append(pl.BlockSpec((1, k), lambda i, j: (0, 0)))
        args.append(gain.reshape(1, k))
    in_specs.append(pl.BlockSpec((k, tn), lambda i, j: (0, j)))
    args.append(w)
    if residual is not None:
        in_specs.append(pl.BlockSpec((tm, tn), lambda i, j: (i, j)))
        args.append(residual)
    if ple is not None:
        p, wp = ple
        kp = p.shape[1]
        in_specs.append(pl.BlockSpec((tm, kp), lambda i, j: (i, 0)))
        in_specs.append(pl.BlockSpec((kp, tn), lambda i, j: (0, j)))
        args += [p, wp]
    return pl.pallas_call(
        functools.partial(_mm_kernel, norm=norm, res=residual is not None, ple=ple is not None),
        grid=(m // tm, n // tn),
        in_specs=in_specs,
        out_specs=pl.BlockSpec((tm, tn), lambda i, j: (i, j)),
        out_shape=jax.ShapeDtypeStruct((m, n), out_dtype),
        scratch_shapes=[pltpu.VMEM((tm, k), BF16)] if norm else [],
        compiler_params=_params("parallel", "arbitrary"),
    )(*args)


def _rmsnorm_kernel(x_ref, g_ref, o_ref):
    o_ref[...] = _norm_rows(x_ref[...], g_ref[...])


def _rmsnorm(x, gain):
    m, k = x.shape
    tm = min(512, m)
    return pl.pallas_call(
        _rmsnorm_kernel,
        grid=(m // tm,),
        in_specs=[pl.BlockSpec((tm, k), lambda i: (i, 0)), pl.BlockSpec((1, k), lambda i: (0, 0))],
        out_specs=pl.BlockSpec((tm, k), lambda i: (i, 0)),
        out_shape=jax.ShapeDtypeStruct((m, k), F32),
        compiler_params=_params("parallel"),
    )(x, gain.reshape(1, k))


def _strict_lower_ones(n):
    r = lax.broadcasted_iota(jnp.int32, (n, n), 0)
    c = lax.broadcasted_iota(jnp.int32, (n, n), 1)
    return jnp.where(r > c, 1.0, 0.0).astype(BF16)


def _sb_block(q, kb, vb, c, acc, tri, mask):
    z = lax.dot_general(q, kb, _NT, preferred_element_type=F32) * SCALE
    sp = jnp.log1p(jnp.exp(-jnp.abs(z)))
    log_beta = jnp.minimum(z, 0.0) - sp
    log_keep = -jnp.maximum(z, 0.0) - sp
    if mask is not None:
        log_keep = jnp.where(mask, log_keep, 0.0)
    hi = log_keep.astype(BF16)
    lo = (log_keep - hi.astype(F32)).astype(BF16)
    after = (jnp.dot(hi, tri, preferred_element_type=F32)
             + jnp.dot(lo, tri, preferred_element_type=F32))
    w = jnp.exp(log_beta + after + c)
    if mask is not None:
        w = jnp.where(mask, w, 0.0)
    acc = acc + jnp.dot(w.astype(BF16), vb, preferred_element_type=F32)
    c = c + jnp.sum(log_keep, axis=1, keepdims=True)
    return c, acc


def _sb_prompt_kernel(q_ref, k_ref, v_ref, o_ref, kb_ref, vb_ref, *, blk):
    seq = q_ref.shape[0]
    kb_ref[...] = k_ref[...].astype(BF16)
    vb_ref[...] = v_ref[...].astype(BF16)
    tri = _strict_lower_ones(blk)
    row = lax.broadcasted_iota(jnp.int32, (blk, blk), 0)
    col = lax.broadcasted_iota(jnp.int32, (blk, blk), 1)
    diag_mask = col < row

    def q_body(i, _):
        q0 = pl.multiple_of(i * blk, blk)
        q = q_ref[pl.ds(q0, blk), :]
        c = jnp.zeros((blk, 1), F32)
        acc = jnp.zeros((blk, HEAD_DIM), F32)
        c, acc = _sb_block(q, kb_ref[pl.ds(q0, blk), :], vb_ref[pl.ds(q0, blk), :], c, acc, tri, diag_mask)

        def k_body(n, carry):
            k0 = pl.multiple_of((i - 1 - n) * blk, blk)
            return _sb_block(q, kb_ref[pl.ds(k0, blk), :], vb_ref[pl.ds(k0, blk), :], *carry, tri, None)

        c, acc = lax.fori_loop(0, i, k_body, (c, acc))
        o_ref[pl.ds(q0, blk), :] = acc.astype(o_ref.dtype)
        return 0

    lax.fori_loop(0, seq // blk, q_body, 0)


def _sb_prompt(q, k, v, *, batch, seq, blk=256):
    t, hd = q.shape
    heads = hd // HEAD_DIM
    blk = min(blk, seq)
    spec = pl.BlockSpec((seq, HEAD_DIM), lambda b, h: (b, h))
    return pl.pallas_call(
        functools.partial(_sb_prompt_kernel, blk=blk),
        grid=(batch, heads),
        in_specs=[spec, spec, spec],
        out_specs=spec,
        out_shape=jax.ShapeDtypeStruct((t, hd), BF16),
        scratch_shapes=[pltpu.VMEM((seq, HEAD_DIM), BF16), pltpu.VMEM((seq, HEAD_DIM), BF16)],
        compiler_params=_params("parallel", "parallel"),
    )(q, k, v)


def _sb_sample_kernel(q_ref, kn_ref, vn_ref, kc_ref, vc_ref, o_ref, *, blk):
    tq = q_ref.shape[0]
    past = kc_ref.shape[0]
    q = q_ref[...]
    tri = _strict_lower_ones(blk)
    row = lax.broadcasted_iota(jnp.int32, (tq, blk), 0)
    col = lax.broadcasted_iota(jnp.int32, (tq, blk), 1)
    c = jnp.zeros((tq, 1), F32)
    acc = jnp.zeros((tq, HEAD_DIM), F32)
    c, acc = _sb_block(q, kn_ref[...], vn_ref[...], c, acc, tri, col < row)

    def k_body(n, carry):
        k0 = pl.multiple_of(past - (n + 1) * blk, blk)
        kb = kc_ref[pl.ds(k0, blk), :].astype(BF16)
        vb = vc_ref[pl.ds(k0, blk), :].astype(BF16)
        return _sb_block(q, kb, vb, *carry, tri, None)

    c, acc = lax.fori_loop(0, past // blk, k_body, (c, acc))
    o_ref[...] = acc.astype(o_ref.dtype)


def _sb_sample(q, k_new, v_new, cache_k, cache_v, *, batch, tq, blk=128):
    t, hd = q.shape
    heads = hd // HEAD_DIM
    past = cache_k.shape[1]

    def pad_block(a):
        a = a.reshape(batch, tq, hd)
        a = jnp.pad(a, ((0, 0), (0, blk - tq), (0, 0)))
        return a.reshape(batch * blk, hd).astype(BF16)

    qspec = pl.BlockSpec((tq, HEAD_DIM), lambda b, h: (b, h))
    nspec = pl.BlockSpec((blk, HEAD_DIM), lambda b, h: (b, h))
    cspec = pl.BlockSpec((None, past, HEAD_DIM), lambda b, h: (b, 0, h))
    return pl.pallas_call(
        functools.partial(_sb_sample_kernel, blk=blk),
        grid=(batch, heads),
        in_specs=[qspec, nspec, nspec, cspec, cspec],
        out_specs=qspec,
        out_shape=jax.ShapeDtypeStruct((t, hd), BF16),
        compiler_params=_params("parallel", "parallel"),
    )(q, pad_block(k_new), pad_block(v_new), cache_k, cache_v)


def _band_kernel(q_ref, k_ref, v_ref, bias_ref, o_ref, *, cq, win, chunked, lo_static):
    if chunked:
        c = pl.program_id(2)
        start = pl.multiple_of(c * cq, cq)
        lo = jnp.maximum(CHUNK, (N_PREV_CHUNKS + 1 - c) * CHUNK)
    else:
        start = 0
        lo = lo_static
    kb = k_ref[pl.ds(start, win), :]
    vb = v_ref[pl.ds(start, win), :]
    qb = q_ref[...]
    q4 = jnp.concatenate([qb[:, g * HEAD_DIM:(g + 1) * HEAD_DIM] for g in range(G_B)], axis=0)
    s = lax.dot_general(q4, kb, _NT, preferred_element_type=F32) * SCALE
    s = s + bias_ref[...].reshape(G_B * cq, win)
    col = lax.broadcasted_iota(jnp.int32, (G_B * cq, win), 1)
    s = jnp.where(col >= lo, s, NEG_INF)
    m = jnp.max(s, axis=1, keepdims=True)
    e = jnp.exp(s - m)
    p = e / jnp.sum(e, axis=1, keepdims=True)
    o = jnp.dot(p.astype(BF16), vb, preferred_element_type=F32)
    for g in range(G_B):
        o_ref[:, g * HEAD_DIM:(g + 1) * HEAD_DIM] = o[g * cq:(g + 1) * cq].astype(o_ref.dtype)


def _band_bias(rel_table, cq, win):
    rel = jnp.arange(cq)[:, None] + (win - cq) - jnp.arange(win)[None, :]
    bias = rel_table[jnp.clip(rel, -REL_CLIP, REL_CLIP) + REL_CLIP].astype(F32)
    return bias.transpose(2, 0, 1)


def _band_attention(q, kpad, vpad, bias, *, batch, nq, cq, win, chunked, lo_static):
    t, hd = q.shape
    rows = kpad.shape[1]
    qw = G_B * HEAD_DIM
    qspec = pl.BlockSpec((cq, qw), lambda b, h, c: (b * nq + c, h))
    kspec = pl.BlockSpec((None, rows, HEAD_DIM), lambda b, h, c: (b, 0, h))
    bspec = pl.BlockSpec((G_B, cq, win), lambda b, h, c: (h, 0, 0))
    return pl.pallas_call(
        functools.partial(_band_kernel, cq=cq, win=win, chunked=chunked, lo_static=lo_static),
        grid=(batch, N_KV_B, nq),
        in_specs=[qspec, kspec, kspec, bspec],
        out_specs=qspec,
        out_shape=jax.ShapeDtypeStruct((t, hd), BF16),
        compiler_params=_params("parallel", "parallel", "arbitrary"),
    )(q, kpad, vpad, bias)


def _split3_dot_nt(a, b):
    a_hi = a.astype(BF16)
    a_lo = (a - a_hi.astype(F32)).astype(BF16)
    b_hi = b.astype(BF16)
    b_lo = (b - b_hi.astype(F32)).astype(BF16)
    return (lax.dot_general(a_hi, b_hi, _NT, preferred_element_type=F32)
            + lax.dot_general(a_lo, b_hi, _NT, preferred_element_type=F32)
            + lax.dot_general(a_hi, b_lo, _NT, preferred_element_type=F32))


def _top16_expanded(s, outer):
    tm = s.shape[0]
    lane = lax.broadcasted_iota(jnp.int32, (tm, PEER_KEYS), 1).astype(F32)
    lane2 = lax.broadcasted_iota(jnp.int32, (tm, PEER_TOPK * PEER_TOPK), 1)
    slot = lane2 // PEER_TOPK if outer else lane2 % PEER_TOPK

    def body(i, carry):
        s, ve, ie = carry
        m = jnp.max(s, axis=1, keepdims=True)
        pos = jnp.min(jnp.where(s == m, lane, float(PEER_KEYS)), axis=1, keepdims=True)
        s = jnp.where(lane == pos, NEG_INF, s)
        sel = slot == i
        return s, jnp.where(sel, m, ve), jnp.where(sel, pos, ie)

    z = jnp.zeros((tm, PEER_TOPK * PEER_TOPK), F32)
    _, ve, ie = lax.fori_loop(0, PEER_TOPK, body, (s, z, z))
    return ve, ie


def _route_kernel(q_ref, sk_ref, e_ref, g_ref):
    tm = q_ref.shape[0]
    dh = sk_ref.shape[-1]
    ncand = PEER_TOPK * PEER_TOPK
    lane = lax.broadcasted_iota(jnp.int32, (tm, PEER_HEADS * PEER_TOPK), 1)
    lane2 = lax.broadcasted_iota(jnp.int32, (tm, ncand), 1).astype(F32)
    e_out = jnp.zeros((tm, PEER_HEADS * PEER_TOPK), F32)
    g_out = jnp.zeros((tm, PEER_HEADS * PEER_TOPK), F32)
    for h in range(PEER_HEADS):
        q0 = q_ref[:, (2 * h) * dh:(2 * h + 1) * dh]
        q1 = q_ref[:, (2 * h + 1) * dh:(2 * h + 2) * dh]
        v0, i0 = _top16_expanded(_split3_dot_nt(q0, sk_ref[h, 0]), outer=True)
        v1, i1 = _top16_expanded(_split3_dot_nt(q1, sk_ref[h, 1]), outer=False)
        cand = v0 + v1
        cidx = i0 * float(PEER_KEYS) + i1

        def body(i, carry, cidx=cidx, h=h):
            cand, fv, fe = carry
            m = jnp.max(cand, axis=1, keepdims=True)
            pos = jnp.min(jnp.where(cand == m, lane2, float(ncand)), axis=1, keepdims=True)
            hit = lane2 == pos
            eid = jnp.max(jnp.where(hit, cidx, -1.0), axis=1, keepdims=True)
            cand = jnp.where(hit, NEG_INF, cand)
            sel = lane == h * PEER_TOPK + i
            return cand, jnp.where(sel, m, fv), jnp.where(sel, eid, fe)

        z = jnp.zeros((tm, PEER_HEADS * PEER_TOPK), F32)
        _, fv, fe = lax.fori_loop(0, PEER_TOPK, body, (cand, z, z))
        grp = lane // PEER_TOPK == h
        mx = jnp.max(jnp.where(grp, fv, NEG_INF), axis=1, keepdims=True)
        ex = jnp.where(grp, jnp.exp(fv - mx), 0.0)
        gate = ex / jnp.sum(ex, axis=1, keepdims=True)
        e_out = jnp.where(grp, fe, e_out)
        g_out = jnp.where(grp, gate, g_out)
    e_ref[...] = e_out.T.astype(jnp.int32)
    g_ref[...] = g_out.T


def _peer_route(qp, sub_keys):
    m, kq = qp.shape
    tm = min(256, m)
    ne = PEER_HEADS * PEER_TOPK
    ospec = pl.BlockSpec((ne, tm), lambda i: (0, i))
    return pl.pallas_call(
        _route_kernel,
        grid=(m // tm,),
        in_specs=[pl.BlockSpec((tm, kq), lambda i: (i, 0)),
                  pl.BlockSpec(sub_keys.shape, lambda i: (0, 0, 0, 0))],
        out_specs=[ospec, ospec],
        out_shape=[jax.ShapeDtypeStruct((ne, m), jnp.int32), jax.ShapeDtypeStruct((ne, m), F32)],
        compiler_params=_params("parallel"),
    )(qp, sub_keys)


def _gelu_tanh(x):
    return 0.5 * x * (1.0 + jnp.tanh(0.7978845608028654 * (x + 0.044715 * (x * x * x))))


_EXPERT_ROWS = 32


def _expert_kernel(h_ref, gain_ref, e_ref, g_ref, u_ref, v_ref, o_ref, xn_ref, w_ref):
    j = pl.program_id(1)
    te = u_ref.shape[0]
    tm = h_ref.shape[0]
    nsel = e_ref.shape[0]

    @pl.when(j == 0)
    def _():
        xn_ref[...] = _norm_rows(h_ref[...], gain_ref[...]).astype(BF16)
        o_ref[...] = h_ref[...]

    act = _gelu_tanh(lax.dot_general(u_ref[...], xn_ref[...], _NT, preferred_element_type=F32))
    for r in range(te // _EXPERT_ROWS):
        eid = j * te + r * _EXPERT_ROWS + lax.broadcasted_iota(jnp.int32, (_EXPERT_ROWS, tm), 0)

        def body(k, gsum, eid=eid):
            return gsum + jnp.where(e_ref[pl.ds(k, 1), :] == eid, g_ref[pl.ds(k, 1), :], 0.0)

        gsum = lax.fori_loop(0, nsel, body, jnp.zeros((_EXPERT_ROWS, tm), F32), unroll=8)
        rows = slice(r * _EXPERT_ROWS, (r + 1) * _EXPERT_ROWS)
        w_ref[rows, :] = (act[rows, :] * gsum).astype(BF16)
    o_ref[...] += lax.dot_general(w_ref[...], v_ref[...], _TN, preferred_element_type=F32)


def _peer_experts(h, gain, eidx_t, gate_t, u_tab, v_tab, *, te=256):
    m, d = h.shape
    ne = u_tab.shape[0]
    nsel = eidx_t.shape[0]
    tm = min(512, m)
    te = min(te, ne)
    return pl.pallas_call(
        _expert_kernel,
        grid=(m // tm, ne // te),
        in_specs=[pl.BlockSpec((tm, d), lambda i, j: (i, 0)),
                  pl.BlockSpec((1, d), lambda i, j: (0, 0)),
                  pl.BlockSpec((nsel, tm), lambda i, j: (0, i)),
                  pl.BlockSpec((nsel, tm), lambda i, j: (0, i)),
                  pl.BlockSpec((te, d), lambda i, j: (j, 0)),
                  pl.BlockSpec((te, d), lambda i, j: (j, 0))],
        out_specs=pl.BlockSpec((tm, d), lambda i, j: (i, 0)),
        out_shape=jax.ShapeDtypeStruct((m, d), F32),
        scratch_shapes=[pltpu.VMEM((tm, d), BF16), pltpu.VMEM((te, tm), BF16)],
        compiler_params=_params("parallel", "arbitrary"),
    )(h, gain.reshape(1, d), eidx_t, gate_t, u_tab, v_tab)


def _peer_ffn(h, i, prm):
    qp = _matmul(h, prm["w_q_peer"][i], gain=prm["g_ffn"][i])
    eidx_t, gate_t = _peer_route(qp, prm["peer_sub_keys"][i])
    return _peer_experts(h, prm["g_ffn"][i], eidx_t, gate_t, prm["peer_u"][i], prm["peer_v"][i])


def _ple(h, p, i, prm):
    return _matmul(h, prm["w_ple_gate"][i], gain=prm["g_ple"][i], residual=h,
                   ple=(p, prm["w_ple_proj"][i]))


def _trunk(x, pe, past, prm):
    b, t, d = x.shape
    m = b * t
    h = x.reshape(m, d)
    pe = pe.reshape(pe.shape[0], m, pe.shape[-1])

    g0 = prm["g_mix"][0]
    q = _matmul(h, prm["w_q_a"], gain=g0, out_dtype=BF16)
    k = _matmul(h, prm["w_k_a"], gain=g0)
    v = _matmul(h, prm["w_v_a"], gain=g0)
    if past is None:
        o = _sb_prompt(q, k, v, batch=b, seq=t)
    else:
        hd = k.shape[1]
        ck = past[0][0].reshape(b, -1, hd)
        cv = past[1][0].reshape(b, -1, hd)
        o = _sb_sample(q, k, v, ck, cv, batch=b, tq=t)
    h = _matmul(o, prm["w_o_a"], residual=h)
    h = _peer_ffn(h, 0, prm)
    h = _ple(h, pe[0], 0, prm)

    kv = _matmul(h, prm["w_kv_b"], gain=prm["g_kv"])
    nkv = N_KV_B * HEAD_DIM
    kb_new = kv[:, :nkv].reshape(b, t, nkv)
    vb_new = kv[:, nkv:].reshape(b, t, nkv)

    qb = _matmul(h, prm["w_q_b"], gain=prm["g_mix"][1], out_dtype=BF16)
    win = (N_PREV_CHUNKS + 2) * CHUNK
    if past is None:
        front = (N_PREV_CHUNKS + 1) * CHUNK
        kpad = jnp.pad(kb_new.astype(BF16), ((0, 0), (front, 0), (0, 0)))
        vpad = jnp.pad(vb_new.astype(BF16), ((0, 0), (front, 0), (0, 0)))
        bias = _band_bias(prm["rel_bias_b"][0], CHUNK, win)
        ob = _band_attention(qb, kpad, vpad, bias, batch=b, nq=t // CHUNK, cq=CHUNK, win=win,
                             chunked=True, lo_static=0)
    else:
        ck = past[2].reshape(b, -1, nkv)
        cv = past[3].reshape(b, -1, nkv)
        front = win - ck.shape[1] - t
        kpad = jnp.pad(jnp.concatenate([ck, kb_new], axis=1).astype(BF16), ((0, 0), (front, 0), (0, 0)))
        vpad = jnp.pad(jnp.concatenate([cv, vb_new], axis=1).astype(BF16), ((0, 0), (front, 0), (0, 0)))
        bias = _band_bias(prm["rel_bias_b"][0], t, win)
        ob = _band_attention(qb, kpad, vpad, bias, batch=b, nq=1, cq=t, win=win,
                             chunked=False, lo_static=front)
    h = _matmul(ob, prm["w_o_b"], residual=h)
    h = _peer_ffn(h, 1, prm)
    h = _ple(h, pe[1], 1, prm)

    y = _rmsnorm(h, prm["g_final"]).reshape(b, t, d)
    heads = k.shape[1] // HEAD_DIM
    a_k = k.reshape(1, b, t, heads, HEAD_DIM)
    a_v = v.reshape(1, b, t, heads, HEAD_DIM)
    if past is None:
        keep = min(N_PREV_CHUNKS * CHUNK, t)
        b_k, b_v = kb_new[:, t - keep:], vb_new[:, t - keep:]
    else:
        b_k, b_v = kb_new, vb_new
    b_k = b_k.reshape(b, -1, N_KV_B, HEAD_DIM)
    b_v = b_v.reshape(b, -1, N_KV_B, HEAD_DIM)
    return y, a_k, a_v, b_k, b_v


def kernel(x_prompt, x_sample, cache_a_k, cache_a_v, cache_b_k, cache_b_v, p_prompt, p_sample, g_mix, w_qkv_a, w_o_a, g_kv, w_kv_b, w_q_b, rel_bias_b, w_o_b, g_ffn, w_q_peer, peer_sub_keys, peer_u, peer_v, g_ple, w_ple_gate, w_ple_proj, g_final):
    d = x_prompt.shape[-1]
    prm = dict(
        g_mix=g_mix, g_kv=g_kv, g_ffn=g_ffn, g_ple=g_ple, g_final=g_final,
        w_q_a=w_qkv_a[0, :, :d].astype(BF16),
        w_k_a=w_qkv_a[0, :, d:2 * d].astype(BF16),
        w_v_a=w_qkv_a[0, :, 2 * d:].astype(BF16),
        w_o_a=w_o_a[0].astype(BF16),
        w_kv_b=w_kv_b.astype(BF16),
        w_q_b=w_q_b[0].astype(BF16),
        w_o_b=w_o_b[0].astype(BF16),
        rel_bias_b=rel_bias_b,
        w_q_peer=w_q_peer.astype(BF16),
        peer_sub_keys=peer_sub_keys,
        peer_u=peer_u.astype(BF16),
        peer_v=peer_v.astype(BF16),
        w_ple_gate=w_ple_gate.astype(BF16),
        w_ple_proj=w_ple_proj.astype(BF16),
    )
    y_p, ak_p, av_p, bk_p, bv_p = _trunk(x_prompt, p_prompt, None, prm)
    y_s, ak_s, av_s, bk_s, bv_s = _trunk(x_sample, p_sample,
                                         (cache_a_k, cache_a_v, cache_b_k, cache_b_v), prm)
    return (y_p, y_s, ak_p, av_p, bk_p, bv_p, ak_s, av_s, bk_s, bv_s)
```

```python
import functools

import jax
import jax.numpy as jnp
from jax import lax
from jax.experimental import pallas as pl
from jax.experimental.pallas import tpu as pltpu

F32 = jnp.float32
BF16 = jnp.bfloat16

EPS = 1e-6
HEAD_DIM = 128
CHUNK = 64
N_PREV_CHUNKS = 8
N_KV_B = 8
G_B = 4
REL_CLIP = 128
PEER_HEADS = 8
PEER_KEYS = 128
PEER_TOPK = 16
SCALE = HEAD_DIM ** -0.5
NEG_INF = float("-inf")

VMEM_LIMIT_BYTES = 56 * 1024 * 1024
LANES = 128

_NT = (((1,), (1,)), ((), ()))


def _params(*sem):
    return pltpu.CompilerParams(dimension_semantics=sem, vmem_limit_bytes=VMEM_LIMIT_BYTES)


def _norm_rows(x, g):
    ms = jnp.mean(x * x, axis=-1, keepdims=True)
    return x * lax.rsqrt(ms + EPS) * g


def _sigmoid(x):
    return 1.0 / (1.0 + jnp.exp(-x))


def _mm_kernel(*refs, norm, res, ple):
    it = iter(refs)
    x_ref = next(it)
    g_ref = next(it) if norm else None
    w_ref = next(it)
    res_ref = next(it) if res else None
    p_ref = next(it) if ple else None
    wp_ref = next(it) if ple else None
    o_ref = next(it)
    xn_ref = next(it) if norm else None

    if norm:
        @pl.when(pl.program_id(1) == 0)
        def _():
            xn_ref[...] = _norm_rows(x_ref[...], g_ref[...]).astype(BF16)
        x = xn_ref[...]
    else:
        x = x_ref[...]
    acc = jnp.dot(x, w_ref[...], preferred_element_type=F32)
    if ple:
        proj = jnp.dot(p_ref[...].astype(BF16), wp_ref[...], preferred_element_type=F32)
        acc = _sigmoid(acc) * proj
    if res:
        acc = acc + res_ref[...]
    o_ref[...] = acc.astype(o_ref.dtype)


def _matmul(x, w, *, gain=None, residual=None, ple=None, out_dtype=F32, tn=512):
    m, k = x.shape
    n = w.shape[1]
    tm = min(512, m)
    tn = min(tn, n)
    norm = gain is not None
    in_specs = [pl.BlockSpec((tm, k), lambda i, j: (i, 0))]
    args = [x]
    if norm:
        in_specs.append(pl.BlockSpec((1, k), lambda i, j: (0, 0)))
        args.append(gain.reshape(1, k))
    in_specs.append(pl.BlockSpec((k, tn), lambda i, j: (0, j)))
    args.append(w)
    if residual is not None:
        in_specs.append(pl.BlockSpec((tm, tn), lambda i, j: (i, j)))
        args.append(residual)
    if ple is not None:
        p, wp = ple
        kp = p.shape[1]
        in_specs.append(pl.BlockSpec((tm, kp), lambda i, j: (i, 0)))
        in_specs.append(pl.BlockSpec((kp, tn), lambda i, j: (0, j)))
        args += [p, wp]
    return pl.pallas_call(
        functools.partial(_mm_kernel, norm=norm, res=residual is not None, ple=ple is not None),
        grid=(m // tm, n // tn),
        in_specs=in_specs,
        out_specs=pl.BlockSpec((tm, tn), lambda i, j: (i, j)),
        out_shape=jax.ShapeDtypeStruct((m, n), out_dtype),
        scratch_shapes=[pltpu.VMEM((tm, k), BF16)] if norm else [],
        compiler_params=_params("parallel", "arbitrary"),
    )(*args)


def _rmsnorm_kernel(x_ref, g_ref, o_ref):
    o_ref[...] = _norm_rows(x_ref[...], g_ref[...])


def _rmsnorm(x, gain):
    m, k = x.shape
    tm = min(512, m)
    return pl.pallas_call(
        _rmsnorm_kernel,
        grid=(m // tm,),
        in_specs=[pl.BlockSpec((tm, k), lambda i: (i, 0)), pl.BlockSpec((1, k), lambda i: (0, 0))],
        out_specs=pl.BlockSpec((tm, k), lambda i: (i, 0)),
        out_shape=jax.ShapeDtypeStruct((m, k), F32),
        compiler_params=_params("parallel"),
    )(x, gain.reshape(1, k))


def _strict_lower_ones(n):
    r = lax.broadcasted_iota(jnp.int32, (n, n), 0)
    c = lax.broadcasted_iota(jnp.int32, (n, n), 1)
    return jnp.where(r > c, 1.0, 0.0).astype(BF16)


_SB_DEAD_BELOW = -104.0


def _sb_block(q, kb, vb, c, acc, tri, mask):
    z = lax.dot_general(q, kb, _NT, preferred_element_type=F32) * SCALE
    sp = jnp.log1p(jnp.exp(-jnp.abs(z)))
    log_beta = jnp.minimum(z, 0.0) - sp
    log_keep = -jnp.maximum(z, 0.0) - sp
    if mask is not None:
        log_keep = jnp.where(mask, log_keep, 0.0)
    hi = log_keep.astype(BF16)
    lo = (log_keep - hi.astype(F32)).astype(BF16)
    after = (jnp.dot(hi, tri, preferred_element_type=F32)
             + jnp.dot(lo, tri, preferred_element_type=F32))
    w = jnp.exp(log_beta + after + c)
    if mask is not None:
        w = jnp.where(mask, w, 0.0)
    acc = acc + jnp.dot(w.astype(BF16), vb, preferred_element_type=F32)
    c = c + jnp.sum(log_keep, axis=1, keepdims=True)
    return c, acc


def _sb_prompt_kernel(q_ref, k_ref, v_ref, o_ref, kb_ref, vb_ref, *, blk):
    seq = q_ref.shape[0]
    kb_ref[...] = k_ref[...].astype(BF16)
    vb_ref[...] = v_ref[...].astype(BF16)
    tri = _strict_lower_ones(blk)
    row = lax.broadcasted_iota(jnp.int32, (blk, blk), 0)
    col = lax.broadcasted_iota(jnp.int32, (blk, blk), 1)
    diag_mask = col < row

    def q_body(i, _):
        q0 = pl.multiple_of(i * blk, blk)
        q = q_ref[pl.ds(q0, blk), :]
        c = jnp.zeros((blk, 1), F32)
        acc = jnp.zeros((blk, HEAD_DIM), F32)
        c, acc = _sb_block(q, kb_ref[pl.ds(q0, blk), :], vb_ref[pl.ds(q0, blk), :], c, acc, tri, diag_mask)

        def k_cond(carry):
            n, cmax, _, _ = carry
            return jnp.logical_and(n < i, cmax > _SB_DEAD_BELOW)

        def k_body(carry):
            n, _, c, acc = carry
            k0 = pl.multiple_of((i - 1 - n) * blk, blk)
            c, acc = _sb_block(q, kb_ref[pl.ds(k0, blk), :], vb_ref[pl.ds(k0, blk), :], c, acc, tri, None)
            return n + 1, jnp.max(c), c, acc

        _, _, _, acc = lax.while_loop(k_cond, k_body, (0, jnp.max(c), c, acc))
        o_ref[pl.ds(q0, blk), :] = acc.astype(o_ref.dtype)
        return 0

    lax.fori_loop(0, seq // blk, q_body, 0)


def _sb_prompt(q, k, v, *, batch, seq, blk=256):
    t, hd = q.shape
    heads = hd // HEAD_DIM
    blk = min(blk, seq)
    spec = pl.BlockSpec((seq, HEAD_DIM), lambda b, h: (b, h))
    return pl.pallas_call(
        functools.partial(_sb_prompt_kernel, blk=blk),
        grid=(batch, heads),
        in_specs=[spec, spec, spec],
        out_specs=spec,
        out_shape=jax.ShapeDtypeStruct((t, hd), BF16),
        scratch_shapes=[pltpu.VMEM((seq, HEAD_DIM), BF16), pltpu.VMEM((seq, HEAD_DIM), BF16)],
        compiler_params=_params("parallel", "parallel"),
    )(q, k, v)


def _sb_sample_kernel(q_ref, kn_ref, vn_ref, kc_ref, vc_ref, o_ref, *, blk):
    tq = q_ref.shape[0]
    past = kc_ref.shape[0]
    q = q_ref[...]
    tri = _strict_lower_ones(blk)
    row = lax.broadcasted_iota(jnp.int32, (tq, blk), 0)
    col = lax.broadcasted_iota(jnp.int32, (tq, blk), 1)
    c = jnp.zeros((tq, 1), F32)
    acc = jnp.zeros((tq, HEAD_DIM), F32)
    c, acc = _sb_block(q, kn_ref[...], vn_ref[...], c, acc, tri, col < row)

    def k_cond(carry):
        n, cmax, _, _ = carry
        return jnp.logical_and(n < past // blk, cmax > _SB_DEAD_BELOW)

    def k_body(carry):
        n, _, c, acc = carry
        k0 = pl.multiple_of(past - (n + 1) * blk, blk)
        kb = kc_ref[pl.ds(k0, blk), :].astype(BF16)
        vb = vc_ref[pl.ds(k0, blk), :].astype(BF16)
        c, acc = _sb_block(q, kb, vb, c, acc, tri, None)
        return n + 1, jnp.max(c), c, acc

    _, _, _, acc = lax.while_loop(k_cond, k_body, (0, jnp.max(c), c, acc))
    o_ref[...] = acc.astype(o_ref.dtype)


def _sb_sample(q, k_new, v_new, cache_k, cache_v, *, batch, tq, blk=128):
    t, hd = q.shape
    heads = hd // HEAD_DIM
    past = cache_k.shape[1]

    def pad_block(a):
        a = a.reshape(batch, tq, hd)
        a = jnp.pad(a, ((0, 0), (0, blk - tq), (0, 0)))
        return a.reshape(batch * blk, hd).astype(BF16)

    qspec = pl.BlockSpec((tq, HEAD_DIM), lambda b, h: (b, h))
    nspec = pl.BlockSpec((blk, HEAD_DIM), lambda b, h: (b, h))
    cspec = pl.BlockSpec((None, past, HEAD_DIM), lambda b, h: (b, 0, h))
    return pl.pallas_call(
        functools.partial(_sb_sample_kernel, blk=blk),
        grid=(batch, heads),
        in_specs=[qspec, nspec, nspec, cspec, cspec],
        out_specs=qspec,
        out_shape=jax.ShapeDtypeStruct((t, hd), BF16),
        compiler_params=_params("parallel", "parallel"),
    )(q, pad_block(k_new), pad_block(v_new), cache_k, cache_v)


def _band_kernel(q_ref, k_ref, v_ref, bias_ref, o_ref, *, cq, win, chunked, lo_static):
    if chunked:
        c = pl.program_id(2)
        start = pl.multiple_of(c * cq, cq)
        lo = jnp.maximum(CHUNK, (N_PREV_CHUNKS + 1 - c) * CHUNK)
    else:
        start = 0
        lo = lo_static
    kb = k_ref[pl.ds(start, win), :]
    vb = v_ref[pl.ds(start, win), :]
    qb = q_ref[...]
    q4 = jnp.concatenate([qb[:, g * HEAD_DIM:(g + 1) * HEAD_DIM] for g in range(G_B)], axis=0)
    s = lax.dot_general(q4, kb, _NT, preferred_element_type=F32) * SCALE
    s = s + bias_ref[...].reshape(G_B * cq, win)
    col = lax.broadcasted_iota(jnp.int32, (G_B * cq, win), 1)
    s = jnp.where(col >= lo, s, NEG_INF)
    m = jnp.max(s, axis=1, keepdims=True)
    e = jnp.exp(s - m)
    p = e / jnp.sum(e, axis=1, keepdims=True)
    o = jnp.dot(p.astype(BF16), vb, preferred_element_type=F32)
    for g in range(G_B):
        o_ref[:, g * HEAD_DIM:(g + 1) * HEAD_DIM] = o[g * cq:(g + 1) * cq].astype(o_ref.dtype)


def _band_bias(rel_table, cq, win):
    rel = jnp.arange(cq)[:, None] + (win - cq) - jnp.arange(win)[None, :]
    bias = rel_table[jnp.clip(rel, -REL_CLIP, REL_CLIP) + REL_CLIP].astype(F32)
    return bias.transpose(2, 0, 1)


def _band_attention(q, kpad, vpad, bias, *, batch, nq, cq, win, chunked, lo_static):
    t, hd = q.shape
    rows = kpad.shape[1]
    qw = G_B * HEAD_DIM
    qspec = pl.BlockSpec((cq, qw), lambda b, h, c: (b * nq + c, h))
    kspec = pl.BlockSpec((None, rows, HEAD_DIM), lambda b, h, c: (b, 0, h))
    bspec = pl.BlockSpec((G_B, cq, win), lambda b, h, c: (h, 0, 0))
    return pl.pallas_call(
        functools.partial(_band_kernel, cq=cq, win=win, chunked=chunked, lo_static=lo_static),
        grid=(batch, N_KV_B, nq),
        in_specs=[qspec, kspec, kspec, bspec],
        out_specs=qspec,
        out_shape=jax.ShapeDtypeStruct((t, hd), BF16),
        compiler_params=_params("parallel", "parallel", "arbitrary"),
    )(q, kpad, vpad, bias)


def _split3_dot_nt(a, b):
    a_hi = a.astype(BF16)
    a_lo = (a - a_hi.astype(F32)).astype(BF16)
    b_hi = b.astype(BF16)
    b_lo = (b - b_hi.astype(F32)).astype(BF16)
    return (lax.dot_general(a_hi, b_hi, _NT, preferred_element_type=F32)
            + lax.dot_general(a_lo, b_hi, _NT, preferred_element_type=F32)
            + lax.dot_general(a_hi, b_lo, _NT, preferred_element_type=F32))


def _take_lanes(x, idx):
    return jnp.take_along_axis(x, idx, axis=1, mode="promise_in_bounds")


def _route_kernel(q_ref, sk_ref, e_ref, g_ref):
    tm = q_ref.shape[0]
    dh = sk_ref.shape[-1]
    nset = 2 * PEER_HEADS

    s = jnp.concatenate(
        [_split3_dot_nt(q_ref[:, n * dh:(n + 1) * dh], sk_ref[n // 2, n % 2]) for n in range(nset)], axis=0)
    lane1 = lax.broadcasted_iota(jnp.int32, s.shape, 1)

    def body1(i, carry):
        s, sv, si = carry
        m = jnp.max(s, axis=1, keepdims=True)
        pos = jnp.argmax(s, axis=1, keepdims=True).astype(jnp.int32)
        s = jnp.where(lane1 == pos, NEG_INF, s)
        sel = lane1 == i
        return s, jnp.where(sel, m, sv), jnp.where(sel, pos.astype(F32), si)

    z1 = jnp.zeros(s.shape, F32)
    _, sv, si = lax.fori_loop(0, PEER_TOPK, body1, (s, z1, z1))

    l128 = lax.broadcasted_iota(jnp.int32, (tm, PEER_KEYS), 1)
    outer_lo = l128 // PEER_TOPK
    outer_hi = outer_lo + PEER_KEYS // PEER_TOPK
    inner = l128 % PEER_TOPK
    c_lo, c_hi, x_lo, x_hi = [], [], [], []
    for h in range(PEER_HEADS):
        r0 = slice((2 * h) * tm, (2 * h + 1) * tm)
        r1 = slice((2 * h + 1) * tm, (2 * h + 2) * tm)
        v1 = _take_lanes(sv[r1], inner)
        i1 = _take_lanes(si[r1], inner)
        c_lo.append(_take_lanes(sv[r0], outer_lo) + v1)
        c_hi.append(_take_lanes(sv[r0], outer_hi) + v1)
        x_lo.append(_take_lanes(si[r0], outer_lo) * float(PEER_KEYS) + i1)
        x_hi.append(_take_lanes(si[r0], outer_hi) * float(PEER_KEYS) + i1)
    c_lo = jnp.concatenate(c_lo, axis=0)
    c_hi = jnp.concatenate(c_hi, axis=0)
    x_lo = jnp.concatenate(x_lo, axis=0)
    x_hi = jnp.concatenate(x_hi, axis=0)

    lane2 = lax.broadcasted_iota(jnp.int32, c_lo.shape, 1)
    head_lane0 = (lax.broadcasted_iota(jnp.int32, c_lo.shape, 0) // tm) * PEER_TOPK

    def body2(i, carry):
        c_lo, c_hi, fv, fe = carry
        m_lo = jnp.max(c_lo, axis=1, keepdims=True)
        m_hi = jnp.max(c_hi, axis=1, keepdims=True)
        p_lo = jnp.argmax(c_lo, axis=1, keepdims=True).astype(jnp.int32)
        p_hi = jnp.argmax(c_hi, axis=1, keepdims=True).astype(jnp.int32)
        use_lo = m_lo >= m_hi
        hit_lo = lane2 == jnp.where(use_lo, p_lo, -1)
        hit_hi = lane2 == jnp.where(use_lo, -1, p_hi)
        eid = jnp.max(jnp.maximum(jnp.where(hit_lo, x_lo, -1.0), jnp.where(hit_hi, x_hi, -1.0)),
                      axis=1, keepdims=True)
        sel = lane2 == head_lane0 + i
        return (jnp.where(hit_lo, NEG_INF, c_lo), jnp.where(hit_hi, NEG_INF, c_hi),
                jnp.where(sel, jnp.maximum(m_lo, m_hi), fv), jnp.where(sel, eid, fe))

    z2 = jnp.zeros(c_lo.shape, F32)
    _, _, fv, fe = lax.fori_loop(0, PEER_TOPK, body2, (c_lo, c_hi, z2, z2))

    grp = lane2 // PEER_TOPK == head_lane0 // PEER_TOPK
    mx = jnp.max(jnp.where(grp, fv, NEG_INF), axis=1, keepdims=True)
    ex = jnp.where(grp, jnp.exp(fv - mx), 0.0)
    gate = ex / jnp.sum(ex, axis=1, keepdims=True)
    e_out = fe[0:tm]
    g_out = gate[0:tm]
    for h in range(1, PEER_HEADS):
        e_out = e_out + fe[h * tm:(h + 1) * tm]
        g_out = g_out + gate[h * tm:(h + 1) * tm]
    e_ref[...] = e_out.astype(jnp.int32)
    g_ref[...] = g_out


def _peer_route(qp, sub_keys):
    m, kq = qp.shape
    tm = min(256, m)
    ne = PEER_HEADS * PEER_TOPK
    ospec = pl.BlockSpec((tm, ne), lambda i: (i, 0))
    return pl.pallas_call(
        _route_kernel,
        grid=(m // tm,),
        in_specs=[pl.BlockSpec((tm, kq), lambda i: (i, 0)),
                  pl.BlockSpec(sub_keys.shape, lambda i: (0, 0, 0, 0))],
        out_specs=[ospec, ospec],
        out_shape=[jax.ShapeDtypeStruct((m, ne), jnp.int32), jax.ShapeDtypeStruct((m, ne), F32)],
        compiler_params=_params("parallel"),
    )(qp, sub_keys)


def _gelu_tanh(x):
    return 0.5 * x * (1.0 + jnp.tanh(0.7978845608028654 * (x + 0.044715 * (x * x * x))))


_TOKEN_GROUP = 16


def _expert_kernel(h_ref, gain_ref, e_ref, g_ref, tab_ref, o_ref,
                   xn_ref, hi_ref, lo_ref, act_ref, stage_ref, w3_ref, *, nj):
    j = pl.program_id(1)
    tm = h_ref.shape[0]
    te = tab_ref.shape[0]
    nb = te // LANES

    @pl.when(j == 0)
    def _():
        xn_ref[...] = _norm_rows(h_ref[...], gain_ref[...]).astype(BF16)
        o_ref[...] = h_ref[...]
        e = e_ref[...]
        hi_ref[...] = e // LANES
        lo_ref[...] = e % LANES
        act_ref[...] = jnp.zeros(act_ref.shape, F32)

    @pl.when(j < nj)
    def _():
        d = lax.dot_general(xn_ref[...], tab_ref[...], _NT, preferred_element_type=F32)
        hi = hi_ref[...]
        lo = lo_ref[...]
        act = act_ref[...]
        for s in range(nb):
            picked = _take_lanes(d[:, s * LANES:(s + 1) * LANES], lo)
            act = jnp.where(hi == j * nb + s, picked, act)
        act_ref[...] = act

    @pl.when(j == nj)
    def _():
        act_ref[...] = g_ref[...] * _gelu_tanh(act_ref[...])
        sub = lax.broadcasted_iota(jnp.int32, (LANES, LANES), 0)

        def group(gi, _):
            t0 = pl.multiple_of(gi * _TOKEN_GROUP, _TOKEN_GROUP)
            for u in range(_TOKEN_GROUP):
                hi_row = hi_ref[pl.ds(t0 + u, 1), :]
                lo_row = lo_ref[pl.ds(t0 + u, 1), :]
                w_row = act_ref[pl.ds(t0 + u, 1), :]
                a_t = jnp.where(hi_row == sub, w_row, 0.0).astype(BF16)
                b_t = jnp.where(lo_row == sub, 1.0, 0.0).astype(BF16)
                stage_ref[u * LANES:(u + 1) * LANES, :] = lax.dot_general(
                    a_t, b_t, _NT, preferred_element_type=F32)

            def move(i1, _):
                rows = stage_ref[pl.ds(i1, _TOKEN_GROUP, stride=LANES), :]
                w3_ref[i1, pl.ds(t0, _TOKEN_GROUP), :] = rows.astype(BF16)
                return 0

            lax.fori_loop(0, LANES, move, 0, unroll=8)
            return 0

        lax.fori_loop(0, tm // _TOKEN_GROUP, group, 0)

    @pl.when(j >= nj)
    def _():
        jj = j - nj
        w = jnp.concatenate([w3_ref[jj * nb + s] for s in range(nb)], axis=1)
        o_ref[...] += jnp.dot(w, tab_ref[...], preferred_element_type=F32)


def _peer_experts(h, gain, eidx, gate, uv_tab, *, te=1024):
    m, d = h.shape
    ne = uv_tab.shape[1]
    nsel = eidx.shape[1]
    tm = min(256, m)
    nj = ne // te
    sel_spec = pl.BlockSpec((tm, nsel), lambda i, j: (i, 0))
    return pl.pallas_call(
        functools.partial(_expert_kernel, nj=nj),
        grid=(m // tm, 2 * nj),
        in_specs=[pl.BlockSpec((tm, d), lambda i, j: (i, 0)),
                  pl.BlockSpec((1, d), lambda i, j: (0, 0)),
                  sel_spec, sel_spec,
                  pl.BlockSpec((None, te, d), lambda i, j: (j // nj, j % nj, 0))],
        out_specs=pl.BlockSpec((tm, d), lambda i, j: (i, 0)),
        out_shape=jax.ShapeDtypeStruct((m, d), F32),
        scratch_shapes=[pltpu.VMEM((tm, d), BF16),
                        pltpu.VMEM((tm, nsel), jnp.int32),
                        pltpu.VMEM((tm, nsel), jnp.int32),
                        pltpu.VMEM((tm, nsel), F32),
                        pltpu.VMEM((_TOKEN_GROUP * LANES, LANES), F32),
                        pltpu.VMEM((ne // LANES, tm, LANES), BF16)],
        compiler_params=_params("parallel", "arbitrary"),
    )(h, gain.reshape(1, d), eidx, gate, uv_tab)


def _peer_ffn(h, i, prm):
    qp = _matmul(h, prm["w_q_peer"][i], gain=prm["g_ffn"][i])
    eidx, gate = _peer_route(qp, prm["peer_sub_keys"][i])
    return _peer_experts(h, prm["g_ffn"][i], eidx, gate, prm["peer_uv"][i])


def _ple(h, p, i, prm):
    return _matmul(h, prm["w_ple_gate"][i], gain=prm["g_ple"][i], residual=h,
                   ple=(p, prm["w_ple_proj"][i]))


def _trunk(x, pe, past, prm):
    b, t, d = x.shape
    m = b * t
    h = x.reshape(m, d)
    pe = pe.reshape(pe.shape[0], m, pe.shape[-1])

    g0 = prm["g_mix"][0]
    q = _matmul(h, prm["w_q_a"], gain=g0, out_dtype=BF16)
    k = _matmul(h, prm["w_k_a"], gain=g0)
    v = _matmul(h, prm["w_v_a"], gain=g0)
    if past is None:
        o = _sb_prompt(q, k, v, batch=b, seq=t)
    else:
        hd = k.shape[1]
        ck = past[0][0].reshape(b, -1, hd)
        cv = past[1][0].reshape(b, -1, hd)
        o = _sb_sample(q, k, v, ck, cv, batch=b, tq=t)
    h = _matmul(o, prm["w_o_a"], residual=h)
    h = _peer_ffn(h, 0, prm)
    h = _ple(h, pe[0], 0, prm)

    kv = _matmul(h, prm["w_kv_b"], gain=prm["g_kv"])
    nkv = N_KV_B * HEAD_DIM
    kb_new = kv[:, :nkv].reshape(b, t, nkv)
    vb_new = kv[:, nkv:].reshape(b, t, nkv)

    qb = _matmul(h, prm["w_q_b"], gain=prm["g_mix"][1], out_dtype=BF16)
    win = (N_PREV_CHUNKS + 2) * CHUNK
    if past is None:
        front = (N_PREV_CHUNKS + 1) * CHUNK
        kpad = jnp.pad(kb_new.astype(BF16), ((0, 0), (front, 0), (0, 0)))
        vpad = jnp.pad(vb_new.astype(BF16), ((0, 0), (front, 0), (0, 0)))
        bias = _band_bias(prm["rel_bias_b"][0], CHUNK, win)
        ob = _band_attention(qb, kpad, vpad, bias, batch=b, nq=t // CHUNK, cq=CHUNK, win=win,
                             chunked=True, lo_static=0)
    else:
        ck = past[2].reshape(b, -1, nkv)
        cv = past[3].reshape(b, -1, nkv)
        front = win - ck.shape[1] - t
        kpad = jnp.pad(jnp.concatenate([ck, kb_new], axis=1).astype(BF16), ((0, 0), (front, 0), (0, 0)))
        vpad = jnp.pad(jnp.concatenate([cv, vb_new], axis=1).astype(BF16), ((0, 0), (front, 0), (0, 0)))
        bias = _band_bias(prm["rel_bias_b"][0], t, win)
        ob = _band_attention(qb, kpad, vpad, bias, batch=b, nq=1, cq=t, win=win,
                             chunked=False, lo_static=front)
    h = _matmul(ob, prm["w_o_b"], residual=h)
    h = _peer_ffn(h, 1, prm)
    h = _ple(h, pe[1], 1, prm)

    y = _rmsnorm(h, prm["g_final"]).reshape(b, t, d)
    heads = k.shape[1] // HEAD_DIM
    a_k = k.reshape(1, b, t, heads, HEAD_DIM)
    a_v = v.reshape(1, b, t, heads, HEAD_DIM)
    if past is None:
        keep = min(N_PREV_CHUNKS * CHUNK, t)
        b_k, b_v = kb_new[:, t - keep:], vb_new[:, t - keep:]
    else:
        b_k, b_v = kb_new, vb_new
    b_k = b_k.reshape(b, -1, N_KV_B, HEAD_DIM)
    b_v = b_v.reshape(b, -1, N_KV_B, HEAD_DIM)
    return y, a_k, a_v, b_k, b_v


def kernel(x_prompt, x_sample, cache_a_k, cache_a_v, cache_b_k, cache_b_v, p_prompt, p_sample, g_mix, w_qkv_a, w_o_a, g_kv, w_kv_b, w_q_b, rel_bias_b, w_o_b, g_ffn, w_q_peer, peer_sub_keys, peer_u, peer_v, g_ple, w_ple_gate, w_ple_proj, g_final):
    d = x_prompt.shape[-1]
    prm = dict(
        g_mix=g_mix, g_kv=g_kv, g_ffn=g_ffn, g_ple=g_ple, g_final=g_final,
        w_q_a=w_qkv_a[0, :, :d].astype(BF16),
        w_k_a=w_qkv_a[0, :, d:2 * d].astype(BF16),
        w_v_a=w_qkv_a[0, :, 2 * d:].astype(BF16),
        w_o_a=w_o_a[0].astype(BF16),
        w_kv_b=w_kv_b.astype(BF16),
        w_q_b=w_q_b[0].astype(BF16),
        w_o_b=w_o_b[0].astype(BF16),
        rel_bias_b=rel_bias_b,
        w_q_peer=w_q_peer.astype(BF16),
        peer_sub_keys=peer_sub_keys,
        peer_uv=jnp.stack([peer_u.astype(BF16), peer_v.astype(BF16)], axis=1),
        w_ple_gate=w_ple_gate.astype(BF16),
        w_ple_proj=w_ple_proj.astype(BF16),
    )
    y_p, ak_p, av_p, bk_p, bv_p = _trunk(x_prompt, p_prompt, None, prm)
    y_s, ak_s, av_s, bk_s, bv_s = _trunk(x_sample, p_sample,
                                         (cache_a_k, cache_a_v, cache_b_k, cache_b_v), prm)
    return (y_p, y_s, ak_p, av_p, bk_p, bv_p, ak_s, av_s, bk_s, bv_s)
```

```python
import functools

import jax
import jax.numpy as jnp
from jax import lax
from jax.experimental import pallas as pl
from jax.experimental.pallas import tpu as pltpu

F32 = jnp.float32
BF16 = jnp.bfloat16

EPS = 1e-6
HEAD_DIM = 128
CHUNK = 64
N_PREV_CHUNKS = 8
N_KV_B = 8
G_B = 4
REL_CLIP = 128
PEER_HEADS = 8
PEER_KEYS = 128
PEER_TOPK = 16
SCALE = HEAD_DIM ** -0.5
NEG_INF = float("-inf")

VMEM_LIMIT_BYTES = 56 * 1024 * 1024
LANES = 128

_NT = (((1,), (1,)), ((), ()))


def _params(*sem):
    return pltpu.CompilerParams(dimension_semantics=sem, vmem_limit_bytes=VMEM_LIMIT_BYTES)


def _norm_rows(x, g):
    ms = jnp.mean(x * x, axis=-1, keepdims=True)
    return x * lax.rsqrt(ms + EPS) * g


def _sigmoid(x):
    return 1.0 / (1.0 + jnp.exp(-x))


def _mm_kernel(*refs, norm, res, ple):
    it = iter(refs)
    x_ref = next(it)
    g_ref = next(it) if norm else None
    w_ref = next(it)
    res_ref = next(it) if res else None
    p_ref = next(it) if ple else None
    wp_ref = next(it) if ple else None
    o_ref = next(it)
    xn_ref = next(it) if norm else None

    if norm:
        @pl.when(pl.program_id(1) == 0)
        def _():
            xn_ref[...] = _norm_rows(x_ref[...], g_ref[...]).astype(BF16)
        x = xn_ref[...]
    else:
        x = x_ref[...]
    acc = jnp.dot(x, w_ref[...], preferred_element_type=F32)
    if ple:
        proj = jnp.dot(p_ref[...].astype(BF16), wp_ref[...], preferred_element_type=F32)
        acc = _sigmoid(acc) * proj
    if res:
        acc = acc + res_ref[...]
    o_ref[...] = acc.astype(o_ref.dtype)


def _matmul(x, w, *, gain=None, residual=None, ple=None, out_dtype=F32, tn=512):
    m, k = x.shape
    n = w.shape[1]
    tm = min(512, m)
    tn = min(tn, n)
    norm = gain is not None
    in_specs = [pl.BlockSpec((tm, k), lambda i, j: (i, 0))]
    args = [x]
    if norm:
        in_specs.append(pl.BlockSpec((1, k), lambda i, j: (0, 0)))
        args.append(gain.reshape(1, k))
    in_specs.append(pl.BlockSpec((k, tn), lambda i, j: (0, j)))
    args.append(w)
    if residual is not None:
        in_specs.append(pl.BlockSpec((tm, tn), lambda i, j: (i, j)))
        args.append(residual)
    if ple is not None:
        p, wp = ple
        kp = p.shape[1]
        in_specs.append(pl.BlockSpec((tm, kp), lambda i, j: (i, 0)))
        in_specs.append(pl.BlockSpec((kp, tn), lambda i, j: (0, j)))
        args += [p, wp]
    return pl.pallas_call(
        functools.partial(_mm_kernel, norm=norm, res=residual is not None, ple=ple is not None),
        grid=(m // tm, n // tn),
        in_specs=in_specs,
        out_specs=pl.BlockSpec((tm, tn), lambda i, j: (i, j)),
        out_shape=jax.ShapeDtypeStruct((m, n), out_dtype),
        scratch_shapes=[pltpu.VMEM((tm, k), BF16)] if norm else [],
        compiler_params=_params("parallel", "arbitrary"),
    )(*args)


def _rmsnorm_kernel(x_ref, g_ref, o_ref):
    o_ref[...] = _norm_rows(x_ref[...], g_ref[...])


def _rmsnorm(x, gain):
    m, k = x.shape
    tm = min(512, m)
    return pl.pallas_call(
        _rmsnorm_kernel,
        grid=(m // tm,),
        in_specs=[pl.BlockSpec((tm, k), lambda i: (i, 0)), pl.BlockSpec((1, k), lambda i: (0, 0))],
        out_specs=pl.BlockSpec((tm, k), lambda i: (i, 0)),
        out_shape=jax.ShapeDtypeStruct((m, k), F32),
        compiler_params=_params("parallel"),
    )(x, gain.reshape(1, k))


def _strict_lower_ones(n):
    r = lax.broadcasted_iota(jnp.int32, (n, n), 0)
    c = lax.broadcasted_iota(jnp.int32, (n, n), 1)
    return jnp.where(r > c, 1.0, 0.0).astype(BF16)


_SB_DEAD_BELOW = -104.0


def _sb_block(q, kb, vb, c, acc, tri, mask):
    z = lax.dot_general(q, kb, _NT, preferred_element_type=F32) * SCALE
    sp = jnp.log1p(jnp.exp(-jnp.abs(z)))
    log_beta = jnp.minimum(z, 0.0) - sp
    log_keep = -jnp.maximum(z, 0.0) - sp
    if mask is not None:
        log_keep = jnp.where(mask, log_keep, 0.0)
    hi = log_keep.astype(BF16)
    lo = (log_keep - hi.astype(F32)).astype(BF16)
    after = (jnp.dot(hi, tri, preferred_element_type=F32)
             + jnp.dot(lo, tri, preferred_element_type=F32))
    w = jnp.exp(log_beta + after + c)
    if mask is not None:
        w = jnp.where(mask, w, 0.0)
    acc = acc + jnp.dot(w.astype(BF16), vb, preferred_element_type=F32)
    c = c + jnp.sum(log_keep, axis=1, keepdims=True)
    return c, acc


def _sb_prompt_kernel(q_ref, k_ref, v_ref, o_ref, kb_ref, vb_ref, *, blk):
    seq = q_ref.shape[0]
    kb_ref[...] = k_ref[...].astype(BF16)
    vb_ref[...] = v_ref[...].astype(BF16)
    tri = _strict_lower_ones(blk)
    row = lax.broadcasted_iota(jnp.int32, (blk, blk), 0)
    col = lax.broadcasted_iota(jnp.int32, (blk, blk), 1)
    diag_mask = col < row

    def q_body(i, _):
        q0 = pl.multiple_of(i * blk, blk)
        q = q_ref[pl.ds(q0, blk), :]
        c = jnp.zeros((blk, 1), F32)
        acc = jnp.zeros((blk, HEAD_DIM), F32)
        c, acc = _sb_block(q, kb_ref[pl.ds(q0, blk), :], vb_ref[pl.ds(q0, blk), :], c, acc, tri, diag_mask)

        def k_cond(carry):
            n, cmax, _, _ = carry
            return jnp.logical_and(n < i, cmax > _SB_DEAD_BELOW)

        def k_body(carry):
            n, _, c, acc = carry
            k0 = pl.multiple_of((i - 1 - n) * blk, blk)
            c, acc = _sb_block(q, kb_ref[pl.ds(k0, blk), :], vb_ref[pl.ds(k0, blk), :], c, acc, tri, None)
            return n + 1, jnp.max(c), c, acc

        _, _, _, acc = lax.while_loop(k_cond, k_body, (0, jnp.max(c), c, acc))
        o_ref[pl.ds(q0, blk), :] = acc.astype(o_ref.dtype)
        return 0

    lax.fori_loop(0, seq // blk, q_body, 0)


def _sb_prompt(q, k, v, *, batch, seq, blk=256):
    t, hd = q.shape
    heads = hd // HEAD_DIM
    blk = min(blk, seq)
    spec = pl.BlockSpec((seq, HEAD_DIM), lambda b, h: (b, h))
    return pl.pallas_call(
        functools.partial(_sb_prompt_kernel, blk=blk),
        grid=(batch, heads),
        in_specs=[spec, spec, spec],
        out_specs=spec,
        out_shape=jax.ShapeDtypeStruct((t, hd), BF16),
        scratch_shapes=[pltpu.VMEM((seq, HEAD_DIM), BF16), pltpu.VMEM((seq, HEAD_DIM), BF16)],
        compiler_params=_params("parallel", "parallel"),
    )(q, k, v)


def _sb_sample_kernel(q_ref, kn_ref, vn_ref, kc_ref, vc_ref, o_ref, *, blk):
    tq = q_ref.shape[0]
    past = kc_ref.shape[0]
    q = q_ref[...]
    tri = _strict_lower_ones(blk)
    row = lax.broadcasted_iota(jnp.int32, (tq, blk), 0)
    col = lax.broadcasted_iota(jnp.int32, (tq, blk), 1)
    c = jnp.zeros((tq, 1), F32)
    acc = jnp.zeros((tq, HEAD_DIM), F32)
    c, acc = _sb_block(q, kn_ref[...], vn_ref[...], c, acc, tri, col < row)

    def k_cond(carry):
        n, cmax, _, _ = carry
        return jnp.logical_and(n < past // blk, cmax > _SB_DEAD_BELOW)

    def k_body(carry):
        n, _, c, acc = carry
        k0 = pl.multiple_of(past - (n + 1) * blk, blk)
        kb = kc_ref[pl.ds(k0, blk), :].astype(BF16)
        vb = vc_ref[pl.ds(k0, blk), :].astype(BF16)
        c, acc = _sb_block(q, kb, vb, c, acc, tri, None)
        return n + 1, jnp.max(c), c, acc

    _, _, _, acc = lax.while_loop(k_cond, k_body, (0, jnp.max(c), c, acc))
    o_ref[...] = acc.astype(o_ref.dtype)


def _sb_sample(q, k_new, v_new, cache_k, cache_v, *, batch, tq, blk=128):
    t, hd = q.shape
    heads = hd // HEAD_DIM
    past = cache_k.shape[1]

    def pad_block(a):
        a = a.reshape(batch, tq, hd)
        a = jnp.pad(a, ((0, 0), (0, blk - tq), (0, 0)))
        return a.reshape(batch * blk, hd).astype(BF16)

    qspec = pl.BlockSpec((tq, HEAD_DIM), lambda b, h: (b, h))
    nspec = pl.BlockSpec((blk, HEAD_DIM), lambda b, h: (b, h))
    cspec = pl.BlockSpec((None, past, HEAD_DIM), lambda b, h: (b, 0, h))
    return pl.pallas_call(
        functools.partial(_sb_sample_kernel, blk=blk),
        grid=(batch, heads),
        in_specs=[qspec, nspec, nspec, cspec, cspec],
        out_specs=qspec,
        out_shape=jax.ShapeDtypeStruct((t, hd), BF16),
        compiler_params=_params("parallel", "parallel"),
    )(q, pad_block(k_new), pad_block(v_new), cache_k, cache_v)


def _band_kernel(q_ref, k_ref, v_ref, bias_ref, o_ref, *, cq, win, nq, chunked, lo_static):
    bias = bias_ref[...].reshape(G_B * cq, win)
    col = lax.broadcasted_iota(jnp.int32, (G_B * cq, win), 1)

    def chunk(c, _):
        r0 = pl.multiple_of(c * cq, cq)
        if chunked:
            lo = jnp.maximum(CHUNK, (N_PREV_CHUNKS + 1 - c) * CHUNK)
        else:
            lo = lo_static
        kb = k_ref[pl.ds(r0, win), :]
        vb = v_ref[pl.ds(r0, win), :]
        qb = q_ref[pl.ds(r0, cq), :]
        q4 = jnp.concatenate([qb[:, g * HEAD_DIM:(g + 1) * HEAD_DIM] for g in range(G_B)], axis=0)
        s = lax.dot_general(q4, kb, _NT, preferred_element_type=F32) * SCALE + bias
        s = jnp.where(col >= lo, s, NEG_INF)
        m = jnp.max(s, axis=1, keepdims=True)
        e = jnp.exp(s - m)
        p = e / jnp.sum(e, axis=1, keepdims=True)
        o = jnp.dot(p.astype(BF16), vb, preferred_element_type=F32)
        for g in range(G_B):
            o_ref[pl.ds(r0, cq), g * HEAD_DIM:(g + 1) * HEAD_DIM] = o[g * cq:(g + 1) * cq].astype(o_ref.dtype)
        return 0

    lax.fori_loop(0, nq, chunk, 0)


def _band_bias(rel_table, cq, win):
    rel = jnp.arange(cq)[:, None] + (win - cq) - jnp.arange(win)[None, :]
    bias = rel_table[jnp.clip(rel, -REL_CLIP, REL_CLIP) + REL_CLIP].astype(F32)
    return bias.transpose(2, 0, 1)


def _band_attention(q, kpad, vpad, bias, *, batch, nq, cq, win, chunked, lo_static):
    t, hd = q.shape
    rows = kpad.shape[1]
    qw = G_B * HEAD_DIM
    qspec = pl.BlockSpec((nq * cq, qw), lambda b, h: (b, h))
    kspec = pl.BlockSpec((None, rows, HEAD_DIM), lambda b, h: (b, 0, h))
    bspec = pl.BlockSpec((G_B, cq, win), lambda b, h: (h, 0, 0))
    return pl.pallas_call(
        functools.partial(_band_kernel, cq=cq, win=win, nq=nq, chunked=chunked, lo_static=lo_static),
        grid=(batch, N_KV_B),
        in_specs=[qspec, kspec, kspec, bspec],
        out_specs=qspec,
        out_shape=jax.ShapeDtypeStruct((t, hd), BF16),
        compiler_params=_params("parallel", "parallel"),
    )(q, kpad, vpad, bias)


def _split3_dot_nt(a, b):
    a_hi = a.astype(BF16)
    a_lo = (a - a_hi.astype(F32)).astype(BF16)
    b_hi = b.astype(BF16)
    b_lo = (b - b_hi.astype(F32)).astype(BF16)
    return (lax.dot_general(a_hi, b_hi, _NT, preferred_element_type=F32)
            + lax.dot_general(a_lo, b_hi, _NT, preferred_element_type=F32)
            + lax.dot_general(a_hi, b_lo, _NT, preferred_element_type=F32))


def _take_lanes(x, idx):
    return jnp.take_along_axis(x, idx, axis=1, mode="promise_in_bounds")


def _route_kernel(q_ref, sk_ref, e_ref, g_ref):
    tm = q_ref.shape[0]
    dh = sk_ref.shape[-1]
    nset = 2 * PEER_HEADS

    s = jnp.concatenate(
        [_split3_dot_nt(q_ref[:, n * dh:(n + 1) * dh], sk_ref[n // 2, n % 2]) for n in range(nset)], axis=0)
    lane1 = lax.broadcasted_iota(jnp.int32, s.shape, 1)

    def body1(i, carry):
        left, si = carry
        pos = jnp.argmax(left, axis=1, keepdims=True).astype(jnp.int32)
        return jnp.where(lane1 == pos, NEG_INF, left), jnp.where(lane1 == i, pos, si)

    _, si = lax.fori_loop(0, PEER_TOPK, body1, (s, jnp.zeros(s.shape, jnp.int32)))
    sv = _take_lanes(s, si)
    si = si.astype(F32)

    l128 = lax.broadcasted_iota(jnp.int32, (tm, PEER_KEYS), 1)
    outer_lo = l128 // PEER_TOPK
    outer_hi = outer_lo + PEER_KEYS // PEER_TOPK
    inner = l128 % PEER_TOPK
    c_lo, c_hi, x_lo, x_hi = [], [], [], []
    for h in range(PEER_HEADS):
        r0 = slice((2 * h) * tm, (2 * h + 1) * tm)
        r1 = slice((2 * h + 1) * tm, (2 * h + 2) * tm)
        v1 = _take_lanes(sv[r1], inner)
        i1 = _take_lanes(si[r1], inner)
        c_lo.append(_take_lanes(sv[r0], outer_lo) + v1)
        c_hi.append(_take_lanes(sv[r0], outer_hi) + v1)
        x_lo.append(_take_lanes(si[r0], outer_lo) * float(PEER_KEYS) + i1)
        x_hi.append(_take_lanes(si[r0], outer_hi) * float(PEER_KEYS) + i1)
    c_lo = jnp.concatenate(c_lo, axis=0)
    c_hi = jnp.concatenate(c_hi, axis=0)
    x_lo = jnp.concatenate(x_lo, axis=0)
    x_hi = jnp.concatenate(x_hi, axis=0)

    lane2 = lax.broadcasted_iota(jnp.int32, c_lo.shape, 1)
    head_lane0 = (lax.broadcasted_iota(jnp.int32, c_lo.shape, 0) // tm) * PEER_TOPK

    def body2(i, carry):
        l_lo, l_hi, fpos = carry
        hi_wins = l_hi > l_lo
        pos = jnp.argmax(jnp.where(hi_wins, l_hi, l_lo), axis=1, keepdims=True).astype(jnp.int32)
        hit = lane2 == pos
        from_hi = jnp.max(jnp.where(hit, jnp.where(hi_wins, 1.0, 0.0), 0.0), axis=1, keepdims=True)
        l_lo = jnp.where(hit, jnp.where(hi_wins, l_lo, NEG_INF), l_lo)
        l_hi = jnp.where(hit, jnp.where(hi_wins, NEG_INF, l_hi), l_hi)
        flat = pos + PEER_KEYS * from_hi.astype(jnp.int32)
        return l_lo, l_hi, jnp.where(lane2 == head_lane0 + i, flat, fpos)

    _, _, fpos = lax.fori_loop(0, PEER_TOPK, body2, (c_lo, c_hi, jnp.zeros(c_lo.shape, jnp.int32)))
    from_hi = fpos >= PEER_KEYS
    fpos = fpos % PEER_KEYS
    fv = jnp.where(from_hi, _take_lanes(c_hi, fpos), _take_lanes(c_lo, fpos))
    fe = jnp.where(from_hi, _take_lanes(x_hi, fpos), _take_lanes(x_lo, fpos))

    grp = lane2 // PEER_TOPK == head_lane0 // PEER_TOPK
    fe = jnp.where(grp, fe, 0.0)
    mx = jnp.max(jnp.where(grp, fv, NEG_INF), axis=1, keepdims=True)
    ex = jnp.where(grp, jnp.exp(fv - mx), 0.0)
    gate = ex / jnp.sum(ex, axis=1, keepdims=True)
    e_out = fe[0:tm]
    g_out = gate[0:tm]
    for h in range(1, PEER_HEADS):
        e_out = e_out + fe[h * tm:(h + 1) * tm]
        g_out = g_out + gate[h * tm:(h + 1) * tm]
    e_ref[...] = e_out.astype(jnp.int32)
    g_ref[...] = g_out


def _peer_route(qp, sub_keys):
    m, kq = qp.shape
    tm = min(256, m)
    ne = PEER_HEADS * PEER_TOPK
    ospec = pl.BlockSpec((tm, ne), lambda i: (i, 0))
    return pl.pallas_call(
        _route_kernel,
        grid=(m // tm,),
        in_specs=[pl.BlockSpec((tm, kq), lambda i: (i, 0)),
                  pl.BlockSpec(sub_keys.shape, lambda i: (0, 0, 0, 0))],
        out_specs=[ospec, ospec],
        out_shape=[jax.ShapeDtypeStruct((m, ne), jnp.int32), jax.ShapeDtypeStruct((m, ne), F32)],
        compiler_params=_params("parallel"),
    )(qp, sub_keys)


def _gelu_tanh(x):
    return 0.5 * x * (1.0 + jnp.tanh(0.7978845608028654 * (x + 0.044715 * (x * x * x))))


_TOKEN_GROUP = 16


def _expert_kernel(h_ref, gain_ref, e_ref, g_ref, tab_ref, o_ref,
                   xn_ref, hi_ref, lo_ref, act_ref, stage_ref, w3_ref, *, nj):
    j = pl.program_id(1)
    tm = h_ref.shape[0]
    te = tab_ref.shape[0]
    nb = te // LANES

    @pl.when(j == 0)
    def _():
        xn_ref[...] = _norm_rows(h_ref[...], gain_ref[...]).astype(BF16)
        o_ref[...] = h_ref[...]
        e = e_ref[...]
        hi_ref[...] = e // LANES
        lo_ref[...] = e % LANES
        act_ref[...] = jnp.zeros(act_ref.shape, F32)

    @pl.when(j < nj)
    def _():
        d = lax.dot_general(xn_ref[...], tab_ref[...], _NT, preferred_element_type=F32)
        hi = hi_ref[...]
        lo = lo_ref[...]
        act = act_ref[...]
        for s in range(nb):
            picked = _take_lanes(d[:, s * LANES:(s + 1) * LANES], lo)
            act = jnp.where(hi == j * nb + s, picked, act)
        act_ref[...] = act

    @pl.when(j == nj)
    def _():
        act_ref[...] = g_ref[...] * _gelu_tanh(act_ref[...])
        sub = lax.broadcasted_iota(jnp.int32, (LANES, LANES), 0)

        def group(gi, _):
            t0 = pl.multiple_of(gi * _TOKEN_GROUP, _TOKEN_GROUP)
            for u in range(_TOKEN_GROUP):
                hi_row = hi_ref[pl.ds(t0 + u, 1), :]
                lo_row = lo_ref[pl.ds(t0 + u, 1), :]
                w_row = act_ref[pl.ds(t0 + u, 1), :]
                a_t = jnp.where(hi_row == sub, w_row, 0.0).astype(BF16)
                b_t = jnp.where(lo_row == sub, 1.0, 0.0).astype(BF16)
                stage_ref[u * LANES:(u + 1) * LANES, :] = lax.dot_general(
                    a_t, b_t, _NT, preferred_element_type=F32)

            def move(i1, _):
                rows = stage_ref[pl.ds(i1, _TOKEN_GROUP, stride=LANES), :]
                w3_ref[i1, pl.ds(t0, _TOKEN_GROUP), :] = rows.astype(BF16)
                return 0

            lax.fori_loop(0, LANES, move, 0, unroll=8)
            return 0

        lax.fori_loop(0, tm // _TOKEN_GROUP, group, 0)

    @pl.when(j >= nj)
    def _():
        jj = j - nj
        w = jnp.concatenate([w3_ref[jj * nb + s] for s in range(nb)], axis=1)
        d = o_ref.shape[1]
        for c0 in range(0, d, _EXPERT_OUT_COLS):
            cols = slice(c0, c0 + _EXPERT_OUT_COLS)
            o_ref[:, cols] += jnp.dot(w, tab_ref[:, cols], preferred_element_type=F32)


_EXPERT_OUT_COLS = 1024


def _peer_experts(h, gain, eidx, gate, uv_tab, *, te=512):
    m, d = h.shape
    ne = uv_tab.shape[1]
    nsel = eidx.shape[1]
    tm = min(512, m)
    nj = ne // te
    once = pl.Buffered(1)
    sel_spec = pl.BlockSpec((tm, nsel), lambda i, j: (i, 0), pipeline_mode=once)
    return pl.pallas_call(
        functools.partial(_expert_kernel, nj=nj),
        grid=(m // tm, 2 * nj),
        in_specs=[pl.BlockSpec((tm, d), lambda i, j: (i, 0), pipeline_mode=once),
                  pl.BlockSpec((1, d), lambda i, j: (0, 0), pipeline_mode=once),
                  sel_spec, sel_spec,
                  pl.BlockSpec((None, te, d), lambda i, j: (j // nj, j % nj, 0))],
        out_specs=pl.BlockSpec((tm, d), lambda i, j: (i, 0), pipeline_mode=once),
        out_shape=jax.ShapeDtypeStruct((m, d), F32),
        scratch_shapes=[pltpu.VMEM((tm, d), BF16),
                        pltpu.VMEM((tm, nsel), jnp.int32),
                        pltpu.VMEM((tm, nsel), jnp.int32),
                        pltpu.VMEM((tm, nsel), F32),
                        pltpu.VMEM((_TOKEN_GROUP * LANES, LANES), F32),
                        pltpu.VMEM((ne // LANES, tm, LANES), BF16)],
        compiler_params=_params("parallel", "arbitrary"),
    )(h, gain.reshape(1, d), eidx, gate, uv_tab)


def _peer_ffn(h, i, prm):
    qp = _matmul(h, prm["w_q_peer"][i], gain=prm["g_ffn"][i])
    eidx, gate = _peer_route(qp, prm["peer_sub_keys"][i])
    return _peer_experts(h, prm["g_ffn"][i], eidx, gate, prm["peer_uv"][i])


def _ple(h, p, i, prm):
    return _matmul(h, prm["w_ple_gate"][i], gain=prm["g_ple"][i], residual=h,
                   ple=(p, prm["w_ple_proj"][i]))


def _trunk(x, pe, past, prm):
    b, t, d = x.shape
    m = b * t
    h = x.reshape(m, d)
    pe = pe.reshape(pe.shape[0], m, pe.shape[-1])

    g0 = prm["g_mix"][0]
    q = _matmul(h, prm["w_q_a"], gain=g0, out_dtype=BF16)
    k = _matmul(h, prm["w_k_a"], gain=g0)
    v = _matmul(h, prm["w_v_a"], gain=g0)
    if past is None:
        o = _sb_prompt(q, k, v, batch=b, seq=t)
    else:
        hd = k.shape[1]
        ck = past[0][0].reshape(b, -1, hd)
        cv = past[1][0].reshape(b, -1, hd)
        o = _sb_sample(q, k, v, ck, cv, batch=b, tq=t)
    h = _matmul(o, prm["w_o_a"], residual=h)
    h = _peer_ffn(h, 0, prm)
    h = _ple(h, pe[0], 0, prm)

    kv = _matmul(h, prm["w_kv_b"], gain=prm["g_kv"])
    nkv = N_KV_B * HEAD_DIM
    kb_new = kv[:, :nkv].reshape(b, t, nkv)
    vb_new = kv[:, nkv:].reshape(b, t, nkv)

    qb = _matmul(h, prm["w_q_b"], gain=prm["g_mix"][1], out_dtype=BF16)
    win = (N_PREV_CHUNKS + 2) * CHUNK
    if past is None:
        front = (N_PREV_CHUNKS + 1) * CHUNK
        kpad = jnp.pad(kb_new.astype(BF16), ((0, 0), (front, 0), (0, 0)))
        vpad = jnp.pad(vb_new.astype(BF16), ((0, 0), (front, 0), (0, 0)))
        bias = _band_bias(prm["rel_bias_b"][0], CHUNK, win)
        ob = _band_attention(qb, kpad, vpad, bias, batch=b, nq=t // CHUNK, cq=CHUNK, win=win,
                             chunked=True, lo_static=0)
    else:
        ck = past[2].reshape(b, -1, nkv)
        cv = past[3].reshape(b, -1, nkv)
        front = win - ck.shape[1] - t
        kpad = jnp.pad(jnp.concatenate([ck, kb_new], axis=1).astype(BF16), ((0, 0), (front, 0), (0, 0)))
        vpad = jnp.pad(jnp.concatenate([cv, vb_new], axis=1).astype(BF16), ((0, 0), (front, 0), (0, 0)))
        bias = _band_bias(prm["rel_bias_b"][0], t, win)
        ob = _band_attention(qb, kpad, vpad, bias, batch=b, nq=1, cq=t, win=win,
                             chunked=False, lo_static=front)
    h = _matmul(ob, prm["w_o_b"], residual=h)
    h = _peer_ffn(h, 1, prm)
    h = _ple(h, pe[1], 1, prm)

    y = _rmsnorm(h, prm["g_final"]).reshape(b, t, d)
    heads = k.shape[1] // HEAD_DIM
    a_k = k.reshape(1, b, t, heads, HEAD_DIM)
    a_v = v.reshape(1, b, t, heads, HEAD_DIM)
    if past is None:
        keep = min(N_PREV_CHUNKS * CHUNK, t)
        b_k, b_v = kb_new[:, t - keep:], vb_new[:, t - keep:]
    else:
        b_k, b_v = kb_new, vb_new
    b_k = b_k.reshape(b, -1, N_KV_B, HEAD_DIM)
    b_v = b_v.reshape(b, -1, N_KV_B, HEAD_DIM)
    return y, a_k, a_v, b_k, b_v


def kernel(x_prompt, x_sample, cache_a_k, cache_a_v, cache_b_k, cache_b_v, p_prompt, p_sample, g_mix, w_qkv_a, w_o_a, g_kv, w_kv_b, w_q_b, rel_bias_b, w_o_b, g_ffn, w_q_peer, peer_sub_keys, peer_u, peer_v, g_ple, w_ple_gate, w_ple_proj, g_final):
    d = x_prompt.shape[-1]
    prm = dict(
        g_mix=g_mix, g_kv=g_kv, g_ffn=g_ffn, g_ple=g_ple, g_final=g_final,
        w_q_a=w_qkv_a[0, :, :d].astype(BF16),
        w_k_a=w_qkv_a[0, :, d:2 * d].astype(BF16),
        w_v_a=w_qkv_a[0, :, 2 * d:].astype(BF16),
        w_o_a=w_o_a[0].astype(BF16),
        w_kv_b=w_kv_b.astype(BF16),
        w_q_b=w_q_b[0].astype(BF16),
        w_o_b=w_o_b[0].astype(BF16),
        rel_bias_b=rel_bias_b,
        w_q_peer=w_q_peer.astype(BF16),
        peer_sub_keys=peer_sub_keys,
        peer_uv=jnp.stack([peer_u.astype(BF16), peer_v.astype(BF16)], axis=1),
        w_ple_gate=w_ple_gate.astype(BF16),
        w_ple_proj=w_ple_proj.astype(BF16),
    )
    y_p, ak_p, av_p, bk_p, bv_p = _trunk(x_prompt, p_prompt, None, prm)
    y_s, ak_s, av_s, bk_s, bv_s = _trunk(x_sample, p_sample,
                                         (cache_a_k, cache_a_v, cache_b_k, cache_b_v), prm)
    return (y_p, y_s, ak_p, av_p, bk_p, bv_p, ak_s, av_s, bk_s, bv_s)
```

```python
import functools

import jax
import jax.numpy as jnp
from jax import lax
from jax.experimental import pallas as pl
from jax.experimental.pallas import tpu as pltpu

F32 = jnp.float32
BF16 = jnp.bfloat16

EPS = 1e-6
HEAD_DIM = 128
CHUNK = 64
N_PREV_CHUNKS = 8
N_KV_B = 8
G_B = 4
REL_CLIP = 128
PEER_HEADS = 8
PEER_KEYS = 128
PEER_TOPK = 16
SCALE = HEAD_DIM ** -0.5
NEG_INF = float("-inf")

VMEM_LIMIT_BYTES = 56 * 1024 * 1024
LANES = 128

_NT = (((1,), (1,)), ((), ()))


def _params(*sem):
    return pltpu.CompilerParams(dimension_semantics=sem, vmem_limit_bytes=VMEM_LIMIT_BYTES)


def _norm_rows(x, g):
    ms = jnp.mean(x * x, axis=-1, keepdims=True)
    return x * lax.rsqrt(ms + EPS) * g


def _sigmoid(x):
    return 1.0 / (1.0 + jnp.exp(-x))


def _mm_kernel(*refs, norm, res, ple):
    it = iter(refs)
    x_ref = next(it)
    g_ref = next(it) if norm else None
    w_ref = next(it)
    res_ref = next(it) if res else None
    p_ref = next(it) if ple else None
    wp_ref = next(it) if ple else None
    o_ref = next(it)
    xn_ref = next(it) if norm else None

    if norm:
        @pl.when(pl.program_id(1) == 0)
        def _():
            xn_ref[...] = _norm_rows(x_ref[...], g_ref[...]).astype(BF16)
        x = xn_ref[...]
    else:
        x = x_ref[...]
    acc = jnp.dot(x, w_ref[...], preferred_element_type=F32)
    if ple:
        proj = jnp.dot(p_ref[...].astype(BF16), wp_ref[...], preferred_element_type=F32)
        acc = _sigmoid(acc) * proj
    if res:
        acc = acc + res_ref[...]
    o_ref[...] = acc.astype(o_ref.dtype)


def _matmul(x, w, *, gain=None, residual=None, ple=None, out_dtype=F32, tn=512):
    m, k = x.shape
    n = w.shape[1]
    tm = min(512, m)
    tn = min(tn, n)
    norm = gain is not None
    in_specs = [pl.BlockSpec((tm, k), lambda i, j: (i, 0))]
    args = [x]
    if norm:
        in_specs.append(pl.BlockSpec((1, k), lambda i, j: (0, 0)))
        args.append(gain.reshape(1, k))
    in_specs.append(pl.BlockSpec((k, tn), lambda i, j: (0, j)))
    args.append(w)
    if residual is not None:
        in_specs.append(pl.BlockSpec((tm, tn), lambda i, j: (i, j)))
        args.append(residual)
    if ple is not None:
        p, wp = ple
        kp = p.shape[1]
        in_specs.append(pl.BlockSpec((tm, kp), lambda i, j: (i, 0)))
        in_specs.append(pl.BlockSpec((kp, tn), lambda i, j: (0, j)))
        args += [p, wp]
    return pl.pallas_call(
        functools.partial(_mm_kernel, norm=norm, res=residual is not None, ple=ple is not None),
        grid=(m // tm, n // tn),
        in_specs=in_specs,
        out_specs=pl.BlockSpec((tm, tn), lambda i, j: (i, j)),
        out_shape=jax.ShapeDtypeStruct((m, n), out_dtype),
        scratch_shapes=[pltpu.VMEM((tm, k), BF16)] if norm else [],
        compiler_params=_params("parallel", "arbitrary"),
    )(*args)


def _qkv_kernel(x_ref, g_ref, w_ref, q_ref, k_ref, v_ref, xn_ref, *, nq):
    j = pl.program_id(1)

    @pl.when(j == 0)
    def _():
        xn_ref[...] = _norm_rows(x_ref[...], g_ref[...]).astype(BF16)

    acc = jnp.dot(xn_ref[...], w_ref[...], preferred_element_type=F32)

    @pl.when(j < nq)
    def _():
        q_ref[...] = acc.astype(q_ref.dtype)

    @pl.when(jnp.logical_and(j >= nq, j < 2 * nq))
    def _():
        k_ref[...] = acc

    @pl.when(j >= 2 * nq)
    def _():
        v_ref[...] = acc


def _qkv_proj(x, w, gain, *, tn=512):
    m, k = x.shape
    n = w.shape[1] // 3
    tm = min(512, m)
    nq = n // tn

    def ospec(first):
        return pl.BlockSpec((tm, tn), lambda i, j: (i, jnp.clip(j - first, 0, nq - 1)))

    return pl.pallas_call(
        functools.partial(_qkv_kernel, nq=nq),
        grid=(m // tm, 3 * nq),
        in_specs=[pl.BlockSpec((tm, k), lambda i, j: (i, 0)),
                  pl.BlockSpec((1, k), lambda i, j: (0, 0)),
                  pl.BlockSpec((k, tn), lambda i, j: (0, j))],
        out_specs=[ospec(0), ospec(nq), ospec(2 * nq)],
        out_shape=[jax.ShapeDtypeStruct((m, n), BF16), jax.ShapeDtypeStruct((m, n), F32),
                   jax.ShapeDtypeStruct((m, n), F32)],
        scratch_shapes=[pltpu.VMEM((tm, k), BF16)],
        compiler_params=_params("parallel", "arbitrary"),
    )(x, gain.reshape(1, k), w)


def _cast_kernel(x_ref, o_ref):
    o_ref[...] = x_ref[...].astype(o_ref.dtype)


_CAST_ROWS = 256
_CAST_COLS = 4096


def _to_bf16(x):
    x2 = x.reshape(-1, x.shape[-1])
    rows, cols = x2.shape
    tr = min(_CAST_ROWS, rows)
    tc = min(_CAST_COLS, cols)
    spec = pl.BlockSpec((tr, tc), lambda i, j: (i, j))
    out = pl.pallas_call(
        _cast_kernel,
        grid=(rows // tr, cols // tc),
        in_specs=[spec],
        out_specs=spec,
        out_shape=jax.ShapeDtypeStruct((rows, cols), BF16),
        compiler_params=_params("parallel", "parallel"),
    )(x2)
    return out.reshape(x.shape)


def _stack_cast_kernel(u_ref, v_ref, o_ref):
    c = pl.program_id(1)

    @pl.when(c == 0)
    def _():
        o_ref[...] = u_ref[...].astype(o_ref.dtype)

    @pl.when(c == 1)
    def _():
        o_ref[...] = v_ref[...].astype(o_ref.dtype)


def _peer_tables_bf16(u, v):
    nl, ne, d = u.shape
    te = _CAST_ROWS
    nblk = ne // te
    uspec = pl.BlockSpec((None, te, d), lambda l, c, e: (l, jnp.where(c == 0, e, nblk - 1), 0))
    vspec = pl.BlockSpec((None, te, d), lambda l, c, e: (l, jnp.where(c == 1, e, 0), 0))
    return pl.pallas_call(
        _stack_cast_kernel,
        grid=(nl, 2, nblk),
        in_specs=[uspec, vspec],
        out_specs=pl.BlockSpec((None, None, te, d), lambda l, c, e: (l, c, e, 0)),
        out_shape=jax.ShapeDtypeStruct((nl, 2, ne, d), BF16),
        compiler_params=_params("parallel", "arbitrary", "arbitrary"),
    )(u, v)


def _rmsnorm_kernel(x_ref, g_ref, o_ref):
    o_ref[...] = _norm_rows(x_ref[...], g_ref[...])


def _rmsnorm(x, gain):
    m, k = x.shape
    tm = min(512, m)
    return pl.pallas_call(
        _rmsnorm_kernel,
        grid=(m // tm,),
        in_specs=[pl.BlockSpec((tm, k), lambda i: (i, 0)), pl.BlockSpec((1, k), lambda i: (0, 0))],
        out_specs=pl.BlockSpec((tm, k), lambda i: (i, 0)),
        out_shape=jax.ShapeDtypeStruct((m, k), F32),
        compiler_params=_params("parallel"),
    )(x, gain.reshape(1, k))


def _strict_lower_ones(n):
    r = lax.broadcasted_iota(jnp.int32, (n, n), 0)
    c = lax.broadcasted_iota(jnp.int32, (n, n), 1)
    return jnp.where(r > c, 1.0, 0.0).astype(BF16)


_SB_DEAD_BELOW = -104.0


def _sb_block(q, kb, vb, c, acc, tri, mask):
    z = lax.dot_general(q, kb, _NT, preferred_element_type=F32) * SCALE
    sp = jnp.log1p(jnp.exp(-jnp.abs(z)))
    log_beta = jnp.minimum(z, 0.0) - sp
    log_keep = -jnp.maximum(z, 0.0) - sp
    if mask is not None:
        log_keep = jnp.where(mask, log_keep, 0.0)
    hi = log_keep.astype(BF16)
    lo = (log_keep - hi.astype(F32)).astype(BF16)
    after = (jnp.dot(hi, tri, preferred_element_type=F32)
             + jnp.dot(lo, tri, preferred_element_type=F32))
    w = jnp.exp(log_beta + after + c)
    if mask is not None:
        w = jnp.where(mask, w, 0.0)
    acc = acc + jnp.dot(w.astype(BF16), vb, preferred_element_type=F32)
    c = c + jnp.sum(log_keep, axis=1, keepdims=True)
    return c, acc


def _sb_prompt_kernel(q_ref, k_ref, v_ref, o_ref, kb_ref, vb_ref, *, blk):
    seq = q_ref.shape[0]
    kb_ref[...] = k_ref[...].astype(BF16)
    vb_ref[...] = v_ref[...].astype(BF16)
    tri = _strict_lower_ones(blk)
    row = lax.broadcasted_iota(jnp.int32, (blk, blk), 0)
    col = lax.broadcasted_iota(jnp.int32, (blk, blk), 1)
    diag_mask = col < row

    def q_body(i, _):
        q0 = pl.multiple_of(i * blk, blk)
        q = q_ref[pl.ds(q0, blk), :]
        c = jnp.zeros((blk, 1), F32)
        acc = jnp.zeros((blk, HEAD_DIM), F32)
        c, acc = _sb_block(q, kb_ref[pl.ds(q0, blk), :], vb_ref[pl.ds(q0, blk), :], c, acc, tri, diag_mask)

        def k_cond(carry):
            n, cmax, _, _ = carry
            return jnp.logical_and(n < i, cmax > _SB_DEAD_BELOW)

        def k_body(carry):
            n, _, c, acc = carry
            k0 = pl.multiple_of((i - 1 - n) * blk, blk)
            c, acc = _sb_block(q, kb_ref[pl.ds(k0, blk), :], vb_ref[pl.ds(k0, blk), :], c, acc, tri, None)
            return n + 1, jnp.max(c), c, acc

        _, _, _, acc = lax.while_loop(k_cond, k_body, (0, jnp.max(c), c, acc))
        o_ref[pl.ds(q0, blk), :] = acc.astype(o_ref.dtype)
        return 0

    lax.fori_loop(0, seq // blk, q_body, 0)


def _sb_prompt(q, k, v, *, batch, seq, blk=256):
    t, hd = q.shape
    heads = hd // HEAD_DIM
    blk = min(blk, seq)
    spec = pl.BlockSpec((seq, HEAD_DIM), lambda b, h: (b, h))
    return pl.pallas_call(
        functools.partial(_sb_prompt_kernel, blk=blk),
        grid=(batch, heads),
        in_specs=[spec, spec, spec],
        out_specs=spec,
        out_shape=jax.ShapeDtypeStruct((t, hd), BF16),
        scratch_shapes=[pltpu.VMEM((seq, HEAD_DIM), BF16), pltpu.VMEM((seq, HEAD_DIM), BF16)],
        compiler_params=_params("parallel", "parallel"),
    )(q, k, v)


def _sb_sample_kernel(q_ref, kn_ref, vn_ref, kc_ref, vc_ref, o_ref, *, blk):
    tq = q_ref.shape[0]
    past = kc_ref.shape[0]
    q = q_ref[...]
    tri = _strict_lower_ones(blk)
    row = lax.broadcasted_iota(jnp.int32, (tq, blk), 0)
    col = lax.broadcasted_iota(jnp.int32, (tq, blk), 1)
    c = jnp.zeros((tq, 1), F32)
    acc = jnp.zeros((tq, HEAD_DIM), F32)
    c, acc = _sb_block(q, kn_ref[...], vn_ref[...], c, acc, tri, col < row)

    def k_cond(carry):
        n, cmax, _, _ = carry
        return jnp.logical_and(n < past // blk, cmax > _SB_DEAD_BELOW)

    def k_body(carry):
        n, _, c, acc = carry
        k0 = pl.multiple_of(past - (n + 1) * blk, blk)
        kb = kc_ref[pl.ds(k0, blk), :].astype(BF16)
        vb = vc_ref[pl.ds(k0, blk), :].astype(BF16)
        c, acc = _sb_block(q, kb, vb, c, acc, tri, None)
        return n + 1, jnp.max(c), c, acc

    _, _, _, acc = lax.while_loop(k_cond, k_body, (0, jnp.max(c), c, acc))
    o_ref[...] = acc.astype(o_ref.dtype)


def _sb_sample(q, k_new, v_new, cache_k, cache_v, *, batch, tq, blk=128):
    t, hd = q.shape
    heads = hd // HEAD_DIM
    past = cache_k.shape[1]

    def pad_block(a):
        a = a.reshape(batch, tq, hd)
        a = jnp.pad(a, ((0, 0), (0, blk - tq), (0, 0)))
        return a.reshape(batch * blk, hd).astype(BF16)

    qspec = pl.BlockSpec((tq, HEAD_DIM), lambda b, h: (b, h))
    nspec = pl.BlockSpec((blk, HEAD_DIM), lambda b, h: (b, h))
    cspec = pl.BlockSpec((None, past, HEAD_DIM), lambda b, h: (b, 0, h))
    return pl.pallas_call(
        functools.partial(_sb_sample_kernel, blk=blk),
        grid=(batch, heads),
        in_specs=[qspec, nspec, nspec, cspec, cspec],
        out_specs=qspec,
        out_shape=jax.ShapeDtypeStruct((t, hd), BF16),
        compiler_params=_params("parallel", "parallel"),
    )(q, pad_block(k_new), pad_block(v_new), cache_k, cache_v)


def _band_kernel(q_ref, k_ref, v_ref, bias_ref, o_ref, *, cq, win, nq, chunked, lo_static):
    bias = bias_ref[...].reshape(G_B * cq, win)
    col = lax.broadcasted_iota(jnp.int32, (G_B * cq, win), 1)

    def chunk(c, _):
        r0 = pl.multiple_of(c * cq, cq)
        if chunked:
            lo = jnp.maximum(CHUNK, (N_PREV_CHUNKS + 1 - c) * CHUNK)
        else:
            lo = lo_static
        kb = k_ref[pl.ds(r0, win), :]
        vb = v_ref[pl.ds(r0, win), :]
        qb = q_ref[pl.ds(r0, cq), :]
        q4 = jnp.concatenate([qb[:, g * HEAD_DIM:(g + 1) * HEAD_DIM] for g in range(G_B)], axis=0)
        s = lax.dot_general(q4, kb, _NT, preferred_element_type=F32) * SCALE + bias
        s = jnp.where(col >= lo, s, NEG_INF)
        m = jnp.max(s, axis=1, keepdims=True)
        e = jnp.exp(s - m)
        p = e / jnp.sum(e, axis=1, keepdims=True)
        o = jnp.dot(p.astype(BF16), vb, preferred_element_type=F32)
        for g in range(G_B):
            o_ref[pl.ds(r0, cq), g * HEAD_DIM:(g + 1) * HEAD_DIM] = o[g * cq:(g + 1) * cq].astype(o_ref.dtype)
        return 0

    lax.fori_loop(0, nq, chunk, 0)


def _band_bias(rel_table, cq, win):
    rel = jnp.arange(cq)[:, None] + (win - cq) - jnp.arange(win)[None, :]
    bias = rel_table[jnp.clip(rel, -REL_CLIP, REL_CLIP) + REL_CLIP].astype(F32)
    return bias.transpose(2, 0, 1)


def _band_attention(q, kpad, vpad, bias, *, batch, nq, cq, win, chunked, lo_static):
    t, hd = q.shape
    rows = kpad.shape[1]
    qw = G_B * HEAD_DIM
    qspec = pl.BlockSpec((nq * cq, qw), lambda b, h: (b, h))
    kspec = pl.BlockSpec((None, rows, HEAD_DIM), lambda b, h: (b, 0, h))
    bspec = pl.BlockSpec((G_B, cq, win), lambda b, h: (h, 0, 0))
    return pl.pallas_call(
        functools.partial(_band_kernel, cq=cq, win=win, nq=nq, chunked=chunked, lo_static=lo_static),
        grid=(batch, N_KV_B),
        in_specs=[qspec, kspec, kspec, bspec],
        out_specs=qspec,
        out_shape=jax.ShapeDtypeStruct((t, hd), BF16),
        compiler_params=_params("parallel", "parallel"),
    )(q, kpad, vpad, bias)


def _split3_dot_nt(a, b):
    a_hi = a.astype(BF16)
    a_lo = (a - a_hi.astype(F32)).astype(BF16)
    b_hi = b.astype(BF16)
    b_lo = (b - b_hi.astype(F32)).astype(BF16)
    return (lax.dot_general(a_hi, b_hi, _NT, preferred_element_type=F32)
            + lax.dot_general(a_lo, b_hi, _NT, preferred_element_type=F32)
            + lax.dot_general(a_hi, b_lo, _NT, preferred_element_type=F32))


def _take_lanes(x, idx):
    return jnp.take_along_axis(x, idx, axis=1, mode="promise_in_bounds")


_ROUTE_ROWS = 64
_ROUTE_UNROLL = 16


def _route_kernel(q_ref, sk_ref, pairs_ref, e_ref, g_ref, s_ref, left_ref, si_ref, cleft_ref, fpos_ref):
    tm = q_ref.shape[0]
    dh = sk_ref.shape[-1]
    nset = 2 * PEER_HEADS
    lane_c = lax.broadcasted_iota(jnp.int32, (_ROUTE_ROWS, PEER_KEYS), 1)

    for n in range(nset):
        sc = _split3_dot_nt(q_ref[:, n * dh:(n + 1) * dh], sk_ref[n // 2, n % 2])
        s_ref[n * tm:(n + 1) * tm, :] = sc
        left_ref[n * tm:(n + 1) * tm, :] = sc
    si_ref[...] = jnp.zeros(si_ref.shape, jnp.int32)

    def body1(i, _):
        def piece(r, _):
            rows = pl.ds(pl.multiple_of(r * _ROUTE_ROWS, _ROUTE_ROWS), _ROUTE_ROWS)
            left = left_ref[rows, :]
            pos = jnp.argmax(left, axis=1, keepdims=True).astype(jnp.int32)
            left_ref[rows, :] = jnp.where(lane_c == pos, NEG_INF, left)
            si_ref[rows, :] = jnp.where(lane_c == i, pos, si_ref[rows, :])
            return 0

        return lax.fori_loop(0, nset * tm // _ROUTE_ROWS, piece, 0, unroll=_ROUTE_UNROLL)

    lax.fori_loop(0, PEER_TOPK, body1, 0)
    si = si_ref[...]
    sv = _take_lanes(s_ref[...], si)
    si = si.astype(F32)

    pair_a = jnp.broadcast_to(pairs_ref[0:1, :], (tm, PEER_KEYS))
    pair_b = jnp.broadcast_to(pairs_ref[1:2, :], (tm, PEER_KEYS))
    pair_ok = jnp.broadcast_to(pairs_ref[2:3, :], (tm, PEER_KEYS)) > 0
    cand, cidx = [], []
    for h in range(PEER_HEADS):
        r0 = slice((2 * h) * tm, (2 * h + 1) * tm)
        r1 = slice((2 * h + 1) * tm, (2 * h + 2) * tm)
        cand.append(jnp.where(pair_ok, _take_lanes(sv[r0], pair_a) + _take_lanes(sv[r1], pair_b), NEG_INF))
        cidx.append(_take_lanes(si[r0], pair_a) * float(PEER_KEYS) + _take_lanes(si[r1], pair_b))
    cand = jnp.concatenate(cand, axis=0)
    cidx = jnp.concatenate(cidx, axis=0)

    lane2 = lax.broadcasted_iota(jnp.int32, cand.shape, 1)
    head_lane0 = (lax.broadcasted_iota(jnp.int32, cand.shape, 0) // tm) * PEER_TOPK
    cleft_ref[...] = cand
    fpos_ref[...] = jnp.zeros(fpos_ref.shape, jnp.int32)

    def body2(i, _):
        def piece(r, _):
            row0 = pl.multiple_of(r * _ROUTE_ROWS, _ROUTE_ROWS)
            rows = pl.ds(row0, _ROUTE_ROWS)
            left = cleft_ref[rows, :]
            pos = jnp.argmax(left, axis=1, keepdims=True).astype(jnp.int32)
            cleft_ref[rows, :] = jnp.where(lane_c == pos, NEG_INF, left)
            out_lane = (row0 // tm) * PEER_TOPK + i
            fpos_ref[rows, :] = jnp.where(lane_c == out_lane, pos, fpos_ref[rows, :])
            return 0

        return lax.fori_loop(0, PEER_HEADS * tm // _ROUTE_ROWS, piece, 0, unroll=_ROUTE_UNROLL)

    lax.fori_loop(0, PEER_TOPK, body2, 0)
    fpos = fpos_ref[...]
    fv = _take_lanes(cand, fpos)
    fe = _take_lanes(cidx, fpos)

    grp = lane2 // PEER_TOPK == head_lane0 // PEER_TOPK
    fe = jnp.where(grp, fe, 0.0)
    mx = jnp.max(jnp.where(grp, fv, NEG_INF), axis=1, keepdims=True)
    ex = jnp.where(grp, jnp.exp(fv - mx), 0.0)
    gate = ex / jnp.sum(ex, axis=1, keepdims=True)
    e_out = fe[0:tm]
    g_out = gate[0:tm]
    for h in range(1, PEER_HEADS):
        e_out = e_out + fe[h * tm:(h + 1) * tm]
        g_out = g_out + gate[h * tm:(h + 1) * tm]
    e_ref[...] = e_out.astype(jnp.int32)
    g_ref[...] = g_out


def _peer_route(qp, sub_keys):
    m, kq = qp.shape
    tm = min(256, m)
    ne = PEER_HEADS * PEER_TOPK
    ospec = pl.BlockSpec((tm, ne), lambda i: (i, 0))
    pairs = [(a, b) for a in range(PEER_TOPK) for b in range(PEER_TOPK) if (a + 1) * (b + 1) <= PEER_TOPK]
    pad = [0] * (PEER_KEYS - len(pairs))
    pair_rows = jnp.array([[a for a, _ in pairs] + pad, [b for _, b in pairs] + pad,
                           [1] * len(pairs) + pad] + [[0] * PEER_KEYS] * 5, jnp.int32)
    return pl.pallas_call(
        _route_kernel,
        grid=(m // tm,),
        in_specs=[pl.BlockSpec((tm, kq), lambda i: (i, 0)),
                  pl.BlockSpec(sub_keys.shape, lambda i: (0, 0, 0, 0)),
                  pl.BlockSpec(pair_rows.shape, lambda i: (0, 0))],
        out_specs=[ospec, ospec],
        out_shape=[jax.ShapeDtypeStruct((m, ne), jnp.int32), jax.ShapeDtypeStruct((m, ne), F32)],
        scratch_shapes=[pltpu.VMEM((2 * PEER_HEADS * tm, PEER_KEYS), F32),
                        pltpu.VMEM((2 * PEER_HEADS * tm, PEER_KEYS), F32),
                        pltpu.VMEM((2 * PEER_HEADS * tm, PEER_KEYS), jnp.int32),
                        pltpu.VMEM((PEER_HEADS * tm, PEER_KEYS), F32),
                        pltpu.VMEM((PEER_HEADS * tm, PEER_KEYS), jnp.int32)],
        compiler_params=_params("parallel"),
    )(qp, sub_keys, pair_rows)


def _gelu_tanh(x):
    return 0.5 * x * (1.0 + jnp.tanh(0.7978845608028654 * (x + 0.044715 * (x * x * x))))


_TOKEN_GROUP = 16


def _expert_kernel(h_ref, gain_ref, e_ref, g_ref, tab_ref, o_ref,
                   xn_ref, hi_ref, lo_ref, act_ref, d_ref, w3_ref, *, nj):
    j = pl.program_id(1)
    tm = h_ref.shape[0]
    te = tab_ref.shape[0]
    nb = te // LANES

    @pl.when(j == 0)
    def _():
        xn_ref[...] = _norm_rows(h_ref[...], gain_ref[...]).astype(BF16)
        o_ref[...] = h_ref[...]
        e = e_ref[...]
        hi_ref[...] = e // LANES
        lo_ref[...] = e % LANES
        act_ref[...] = jnp.zeros(act_ref.shape, F32)

    def score_slab(slot):
        d_ref[slot] = lax.dot_general(xn_ref[...], tab_ref[...], _NT, preferred_element_type=F32)

    def pick_from_slab(slot, slab):
        hi = hi_ref[...]
        lo = lo_ref[...]
        act = act_ref[...]
        for s in range(nb):
            picked = _take_lanes(d_ref[slot, :, s * LANES:(s + 1) * LANES], lo)
            act = jnp.where(hi == slab * nb + s, picked, act)
        act_ref[...] = act

    @pl.when(j == 0)
    def _():
        score_slab(0)

    @pl.when(jnp.logical_and(j >= 1, j < nj))
    def _():
        pick_from_slab((j - 1) % 2, j - 1)
        score_slab(j % 2)

    @pl.when(j == nj)
    def _():
        pick_from_slab((nj - 1) % 2, nj - 1)
        act_ref[...] = g_ref[...] * _gelu_tanh(act_ref[...])
        sub = lax.broadcasted_iota(jnp.int32, (LANES, LANES), 0)

        def group(gi, _):
            t0 = pl.multiple_of(gi * _TOKEN_GROUP, _TOKEN_GROUP)
            grids = []
            for u in range(_TOKEN_GROUP):
                hi_row = hi_ref[pl.ds(t0 + u, 1), :]
                lo_row = lo_ref[pl.ds(t0 + u, 1), :]
                w_row = act_ref[pl.ds(t0 + u, 1), :]
                a_t = jnp.where(hi_row == sub, w_row, 0.0).astype(BF16)
                b_t = jnp.where(lo_row == sub, 1.0, 0.0).astype(BF16)
                grids.append(lax.dot_general(a_t, b_t, _NT, preferred_element_type=F32))
            w3_ref[:, pl.ds(t0, _TOKEN_GROUP), :] = pltpu.einshape(
                "uik->iuk", jnp.stack(grids, axis=0)).astype(BF16)
            return 0

        lax.fori_loop(0, tm // _TOKEN_GROUP, group, 0)

    @pl.when(j >= nj)
    def _():
        jj = j - nj
        w = jnp.concatenate([w3_ref[jj * nb + s] for s in range(nb)], axis=1)
        d = o_ref.shape[1]
        for c0 in range(0, d, _EXPERT_OUT_COLS):
            cols = slice(c0, c0 + _EXPERT_OUT_COLS)
            o_ref[:, cols] += jnp.dot(w, tab_ref[:, cols], preferred_element_type=F32)


_EXPERT_OUT_COLS = 1024


def _peer_experts(h, gain, eidx, gate, uv_tab, layer, *, te=512):
    m, d = h.shape
    ne = uv_tab.shape[2]
    nsel = eidx.shape[1]
    tm = min(512, m)
    nj = ne // te
    once = pl.Buffered(1)
    sel_spec = pl.BlockSpec((tm, nsel), lambda i, j: (i, 0), pipeline_mode=once)
    return pl.pallas_call(
        functools.partial(_expert_kernel, nj=nj),
        grid=(m // tm, 2 * nj),
        in_specs=[pl.BlockSpec((tm, d), lambda i, j: (i, 0), pipeline_mode=once),
                  pl.BlockSpec((1, d), lambda i, j: (0, 0), pipeline_mode=once),
                  sel_spec, sel_spec,
                  pl.BlockSpec((None, None, te, d), lambda i, j: (layer, j // nj, j % nj, 0))],
        out_specs=pl.BlockSpec((tm, d), lambda i, j: (i, 0), pipeline_mode=once),
        out_shape=jax.ShapeDtypeStruct((m, d), F32),
        scratch_shapes=[pltpu.VMEM((tm, d), BF16),
                        pltpu.VMEM((tm, nsel), jnp.int32),
                        pltpu.VMEM((tm, nsel), jnp.int32),
                        pltpu.VMEM((tm, nsel), F32),
                        pltpu.VMEM((2, tm, te), F32),
                        pltpu.VMEM((ne // LANES, tm, LANES), BF16)],
        compiler_params=_params("parallel", "arbitrary"),
    )(h, gain.reshape(1, d), eidx, gate, uv_tab)


def _peer_ffn(h, i, prm):
    qp = _matmul(h, prm["w_q_peer"][i], gain=prm["g_ffn"][i])
    eidx, gate = _peer_route(qp, prm["peer_sub_keys"][i])
    return _peer_experts(h, prm["g_ffn"][i], eidx, gate, prm["peer_uv"], i)


def _ple(h, p, i, prm):
    return _matmul(h, prm["w_ple_gate"][i], gain=prm["g_ple"][i], residual=h,
                   ple=(p, prm["w_ple_proj"][i]))


def _trunk(x, pe, past, prm):
    b, t, d = x.shape
    m = b * t
    h = x.reshape(m, d)
    pe = pe.reshape(pe.shape[0], m, pe.shape[-1])

    q, k, v = _qkv_proj(h, prm["w_qkv_a"], prm["g_mix"][0])
    if past is None:
        o = _sb_prompt(q, k, v, batch=b, seq=t)
    else:
        hd = k.shape[1]
        ck = past[0][0].reshape(b, -1, hd)
        cv = past[1][0].reshape(b, -1, hd)
        o = _sb_sample(q, k, v, ck, cv, batch=b, tq=t)
    h = _matmul(o, prm["w_o_a"], residual=h)
    h = _peer_ffn(h, 0, prm)
    h = _ple(h, pe[0], 0, prm)

    kv = _matmul(h, prm["w_kv_b"], gain=prm["g_kv"])
    nkv = N_KV_B * HEAD_DIM
    kb_new = kv[:, :nkv].reshape(b, t, nkv)
    vb_new = kv[:, nkv:].reshape(b, t, nkv)

    qb = _matmul(h, prm["w_q_b"], gain=prm["g_mix"][1], out_dtype=BF16)
    win = (N_PREV_CHUNKS + 2) * CHUNK
    if past is None:
        front = (N_PREV_CHUNKS + 1) * CHUNK
        kpad = jnp.pad(kb_new.astype(BF16), ((0, 0), (front, 0), (0, 0)))
        vpad = jnp.pad(vb_new.astype(BF16), ((0, 0), (front, 0), (0, 0)))
        bias = _band_bias(prm["rel_bias_b"][0], CHUNK, win)
        ob = _band_attention(qb, kpad, vpad, bias, batch=b, nq=t // CHUNK, cq=CHUNK, win=win,
                             chunked=True, lo_static=0)
    else:
        ck = past[2].reshape(b, -1, nkv)
        cv = past[3].reshape(b, -1, nkv)
        front = win - ck.shape[1] - t
        kpad = jnp.pad(jnp.concatenate([ck, kb_new], axis=1).astype(BF16), ((0, 0), (front, 0), (0, 0)))
        vpad = jnp.pad(jnp.concatenate([cv, vb_new], axis=1).astype(BF16), ((0, 0), (front, 0), (0, 0)))
        bias = _band_bias(prm["rel_bias_b"][0], t, win)
        ob = _band_attention(qb, kpad, vpad, bias, batch=b, nq=1, cq=t, win=win,
                             chunked=False, lo_static=front)
    h = _matmul(ob, prm["w_o_b"], residual=h)
    h = _peer_ffn(h, 1, prm)
    h = _ple(h, pe[1], 1, prm)

    y = _rmsnorm(h, prm["g_final"]).reshape(b, t, d)
    heads = k.shape[1] // HEAD_DIM
    a_k = k.reshape(1, b, t, heads, HEAD_DIM)
    a_v = v.reshape(1, b, t, heads, HEAD_DIM)
    if past is None:
        keep = min(N_PREV_CHUNKS * CHUNK, t)
        b_k, b_v = kb_new[:, t - keep:], vb_new[:, t - keep:]
    else:
        b_k, b_v = kb_new, vb_new
    b_k = b_k.reshape(b, -1, N_KV_B, HEAD_DIM)
    b_v = b_v.reshape(b, -1, N_KV_B, HEAD_DIM)
    return y, a_k, a_v, b_k, b_v


def kernel(x_prompt, x_sample, cache_a_k, cache_a_v, cache_b_k, cache_b_v, p_prompt, p_sample, g_mix, w_qkv_a, w_o_a, g_kv, w_kv_b, w_q_b, rel_bias_b, w_o_b, g_ffn, w_q_peer, peer_sub_keys, peer_u, peer_v, g_ple, w_ple_gate, w_ple_proj, g_final):
    prm = dict(
        g_mix=g_mix, g_kv=g_kv, g_ffn=g_ffn, g_ple=g_ple, g_final=g_final,
        w_qkv_a=_to_bf16(w_qkv_a)[0],
        w_o_a=_to_bf16(w_o_a)[0],
        w_kv_b=_to_bf16(w_kv_b),
        w_q_b=_to_bf16(w_q_b)[0],
        w_o_b=_to_bf16(w_o_b)[0],
        rel_bias_b=rel_bias_b,
        w_q_peer=_to_bf16(w_q_peer),
        peer_sub_keys=peer_sub_keys,
        peer_uv=_peer_tables_bf16(peer_u, peer_v),
        w_ple_gate=_to_bf16(w_ple_gate),
        w_ple_proj=_to_bf16(w_ple_proj),
    )
    y_p, ak_p, av_p, bk_p, bv_p = _trunk(x_prompt, p_prompt, None, prm)
    y_s, ak_s, av_s, bk_s, bv_s = _trunk(x_sample, p_sample,
                                         (cache_a_k, cache_a_v, cache_b_k, cache_b_v), prm)
    return (y_p, y_s, ak_p, av_p, bk_p, bv_p, ak_s, av_s, bk_s, bv_s)
```

```python
import functools

import jax
import jax.numpy as jnp
from jax import lax
from jax.experimental import pallas as pl
from jax.experimental.pallas import tpu as pltpu

F32 = jnp.float32
BF16 = jnp.bfloat16

EPS = 1e-6
HEAD_DIM = 128
CHUNK = 64
N_PREV_CHUNKS = 8
N_KV_B = 8
G_B = 4
REL_CLIP = 128
PEER_HEADS = 8
PEER_KEYS = 128
PEER_TOPK = 16
SCALE = HEAD_DIM ** -0.5
NEG_INF = float("-inf")

VMEM_LIMIT_BYTES = 56 * 1024 * 1024
LANES = 128

_NT = (((1,), (1,)), ((), ()))


def _params(*sem):
    return pltpu.CompilerParams(dimension_semantics=sem, vmem_limit_bytes=VMEM_LIMIT_BYTES)


def _norm_rows(x, g):
    ms = jnp.mean(x * x, axis=-1, keepdims=True)
    return x * lax.rsqrt(ms + EPS) * g


def _sigmoid(x):
    return 1.0 / (1.0 + jnp.exp(-x))


def _mm_kernel(*refs, norm, res, ple):
    it = iter(refs)
    x_ref = next(it)
    g_ref = next(it) if norm else None
    w_ref = next(it)
    res_ref = next(it) if res else None
    p_ref = next(it) if ple else None
    wp_ref = next(it) if ple else None
    o_ref = next(it)
    xn_ref = next(it) if norm else None

    if norm:
        @pl.when(pl.program_id(1) == 0)
        def _():
            xn_ref[...] = _norm_rows(x_ref[...], g_ref[...]).astype(BF16)
        x = xn_ref[...]
    else:
        x = x_ref[...]
    acc = jnp.dot(x, w_ref[...], preferred_element_type=F32)
    if ple:
        proj = jnp.dot(p_ref[...].astype(BF16), wp_ref[...], preferred_element_type=F32)
        acc = _sigmoid(acc) * proj
    if res:
        acc = acc + res_ref[...]
    o_ref[...] = acc.astype(o_ref.dtype)


def _matmul(x, w, *, gain=None, residual=None, ple=None, out_dtype=F32, tn=512):
    m, k = x.shape
    n = w.shape[1]
    tm = min(512, m)
    tn = min(tn, n)
    norm = gain is not None
    in_specs = [pl.BlockSpec((tm, k), lambda i, j: (i, 0))]
    args = [x]
    if norm:
        in_specs.append(pl.BlockSpec((1, k), lambda i, j: (0, 0)))
        args.append(gain.reshape(1, k))
    in_specs.append(pl.BlockSpec((k, tn), lambda i, j: (0, j)))
    args.append(w)
    if residual is not None:
        in_specs.append(pl.BlockSpec((tm, tn), lambda i, j: (i, j)))
        args.append(residual)
    if ple is not None:
        p, wp = ple
        kp = p.shape[1]
        in_specs.append(pl.BlockSpec((tm, kp), lambda i, j: (i, 0)))
        in_specs.append(pl.BlockSpec((kp, tn), lambda i, j: (0, j)))
        args += [p, wp]
    return pl.pallas_call(
        functools.partial(_mm_kernel, norm=norm, res=residual is not None, ple=ple is not None),
        grid=(m // tm, n // tn),
        in_specs=in_specs,
        out_specs=pl.BlockSpec((tm, tn), lambda i, j: (i, j)),
        out_shape=jax.ShapeDtypeStruct((m, n), out_dtype),
        scratch_shapes=[pltpu.VMEM((tm, k), BF16)] if norm else [],
        compiler_params=_params("parallel", "arbitrary"),
    )(*args)


def _qkv_kernel(x_ref, g_ref, w_ref, q_ref, k_ref, v_ref, xn_ref, *, nq):
    j = pl.program_id(1)

    @pl.when(j == 0)
    def _():
        xn_ref[...] = _norm_rows(x_ref[...], g_ref[...]).astype(BF16)

    acc = jnp.dot(xn_ref[...], w_ref[...], preferred_element_type=F32)

    @pl.when(j < nq)
    def _():
        q_ref[...] = acc.astype(q_ref.dtype)

    @pl.when(jnp.logical_and(j >= nq, j < 2 * nq))
    def _():
        k_ref[...] = acc

    @pl.when(j >= 2 * nq)
    def _():
        v_ref[...] = acc


def _qkv_proj(x, w, gain, *, tn=512):
    m, k = x.shape
    n = w.shape[1] // 3
    tm = min(512, m)
    nq = n // tn

    def ospec(first):
        return pl.BlockSpec((tm, tn), lambda i, j: (i, jnp.clip(j - first, 0, nq - 1)))

    return pl.pallas_call(
        functools.partial(_qkv_kernel, nq=nq),
        grid=(m // tm, 3 * nq),
        in_specs=[pl.BlockSpec((tm, k), lambda i, j: (i, 0)),
                  pl.BlockSpec((1, k), lambda i, j: (0, 0)),
                  pl.BlockSpec((k, tn), lambda i, j: (0, j))],
        out_specs=[ospec(0), ospec(nq), ospec(2 * nq)],
        out_shape=[jax.ShapeDtypeStruct((m, n), BF16), jax.ShapeDtypeStruct((m, n), F32),
                   jax.ShapeDtypeStruct((m, n), F32)],
        scratch_shapes=[pltpu.VMEM((tm, k), BF16)],
        compiler_params=_params("parallel", "arbitrary"),
    )(x, gain.reshape(1, k), w)


def _cast_kernel(x_ref, o_ref):
    o_ref[...] = x_ref[...].astype(o_ref.dtype)


_CAST_ROWS = 256
_CAST_COLS = 4096


def _to_bf16(x):
    x2 = x.reshape(-1, x.shape[-1])
    rows, cols = x2.shape
    tr = min(_CAST_ROWS, rows)
    tc = min(_CAST_COLS, cols)
    spec = pl.BlockSpec((tr, tc), lambda i, j: (i, j))
    out = pl.pallas_call(
        _cast_kernel,
        grid=(rows // tr, cols // tc),
        in_specs=[spec],
        out_specs=spec,
        out_shape=jax.ShapeDtypeStruct((rows, cols), BF16),
        compiler_params=_params("parallel", "parallel"),
    )(x2)
    return out.reshape(x.shape)


def _stack_cast_kernel(u_ref, v_ref, o_ref):
    c = pl.program_id(1)

    @pl.when(c == 0)
    def _():
        o_ref[...] = u_ref[...].astype(o_ref.dtype)

    @pl.when(c == 1)
    def _():
        o_ref[...] = v_ref[...].astype(o_ref.dtype)


def _peer_tables_bf16(u, v):
    nl, ne, d = u.shape
    te = _CAST_ROWS
    nblk = ne // te
    uspec = pl.BlockSpec((None, te, d), lambda l, c, e: (l, jnp.where(c == 0, e, nblk - 1), 0))
    vspec = pl.BlockSpec((None, te, d), lambda l, c, e: (l, jnp.where(c == 1, e, 0), 0))
    return pl.pallas_call(
        _stack_cast_kernel,
        grid=(nl, 2, nblk),
        in_specs=[uspec, vspec],
        out_specs=pl.BlockSpec((None, None, te, d), lambda l, c, e: (l, c, e, 0)),
        out_shape=jax.ShapeDtypeStruct((nl, 2, ne, d), BF16),
        compiler_params=_params("parallel", "arbitrary", "arbitrary"),
    )(u, v)


def _rmsnorm_kernel(x_ref, g_ref, o_ref):
    o_ref[...] = _norm_rows(x_ref[...], g_ref[...])


def _rmsnorm(x, gain):
    m, k = x.shape
    tm = min(512, m)
    return pl.pallas_call(
        _rmsnorm_kernel,
        grid=(m // tm,),
        in_specs=[pl.BlockSpec((tm, k), lambda i: (i, 0)), pl.BlockSpec((1, k), lambda i: (0, 0))],
        out_specs=pl.BlockSpec((tm, k), lambda i: (i, 0)),
        out_shape=jax.ShapeDtypeStruct((m, k), F32),
        compiler_params=_params("parallel"),
    )(x, gain.reshape(1, k))


def _strict_lower_ones(n):
    r = lax.broadcasted_iota(jnp.int32, (n, n), 0)
    c = lax.broadcasted_iota(jnp.int32, (n, n), 1)
    return jnp.where(r > c, 1.0, 0.0).astype(BF16)


_SB_DEAD_BELOW = -104.0


def _sb_block(q, kb, vb, c, acc, tri, mask):
    z = lax.dot_general(q, kb, _NT, preferred_element_type=F32) * SCALE
    sp = jnp.log1p(jnp.exp(-jnp.abs(z)))
    log_beta = jnp.minimum(z, 0.0) - sp
    log_keep = -jnp.maximum(z, 0.0) - sp
    if mask is not None:
        log_keep = jnp.where(mask, log_keep, 0.0)
    hi = log_keep.astype(BF16)
    lo = (log_keep - hi.astype(F32)).astype(BF16)
    after = (jnp.dot(hi, tri, preferred_element_type=F32)
             + jnp.dot(lo, tri, preferred_element_type=F32))
    w = jnp.exp(log_beta + after + c)
    if mask is not None:
        w = jnp.where(mask, w, 0.0)
    acc = acc + jnp.dot(w.astype(BF16), vb, preferred_element_type=F32)
    c = c + jnp.sum(log_keep, axis=1, keepdims=True)
    return c, acc


def _sb_prompt_kernel(q_ref, k_ref, v_ref, o_ref, kb_ref, vb_ref, *, blk):
    seq = q_ref.shape[0]
    kb_ref[pl.ds(0, blk), :] = jnp.zeros((blk, HEAD_DIM), BF16)
    vb_ref[pl.ds(0, blk), :] = jnp.zeros((blk, HEAD_DIM), BF16)
    kb_ref[pl.ds(blk, seq), :] = k_ref[...].astype(BF16)
    vb_ref[pl.ds(blk, seq), :] = v_ref[...].astype(BF16)
    tri = _strict_lower_ones(blk)
    row = lax.broadcasted_iota(jnp.int32, (blk, blk), 0)
    col = lax.broadcasted_iota(jnp.int32, (blk, blk), 1)
    diag_mask = col < row

    def kv_block(j):
        r0 = pl.multiple_of((j + 1) * blk, blk)
        return kb_ref[pl.ds(r0, blk), :], vb_ref[pl.ds(r0, blk), :]

    def older_blocks(i, q, c, acc):
        def k_cond(carry):
            j, cmax, _, _ = carry
            return jnp.logical_and(j >= 0, cmax > _SB_DEAD_BELOW)

        def k_body(carry):
            j, _, c, acc = carry
            c, acc = _sb_block(q, *kv_block(j), c, acc, tri, None)
            return j - 1, jnp.max(c), c, acc

        return lax.while_loop(k_cond, k_body, (i - 2, jnp.max(c), c, acc))[3]

    def pair_body(p, _):
        state = []
        for i in (2 * p, 2 * p + 1):
            q = q_ref[pl.ds(pl.multiple_of(i * blk, blk), blk), :]
            c = jnp.zeros((blk, 1), F32)
            acc = jnp.zeros((blk, HEAD_DIM), F32)
            c, acc = _sb_block(q, *kv_block(i), c, acc, tri, diag_mask)
            c, acc = _sb_block(q, *kv_block(i - 1), c, acc, tri, None)
            state.append((i, q, c, acc))
        for i, q, c, acc in state:
            acc = older_blocks(i, q, c, acc)
            o_ref[pl.ds(pl.multiple_of(i * blk, blk), blk), :] = acc.astype(o_ref.dtype)
        return 0

    lax.fori_loop(0, seq // (2 * blk), pair_body, 0)


def _sb_prompt(q, k, v, *, batch, seq, blk=256):
    t, hd = q.shape
    heads = hd // HEAD_DIM
    blk = min(blk, seq)
    spec = pl.BlockSpec((seq, HEAD_DIM), lambda b, h: (b, h))
    return pl.pallas_call(
        functools.partial(_sb_prompt_kernel, blk=blk),
        grid=(batch, heads),
        in_specs=[spec, spec, spec],
        out_specs=spec,
        out_shape=jax.ShapeDtypeStruct((t, hd), BF16),
        scratch_shapes=[pltpu.VMEM((seq + blk, HEAD_DIM), BF16), pltpu.VMEM((seq + blk, HEAD_DIM), BF16)],
        compiler_params=_params("parallel", "parallel"),
    )(q, k, v)


_SB_HEAD_GROUP = 8


def _sb_sample_kernel(q_ref, kn_ref, vn_ref, kc_ref, vc_ref, o_ref, c_ref, acc_ref, *, blk):
    n = pl.program_id(2)
    tq = q_ref.shape[0]
    tri = _strict_lower_ones(blk)

    @pl.when(n == 0)
    def _():
        row = lax.broadcasted_iota(jnp.int32, (tq, blk), 0)
        col = lax.broadcasted_iota(jnp.int32, (tq, blk), 1)
        for g in range(_SB_HEAD_GROUP):
            lanes = slice(g * HEAD_DIM, (g + 1) * HEAD_DIM)
            c, acc = _sb_block(q_ref[:, lanes], kn_ref[:, lanes], vn_ref[:, lanes],
                               jnp.zeros((tq, 1), F32), jnp.zeros((tq, HEAD_DIM), F32), tri, col < row)
            c_ref[g] = c
            acc_ref[g] = acc

    @pl.when(jnp.max(c_ref[...]) > _SB_DEAD_BELOW)
    def _():
        kb = pltpu.einshape("thd->htd", kc_ref[...]).astype(BF16)
        vb = pltpu.einshape("thd->htd", vc_ref[...]).astype(BF16)
        for g in range(_SB_HEAD_GROUP):
            lanes = slice(g * HEAD_DIM, (g + 1) * HEAD_DIM)
            c, acc = _sb_block(q_ref[:, lanes], kb[g], vb[g], c_ref[g], acc_ref[g], tri, None)
            c_ref[g] = c
            acc_ref[g] = acc

    @pl.when(n == pl.num_programs(2) - 1)
    def _():
        for g in range(_SB_HEAD_GROUP):
            o_ref[:, g * HEAD_DIM:(g + 1) * HEAD_DIM] = acc_ref[g].astype(o_ref.dtype)


def _sb_sample(q, k_new, v_new, cache_k, cache_v, *, batch, tq, blk=128):
    t, hd = q.shape
    heads = hd // HEAD_DIM
    past = cache_k.shape[1]
    nblk = past // blk
    gw = _SB_HEAD_GROUP * HEAD_DIM

    def pad_block(a):
        a = a.reshape(batch, tq, hd)
        a = jnp.pad(a, ((0, 0), (0, blk - tq), (0, 0)))
        return a.reshape(batch * blk, hd).astype(BF16)

    qspec = pl.BlockSpec((tq, gw), lambda b, h, n: (b, h))
    nspec = pl.BlockSpec((blk, gw), lambda b, h, n: (b, h))
    cspec = pl.BlockSpec((None, blk, _SB_HEAD_GROUP, HEAD_DIM), lambda b, h, n: (b, nblk - 1 - n, h, 0))
    return pl.pallas_call(
        functools.partial(_sb_sample_kernel, blk=blk),
        grid=(batch, heads // _SB_HEAD_GROUP, nblk),
        in_specs=[qspec, nspec, nspec, cspec, cspec],
        out_specs=qspec,
        out_shape=jax.ShapeDtypeStruct((t, hd), BF16),
        scratch_shapes=[pltpu.VMEM((_SB_HEAD_GROUP, tq, 1), F32),
                        pltpu.VMEM((_SB_HEAD_GROUP, tq, HEAD_DIM), F32)],
        compiler_params=_params("parallel", "parallel", "arbitrary"),
    )(q, pad_block(k_new), pad_block(v_new), cache_k, cache_v)


def _band_kernel(q_ref, k_ref, v_ref, bias_ref, o_ref, *, cq, win, nq, chunked, lo_static):
    bias = bias_ref[...].reshape(G_B * cq, win)
    col = lax.broadcasted_iota(jnp.int32, (G_B * cq, win), 1)

    def chunk(c, _):
        r0 = pl.multiple_of(c * cq, cq)
        if chunked:
            lo = jnp.maximum(CHUNK, (N_PREV_CHUNKS + 1 - c) * CHUNK)
        else:
            lo = lo_static
        kb = k_ref[pl.ds(r0, win), :]
        vb = v_ref[pl.ds(r0, win), :]
        qb = q_ref[pl.ds(r0, cq), :]
        q4 = jnp.concatenate([qb[:, g * HEAD_DIM:(g + 1) * HEAD_DIM] for g in range(G_B)], axis=0)
        s = lax.dot_general(q4, kb, _NT, preferred_element_type=F32) * SCALE + bias
        s = jnp.where(col >= lo, s, NEG_INF)
        m = jnp.max(s, axis=1, keepdims=True)
        e = jnp.exp(s - m)
        p = e / jnp.sum(e, axis=1, keepdims=True)
        o = jnp.dot(p.astype(BF16), vb, preferred_element_type=F32)
        for g in range(G_B):
            o_ref[pl.ds(r0, cq), g * HEAD_DIM:(g + 1) * HEAD_DIM] = o[g * cq:(g + 1) * cq].astype(o_ref.dtype)
        return 0

    lax.fori_loop(0, nq, chunk, 0, unroll=2 if nq % 2 == 0 else 1)


def _band_bias(rel_table, cq, win):
    rel = jnp.arange(cq)[:, None] + (win - cq) - jnp.arange(win)[None, :]
    bias = rel_table[jnp.clip(rel, -REL_CLIP, REL_CLIP) + REL_CLIP].astype(F32)
    return bias.transpose(2, 0, 1)


def _band_attention(q, kpad, vpad, bias, *, batch, nq, cq, win, chunked, lo_static):
    t, hd = q.shape
    rows = kpad.shape[1]
    qw = G_B * HEAD_DIM
    qspec = pl.BlockSpec((nq * cq, qw), lambda b, h: (b, h))
    kspec = pl.BlockSpec((None, rows, HEAD_DIM), lambda b, h: (b, 0, h))
    bspec = pl.BlockSpec((G_B, cq, win), lambda b, h: (h, 0, 0))
    return pl.pallas_call(
        functools.partial(_band_kernel, cq=cq, win=win, nq=nq, chunked=chunked, lo_static=lo_static),
        grid=(batch, N_KV_B),
        in_specs=[qspec, kspec, kspec, bspec],
        out_specs=qspec,
        out_shape=jax.ShapeDtypeStruct((t, hd), BF16),
        compiler_params=_params("parallel", "parallel"),
    )(q, kpad, vpad, bias)


def _split3_dot_nt(a, b):
    a_hi = a.astype(BF16)
    a_lo = (a - a_hi.astype(F32)).astype(BF16)
    b_hi = b.astype(BF16)
    b_lo = (b - b_hi.astype(F32)).astype(BF16)
    return (lax.dot_general(a_hi, b_hi, _NT, preferred_element_type=F32)
            + lax.dot_general(a_lo, b_hi, _NT, preferred_element_type=F32)
            + lax.dot_general(a_hi, b_lo, _NT, preferred_element_type=F32))


def _take_lanes(x, idx):
    return jnp.take_along_axis(x, idx, axis=1, mode="promise_in_bounds")


_ROUTE_ROWS = 64
_ROUTE_UNROLL = 16


def _route_kernel(q_ref, sk_ref, pairs_ref, e_ref, g_ref, s_ref, left_ref, si_ref, cleft_ref, fpos_ref):
    tm = q_ref.shape[0]
    dh = sk_ref.shape[-1]
    nset = 2 * PEER_HEADS
    lane_c = lax.broadcasted_iota(jnp.int32, (_ROUTE_ROWS, PEER_KEYS), 1)

    for n in range(nset):
        sc = _split3_dot_nt(q_ref[:, n * dh:(n + 1) * dh], sk_ref[n // 2, n % 2])
        s_ref[n * tm:(n + 1) * tm, :] = sc
        left_ref[n * tm:(n + 1) * tm, :] = sc
    si_ref[...] = jnp.zeros(si_ref.shape, jnp.int32)

    def body1(i, _):
        def piece(r, _):
            rows = pl.ds(pl.multiple_of(r * _ROUTE_ROWS, _ROUTE_ROWS), _ROUTE_ROWS)
            left = left_ref[rows, :]
            pos = jnp.argmax(left, axis=1, keepdims=True).astype(jnp.int32)
            left_ref[rows, :] = jnp.where(lane_c == pos, NEG_INF, left)
            si_ref[rows, :] = jnp.where(lane_c == i, pos, si_ref[rows, :])
            return 0

        return lax.fori_loop(0, nset * tm // _ROUTE_ROWS, piece, 0, unroll=_ROUTE_UNROLL)

    lax.fori_loop(0, PEER_TOPK, body1, 0)
    si = si_ref[...]
    sv = _take_lanes(s_ref[...], si)
    si = si.astype(F32)

    pair_a = jnp.broadcast_to(pairs_ref[0:1, :], (tm, PEER_KEYS))
    pair_b = jnp.broadcast_to(pairs_ref[1:2, :], (tm, PEER_KEYS))
    pair_ok = jnp.broadcast_to(pairs_ref[2:3, :], (tm, PEER_KEYS)) > 0
    cand, cidx = [], []
    for h in range(PEER_HEADS):
        r0 = slice((2 * h) * tm, (2 * h + 1) * tm)
        r1 = slice((2 * h + 1) * tm, (2 * h + 2) * tm)
        cand.append(jnp.where(pair_ok, _take_lanes(sv[r0], pair_a) + _take_lanes(sv[r1], pair_b), NEG_INF))
        cidx.append(_take_lanes(si[r0], pair_a) * float(PEER_KEYS) + _take_lanes(si[r1], pair_b))
    cand = jnp.concatenate(cand, axis=0)
    cidx = jnp.concatenate(cidx, axis=0)

    lane2 = lax.broadcasted_iota(jnp.int32, cand.shape, 1)
    head_lane0 = (lax.broadcasted_iota(jnp.int32, cand.shape, 0) // tm) * PEER_TOPK
    cleft_ref[...] = cand
    fpos_ref[...] = jnp.zeros(fpos_ref.shape, jnp.int32)

    def body2(i, _):
        def piece(r, _):
            row0 = pl.multiple_of(r * _ROUTE_ROWS, _ROUTE_ROWS)
            rows = pl.ds(row0, _ROUTE_ROWS)
            left = cleft_ref[rows, :]
            pos = jnp.argmax(left, axis=1, keepdims=True).astype(jnp.int32)
            cleft_ref[rows, :] = jnp.where(lane_c == pos, NEG_INF, left)
            out_lane = (row0 // tm) * PEER_TOPK + i
            fpos_ref[rows, :] = jnp.where(lane_c == out_lane, pos, fpos_ref[rows, :])
            return 0

        return lax.fori_loop(0, PEER_HEADS * tm // _ROUTE_ROWS, piece, 0, unroll=_ROUTE_UNROLL)

    lax.fori_loop(0, PEER_TOPK, body2, 0)
    fpos = fpos_ref[...]
    fv = _take_lanes(cand, fpos)
    fe = _take_lanes(cidx, fpos)

    grp = lane2 // PEER_TOPK == head_lane0 // PEER_TOPK
    fe = jnp.where(grp, fe, 0.0)
    mx = jnp.max(jnp.where(grp, fv, NEG_INF), axis=1, keepdims=True)
    ex = jnp.where(grp, jnp.exp(fv - mx), 0.0)
    gate = ex / jnp.sum(ex, axis=1, keepdims=True)
    e_out = fe[0:tm]
    g_out = gate[0:tm]
    for h in range(1, PEER_HEADS):
        e_out = e_out + fe[h * tm:(h + 1) * tm]
        g_out = g_out + gate[h * tm:(h + 1) * tm]
    e_ref[...] = e_out.astype(jnp.int32)
    g_ref[...] = g_out


def _peer_route(qp, sub_keys):
    m, kq = qp.shape
    tm = min(256, m)
    ne = PEER_HEADS * PEER_TOPK
    ospec = pl.BlockSpec((tm, ne), lambda i: (i, 0))
    pairs = [(a, b) for a in range(PEER_TOPK) for b in range(PEER_TOPK) if (a + 1) * (b + 1) <= PEER_TOPK]
    pad = [0] * (PEER_KEYS - len(pairs))
    pair_rows = jnp.array([[a for a, _ in pairs] + pad, [b for _, b in pairs] + pad,
                           [1] * len(pairs) + pad] + [[0] * PEER_KEYS] * 5, jnp.int32)
    return pl.pallas_call(
        _route_kernel,
        grid=(m // tm,),
        in_specs=[pl.BlockSpec((tm, kq), lambda i: (i, 0)),
                  pl.BlockSpec(sub_keys.shape, lambda i: (0, 0, 0, 0)),
                  pl.BlockSpec(pair_rows.shape, lambda i: (0, 0))],
        out_specs=[ospec, ospec],
        out_shape=[jax.ShapeDtypeStruct((m, ne), jnp.int32), jax.ShapeDtypeStruct((m, ne), F32)],
        scratch_shapes=[pltpu.VMEM((2 * PEER_HEADS * tm, PEER_KEYS), F32),
                        pltpu.VMEM((2 * PEER_HEADS * tm, PEER_KEYS), F32),
                        pltpu.VMEM((2 * PEER_HEADS * tm, PEER_KEYS), jnp.int32),
                        pltpu.VMEM((PEER_HEADS * tm, PEER_KEYS), F32),
                        pltpu.VMEM((PEER_HEADS * tm, PEER_KEYS), jnp.int32)],
        compiler_params=_params("parallel"),
    )(qp, sub_keys, pair_rows)


def _gelu_tanh(x):
    return 0.5 * x * (1.0 + jnp.tanh(0.7978845608028654 * (x + 0.044715 * (x * x * x))))


_TOKEN_GROUP = 16


def _expert_kernel(h_ref, gain_ref, e_ref, g_ref, tab_ref, o_ref,
                   xn_ref, hi_ref, lo_ref, act_ref, d_ref, w3_ref, *, nj):
    j = pl.program_id(1)
    tm = h_ref.shape[0]
    te = tab_ref.shape[0]
    nb = te // LANES

    @pl.when(j == 0)
    def _():
        xn_ref[...] = _norm_rows(h_ref[...], gain_ref[...]).astype(BF16)
        o_ref[...] = h_ref[...]
        e = e_ref[...]
        hi_ref[...] = e // LANES
        lo_ref[...] = e % LANES
        act_ref[...] = jnp.zeros(act_ref.shape, F32)

    def score_slab(slot):
        d_ref[slot] = lax.dot_general(xn_ref[...], tab_ref[...], _NT, preferred_element_type=F32)

    def pick_from_slab(slot, slab):
        hi = hi_ref[...]
        lo = lo_ref[...]
        act = act_ref[...]
        for s in range(nb):
            picked = _take_lanes(d_ref[slot, :, s * LANES:(s + 1) * LANES], lo)
            act = jnp.where(hi == slab * nb + s, picked, act)
        act_ref[...] = act

    @pl.when(j == 0)
    def _():
        score_slab(0)

    @pl.when(jnp.logical_and(j >= 1, j < nj))
    def _():
        pick_from_slab((j - 1) % 2, j - 1)
        score_slab(j % 2)

    @pl.when(j == nj)
    def _():
        pick_from_slab((nj - 1) % 2, nj - 1)
        act_ref[...] = g_ref[...] * _gelu_tanh(act_ref[...])
        sub = lax.broadcasted_iota(jnp.int32, (LANES, LANES), 0)

        def group(gi, _):
            t0 = pl.multiple_of(gi * _TOKEN_GROUP, _TOKEN_GROUP)
            grids = []
            for u in range(_TOKEN_GROUP):
                hi_row = hi_ref[pl.ds(t0 + u, 1), :]
                lo_row = lo_ref[pl.ds(t0 + u, 1), :]
                w_row = act_ref[pl.ds(t0 + u, 1), :]
                a_t = jnp.where(hi_row == sub, w_row, 0.0).astype(BF16)
                b_t = jnp.where(lo_row == sub, 1.0, 0.0).astype(BF16)
                grids.append(lax.dot_general(a_t, b_t, _NT, preferred_element_type=F32))
            w3_ref[:, pl.ds(t0, _TOKEN_GROUP), :] = pltpu.einshape(
                "uik->iuk", jnp.stack(grids, axis=0)).astype(BF16)
            return 0

        lax.fori_loop(0, tm // _TOKEN_GROUP, group, 0)

    @pl.when(j >= nj)
    def _():
        jj = j - nj
        w = jnp.concatenate([w3_ref[jj * nb + s] for s in range(nb)], axis=1)
        d = o_ref.shape[1]
        for c0 in range(0, d, _EXPERT_OUT_COLS):
            cols = slice(c0, c0 + _EXPERT_OUT_COLS)
            o_ref[:, cols] += jnp.dot(w, tab_ref[:, cols], preferred_element_type=F32)


_EXPERT_OUT_COLS = 1024


def _peer_experts(h, gain, eidx, gate, uv_tab, layer, *, te=512):
    m, d = h.shape
    ne = uv_tab.shape[2]
    nsel = eidx.shape[1]
    tm = min(512, m)
    nj = ne // te
    once = pl.Buffered(1)
    sel_spec = pl.BlockSpec((tm, nsel), lambda i, j: (i, 0), pipeline_mode=once)
    return pl.pallas_call(
        functools.partial(_expert_kernel, nj=nj),
        grid=(m // tm, 2 * nj),
        in_specs=[pl.BlockSpec((tm, d), lambda i, j: (i, 0), pipeline_mode=once),
                  pl.BlockSpec((1, d), lambda i, j: (0, 0), pipeline_mode=once),
                  sel_spec, sel_spec,
                  pl.BlockSpec((None, None, te, d), lambda i, j: (layer, j // nj, j % nj, 0))],
        out_specs=pl.BlockSpec((tm, d), lambda i, j: (i, 0), pipeline_mode=once),
        out_shape=jax.ShapeDtypeStruct((m, d), F32),
        scratch_shapes=[pltpu.VMEM((tm, d), BF16),
                        pltpu.VMEM((tm, nsel), jnp.int32),
                        pltpu.VMEM((tm, nsel), jnp.int32),
                        pltpu.VMEM((tm, nsel), F32),
                        pltpu.VMEM((2, tm, te), F32),
                        pltpu.VMEM((ne // LANES, tm, LANES), BF16)],
        compiler_params=_params("parallel", "arbitrary"),
    )(h, gain.reshape(1, d), eidx, gate, uv_tab)


def _peer_ffn(h, i, prm):
    qp = _matmul(h, prm["w_q_peer"][i], gain=prm["g_ffn"][i])
    eidx, gate = _peer_route(qp, prm["peer_sub_keys"][i])
    return _peer_experts(h, prm["g_ffn"][i], eidx, gate, prm["peer_uv"], i)


def _ple(h, p, i, prm):
    return _matmul(h, prm["w_ple_gate"][i], gain=prm["g_ple"][i], residual=h,
                   ple=(p, prm["w_ple_proj"][i]))


def _trunk(x, pe, past, prm):
    b, t, d = x.shape
    m = b * t
    h = x.reshape(m, d)
    pe = pe.reshape(pe.shape[0], m, pe.shape[-1])

    q, k, v = _qkv_proj(h, prm["w_qkv_a"], prm["g_mix"][0])
    if past is None:
        o = _sb_prompt(q, k, v, batch=b, seq=t)
    else:
        o = _sb_sample(q, k, v, past[0][0], past[1][0], batch=b, tq=t)
    h = _matmul(o, prm["w_o_a"], residual=h)
    h = _peer_ffn(h, 0, prm)
    h = _ple(h, pe[0], 0, prm)

    kv = _matmul(h, prm["w_kv_b"], gain=prm["g_kv"])
    nkv = N_KV_B * HEAD_DIM
    kb_new = kv[:, :nkv].reshape(b, t, nkv)
    vb_new = kv[:, nkv:].reshape(b, t, nkv)

    qb = _matmul(h, prm["w_q_b"], gain=prm["g_mix"][1], out_dtype=BF16)
    win = (N_PREV_CHUNKS + 2) * CHUNK
    if past is None:
        front = (N_PREV_CHUNKS + 1) * CHUNK
        kpad = jnp.pad(kb_new.astype(BF16), ((0, 0), (front, 0), (0, 0)))
        vpad = jnp.pad(vb_new.astype(BF16), ((0, 0), (front, 0), (0, 0)))
        bias = _band_bias(prm["rel_bias_b"][0], CHUNK, win)
        ob = _band_attention(qb, kpad, vpad, bias, batch=b, nq=t // CHUNK, cq=CHUNK, win=win,
                             chunked=True, lo_static=0)
    else:
        ck = past[2].reshape(b, -1, nkv)
        cv = past[3].reshape(b, -1, nkv)
        front = win - ck.shape[1] - t
        kpad = jnp.pad(jnp.concatenate([ck, kb_new], axis=1).astype(BF16), ((0, 0), (front, 0), (0, 0)))
        vpad = jnp.pad(jnp.concatenate([cv, vb_new], axis=1).astype(BF16), ((0, 0), (front, 0), (0, 0)))
        bias = _band_bias(prm["rel_bias_b"][0], t, win)
        ob = _band_attention(qb, kpad, vpad, bias, batch=b, nq=1, cq=t, win=win,
                             chunked=False, lo_static=front)
    h = _matmul(ob, prm["w_o_b"], residual=h)
    h = _peer_ffn(h, 1, prm)
    h = _ple(h, pe[1], 1, prm)

    y = _rmsnorm(h, prm["g_final"]).reshape(b, t, d)
    heads = k.shape[1] // HEAD_DIM
    a_k = k.reshape(1, b, t, heads, HEAD_DIM)
    a_v = v.reshape(1, b, t, heads, HEAD_DIM)
    if past is None:
        keep = min(N_PREV_CHUNKS * CHUNK, t)
        b_k, b_v = kb_new[:, t - keep:], vb_new[:, t - keep:]
    else:
        b_k, b_v = kb_new, vb_new
    b_k = b_k.reshape(b, -1, N_KV_B, HEAD_DIM)
    b_v = b_v.reshape(b, -1, N_KV_B, HEAD_DIM)
    return y, a_k, a_v, b_k, b_v


def kernel(x_prompt, x_sample, cache_a_k, cache_a_v, cache_b_k, cache_b_v, p_prompt, p_sample, g_mix, w_qkv_a, w_o_a, g_kv, w_kv_b, w_q_b, rel_bias_b, w_o_b, g_ffn, w_q_peer, peer_sub_keys, peer_u, peer_v, g_ple, w_ple_gate, w_ple_proj, g_final):
    prm = dict(
        g_mix=g_mix, g_kv=g_kv, g_ffn=g_ffn, g_ple=g_ple, g_final=g_final,
        w_qkv_a=_to_bf16(w_qkv_a)[0],
        w_o_a=_to_bf16(w_o_a)[0],
        w_kv_b=_to_bf16(w_kv_b),
        w_q_b=_to_bf16(w_q_b)[0],
        w_o_b=_to_bf16(w_o_b)[0],
        rel_bias_b=rel_bias_b,
        w_q_peer=_to_bf16(w_q_peer),
        peer_sub_keys=peer_sub_keys,
        peer_uv=_peer_tables_bf16(peer_u, peer_v),
        w_ple_gate=_to_bf16(w_ple_gate),
        w_ple_proj=_to_bf16(w_ple_proj),
    )
    y_p, ak_p, av_p, bk_p, bv_p = _trunk(x_prompt, p_prompt, None, prm)
    y_s, ak_s, av_s, bk_s, bv_s = _trunk(x_sample, p_sample,
                                         (cache_a_k, cache_a_v, cache_b_k, cache_b_v), prm)
    return (y_p, y_s, ak_p, av_p, bk_p, bv_p, ak_s, av_s, bk_s, bv_s)
```

```python
import functools

import jax
import jax.numpy as jnp
from jax import lax
from jax.experimental import pallas as pl
from jax.experimental.pallas import tpu as pltpu

F32 = jnp.float32
BF16 = jnp.bfloat16

EPS = 1e-6
HEAD_DIM = 128
CHUNK = 64
N_PREV_CHUNKS = 8
N_KV_B = 8
G_B = 4
REL_CLIP = 128
PEER_HEADS = 8
PEER_KEYS = 128
PEER_TOPK = 16
SCALE = HEAD_DIM ** -0.5
NEG_INF = float("-inf")

VMEM_LIMIT_BYTES = 60 * 1024 * 1024
LANES = 128

_NT = (((1,), (1,)), ((), ()))


def _params(*sem):
    return pltpu.CompilerParams(dimension_semantics=sem, vmem_limit_bytes=VMEM_LIMIT_BYTES)


def _norm_rows(x, g):
    ms = jnp.mean(x * x, axis=-1, keepdims=True)
    return x * lax.rsqrt(ms + EPS) * g


def _sigmoid(x):
    return 1.0 / (1.0 + jnp.exp(-x))


def _mm_kernel(*refs, norm, res, ple, out_scale):
    it = iter(refs)
    x_ref = next(it)
    g_ref = next(it) if norm else None
    w_ref = next(it)
    res_ref = next(it) if res else None
    p_ref = next(it) if ple else None
    wp_ref = next(it) if ple else None
    o_ref = next(it)
    xn_ref = next(it) if norm else None

    if norm:
        @pl.when(pl.program_id(1) == 0)
        def _():
            xn_ref[...] = _norm_rows(x_ref[...], g_ref[...]).astype(BF16)
        x = xn_ref[...]
    else:
        x = x_ref[...]
    acc = jnp.dot(x, w_ref[...], preferred_element_type=F32)
    if ple:
        proj = jnp.dot(p_ref[...].astype(BF16), wp_ref[...], preferred_element_type=F32)
        acc = _sigmoid(acc) * proj
    if res:
        acc = acc + res_ref[...]
    if out_scale is not None:
        acc = acc * out_scale
    o_ref[...] = acc.astype(o_ref.dtype)


def _matmul(x, w, *, gain=None, residual=None, ple=None, out_dtype=F32, out_scale=None, tn=1024):
    m, k = x.shape
    n = w.shape[1]
    tm = min(512, m)
    tn = min(tn, n)
    norm = gain is not None
    in_specs = [pl.BlockSpec((tm, k), lambda i, j: (i, 0))]
    args = [x]
    if norm:
        in_specs.append(pl.BlockSpec((1, k), lambda i, j: (0, 0)))
        args.append(gain.reshape(1, k))
    in_specs.append(pl.BlockSpec((k, tn), lambda i, j: (0, j)))
    args.append(w)
    if residual is not None:
        in_specs.append(pl.BlockSpec((tm, tn), lambda i, j: (i, j)))
        args.append(residual)
    if ple is not None:
        p, wp = ple
        kp = p.shape[1]
        in_specs.append(pl.BlockSpec((tm, kp), lambda i, j: (i, 0)))
        in_specs.append(pl.BlockSpec((kp, tn), lambda i, j: (0, j)))
        args += [p, wp]
    return pl.pallas_call(
        functools.partial(_mm_kernel, norm=norm, res=residual is not None, ple=ple is not None,
                          out_scale=out_scale),
        grid=(m // tm, n // tn),
        in_specs=in_specs,
        out_specs=pl.BlockSpec((tm, tn), lambda i, j: (i, j)),
        out_shape=jax.ShapeDtypeStruct((m, n), out_dtype),
        scratch_shapes=[pltpu.VMEM((tm, k), BF16)] if norm else [],
        compiler_params=_params("parallel", "arbitrary"),
    )(*args)


def _qkv_kernel(x_ref, g_ref, w_ref, q_ref, k_ref, v_ref, xn_ref, *, nq):
    j = pl.program_id(1)

    @pl.when(j == 0)
    def _():
        xn_ref[...] = _norm_rows(x_ref[...], g_ref[...]).astype(BF16)

    acc = jnp.dot(xn_ref[...], w_ref[...], preferred_element_type=F32)

    @pl.when(j < nq)
    def _():
        q_ref[...] = (acc * SCALE).astype(q_ref.dtype)

    @pl.when(jnp.logical_and(j >= nq, j < 2 * nq))
    def _():
        k_ref[...] = acc

    @pl.when(j >= 2 * nq)
    def _():
        v_ref[...] = acc


def _qkv_proj(x, w, gain, *, tn=1024):
    m, k = x.shape
    n = w.shape[1] // 3
    tm = min(512, m)
    nq = n // tn

    def ospec(first):
        return pl.BlockSpec((tm, tn), lambda i, j: (i, jnp.clip(j - first, 0, nq - 1)))

    return pl.pallas_call(
        functools.partial(_qkv_kernel, nq=nq),
        grid=(m // tm, 3 * nq),
        in_specs=[pl.BlockSpec((tm, k), lambda i, j: (i, 0)),
                  pl.BlockSpec((1, k), lambda i, j: (0, 0)),
                  pl.BlockSpec((k, tn), lambda i, j: (0, j))],
        out_specs=[ospec(0), ospec(nq), ospec(2 * nq)],
        out_shape=[jax.ShapeDtypeStruct((m, n), BF16), jax.ShapeDtypeStruct((m, n), F32),
                   jax.ShapeDtypeStruct((m, n), F32)],
        scratch_shapes=[pltpu.VMEM((tm, k), BF16)],
        compiler_params=_params("parallel", "arbitrary"),
    )(x, gain.reshape(1, k), w)


def _cast_kernel(x_ref, o_ref):
    o_ref[...] = x_ref[...].astype(o_ref.dtype)


_CAST_ROWS = 256
_CAST_COLS = 4096


def _to_bf16(x):
    x2 = x.reshape(-1, x.shape[-1])
    rows, cols = x2.shape
    tr = min(_CAST_ROWS, rows)
    tc = min(_CAST_COLS, cols)
    spec = pl.BlockSpec((tr, tc), lambda i, j: (i, j))
    out = pl.pallas_call(
        _cast_kernel,
        grid=(rows // tr, cols // tc),
        in_specs=[spec],
        out_specs=spec,
        out_shape=jax.ShapeDtypeStruct((rows, cols), BF16),
        compiler_params=_params("parallel", "parallel"),
    )(x2)
    return out.reshape(x.shape)


def _stack_cast_kernel(u_ref, v_ref, o_ref):
    c = pl.program_id(1)

    @pl.when(c == 0)
    def _():
        o_ref[...] = u_ref[...].astype(o_ref.dtype)

    @pl.when(c == 1)
    def _():
        o_ref[...] = v_ref[...].astype(o_ref.dtype)


def _peer_tables_bf16(u, v):
    nl, ne, d = u.shape
    te = _CAST_ROWS
    nblk = ne // te
    uspec = pl.BlockSpec((None, te, d), lambda l, c, e: (l, jnp.where(c == 0, e, nblk - 1), 0))
    vspec = pl.BlockSpec((None, te, d), lambda l, c, e: (l, jnp.where(c == 1, e, 0), 0))
    return pl.pallas_call(
        _stack_cast_kernel,
        grid=(nl, 2, nblk),
        in_specs=[uspec, vspec],
        out_specs=pl.BlockSpec((None, None, te, d), lambda l, c, e: (l, c, e, 0)),
        out_shape=jax.ShapeDtypeStruct((nl, 2, ne, d), BF16),
        compiler_params=_params("parallel", "arbitrary", "arbitrary"),
    )(u, v)


def _rmsnorm_kernel(x_ref, g_ref, o_ref):
    o_ref[...] = _norm_rows(x_ref[...], g_ref[...])


def _rmsnorm(x, gain):
    m, k = x.shape
    tm = min(512, m)
    return pl.pallas_call(
        _rmsnorm_kernel,
        grid=(m // tm,),
        in_specs=[pl.BlockSpec((tm, k), lambda i: (i, 0)), pl.BlockSpec((1, k), lambda i: (0, 0))],
        out_specs=pl.BlockSpec((tm, k), lambda i: (i, 0)),
        out_shape=jax.ShapeDtypeStruct((m, k), F32),
        compiler_params=_params("parallel"),
    )(x, gain.reshape(1, k))


def _strict_lower_ones(n):
    r = lax.broadcasted_iota(jnp.int32, (2 * n, n), 0) % n
    c = lax.broadcasted_iota(jnp.int32, (2 * n, n), 1)
    return jnp.where(r > c, 1.0, 0.0).astype(BF16)


_SB_DEAD_BELOW = -104.0


def _sb_block(q, kb, vb, c, acc, tri2, mask):
    z = lax.dot_general(q, kb, _NT, preferred_element_type=F32)
    nz = -z
    sp = jnp.log(1.0 + jnp.exp(-jnp.maximum(z, nz)))
    log_beta = jnp.minimum(z, 0.0) - sp
    log_keep = jnp.minimum(nz, 0.0) - sp
    if mask is not None:
        log_keep = jnp.where(mask, log_keep, 0.0)
    hi = log_keep.astype(BF16)
    lo = (log_keep - hi.astype(F32)).astype(BF16)
    after = jnp.dot(jnp.concatenate([hi, lo], axis=1), tri2, preferred_element_type=F32)
    w = jnp.exp(log_beta + after + c)
    if mask is not None:
        w = jnp.where(mask, w, 0.0)
    acc = acc + jnp.dot(w.astype(BF16), vb, preferred_element_type=F32)
    c = c + jnp.sum(log_keep, axis=1, keepdims=True)
    return c, acc


def _sb_prompt_kernel(q_ref, k_ref, v_ref, o_ref, kb_ref, vb_ref, *, blk):
    seq = q_ref.shape[0]
    kb_ref[pl.ds(0, blk), :] = jnp.zeros((blk, HEAD_DIM), BF16)
    vb_ref[pl.ds(0, blk), :] = jnp.zeros((blk, HEAD_DIM), BF16)
    kb_ref[pl.ds(blk, seq), :] = k_ref[...].astype(BF16)
    vb_ref[pl.ds(blk, seq), :] = v_ref[...].astype(BF16)
    tri = _strict_lower_ones(blk)
    row = lax.broadcasted_iota(jnp.int32, (blk, blk), 0)
    col = lax.broadcasted_iota(jnp.int32, (blk, blk), 1)
    diag_mask = col < row

    def kv_block(j):
        r0 = pl.multiple_of((j + 1) * blk, blk)
        return kb_ref[pl.ds(r0, blk), :], vb_ref[pl.ds(r0, blk), :]

    def older_blocks(i, q, c, acc):
        def k_cond(carry):
            j, cmax, _, _ = carry
            return jnp.logical_and(j >= 0, cmax > _SB_DEAD_BELOW)

        def k_body(carry):
            j, _, c, acc = carry
            c, acc = _sb_block(q, *kv_block(j), c, acc, tri, None)
            return j - 1, jnp.max(c), c, acc

        return lax.while_loop(k_cond, k_body, (i - 2, jnp.max(c), c, acc))[3]

    def pair_body(p, _):
        state = []
        for i in (2 * p, 2 * p + 1):
            q = q_ref[pl.ds(pl.multiple_of(i * blk, blk), blk), :]
            c = jnp.zeros((blk, 1), F32)
            acc = jnp.zeros((blk, HEAD_DIM), F32)
            c, acc = _sb_block(q, *kv_block(i), c, acc, tri, diag_mask)
            c, acc = _sb_block(q, *kv_block(i - 1), c, acc, tri, None)
            state.append((i, q, c, acc))
        for i, q, c, acc in state:
            acc = older_blocks(i, q, c, acc)
            o_ref[pl.ds(pl.multiple_of(i * blk, blk), blk), :] = acc.astype(o_ref.dtype)
        return 0

    lax.fori_loop(0, seq // (2 * blk), pair_body, 0)


def _sb_prompt(q, k, v, *, batch, seq, blk=256):
    t, hd = q.shape
    heads = hd // HEAD_DIM
    blk = min(blk, seq)
    spec = pl.BlockSpec((seq, HEAD_DIM), lambda b, h: (b, h))
    return pl.pallas_call(
        functools.partial(_sb_prompt_kernel, blk=blk),
        grid=(batch, heads),
        in_specs=[spec, spec, spec],
        out_specs=spec,
        out_shape=jax.ShapeDtypeStruct((t, hd), BF16),
        scratch_shapes=[pltpu.VMEM((seq + blk, HEAD_DIM), BF16), pltpu.VMEM((seq + blk, HEAD_DIM), BF16)],
        compiler_params=_params("parallel", "parallel"),
    )(q, k, v)


_SB_HEAD_GROUP = 32


def _sb_sample_kernel(q_ref, kn_ref, vn_ref, kc_ref, vc_ref, o_ref, c_ref, acc_ref, *, blk):
    n = pl.program_id(2)
    tq = q_ref.shape[0]
    tri = _strict_lower_ones(blk)

    @pl.when(n == 0)
    def _():
        row = lax.broadcasted_iota(jnp.int32, (tq, blk), 0)
        col = lax.broadcasted_iota(jnp.int32, (tq, blk), 1)
        for g in range(_SB_HEAD_GROUP):
            lanes = slice(g * HEAD_DIM, (g + 1) * HEAD_DIM)
            c, acc = _sb_block(q_ref[:, lanes], kn_ref[:, lanes], vn_ref[:, lanes],
                               jnp.zeros((tq, 1), F32), jnp.zeros((tq, HEAD_DIM), F32), tri, col < row)
            c_ref[g] = c
            acc_ref[g] = acc

    @pl.when(jnp.max(c_ref[...]) > _SB_DEAD_BELOW)
    def _():
        kb = pltpu.einshape("thd->htd", kc_ref[...]).astype(BF16)
        vb = pltpu.einshape("thd->htd", vc_ref[...]).astype(BF16)
        for g in range(_SB_HEAD_GROUP):
            lanes = slice(g * HEAD_DIM, (g + 1) * HEAD_DIM)
            c, acc = _sb_block(q_ref[:, lanes], kb[g], vb[g], c_ref[g], acc_ref[g], tri, None)
            c_ref[g] = c
            acc_ref[g] = acc

    @pl.when(n == pl.num_programs(2) - 1)
    def _():
        for g in range(_SB_HEAD_GROUP):
            o_ref[:, g * HEAD_DIM:(g + 1) * HEAD_DIM] = acc_ref[g].astype(o_ref.dtype)


def _sb_sample(q, k_new, v_new, cache_k, cache_v, *, batch, tq, blk=128):
    t, hd = q.shape
    heads = hd // HEAD_DIM
    past = cache_k.shape[1]
    nblk = past // blk
    gw = _SB_HEAD_GROUP * HEAD_DIM

    def pad_block(a):
        a = a.reshape(batch, tq, hd)
        a = jnp.pad(a, ((0, 0), (0, blk - tq), (0, 0)))
        return a.reshape(batch * blk, hd).astype(BF16)

    qspec = pl.BlockSpec((tq, gw), lambda b, h, n: (b, h))
    nspec = pl.BlockSpec((blk, gw), lambda b, h, n: (b, h))
    cspec = pl.BlockSpec((None, blk, _SB_HEAD_GROUP, HEAD_DIM), lambda b, h, n: (b, nblk - 1 - n, h, 0))
    return pl.pallas_call(
        functools.partial(_sb_sample_kernel, blk=blk),
        grid=(batch, heads // _SB_HEAD_GROUP, nblk),
        in_specs=[qspec, nspec, nspec, cspec, cspec],
        out_specs=qspec,
        out_shape=jax.ShapeDtypeStruct((t, hd), BF16),
        scratch_shapes=[pltpu.VMEM((_SB_HEAD_GROUP, tq, 1), F32),
                        pltpu.VMEM((_SB_HEAD_GROUP, tq, HEAD_DIM), F32)],
        compiler_params=_params("parallel", "parallel", "arbitrary"),
    )(q, pad_block(k_new), pad_block(v_new), cache_k, cache_v)


def _band_kernel(q_ref, k_ref, v_ref, bias_ref, o_ref, *, cq, win, nq, chunked, lo_static):
    bias = bias_ref[...].reshape(G_B * cq, win)
    col = lax.broadcasted_iota(jnp.int32, (G_B * cq, win), 1)

    def chunk(c, _):
        r0 = pl.multiple_of(c * cq, cq)
        if chunked:
            lo = jnp.maximum(CHUNK, (N_PREV_CHUNKS + 1 - c) * CHUNK)
        else:
            lo = lo_static
        kb = k_ref[pl.ds(r0, win), :]
        vb = v_ref[pl.ds(r0, win), :]
        qb = q_ref[pl.ds(r0, cq), :]
        q4 = jnp.concatenate([qb[:, g * HEAD_DIM:(g + 1) * HEAD_DIM] for g in range(G_B)], axis=0)
        s = lax.dot_general(q4, kb, _NT, preferred_element_type=F32) + bias
        s = jnp.where(col >= lo, s, NEG_INF)
        m = jnp.max(s, axis=1, keepdims=True)
        e = jnp.exp(s - m)
        o = jnp.dot(e.astype(BF16), vb, preferred_element_type=F32) * (1.0 / jnp.sum(e, axis=1, keepdims=True))
        for g in range(G_B):
            o_ref[pl.ds(r0, cq), g * HEAD_DIM:(g + 1) * HEAD_DIM] = o[g * cq:(g + 1) * cq].astype(o_ref.dtype)
        return 0

    lax.fori_loop(0, nq, chunk, 0, unroll=2 if nq % 2 == 0 else 1)


def _band_bias(rel_table, cq, win):
    rel = jnp.arange(cq)[:, None] + (win - cq) - jnp.arange(win)[None, :]
    bias = rel_table[jnp.clip(rel, -REL_CLIP, REL_CLIP) + REL_CLIP].astype(F32)
    return bias.transpose(2, 0, 1)


def _band_attention(q, kpad, vpad, bias, *, batch, nq, cq, win, chunked, lo_static):
    t, hd = q.shape
    rows = kpad.shape[1]
    qw = G_B * HEAD_DIM
    qspec = pl.BlockSpec((nq * cq, qw), lambda b, h: (b, h))
    kspec = pl.BlockSpec((None, rows, HEAD_DIM), lambda b, h: (b, 0, h))
    bspec = pl.BlockSpec((G_B, cq, win), lambda b, h: (h, 0, 0))
    return pl.pallas_call(
        functools.partial(_band_kernel, cq=cq, win=win, nq=nq, chunked=chunked, lo_static=lo_static),
        grid=(batch, N_KV_B),
        in_specs=[qspec, kspec, kspec, bspec],
        out_specs=qspec,
        out_shape=jax.ShapeDtypeStruct((t, hd), BF16),
        compiler_params=_params("parallel", "parallel"),
    )(q, kpad, vpad, bias)


def _split3_dot_nt(a, b):
    a_hi = a.astype(BF16)
    a_lo = (a - a_hi.astype(F32)).astype(BF16)
    b_hi = b.astype(BF16)
    b_lo = (b - b_hi.astype(F32)).astype(BF16)
    return (lax.dot_general(a_hi, b_hi, _NT, preferred_element_type=F32)
            + lax.dot_general(a_lo, b_hi, _NT, preferred_element_type=F32)
            + lax.dot_general(a_hi, b_lo, _NT, preferred_element_type=F32))


def _take_lanes(x, idx):
    return jnp.take_along_axis(x, idx, axis=1, mode="promise_in_bounds")


_ROUTE_ROWS = 64
_ROUTE_UNROLL = 16


def _route_kernel(q_ref, sk_ref, pairs_ref, e_ref, g_ref, s_ref, left_ref, si_ref, cleft_ref, fpos_ref):
    tm = q_ref.shape[0]
    dh = sk_ref.shape[-1]
    nset = 2 * PEER_HEADS
    lane_c = lax.broadcasted_iota(jnp.int32, (_ROUTE_ROWS, PEER_KEYS), 1)

    for n in range(nset):
        sc = _split3_dot_nt(q_ref[:, n * dh:(n + 1) * dh], sk_ref[n // 2, n % 2])
        s_ref[n * tm:(n + 1) * tm, :] = sc
        left_ref[n * tm:(n + 1) * tm, :] = sc
    si_ref[...] = jnp.zeros(si_ref.shape, jnp.int32)

    def body1(i, _):
        def piece(r, _):
            rows = pl.ds(pl.multiple_of(r * _ROUTE_ROWS, _ROUTE_ROWS), _ROUTE_ROWS)
            left = left_ref[rows, :]
            pos = jnp.argmax(left, axis=1, keepdims=True).astype(jnp.int32)
            left_ref[rows, :] = jnp.where(lane_c == pos, NEG_INF, left)
            si_ref[rows, :] = jnp.where(lane_c == i, pos, si_ref[rows, :])
            return 0

        return lax.fori_loop(0, nset * tm // _ROUTE_ROWS, piece, 0, unroll=_ROUTE_UNROLL)

    lax.fori_loop(0, PEER_TOPK, body1, 0)
    si = si_ref[...]
    sv = _take_lanes(s_ref[...], si)
    si = si.astype(F32)

    pair_a = jnp.broadcast_to(pairs_ref[0:1, :], (tm, PEER_KEYS))
    pair_b = jnp.broadcast_to(pairs_ref[1:2, :], (tm, PEER_KEYS))
    pair_ok = jnp.broadcast_to(pairs_ref[2:3, :], (tm, PEER_KEYS)) > 0
    cand, cidx = [], []
    for h in range(PEER_HEADS):
        r0 = slice((2 * h) * tm, (2 * h + 1) * tm)
        r1 = slice((2 * h + 1) * tm, (2 * h + 2) * tm)
        cand.append(jnp.where(pair_ok, _take_lanes(sv[r0], pair_a) + _take_lanes(sv[r1], pair_b), NEG_INF))
        cidx.append(_take_lanes(si[r0], pair_a) * float(PEER_KEYS) + _take_lanes(si[r1], pair_b))
    cand = jnp.concatenate(cand, axis=0)
    cidx = jnp.concatenate(cidx, axis=0)

    lane2 = lax.broadcasted_iota(jnp.int32, cand.shape, 1)
    head_lane0 = (lax.broadcasted_iota(jnp.int32, cand.shape, 0) // tm) * PEER_TOPK
    cleft_ref[...] = cand
    fpos_ref[...] = jnp.zeros(fpos_ref.shape, jnp.int32)

    def body2(i, _):
        def piece(r, _):
            row0 = pl.multiple_of(r * _ROUTE_ROWS, _ROUTE_ROWS)
            rows = pl.ds(row0, _ROUTE_ROWS)
            left = cleft_ref[rows, :]
            pos = jnp.argmax(left, axis=1, keepdims=True).astype(jnp.int32)
            cleft_ref[rows, :] = jnp.where(lane_c == pos, NEG_INF, left)
            out_lane = (row0 // tm) * PEER_TOPK + i
            fpos_ref[rows, :] = jnp.where(lane_c == out_lane, pos, fpos_ref[rows, :])
            return 0

        return lax.fori_loop(0, PEER_HEADS * tm // _ROUTE_ROWS, piece, 0, unroll=_ROUTE_UNROLL)

    lax.fori_loop(0, PEER_TOPK, body2, 0)
    fpos = fpos_ref[...]
    fv = _take_lanes(cand, fpos)
    fe = _take_lanes(cidx, fpos)

    grp = lane2 // PEER_TOPK == head_lane0 // PEER_TOPK
    fe = jnp.where(grp, fe, 0.0)
    mx = jnp.max(jnp.where(grp, fv, NEG_INF), axis=1, keepdims=True)
    ex = jnp.where(grp, jnp.exp(fv - mx), 0.0)
    gate = ex / jnp.sum(ex, axis=1, keepdims=True)
    e_out = fe[0:tm]
    g_out = gate[0:tm]
    for h in range(1, PEER_HEADS):
        e_out = e_out + fe[h * tm:(h + 1) * tm]
        g_out = g_out + gate[h * tm:(h + 1) * tm]
    e_ref[...] = e_out.astype(jnp.int32)
    g_ref[...] = g_out


def _peer_route(qp, sub_keys):
    m, kq = qp.shape
    tm = min(256, m)
    ne = PEER_HEADS * PEER_TOPK
    ospec = pl.BlockSpec((tm, ne), lambda i: (i, 0))
    pairs = [(a, b) for a in range(PEER_TOPK) for b in range(PEER_TOPK) if (a + 1) * (b + 1) <= PEER_TOPK]
    pad = [0] * (PEER_KEYS - len(pairs))
    pair_rows = jnp.array([[a for a, _ in pairs] + pad, [b for _, b in pairs] + pad,
                           [1] * len(pairs) + pad] + [[0] * PEER_KEYS] * 5, jnp.int32)
    return pl.pallas_call(
        _route_kernel,
        grid=(m // tm,),
        in_specs=[pl.BlockSpec((tm, kq), lambda i: (i, 0)),
                  pl.BlockSpec(sub_keys.shape, lambda i: (0, 0, 0, 0)),
                  pl.BlockSpec(pair_rows.shape, lambda i: (0, 0))],
        out_specs=[ospec, ospec],
        out_shape=[jax.ShapeDtypeStruct((m, ne), jnp.int32), jax.ShapeDtypeStruct((m, ne), F32)],
        scratch_shapes=[pltpu.VMEM((2 * PEER_HEADS * tm, PEER_KEYS), F32),
                        pltpu.VMEM((2 * PEER_HEADS * tm, PEER_KEYS), F32),
                        pltpu.VMEM((2 * PEER_HEADS * tm, PEER_KEYS), jnp.int32),
                        pltpu.VMEM((PEER_HEADS * tm, PEER_KEYS), F32),
                        pltpu.VMEM((PEER_HEADS * tm, PEER_KEYS), jnp.int32)],
        compiler_params=_params("parallel"),
    )(qp, sub_keys, pair_rows)


def _gelu_tanh(x):
    return 0.5 * x * (1.0 + jnp.tanh(0.7978845608028654 * (x + 0.044715 * (x * x * x))))


_TOKEN_GROUP = 16


def _expert_kernel(h_ref, gain_ref, e_ref, g_ref, tab_ref, o_ref,
                   xn_ref, hi_ref, lo_ref, act_ref, d_ref, w3_ref, *, nj):
    j = pl.program_id(1)
    tm = h_ref.shape[0]
    te = tab_ref.shape[0]
    nb = te // LANES

    @pl.when(j == 0)
    def _():
        xn_ref[...] = _norm_rows(h_ref[...], gain_ref[...]).astype(BF16)
        o_ref[...] = h_ref[...]
        e = e_ref[...]
        hi_ref[...] = e // LANES
        lo_ref[...] = e % LANES
        act_ref[...] = jnp.zeros(act_ref.shape, F32)

    def score_slab(slot):
        d_ref[slot] = lax.dot_general(xn_ref[...], tab_ref[...], _NT, preferred_element_type=F32)

    def pick_from_slab(slot, slab):
        hi = hi_ref[...]
        lo = lo_ref[...]
        act = act_ref[...]
        for s in range(nb):
            picked = _take_lanes(d_ref[slot, :, s * LANES:(s + 1) * LANES], lo)
            act = jnp.where(hi == slab * nb + s, picked, act)
        act_ref[...] = act

    @pl.when(j == 0)
    def _():
        score_slab(0)

    @pl.when(jnp.logical_and(j >= 1, j < nj))
    def _():
        pick_from_slab((j - 1) % 2, j - 1)
        score_slab(j % 2)

    @pl.when(j == nj)
    def _():
        pick_from_slab((nj - 1) % 2, nj - 1)
        act_ref[...] = g_ref[...] * _gelu_tanh(act_ref[...])
        sub = lax.broadcasted_iota(jnp.int32, (LANES, LANES), 0)

        def group(gi, _):
            t0 = pl.multiple_of(gi * _TOKEN_GROUP, _TOKEN_GROUP)
            grids = []
            for u in range(_TOKEN_GROUP):
                hi_row = hi_ref[pl.ds(t0 + u, 1), :]
                lo_row = lo_ref[pl.ds(t0 + u, 1), :]
                w_row = act_ref[pl.ds(t0 + u, 1), :]
                a_t = jnp.where(hi_row == sub, w_row, 0.0).astype(BF16)
                b_t = jnp.where(lo_row == sub, 1.0, 0.0).astype(BF16)
                grids.append(lax.dot_general(a_t, b_t, _NT, preferred_element_type=F32))
            w3_ref[:, pl.ds(t0, _TOKEN_GROUP), :] = pltpu.einshape(
                "uik->iuk", jnp.stack(grids, axis=0)).astype(BF16)
            return 0

        lax.fori_loop(0, tm // _TOKEN_GROUP, group, 0)

    @pl.when(j >= nj)
    def _():
        jj = j - nj
        w = jnp.concatenate([w3_ref[jj * nb + s] for s in range(nb)], axis=1)
        d = o_ref.shape[1]
        for c0 in range(0, d, _EXPERT_OUT_COLS):
            cols = slice(c0, c0 + _EXPERT_OUT_COLS)
            o_ref[:, cols] += jnp.dot(w, tab_ref[:, cols], preferred_element_type=F32)


_EXPERT_OUT_COLS = 1024


def _peer_experts(h, gain, eidx, gate, uv_tab, layer, *, te=512):
    m, d = h.shape
    ne = uv_tab.shape[2]
    nsel = eidx.shape[1]
    tm = min(512, m)
    nj = ne // te
    once = pl.Buffered(1)
    sel_spec = pl.BlockSpec((tm, nsel), lambda i, j: (i, 0), pipeline_mode=once)
    return pl.pallas_call(
        functools.partial(_expert_kernel, nj=nj),
        grid=(m // tm, 2 * nj),
        in_specs=[pl.BlockSpec((tm, d), lambda i, j: (i, 0), pipeline_mode=once),
                  pl.BlockSpec((1, d), lambda i, j: (0, 0), pipeline_mode=once),
                  sel_spec, sel_spec,
                  pl.BlockSpec((None, None, te, d), lambda i, j: (layer, j // nj, j % nj, 0))],
        out_specs=pl.BlockSpec((tm, d), lambda i, j: (i, 0), pipeline_mode=once),
        out_shape=jax.ShapeDtypeStruct((m, d), F32),
        scratch_shapes=[pltpu.VMEM((tm, d), BF16),
                        pltpu.VMEM((tm, nsel), jnp.int32),
                        pltpu.VMEM((tm, nsel), jnp.int32),
                        pltpu.VMEM((tm, nsel), F32),
                        pltpu.VMEM((2, tm, te), F32),
                        pltpu.VMEM((ne // LANES, tm, LANES), BF16)],
        compiler_params=_params("parallel", "arbitrary"),
    )(h, gain.reshape(1, d), eidx, gate, uv_tab)


def _peer_ffn(h, i, prm):
    qp = _matmul(h, prm["w_q_peer"][i], gain=prm["g_ffn"][i])
    eidx, gate = _peer_route(qp, prm["peer_sub_keys"][i])
    return _peer_experts(h, prm["g_ffn"][i], eidx, gate, prm["peer_uv"], i)


def _ple(h, p, i, prm):
    return _matmul(h, prm["w_ple_gate"][i], gain=prm["g_ple"][i], residual=h,
                   ple=(p, prm["w_ple_proj"][i]))


def _trunk(x, pe, past, prm):
    b, t, d = x.shape
    m = b * t
    h = x.reshape(m, d)
    pe = pe.reshape(pe.shape[0], m, pe.shape[-1])

    q, k, v = _qkv_proj(h, prm["w_qkv_a"], prm["g_mix"][0])
    if past is None:
        o = _sb_prompt(q, k, v, batch=b, seq=t)
    else:
        o = _sb_sample(q, k, v, past[0][0], past[1][0], batch=b, tq=t)
    h = _matmul(o, prm["w_o_a"], residual=h)
    h = _peer_ffn(h, 0, prm)
    h = _ple(h, pe[0], 0, prm)

    kv = _matmul(h, prm["w_kv_b"], gain=prm["g_kv"])
    nkv = N_KV_B * HEAD_DIM
    kb_new = kv[:, :nkv].reshape(b, t, nkv)
    vb_new = kv[:, nkv:].reshape(b, t, nkv)

    qb = _matmul(h, prm["w_q_b"], gain=prm["g_mix"][1], out_dtype=BF16, out_scale=SCALE)
    win = (N_PREV_CHUNKS + 2) * CHUNK
    if past is None:
        front = (N_PREV_CHUNKS + 1) * CHUNK
        kpad = jnp.pad(kb_new.astype(BF16), ((0, 0), (front, 0), (0, 0)))
        vpad = jnp.pad(vb_new.astype(BF16), ((0, 0), (front, 0), (0, 0)))
        bias = _band_bias(prm["rel_bias_b"][0], CHUNK, win)
        ob = _band_attention(qb, kpad, vpad, bias, batch=b, nq=t // CHUNK, cq=CHUNK, win=win,
                             chunked=True, lo_static=0)
    else:
        ck = past[2].reshape(b, -1, nkv)
        cv = past[3].reshape(b, -1, nkv)
        front = win - ck.shape[1] - t
        kpad = jnp.pad(jnp.concatenate([ck, kb_new], axis=1).astype(BF16), ((0, 0), (front, 0), (0, 0)))
        vpad = jnp.pad(jnp.concatenate([cv, vb_new], axis=1).astype(BF16), ((0, 0), (front, 0), (0, 0)))
        bias = _band_bias(prm["rel_bias_b"][0], t, win)
        ob = _band_attention(qb, kpad, vpad, bias, batch=b, nq=1, cq=t, win=win,
                             chunked=False, lo_static=front)
    h = _matmul(ob, prm["w_o_b"], residual=h)
    h = _peer_ffn(h, 1, prm)
    h = _ple(h, pe[1], 1, prm)

    y = _rmsnorm(h, prm["g_final"]).reshape(b, t, d)
    heads = k.shape[1] // HEAD_DIM
    a_k = k.reshape(1, b, t, heads, HEAD_DIM)
    a_v = v.reshape(1, b, t, heads, HEAD_DIM)
    if past is None:
        keep = min(N_PREV_CHUNKS * CHUNK, t)
        b_k, b_v = kb_new[:, t - keep:], vb_new[:, t - keep:]
    else:
        b_k, b_v = kb_new, vb_new
    b_k = b_k.reshape(b, -1, N_KV_B, HEAD_DIM)
    b_v = b_v.reshape(b, -1, N_KV_B, HEAD_DIM)
    return y, a_k, a_v, b_k, b_v


def kernel(x_prompt, x_sample, cache_a_k, cache_a_v, cache_b_k, cache_b_v, p_prompt, p_sample, g_mix, w_qkv_a, w_o_a, g_kv, w_kv_b, w_q_b, rel_bias_b, w_o_b, g_ffn, w_q_peer, peer_sub_keys, peer_u, peer_v, g_ple, w_ple_gate, w_ple_proj, g_final):
    prm = dict(
        g_mix=g_mix, g_kv=g_kv, g_ffn=g_ffn, g_ple=g_ple, g_final=g_final,
        w_qkv_a=_to_bf16(w_qkv_a)[0],
        w_o_a=_to_bf16(w_o_a)[0],
        w_kv_b=_to_bf16(w_kv_b),
        w_q_b=_to_bf16(w_q_b)[0],
        w_o_b=_to_bf16(w_o_b)[0],
        rel_bias_b=rel_bias_b,
        w_q_peer=_to_bf16(w_q_peer),
        peer_sub_keys=peer_sub_keys,
        peer_uv=_peer_tables_bf16(peer_u, peer_v),
        w_ple_gate=_to_bf16(w_ple_gate),
        w_ple_proj=_to_bf16(w_ple_proj),
    )
    y_p, ak_p, av_p, bk_p, bv_p = _trunk(x_prompt, p_prompt, None, prm)
    y_s, ak_s, av_s, bk_s, bv_s = _trunk(x_sample, p_sample,
                                         (cache_a_k, cache_a_v, cache_b_k, cache_b_v), prm)
    return (y_p, y_s, ak_p, av_p, bk_p, bv_p, ak_s, av_s, bk_s, bv_s)
```

```python
import functools

import jax
import jax.numpy as jnp
from jax import lax
from jax.experimental import pallas as pl
from jax.experimental.pallas import tpu as pltpu

F32 = jnp.float32
BF16 = jnp.bfloat16

EPS = 1e-6
HEAD_DIM = 128
CHUNK = 64
N_PREV_CHUNKS = 8
N_KV_B = 8
G_B = 4
REL_CLIP = 128
PEER_HEADS = 8
PEER_KEYS = 128
PEER_TOPK = 16
SCALE = HEAD_DIM ** -0.5
NEG_INF = float("-inf")

VMEM_LIMIT_BYTES = 60 * 1024 * 1024
LANES = 128

_NT = (((1,), (1,)), ((), ()))


def _params(*sem):
    return pltpu.CompilerParams(dimension_semantics=sem, vmem_limit_bytes=VMEM_LIMIT_BYTES)


def _norm_rows(x, g):
    ms = jnp.mean(x * x, axis=-1, keepdims=True)
    return x * lax.rsqrt(ms + EPS) * g


def _sigmoid(x):
    return 1.0 / (1.0 + jnp.exp(-x))


def _mm_kernel(*refs, norm, res, ple, out_scale):
    it = iter(refs)
    x_ref = next(it)
    g_ref = next(it) if norm else None
    w_ref = next(it)
    res_ref = next(it) if res else None
    p_ref = next(it) if ple else None
    wp_ref = next(it) if ple else None
    o_ref = next(it)
    xn_ref = next(it) if norm else None

    if norm:
        @pl.when(pl.program_id(1) == 0)
        def _():
            xn_ref[...] = _norm_rows(x_ref[...], g_ref[...]).astype(BF16)
        x = xn_ref[...]
    else:
        x = x_ref[...]
    acc = jnp.dot(x, w_ref[...], preferred_element_type=F32)
    if ple:
        proj = jnp.dot(p_ref[...].astype(BF16), wp_ref[...], preferred_element_type=F32)
        acc = _sigmoid(acc) * proj
    if res:
        acc = acc + res_ref[...]
    if out_scale is not None:
        acc = acc * out_scale
    o_ref[...] = acc.astype(o_ref.dtype)


def _matmul(x, w, *, gain=None, residual=None, ple=None, out_dtype=F32, out_scale=None, tn=1024):
    m, k = x.shape
    n = w.shape[1]
    tm = min(512, m)
    tn = min(tn, n)
    norm = gain is not None
    in_specs = [pl.BlockSpec((tm, k), lambda i, j: (i, 0))]
    args = [x]
    if norm:
        in_specs.append(pl.BlockSpec((1, k), lambda i, j: (0, 0)))
        args.append(gain.reshape(1, k))
    in_specs.append(pl.BlockSpec((k, tn), lambda i, j: (0, j)))
    args.append(w)
    if residual is not None:
        in_specs.append(pl.BlockSpec((tm, tn), lambda i, j: (i, j)))
        args.append(residual)
    if ple is not None:
        p, wp = ple
        kp = p.shape[1]
        in_specs.append(pl.BlockSpec((tm, kp), lambda i, j: (i, 0)))
        in_specs.append(pl.BlockSpec((kp, tn), lambda i, j: (0, j)))
        args += [p, wp]
    return pl.pallas_call(
        functools.partial(_mm_kernel, norm=norm, res=residual is not None, ple=ple is not None,
                          out_scale=out_scale),
        grid=(m // tm, n // tn),
        in_specs=in_specs,
        out_specs=pl.BlockSpec((tm, tn), lambda i, j: (i, j)),
        out_shape=jax.ShapeDtypeStruct((m, n), out_dtype),
        scratch_shapes=[pltpu.VMEM((tm, k), BF16)] if norm else [],
        compiler_params=_params("parallel", "arbitrary"),
    )(*args)


def _qkv_kernel(x_ref, g_ref, w_ref, q_ref, k_ref, v_ref, xn_ref, *, nq):
    j = pl.program_id(1)

    @pl.when(j == 0)
    def _():
        xn_ref[...] = _norm_rows(x_ref[...], g_ref[...]).astype(BF16)

    acc = jnp.dot(xn_ref[...], w_ref[...], preferred_element_type=F32)

    @pl.when(j < nq)
    def _():
        q_ref[...] = (acc * SCALE).astype(q_ref.dtype)

    @pl.when(jnp.logical_and(j >= nq, j < 2 * nq))
    def _():
        k_ref[...] = acc

    @pl.when(j >= 2 * nq)
    def _():
        v_ref[...] = acc


def _qkv_proj(x, w, gain, *, tn=1024):
    m, k = x.shape
    n = w.shape[1] // 3
    tm = min(512, m)
    nq = n // tn

    def ospec(first):
        return pl.BlockSpec((tm, tn), lambda i, j: (i, jnp.clip(j - first, 0, nq - 1)))

    return pl.pallas_call(
        functools.partial(_qkv_kernel, nq=nq),
        grid=(m // tm, 3 * nq),
        in_specs=[pl.BlockSpec((tm, k), lambda i, j: (i, 0)),
                  pl.BlockSpec((1, k), lambda i, j: (0, 0)),
                  pl.BlockSpec((k, tn), lambda i, j: (0, j))],
        out_specs=[ospec(0), ospec(nq), ospec(2 * nq)],
        out_shape=[jax.ShapeDtypeStruct((m, n), BF16), jax.ShapeDtypeStruct((m, n), F32),
                   jax.ShapeDtypeStruct((m, n), F32)],
        scratch_shapes=[pltpu.VMEM((tm, k), BF16)],
        compiler_params=_params("parallel", "arbitrary"),
    )(x, gain.reshape(1, k), w)


def _cast_kernel(x_ref, o_ref):
    o_ref[...] = x_ref[...].astype(o_ref.dtype)


_CAST_ROWS = 256
_CAST_COLS = 4096


def _to_bf16(x):
    x2 = x.reshape(-1, x.shape[-1])
    rows, cols = x2.shape
    tr = min(_CAST_ROWS, rows)
    tc = min(_CAST_COLS, cols)
    spec = pl.BlockSpec((tr, tc), lambda i, j: (i, j))
    out = pl.pallas_call(
        _cast_kernel,
        grid=(rows // tr, cols // tc),
        in_specs=[spec],
        out_specs=spec,
        out_shape=jax.ShapeDtypeStruct((rows, cols), BF16),
        compiler_params=_params("parallel", "parallel"),
    )(x2)
    return out.reshape(x.shape)


def _stack_cast_kernel(u_ref, v_ref, o_ref):
    c = pl.program_id(1)

    @pl.when(c == 0)
    def _():
        o_ref[...] = u_ref[...].astype(o_ref.dtype)

    @pl.when(c == 1)
    def _():
        o_ref[...] = v_ref[...].astype(o_ref.dtype)


def _peer_tables_bf16(u, v):
    nl, ne, d = u.shape
    te = _CAST_ROWS
    nblk = ne // te
    uspec = pl.BlockSpec((None, te, d), lambda l, c, e: (l, jnp.where(c == 0, e, nblk - 1), 0))
    vspec = pl.BlockSpec((None, te, d), lambda l, c, e: (l, jnp.where(c == 1, e, 0), 0))
    return pl.pallas_call(
        _stack_cast_kernel,
        grid=(nl, 2, nblk),
        in_specs=[uspec, vspec],
        out_specs=pl.BlockSpec((None, None, te, d), lambda l, c, e: (l, c, e, 0)),
        out_shape=jax.ShapeDtypeStruct((nl, 2, ne, d), BF16),
        compiler_params=_params("parallel", "arbitrary", "arbitrary"),
    )(u, v)


def _rmsnorm_kernel(x_ref, g_ref, o_ref):
    o_ref[...] = _norm_rows(x_ref[...], g_ref[...])


def _rmsnorm(x, gain):
    m, k = x.shape
    tm = min(512, m)
    return pl.pallas_call(
        _rmsnorm_kernel,
        grid=(m // tm,),
        in_specs=[pl.BlockSpec((tm, k), lambda i: (i, 0)), pl.BlockSpec((1, k), lambda i: (0, 0))],
        out_specs=pl.BlockSpec((tm, k), lambda i: (i, 0)),
        out_shape=jax.ShapeDtypeStruct((m, k), F32),
        compiler_params=_params("parallel"),
    )(x, gain.reshape(1, k))


def _strict_lower_ones(n):
    r = lax.broadcasted_iota(jnp.int32, (2 * n, n), 0) % n
    c = lax.broadcasted_iota(jnp.int32, (2 * n, n), 1)
    return jnp.where(r > c, 1.0, 0.0).astype(BF16)


_SB_DEAD_BELOW = -104.0


def _sb_block(q, kb, vb, c, acc, tri2, mask):
    z = lax.dot_general(q, kb, _NT, preferred_element_type=F32)
    nz = -z
    sp = jnp.log(1.0 + jnp.exp(-jnp.maximum(z, nz)))
    log_beta = jnp.minimum(z, 0.0) - sp
    log_keep = jnp.minimum(nz, 0.0) - sp
    if mask is not None:
        log_keep = jnp.where(mask, log_keep, 0.0)
    hi = log_keep.astype(BF16)
    lo = (log_keep - hi.astype(F32)).astype(BF16)
    after = jnp.dot(jnp.concatenate([hi, lo], axis=1), tri2, preferred_element_type=F32)
    w = jnp.exp(log_beta + after + c)
    if mask is not None:
        w = jnp.where(mask, w, 0.0)
    acc = acc + jnp.dot(w.astype(BF16), vb, preferred_element_type=F32)
    c = c + jnp.sum(log_keep, axis=1, keepdims=True)
    return c, acc


_SB_QUERY_GROUP = 4


def _sb_prompt_kernel(q_ref, k_ref, v_ref, o_ref, kb_ref, vb_ref, *, blk):
    seq = q_ref.shape[0]
    kb_ref[pl.ds(0, blk), :] = jnp.zeros((blk, HEAD_DIM), BF16)
    vb_ref[pl.ds(0, blk), :] = jnp.zeros((blk, HEAD_DIM), BF16)
    kb_ref[pl.ds(blk, seq), :] = k_ref[...].astype(BF16)
    vb_ref[pl.ds(blk, seq), :] = v_ref[...].astype(BF16)
    tri = _strict_lower_ones(blk)
    row = lax.broadcasted_iota(jnp.int32, (blk, blk), 0)
    col = lax.broadcasted_iota(jnp.int32, (blk, blk), 1)
    diag_mask = col < row

    def kv_block(j):
        r0 = pl.multiple_of((j + 1) * blk, blk)
        return kb_ref[pl.ds(r0, blk), :], vb_ref[pl.ds(r0, blk), :]

    def older_blocks(i, q, c, acc):
        def k_cond(carry):
            j, cmax, _, _ = carry
            return jnp.logical_and(j >= 0, cmax > _SB_DEAD_BELOW)

        def k_body(carry):
            j, _, c, acc = carry
            c, acc = _sb_block(q, *kv_block(j), c, acc, tri, None)
            return j - 1, jnp.max(c), c, acc

        return lax.while_loop(k_cond, k_body, (i - 2, jnp.max(c), c, acc))[3]

    def group_body(p, _):
        state = []
        for i in [_SB_QUERY_GROUP * p + u for u in range(_SB_QUERY_GROUP)]:
            q = q_ref[pl.ds(pl.multiple_of(i * blk, blk), blk), :]
            c = jnp.zeros((blk, 1), F32)
            acc = jnp.zeros((blk, HEAD_DIM), F32)
            c, acc = _sb_block(q, *kv_block(i), c, acc, tri, diag_mask)
            c, acc = _sb_block(q, *kv_block(i - 1), c, acc, tri, None)
            state.append((i, q, c, acc))
        for i, q, c, acc in state:
            acc = older_blocks(i, q, c, acc)
            o_ref[pl.ds(pl.multiple_of(i * blk, blk), blk), :] = acc.astype(o_ref.dtype)
        return 0

    lax.fori_loop(0, seq // (_SB_QUERY_GROUP * blk), group_body, 0)


def _sb_prompt(q, k, v, *, batch, seq, blk=256):
    t, hd = q.shape
    heads = hd // HEAD_DIM
    blk = min(blk, seq)
    spec = pl.BlockSpec((seq, HEAD_DIM), lambda b, h: (b, h))
    return pl.pallas_call(
        functools.partial(_sb_prompt_kernel, blk=blk),
        grid=(batch, heads),
        in_specs=[spec, spec, spec],
        out_specs=spec,
        out_shape=jax.ShapeDtypeStruct((t, hd), BF16),
        scratch_shapes=[pltpu.VMEM((seq + blk, HEAD_DIM), BF16), pltpu.VMEM((seq + blk, HEAD_DIM), BF16)],
        compiler_params=_params("parallel", "parallel"),
    )(q, k, v)


_SB_HEAD_GROUP = 32


def _sb_sample_kernel(q_ref, kn_ref, vn_ref, kc_ref, vc_ref, o_ref, c_ref, acc_ref, *, blk):
    n = pl.program_id(2)
    tq = q_ref.shape[0]
    tri = _strict_lower_ones(blk)

    @pl.when(n == 0)
    def _():
        row = lax.broadcasted_iota(jnp.int32, (tq, blk), 0)
        col = lax.broadcasted_iota(jnp.int32, (tq, blk), 1)
        for g in range(_SB_HEAD_GROUP):
            lanes = slice(g * HEAD_DIM, (g + 1) * HEAD_DIM)
            c, acc = _sb_block(q_ref[:, lanes], kn_ref[:, lanes], vn_ref[:, lanes],
                               jnp.zeros((tq, 1), F32), jnp.zeros((tq, HEAD_DIM), F32), tri, col < row)
            c_ref[g] = c
            acc_ref[g] = acc

    @pl.when(jnp.max(c_ref[...]) > _SB_DEAD_BELOW)
    def _():
        kb = pltpu.einshape("thd->htd", kc_ref[...]).astype(BF16)
        vb = pltpu.einshape("thd->htd", vc_ref[...]).astype(BF16)
        for g in range(_SB_HEAD_GROUP):
            lanes = slice(g * HEAD_DIM, (g + 1) * HEAD_DIM)
            c, acc = _sb_block(q_ref[:, lanes], kb[g], vb[g], c_ref[g], acc_ref[g], tri, None)
            c_ref[g] = c
            acc_ref[g] = acc

    @pl.when(n == pl.num_programs(2) - 1)
    def _():
        for g in range(_SB_HEAD_GROUP):
            o_ref[:, g * HEAD_DIM:(g + 1) * HEAD_DIM] = acc_ref[g].astype(o_ref.dtype)


def _sb_sample(q, k_new, v_new, cache_k, cache_v, *, batch, tq, blk=256):
    t, hd = q.shape
    heads = hd // HEAD_DIM
    past = cache_k.shape[1]
    nblk = past // blk
    gw = _SB_HEAD_GROUP * HEAD_DIM

    def pad_block(a):
        a = a.reshape(batch, tq, hd)
        a = jnp.pad(a, ((0, 0), (0, blk - tq), (0, 0)))
        return a.reshape(batch * blk, hd).astype(BF16)

    qspec = pl.BlockSpec((tq, gw), lambda b, h, n: (b, h))
    nspec = pl.BlockSpec((blk, gw), lambda b, h, n: (b, h))
    cspec = pl.BlockSpec((None, blk, _SB_HEAD_GROUP, HEAD_DIM), lambda b, h, n: (b, nblk - 1 - n, h, 0))
    return pl.pallas_call(
        functools.partial(_sb_sample_kernel, blk=blk),
        grid=(batch, heads // _SB_HEAD_GROUP, nblk),
        in_specs=[qspec, nspec, nspec, cspec, cspec],
        out_specs=qspec,
        out_shape=jax.ShapeDtypeStruct((t, hd), BF16),
        scratch_shapes=[pltpu.VMEM((_SB_HEAD_GROUP, tq, 1), F32),
                        pltpu.VMEM((_SB_HEAD_GROUP, tq, HEAD_DIM), F32)],
        compiler_params=_params("parallel", "parallel", "arbitrary"),
    )(q, pad_block(k_new), pad_block(v_new), cache_k, cache_v)


_BAND_UNROLL = 4


def _band_kernel(q_ref, k_ref, v_ref, bias_ref, o_ref, *, cq, win, nq, chunked, lo_static):
    bias = bias_ref[...].reshape(G_B * cq, win)
    col = lax.broadcasted_iota(jnp.int32, (G_B * cq, win), 1)

    def chunk(c, _):
        r0 = pl.multiple_of(c * cq, cq)
        if chunked:
            lo = jnp.maximum(CHUNK, (N_PREV_CHUNKS + 1 - c) * CHUNK)
        else:
            lo = lo_static
        kb = k_ref[pl.ds(r0, win), :]
        vb = v_ref[pl.ds(r0, win), :]
        qb = q_ref[pl.ds(r0, cq), :]
        q4 = jnp.concatenate([qb[:, g * HEAD_DIM:(g + 1) * HEAD_DIM] for g in range(G_B)], axis=0)
        s = lax.dot_general(q4, kb, _NT, preferred_element_type=F32) + bias
        s = jnp.where(col >= lo, s, NEG_INF)
        m = jnp.max(s, axis=1, keepdims=True)
        e = jnp.exp(s - m)
        o = jnp.dot(e.astype(BF16), vb, preferred_element_type=F32) * (1.0 / jnp.sum(e, axis=1, keepdims=True))
        for g in range(G_B):
            o_ref[pl.ds(r0, cq), g * HEAD_DIM:(g + 1) * HEAD_DIM] = o[g * cq:(g + 1) * cq].astype(o_ref.dtype)
        return 0

    lax.fori_loop(0, nq, chunk, 0, unroll=_BAND_UNROLL if nq % _BAND_UNROLL == 0 else 1)


def _band_bias(rel_table, cq, win):
    rel = jnp.arange(cq)[:, None] + (win - cq) - jnp.arange(win)[None, :]
    bias = rel_table[jnp.clip(rel, -REL_CLIP, REL_CLIP) + REL_CLIP].astype(F32)
    return bias.transpose(2, 0, 1)


def _band_attention(q, kpad, vpad, bias, *, batch, nq, cq, win, chunked, lo_static):
    t, hd = q.shape
    rows = kpad.shape[1]
    qw = G_B * HEAD_DIM
    qspec = pl.BlockSpec((nq * cq, qw), lambda b, h: (b, h))
    kspec = pl.BlockSpec((None, rows, HEAD_DIM), lambda b, h: (b, 0, h))
    bspec = pl.BlockSpec((G_B, cq, win), lambda b, h: (h, 0, 0))
    return pl.pallas_call(
        functools.partial(_band_kernel, cq=cq, win=win, nq=nq, chunked=chunked, lo_static=lo_static),
        grid=(batch, N_KV_B),
        in_specs=[qspec, kspec, kspec, bspec],
        out_specs=qspec,
        out_shape=jax.ShapeDtypeStruct((t, hd), BF16),
        compiler_params=_params("parallel", "parallel"),
    )(q, kpad, vpad, bias)


def _split3_dot_nt(a, b):
    a_hi = a.astype(BF16)
    a_lo = (a - a_hi.astype(F32)).astype(BF16)
    b_hi = b.astype(BF16)
    b_lo = (b - b_hi.astype(F32)).astype(BF16)
    return (lax.dot_general(a_hi, b_hi, _NT, preferred_element_type=F32)
            + lax.dot_general(a_lo, b_hi, _NT, preferred_element_type=F32)
            + lax.dot_general(a_hi, b_lo, _NT, preferred_element_type=F32))


def _take_lanes(x, idx):
    return jnp.take_along_axis(x, idx, axis=1, mode="promise_in_bounds")


_ROUTE_ROWS = 64
_ROUTE_UNROLL = 16


def _route_kernel(q_ref, sk_ref, pairs_ref, e_ref, g_ref, s_ref, left_ref, si_ref, cleft_ref, fpos_ref):
    tm = q_ref.shape[0]
    dh = sk_ref.shape[-1]
    nset = 2 * PEER_HEADS
    lane_c = lax.broadcasted_iota(jnp.int32, (_ROUTE_ROWS, PEER_KEYS), 1)

    for n in range(nset):
        sc = _split3_dot_nt(q_ref[:, n * dh:(n + 1) * dh], sk_ref[n // 2, n % 2])
        s_ref[n * tm:(n + 1) * tm, :] = sc
        left_ref[n * tm:(n + 1) * tm, :] = sc
    si_ref[...] = jnp.zeros(si_ref.shape, jnp.int32)

    def body1(i, _):
        def piece(r, _):
            rows = pl.ds(pl.multiple_of(r * _ROUTE_ROWS, _ROUTE_ROWS), _ROUTE_ROWS)
            left = left_ref[rows, :]
            pos = jnp.argmax(left, axis=1, keepdims=True).astype(jnp.int32)
            left_ref[rows, :] = jnp.where(lane_c == pos, NEG_INF, left)
            si_ref[rows, :] = jnp.where(lane_c == i, pos, si_ref[rows, :])
            return 0

        return lax.fori_loop(0, nset * tm // _ROUTE_ROWS, piece, 0, unroll=_ROUTE_UNROLL)

    lax.fori_loop(0, PEER_TOPK, body1, 0)
    si = si_ref[...]
    sv = _take_lanes(s_ref[...], si)
    si = si.astype(F32)

    pair_a = jnp.broadcast_to(pairs_ref[0:1, :], (tm, PEER_KEYS))
    pair_b = jnp.broadcast_to(pairs_ref[1:2, :], (tm, PEER_KEYS))
    pair_ok = jnp.broadcast_to(pairs_ref[2:3, :], (tm, PEER_KEYS)) > 0
    cand, cidx = [], []
    for h in range(PEER_HEADS):
        r0 = slice((2 * h) * tm, (2 * h + 1) * tm)
        r1 = slice((2 * h + 1) * tm, (2 * h + 2) * tm)
        cand.append(jnp.where(pair_ok, _take_lanes(sv[r0], pair_a) + _take_lanes(sv[r1], pair_b), NEG_INF))
        cidx.append(_take_lanes(si[r0], pair_a) * float(PEER_KEYS) + _take_lanes(si[r1], pair_b))
    cand = jnp.concatenate(cand, axis=0)
    cidx = jnp.concatenate(cidx, axis=0)

    lane2 = lax.broadcasted_iota(jnp.int32, cand.shape, 1)
    head_lane0 = (lax.broadcasted_iota(jnp.int32, cand.shape, 0) // tm) * PEER_TOPK
    cleft_ref[...] = cand
    fpos_ref[...] = jnp.zeros(fpos_ref.shape, jnp.int32)

    def body2(i, _):
        def piece(r, _):
            row0 = pl.multiple_of(r * _ROUTE_ROWS, _ROUTE_ROWS)
            rows = pl.ds(row0, _ROUTE_ROWS)
            left = cleft_ref[rows, :]
            pos = jnp.argmax(left, axis=1, keepdims=True).astype(jnp.int32)
            cleft_ref[rows, :] = jnp.where(lane_c == pos, NEG_INF, left)
            out_lane = (row0 // tm) * PEER_TOPK + i
            fpos_ref[rows, :] = jnp.where(lane_c == out_lane, pos, fpos_ref[rows, :])
            return 0

        return lax.fori_loop(0, PEER_HEADS * tm // _ROUTE_ROWS, piece, 0, unroll=_ROUTE_UNROLL)

    lax.fori_loop(0, PEER_TOPK, body2, 0)
    fpos = fpos_ref[...]
    fv = _take_lanes(cand, fpos)
    fe = _take_lanes(cidx, fpos)

    grp = lane2 // PEER_TOPK == head_lane0 // PEER_TOPK
    fe = jnp.where(grp, fe, 0.0)
    mx = jnp.max(jnp.where(grp, fv, NEG_INF), axis=1, keepdims=True)
    ex = jnp.where(grp, jnp.exp(fv - mx), 0.0)
    gate = ex / jnp.sum(ex, axis=1, keepdims=True)
    e_out = fe[0:tm]
    g_out = gate[0:tm]
    for h in range(1, PEER_HEADS):
        e_out = e_out + fe[h * tm:(h + 1) * tm]
        g_out = g_out + gate[h * tm:(h + 1) * tm]
    e_ref[...] = e_out.astype(jnp.int32)
    g_ref[...] = g_out


def _peer_route(qp, sub_keys):
    m, kq = qp.shape
    tm = min(256, m)
    ne = PEER_HEADS * PEER_TOPK
    ospec = pl.BlockSpec((tm, ne), lambda i: (i, 0))
    pairs = [(a, b) for a in range(PEER_TOPK) for b in range(PEER_TOPK) if (a + 1) * (b + 1) <= PEER_TOPK]
    pad = [0] * (PEER_KEYS - len(pairs))
    pair_rows = jnp.array([[a for a, _ in pairs] + pad, [b for _, b in pairs] + pad,
                           [1] * len(pairs) + pad] + [[0] * PEER_KEYS] * 5, jnp.int32)
    return pl.pallas_call(
        _route_kernel,
        grid=(m // tm,),
        in_specs=[pl.BlockSpec((tm, kq), lambda i: (i, 0)),
                  pl.BlockSpec(sub_keys.shape, lambda i: (0, 0, 0, 0)),
                  pl.BlockSpec(pair_rows.shape, lambda i: (0, 0))],
        out_specs=[ospec, ospec],
        out_shape=[jax.ShapeDtypeStruct((m, ne), jnp.int32), jax.ShapeDtypeStruct((m, ne), F32)],
        scratch_shapes=[pltpu.VMEM((2 * PEER_HEADS * tm, PEER_KEYS), F32),
                        pltpu.VMEM((2 * PEER_HEADS * tm, PEER_KEYS), F32),
                        pltpu.VMEM((2 * PEER_HEADS * tm, PEER_KEYS), jnp.int32),
                        pltpu.VMEM((PEER_HEADS * tm, PEER_KEYS), F32),
                        pltpu.VMEM((PEER_HEADS * tm, PEER_KEYS), jnp.int32)],
        compiler_params=_params("parallel"),
    )(qp, sub_keys, pair_rows)


def _gelu_tanh(x):
    return 0.5 * x * (1.0 + jnp.tanh(0.7978845608028654 * (x + 0.044715 * (x * x * x))))


_TOKEN_GROUP = 16


def _expert_kernel(h_ref, gain_ref, e_ref, g_ref, tab_ref, o_ref,
                   xn_ref, hi_ref, lo_ref, act_ref, d_ref, w3_ref, *, nj):
    j = pl.program_id(1)
    tm = h_ref.shape[0]
    te = tab_ref.shape[0]
    nb = te // LANES

    @pl.when(j == 0)
    def _():
        xn_ref[...] = _norm_rows(h_ref[...], gain_ref[...]).astype(BF16)
        o_ref[...] = h_ref[...]
        e = e_ref[...]
        hi_ref[...] = e // LANES
        lo_ref[...] = e % LANES
        act_ref[...] = jnp.zeros(act_ref.shape, F32)

    def score_slab(slot):
        d_ref[slot] = lax.dot_general(xn_ref[...], tab_ref[...], _NT, preferred_element_type=F32)

    def pick_from_slab(slot, slab):
        hi = hi_ref[...]
        lo = lo_ref[...]
        act = act_ref[...]
        for s in range(nb):
            picked = _take_lanes(d_ref[slot, :, s * LANES:(s + 1) * LANES], lo)
            act = jnp.where(hi == slab * nb + s, picked, act)
        act_ref[...] = act

    @pl.when(j == 0)
    def _():
        score_slab(0)

    @pl.when(jnp.logical_and(j >= 1, j < nj))
    def _():
        pick_from_slab((j - 1) % 2, j - 1)
        score_slab(j % 2)

    @pl.when(j == nj)
    def _():
        pick_from_slab((nj - 1) % 2, nj - 1)
        act_ref[...] = g_ref[...] * _gelu_tanh(act_ref[...])
        sub = lax.broadcasted_iota(jnp.int32, (LANES, LANES), 0)

        def group(gi, _):
            t0 = pl.multiple_of(gi * _TOKEN_GROUP, _TOKEN_GROUP)
            grids = []
            for u in range(_TOKEN_GROUP):
                hi_row = hi_ref[pl.ds(t0 + u, 1), :]
                lo_row = lo_ref[pl.ds(t0 + u, 1), :]
                w_row = act_ref[pl.ds(t0 + u, 1), :]
                a_t = jnp.where(hi_row == sub, w_row, 0.0).astype(BF16)
                b_t = jnp.where(lo_row == sub, 1.0, 0.0).astype(BF16)
                grids.append(lax.dot_general(a_t, b_t, _NT, preferred_element_type=F32))
            w3_ref[:, pl.ds(t0, _TOKEN_GROUP), :] = pltpu.einshape(
                "uik->iuk", jnp.stack(grids, axis=0)).astype(BF16)
            return 0

        lax.fori_loop(0, tm // _TOKEN_GROUP, group, 0)

    @pl.when(j >= nj)
    def _():
        jj = j - nj
        w = jnp.concatenate([w3_ref[jj * nb + s] for s in range(nb)], axis=1)
        d = o_ref.shape[1]
        for c0 in range(0, d, _EXPERT_OUT_COLS):
            cols = slice(c0, c0 + _EXPERT_OUT_COLS)
            o_ref[:, cols] += jnp.dot(w, tab_ref[:, cols], preferred_element_type=F32)


_EXPERT_OUT_COLS = 1024


def _peer_experts(h, gain, eidx, gate, uv_tab, layer, *, te=512):
    m, d = h.shape
    ne = uv_tab.shape[2]
    nsel = eidx.shape[1]
    tm = min(512, m)
    nj = ne // te
    once = pl.Buffered(1)
    sel_spec = pl.BlockSpec((tm, nsel), lambda i, j: (i, 0), pipeline_mode=once)
    return pl.pallas_call(
        functools.partial(_expert_kernel, nj=nj),
        grid=(m // tm, 2 * nj),
        in_specs=[pl.BlockSpec((tm, d), lambda i, j: (i, 0)),
                  pl.BlockSpec((1, d), lambda i, j: (0, 0), pipeline_mode=once),
                  sel_spec, sel_spec,
                  pl.BlockSpec((None, None, te, d), lambda i, j: (layer, j // nj, j % nj, 0))],
        out_specs=pl.BlockSpec((tm, d), lambda i, j: (i, 0), pipeline_mode=once),
        out_shape=jax.ShapeDtypeStruct((m, d), F32),
        scratch_shapes=[pltpu.VMEM((tm, d), BF16),
                        pltpu.VMEM((tm, nsel), jnp.int32),
                        pltpu.VMEM((tm, nsel), jnp.int32),
                        pltpu.VMEM((tm, nsel), F32),
                        pltpu.VMEM((2, tm, te), F32),
                        pltpu.VMEM((ne // LANES, tm, LANES), BF16)],
        compiler_params=_params("parallel", "arbitrary"),
    )(h, gain.reshape(1, d), eidx, gate, uv_tab)


def _peer_ffn(h, i, prm):
    qp = _matmul(h, prm["w_q_peer"][i], gain=prm["g_ffn"][i])
    eidx, gate = _peer_route(qp, prm["peer_sub_keys"][i])
    return _peer_experts(h, prm["g_ffn"][i], eidx, gate, prm["peer_uv"], i)


def _ple(h, p, i, prm):
    return _matmul(h, prm["w_ple_gate"][i], gain=prm["g_ple"][i], residual=h,
                   ple=(p, prm["w_ple_proj"][i]))


def _trunk(x, pe, past, prm):
    b, t, d = x.shape
    m = b * t
    h = x.reshape(m, d)
    pe = pe.reshape(pe.shape[0], m, pe.shape[-1])

    q, k, v = _qkv_proj(h, prm["w_qkv_a"], prm["g_mix"][0])
    if past is None:
        o = _sb_prompt(q, k, v, batch=b, seq=t)
    else:
        o = _sb_sample(q, k, v, past[0][0], past[1][0], batch=b, tq=t)
    h = _matmul(o, prm["w_o_a"], residual=h)
    h = _peer_ffn(h, 0, prm)
    h = _ple(h, pe[0], 0, prm)

    kv = _matmul(h, prm["w_kv_b"], gain=prm["g_kv"])
    nkv = N_KV_B * HEAD_DIM
    kb_new = kv[:, :nkv].reshape(b, t, nkv)
    vb_new = kv[:, nkv:].reshape(b, t, nkv)

    qb = _matmul(h, prm["w_q_b"], gain=prm["g_mix"][1], out_dtype=BF16, out_scale=SCALE)
    win = (N_PREV_CHUNKS + 2) * CHUNK
    if past is None:
        front = (N_PREV_CHUNKS + 1) * CHUNK
        kpad = jnp.pad(kb_new.astype(BF16), ((0, 0), (front, 0), (0, 0)))
        vpad = jnp.pad(vb_new.astype(BF16), ((0, 0), (front, 0), (0, 0)))
        bias = _band_bias(prm["rel_bias_b"][0], CHUNK, win)
        ob = _band_attention(qb, kpad, vpad, bias, batch=b, nq=t // CHUNK, cq=CHUNK, win=win,
                             chunked=True, lo_static=0)
    else:
        ck = past[2].reshape(b, -1, nkv)
        cv = past[3].reshape(b, -1, nkv)
        front = win - ck.shape[1] - t
        kpad = jnp.pad(jnp.concatenate([ck, kb_new], axis=1).astype(BF16), ((0, 0), (front, 0), (0, 0)))
        vpad = jnp.pad(jnp.concatenate([cv, vb_new], axis=1).astype(BF16), ((0, 0), (front, 0), (0, 0)))
        bias = _band_bias(prm["rel_bias_b"][0], t, win)
        ob = _band_attention(qb, kpad, vpad, bias, batch=b, nq=1, cq=t, win=win,
                             chunked=False, lo_static=front)
    h = _matmul(ob, prm["w_o_b"], residual=h)
    h = _peer_ffn(h, 1, prm)
    h = _ple(h, pe[1], 1, prm)

    y = _rmsnorm(h, prm["g_final"]).reshape(b, t, d)
    heads = k.shape[1] // HEAD_DIM
    a_k = k.reshape(1, b, t, heads, HEAD_DIM)
    a_v = v.reshape(1, b, t, heads, HEAD_DIM)
    if past is None:
        keep = min(N_PREV_CHUNKS * CHUNK, t)
        b_k, b_v = kb_new[:, t - keep:], vb_new[:, t - keep:]
    else:
        b_k, b_v = kb_new, vb_new
    b_k = b_k.reshape(b, -1, N_KV_B, HEAD_DIM)
    b_v = b_v.reshape(b, -1, N_KV_B, HEAD_DIM)
    return y, a_k, a_v, b_k, b_v


def kernel(x_prompt, x_sample, cache_a_k, cache_a_v, cache_b_k, cache_b_v, p_prompt, p_sample, g_mix, w_qkv_a, w_o_a, g_kv, w_kv_b, w_q_b, rel_bias_b, w_o_b, g_ffn, w_q_peer, peer_sub_keys, peer_u, peer_v, g_ple, w_ple_gate, w_ple_proj, g_final):
    prm = dict(
        g_mix=g_mix, g_kv=g_kv, g_ffn=g_ffn, g_ple=g_ple, g_final=g_final,
        w_qkv_a=_to_bf16(w_qkv_a)[0],
        w_o_a=_to_bf16(w_o_a)[0],
        w_kv_b=_to_bf16(w_kv_b),
        w_q_b=_to_bf16(w_q_b)[0],
        w_o_b=_to_bf16(w_o_b)[0],
        rel_bias_b=rel_bias_b,
        w_q_peer=_to_bf16(w_q_peer),
        peer_sub_keys=peer_sub_keys,
        peer_uv=_peer_tables_bf16(peer_u, peer_v),
        w_ple_gate=_to_bf16(w_ple_gate),
        w_ple_proj=_to_bf16(w_ple_proj),
    )
    y_p, ak_p, av_p, bk_p, bv_p = _trunk(x_prompt, p_prompt, None, prm)
    y_s, ak_s, av_s, bk_s, bv_s = _trunk(x_sample, p_sample,
                                         (cache_a_k, cache_a_v, cache_b_k, cache_b_v), prm)
    return (y_p, y_s, ak_p, av_p, bk_p, bv_p, ak_s, av_s, bk_s, bv_s)
```

```python
import functools

import jax
import jax.numpy as jnp
import numpy as np
from jax import lax
from jax.experimental import pallas as pl
from jax.experimental.pallas import tpu as pltpu

F32 = jnp.float32
BF16 = jnp.bfloat16

EPS = 1e-6
HEAD_DIM = 128
CHUNK = 64
N_PREV_CHUNKS = 8
N_KV_B = 8
G_B = 4
REL_CLIP = 128
PEER_HEADS = 8
PEER_KEYS = 128
PEER_TOPK = 16
SCALE = HEAD_DIM ** -0.5
NEG_INF = float("-inf")

VMEM_LIMIT_BYTES = 60 * 1024 * 1024
LANES = 128

_NT = (((1,), (1,)), ((), ()))


def _params(*sem):
    return pltpu.CompilerParams(dimension_semantics=sem, vmem_limit_bytes=VMEM_LIMIT_BYTES)


def _norm_rows(x, g):
    ms = jnp.mean(x * x, axis=-1, keepdims=True)
    return x * lax.rsqrt(ms + EPS) * g


def _sigmoid(x):
    return 1.0 / (1.0 + jnp.exp(-x))


def _mm_kernel(*refs, norm, res, ple, out_scale):
    it = iter(refs)
    x_ref = next(it)
    g_ref = next(it) if norm else None
    w_ref = next(it)
    res_ref = next(it) if res else None
    p_ref = next(it) if ple else None
    wp_ref = next(it) if ple else None
    o_ref = next(it)
    xn_ref = next(it) if norm else None

    if norm:
        @pl.when(pl.program_id(1) == 0)
        def _():
            xn_ref[...] = _norm_rows(x_ref[...], g_ref[...]).astype(BF16)
        x = xn_ref[...]
    else:
        x = x_ref[...]
    acc = jnp.dot(x, w_ref[...], preferred_element_type=F32)
    if ple:
        proj = jnp.dot(p_ref[...].astype(BF16), wp_ref[...], preferred_element_type=F32)
        acc = _sigmoid(acc) * proj
    if res:
        acc = acc + res_ref[...]
    if out_scale is not None:
        acc = acc * out_scale
    o_ref[...] = acc.astype(o_ref.dtype)


def _matmul(x, w, *, gain=None, residual=None, ple=None, out_dtype=F32, out_scale=None, tn=1024):
    m, k = x.shape
    n = w.shape[1]
    tm = min(512, m)
    tn = min(tn, n)
    norm = gain is not None
    in_specs = [pl.BlockSpec((tm, k), lambda i, j: (i, 0))]
    args = [x]
    if norm:
        in_specs.append(pl.BlockSpec((1, k), lambda i, j: (0, 0)))
        args.append(gain.reshape(1, k))
    in_specs.append(pl.BlockSpec((k, tn), lambda i, j: (0, j)))
    args.append(w)
    if residual is not None:
        in_specs.append(pl.BlockSpec((tm, tn), lambda i, j: (i, j)))
        args.append(residual)
    if ple is not None:
        p, wp = ple
        kp = p.shape[1]
        in_specs.append(pl.BlockSpec((tm, kp), lambda i, j: (i, 0)))
        in_specs.append(pl.BlockSpec((kp, tn), lambda i, j: (0, j)))
        args += [p, wp]
    return pl.pallas_call(
        functools.partial(_mm_kernel, norm=norm, res=residual is not None, ple=ple is not None,
                          out_scale=out_scale),
        grid=(m // tm, n // tn),
        in_specs=in_specs,
        out_specs=pl.BlockSpec((tm, tn), lambda i, j: (i, j)),
        out_shape=jax.ShapeDtypeStruct((m, n), out_dtype),
        scratch_shapes=[pltpu.VMEM((tm, k), BF16)] if norm else [],
        compiler_params=_params("parallel", "arbitrary"),
    )(*args)


def _qkv_kernel(x_ref, g_ref, w_ref, q_ref, k_ref, v_ref, xn_ref, *, nq):
    j = pl.program_id(1)

    @pl.when(j == 0)
    def _():
        xn_ref[...] = _norm_rows(x_ref[...], g_ref[...]).astype(BF16)

    acc = jnp.dot(xn_ref[...], w_ref[...], preferred_element_type=F32)

    @pl.when(j < nq)
    def _():
        q_ref[...] = (acc * SCALE).astype(q_ref.dtype)

    @pl.when(jnp.logical_and(j >= nq, j < 2 * nq))
    def _():
        k_ref[...] = acc

    @pl.when(j >= 2 * nq)
    def _():
        v_ref[...] = acc


def _qkv_proj(x, w, gain, *, tn=1024):
    m, k = x.shape
    n = w.shape[1] // 3
    tm = min(512, m)
    nq = n // tn

    def ospec(first):
        return pl.BlockSpec((tm, tn), lambda i, j: (i, jnp.clip(j - first, 0, nq - 1)))

    return pl.pallas_call(
        functools.partial(_qkv_kernel, nq=nq),
        grid=(m // tm, 3 * nq),
        in_specs=[pl.BlockSpec((tm, k), lambda i, j: (i, 0)),
                  pl.BlockSpec((1, k), lambda i, j: (0, 0)),
                  pl.BlockSpec((k, tn), lambda i, j: (0, j))],
        out_specs=[ospec(0), ospec(nq), ospec(2 * nq)],
        out_shape=[jax.ShapeDtypeStruct((m, n), BF16), jax.ShapeDtypeStruct((m, n), F32),
                   jax.ShapeDtypeStruct((m, n), F32)],
        scratch_shapes=[pltpu.VMEM((tm, k), BF16)],
        compiler_params=_params("parallel", "arbitrary"),
    )(x, gain.reshape(1, k), w)


def _cast_kernel(x_ref, o_ref):
    o_ref[...] = x_ref[...].astype(o_ref.dtype)


_CAST_ROWS = 256
_CAST_COLS = 4096


def _to_bf16(x):
    x2 = x.reshape(-1, x.shape[-1])
    rows, cols = x2.shape
    tr = min(_CAST_ROWS, rows)
    tc = min(_CAST_COLS, cols)
    spec = pl.BlockSpec((tr, tc), lambda i, j: (i, j))
    out = pl.pallas_call(
        _cast_kernel,
        grid=(rows // tr, cols // tc),
        in_specs=[spec],
        out_specs=spec,
        out_shape=jax.ShapeDtypeStruct((rows, cols), BF16),
        compiler_params=_params("parallel", "parallel"),
    )(x2)
    return out.reshape(x.shape)


def _stack_cast_kernel(u_ref, v_ref, o_ref):
    c = pl.program_id(1)

    @pl.when(c == 0)
    def _():
        o_ref[...] = u_ref[...].astype(o_ref.dtype)

    @pl.when(c == 1)
    def _():
        o_ref[...] = v_ref[...].astype(o_ref.dtype)


def _peer_tables_bf16(u, v):
    nl, ne, d = u.shape
    te = _CAST_ROWS
    nblk = ne // te
    uspec = pl.BlockSpec((None, te, d), lambda l, c, e: (l, jnp.where(c == 0, e, nblk - 1), 0))
    vspec = pl.BlockSpec((None, te, d), lambda l, c, e: (l, jnp.where(c == 1, e, 0), 0))
    return pl.pallas_call(
        _stack_cast_kernel,
        grid=(nl, 2, nblk),
        in_specs=[uspec, vspec],
        out_specs=pl.BlockSpec((None, None, te, d), lambda l, c, e: (l, c, e, 0)),
        out_shape=jax.ShapeDtypeStruct((nl, 2, ne, d), BF16),
        compiler_params=_params("parallel", "arbitrary", "arbitrary"),
    )(u, v)


def _rmsnorm_kernel(x_ref, g_ref, o_ref):
    o_ref[...] = _norm_rows(x_ref[...], g_ref[...])


def _rmsnorm(x, gain):
    m, k = x.shape
    tm = min(512, m)
    return pl.pallas_call(
        _rmsnorm_kernel,
        grid=(m // tm,),
        in_specs=[pl.BlockSpec((tm, k), lambda i: (i, 0)), pl.BlockSpec((1, k), lambda i: (0, 0))],
        out_specs=pl.BlockSpec((tm, k), lambda i: (i, 0)),
        out_shape=jax.ShapeDtypeStruct((m, k), F32),
        compiler_params=_params("parallel"),
    )(x, gain.reshape(1, k))


def _strict_lower_ones(n):
    r = lax.broadcasted_iota(jnp.int32, (2 * n, n), 0) % n
    c = lax.broadcasted_iota(jnp.int32, (2 * n, n), 1)
    return jnp.where(r > c, 1.0, 0.0).astype(BF16)


_SB_DEAD_BELOW = -104.0


def _sb_log_terms(q, kb, mask):
    z = lax.dot_general(q, kb, _NT, preferred_element_type=F32)
    nz = -z
    sp = jnp.log(1.0 + jnp.exp(-jnp.maximum(z, nz)))
    log_beta = jnp.minimum(z, 0.0) - sp
    log_keep = jnp.minimum(nz, 0.0) - sp
    if mask is not None:
        log_keep = jnp.where(mask, log_keep, 0.0)
    return log_beta, log_keep


def _sb_block(q, kb, vb, c, acc, tri2, mask):
    log_beta, log_keep = _sb_log_terms(q, kb, mask)
    hi = log_keep.astype(BF16)
    lo = (log_keep - hi.astype(F32)).astype(BF16)
    after = jnp.dot(jnp.concatenate([hi, lo], axis=1), tri2, preferred_element_type=F32)
    w = jnp.exp(log_beta + after + c)
    if mask is not None:
        w = jnp.where(mask, w, 0.0)
    acc = acc + jnp.dot(w.astype(BF16), vb, preferred_element_type=F32)
    c = c + jnp.sum(log_keep, axis=1, keepdims=True)
    return c, acc


_SB_QUERY_GROUP = 4


def _sb_prompt_kernel(q_ref, k_ref, v_ref, o_ref, kb_ref, vb_ref, *, blk):
    seq = q_ref.shape[0]
    kb_ref[pl.ds(0, blk), :] = jnp.zeros((blk, HEAD_DIM), BF16)
    vb_ref[pl.ds(0, blk), :] = jnp.zeros((blk, HEAD_DIM), BF16)
    kb_ref[pl.ds(blk, seq), :] = k_ref[...].astype(BF16)
    vb_ref[pl.ds(blk, seq), :] = v_ref[...].astype(BF16)
    tri = _strict_lower_ones(blk)
    row = lax.broadcasted_iota(jnp.int32, (blk, blk), 0)
    col = lax.broadcasted_iota(jnp.int32, (blk, blk), 1)
    diag_mask = col < row

    def kv_block(j):
        r0 = pl.multiple_of((j + 1) * blk, blk)
        return kb_ref[pl.ds(r0, blk), :], vb_ref[pl.ds(r0, blk), :]

    def older_blocks(i, q, c, acc):
        def k_cond(carry):
            j, cmax, _, _ = carry
            return jnp.logical_and(j >= 0, cmax > _SB_DEAD_BELOW)

        def k_body(carry):
            j, _, c, acc = carry
            c, acc = _sb_block(q, *kv_block(j), c, acc, tri, None)
            return j - 1, jnp.max(c), c, acc

        return lax.while_loop(k_cond, k_body, (i - 2, jnp.max(c), c, acc))[3]

    def group_body(p, _):
        state = []
        for i in [_SB_QUERY_GROUP * p + u for u in range(_SB_QUERY_GROUP)]:
            q = q_ref[pl.ds(pl.multiple_of(i * blk, blk), blk), :]
            c = jnp.zeros((blk, 1), F32)
            acc = jnp.zeros((blk, HEAD_DIM), F32)
            c, acc = _sb_block(q, *kv_block(i), c, acc, tri, diag_mask)
            c, acc = _sb_block(q, *kv_block(i - 1), c, acc, tri, None)
            state.append((i, q, c, acc))
        for i, q, c, acc in state:
            acc = older_blocks(i, q, c, acc)
            o_ref[pl.ds(pl.multiple_of(i * blk, blk), blk), :] = acc.astype(o_ref.dtype)
        return 0

    lax.fori_loop(0, seq // (_SB_QUERY_GROUP * blk), group_body, 0)


def _sb_prompt(q, k, v, *, batch, seq, blk=256):
    t, hd = q.shape
    heads = hd // HEAD_DIM
    blk = min(blk, seq)
    spec = pl.BlockSpec((seq, HEAD_DIM), lambda b, h: (b, h))
    return pl.pallas_call(
        functools.partial(_sb_prompt_kernel, blk=blk),
        grid=(batch, heads),
        in_specs=[spec, spec, spec],
        out_specs=spec,
        out_shape=jax.ShapeDtypeStruct((t, hd), BF16),
        scratch_shapes=[pltpu.VMEM((seq + blk, HEAD_DIM), BF16), pltpu.VMEM((seq + blk, HEAD_DIM), BF16)],
        compiler_params=_params("parallel", "parallel"),
    )(q, k, v)


_SB_HEAD_GROUP = 32


def _sb_probe_kernel(q_ref, kn_ref, kc_ref, alive_ref, *, blk):
    tq = q_ref.shape[0]
    row = lax.broadcasted_iota(jnp.int32, (tq, blk), 0)
    col = lax.broadcasted_iota(jnp.int32, (tq, blk), 1)
    kb = pltpu.einshape("thd->htd", kc_ref[...]).astype(BF16)
    cmax = jnp.full((tq, 1), NEG_INF, F32)
    for g in range(_SB_HEAD_GROUP):
        lanes = slice(g * HEAD_DIM, (g + 1) * HEAD_DIM)
        q = q_ref[:, lanes]
        c = jnp.sum(_sb_log_terms(q, kn_ref[:, lanes], col < row)[1], axis=1, keepdims=True)
        c = c + jnp.sum(_sb_log_terms(q, kb[g], None)[1], axis=1, keepdims=True)
        cmax = jnp.maximum(cmax, c)
    flag = jnp.where(jnp.max(cmax) > _SB_DEAD_BELOW - 1.0, 1, 0)
    alive_ref[...] = jnp.full(alive_ref.shape, flag, jnp.int32)


def _sb_sample_kernel(alive_ref, q_ref, kn_ref, vn_ref, kc_ref, vc_ref, o_ref, c_ref, acc_ref, *, blk):
    del alive_ref
    n = pl.program_id(2)
    tq = q_ref.shape[0]
    tri = _strict_lower_ones(blk)

    @pl.when(n == 0)
    def _():
        row = lax.broadcasted_iota(jnp.int32, (tq, blk), 0)
        col = lax.broadcasted_iota(jnp.int32, (tq, blk), 1)
        for g in range(_SB_HEAD_GROUP):
            lanes = slice(g * HEAD_DIM, (g + 1) * HEAD_DIM)
            c, acc = _sb_block(q_ref[:, lanes], kn_ref[:, lanes], vn_ref[:, lanes],
                               jnp.zeros((tq, 1), F32), jnp.zeros((tq, HEAD_DIM), F32), tri, col < row)
            c_ref[g] = c
            acc_ref[g] = acc

    @pl.when(jnp.max(c_ref[...]) > _SB_DEAD_BELOW)
    def _():
        kb = pltpu.einshape("thd->htd", kc_ref[...]).astype(BF16)
        vb = pltpu.einshape("thd->htd", vc_ref[...]).astype(BF16)
        for g in range(_SB_HEAD_GROUP):
            lanes = slice(g * HEAD_DIM, (g + 1) * HEAD_DIM)
            c, acc = _sb_block(q_ref[:, lanes], kb[g], vb[g], c_ref[g], acc_ref[g], tri, None)
            c_ref[g] = c
            acc_ref[g] = acc

    @pl.when(n == pl.num_programs(2) - 1)
    def _():
        for g in range(_SB_HEAD_GROUP):
            o_ref[:, g * HEAD_DIM:(g + 1) * HEAD_DIM] = acc_ref[g].astype(o_ref.dtype)


def _sb_sample(q, k_new, v_new, cache_k, cache_v, *, batch, tq, blk=256):
    t, hd = q.shape
    heads = hd // HEAD_DIM
    past = cache_k.shape[1]
    nblk = past // blk
    gw = _SB_HEAD_GROUP * HEAD_DIM

    def pad_block(a):
        a = a.reshape(batch, tq, hd)
        a = jnp.pad(a, ((0, 0), (0, blk - tq), (0, 0)))
        return a.reshape(batch * blk, hd).astype(BF16)

    assert heads == _SB_HEAD_GROUP
    kn = pad_block(k_new)
    vn = pad_block(v_new)
    newest = (None, blk, _SB_HEAD_GROUP, HEAD_DIM)
    alive = pl.pallas_call(
        functools.partial(_sb_probe_kernel, blk=blk),
        grid=(batch,),
        in_specs=[pl.BlockSpec((tq, gw), lambda b: (b, 0)),
                  pl.BlockSpec((blk, gw), lambda b: (b, 0)),
                  pl.BlockSpec(newest, lambda b: (b, nblk - 1, 0, 0))],
        out_specs=pl.BlockSpec((None, 8, LANES), lambda b: (b, 0, 0)),
        out_shape=jax.ShapeDtypeStruct((batch, 8, LANES), jnp.int32),
        compiler_params=_params("parallel"),
    )(q, kn, cache_k)[:, 0, 0]

    def cache_block(b, h, n, alive_ref):
        return (b, nblk - 1 - jnp.where(alive_ref[b] > 0, n, 0), h, 0)

    qspec = pl.BlockSpec((tq, gw), lambda b, h, n, alive_ref: (b, h))
    nspec = pl.BlockSpec((blk, gw), lambda b, h, n, alive_ref: (b, h))
    cspec = pl.BlockSpec(newest, cache_block)
    return pl.pallas_call(
        functools.partial(_sb_sample_kernel, blk=blk),
        grid_spec=pltpu.PrefetchScalarGridSpec(
            num_scalar_prefetch=1,
            grid=(batch, heads // _SB_HEAD_GROUP, nblk),
            in_specs=[qspec, nspec, nspec, cspec, cspec],
            out_specs=qspec,
            scratch_shapes=[pltpu.VMEM((_SB_HEAD_GROUP, tq, 1), F32),
                            pltpu.VMEM((_SB_HEAD_GROUP, tq, HEAD_DIM), F32)]),
        out_shape=jax.ShapeDtypeStruct((t, hd), BF16),
        compiler_params=_params("parallel", "parallel", "arbitrary"),
    )(alive, q, kn, vn, cache_k, cache_v)


_BAND_UNROLL = 4


def _band_kernel(q_ref, k_ref, v_ref, bias_ref, o_ref, *, cq, win, nq, chunked, lo_static):
    bias = bias_ref[...].reshape(G_B * cq, win)
    col = lax.broadcasted_iota(jnp.int32, (G_B * cq, win), 1)

    def chunk(c, _):
        r0 = pl.multiple_of(c * cq, cq)
        if chunked:
            lo = jnp.maximum(CHUNK, (N_PREV_CHUNKS + 1 - c) * CHUNK)
        else:
            lo = lo_static
        kb = k_ref[pl.ds(r0, win), :]
        vb = v_ref[pl.ds(r0, win), :]
        qb = q_ref[pl.ds(r0, cq), :]
        q4 = jnp.concatenate([qb[:, g * HEAD_DIM:(g + 1) * HEAD_DIM] for g in range(G_B)], axis=0)
        s = lax.dot_general(q4, kb, _NT, preferred_element_type=F32) + bias
        s = jnp.where(col >= lo, s, NEG_INF)
        m = jnp.max(s, axis=1, keepdims=True)
        e = jnp.exp(s - m)
        o = jnp.dot(e.astype(BF16), vb, preferred_element_type=F32) * (1.0 / jnp.sum(e, axis=1, keepdims=True))
        for g in range(G_B):
            o_ref[pl.ds(r0, cq), g * HEAD_DIM:(g + 1) * HEAD_DIM] = o[g * cq:(g + 1) * cq].astype(o_ref.dtype)
        return 0

    lax.fori_loop(0, nq, chunk, 0, unroll=_BAND_UNROLL if nq % _BAND_UNROLL == 0 else 1)


def _band_bias(rel_table, cq, win):
    lg = win + cq - 1
    idx = np.clip(win - 1 - np.arange(lg), -REL_CLIP, REL_CLIP) + REL_CLIP
    g = rel_table[idx].astype(F32).T
    rows = jnp.tile(g, (1, cq + 1))[:, :cq * (lg + 1)].reshape(-1, cq, lg + 1)
    return rows[:, ::-1, :win]


def _band_attention(q, kpad, vpad, bias, *, batch, nq, cq, win, chunked, lo_static):
    t, hd = q.shape
    rows = kpad.shape[1]
    qw = G_B * HEAD_DIM
    qspec = pl.BlockSpec((nq * cq, qw), lambda b, h: (b, h))
    kspec = pl.BlockSpec((None, rows, HEAD_DIM), lambda b, h: (b, 0, h))
    bspec = pl.BlockSpec((G_B, cq, win), lambda b, h: (h, 0, 0))
    return pl.pallas_call(
        functools.partial(_band_kernel, cq=cq, win=win, nq=nq, chunked=chunked, lo_static=lo_static),
        grid=(batch, N_KV_B),
        in_specs=[qspec, kspec, kspec, bspec],
        out_specs=qspec,
        out_shape=jax.ShapeDtypeStruct((t, hd), BF16),
        compiler_params=_params("parallel", "parallel"),
    )(q, kpad, vpad, bias)


def _split3_dot_nt(a, b):
    a_hi = a.astype(BF16)
    a_lo = (a - a_hi.astype(F32)).astype(BF16)
    b_hi = b.astype(BF16)
    b_lo = (b - b_hi.astype(F32)).astype(BF16)
    return (lax.dot_general(a_hi, b_hi, _NT, preferred_element_type=F32)
            + lax.dot_general(a_lo, b_hi, _NT, preferred_element_type=F32)
            + lax.dot_general(a_hi, b_lo, _NT, preferred_element_type=F32))


def _take_lanes(x, idx):
    return jnp.take_along_axis(x, idx, axis=1, mode="promise_in_bounds")


_ROUTE_ROWS = 64
_ROUTE_UNROLL = 16


def _route_kernel(q_ref, sk_ref, pairs_ref, e_ref, g_ref, s_ref, left_ref, si_ref, cleft_ref, fpos_ref):
    tm = q_ref.shape[0]
    dh = sk_ref.shape[-1]
    nset = 2 * PEER_HEADS
    lane_c = lax.broadcasted_iota(jnp.int32, (_ROUTE_ROWS, PEER_KEYS), 1)

    for n in range(nset):
        sc = _split3_dot_nt(q_ref[:, n * dh:(n + 1) * dh], sk_ref[n // 2, n % 2])
        s_ref[n * tm:(n + 1) * tm, :] = sc
        left_ref[n * tm:(n + 1) * tm, :] = sc
    si_ref[...] = jnp.zeros(si_ref.shape, jnp.int32)

    def body1(i, _):
        def piece(r, _):
            rows = pl.ds(pl.multiple_of(r * _ROUTE_ROWS, _ROUTE_ROWS), _ROUTE_ROWS)
            left = left_ref[rows, :]
            pos = jnp.argmax(left, axis=1, keepdims=True).astype(jnp.int32)
            left_ref[rows, :] = jnp.where(lane_c == pos, NEG_INF, left)
            si_ref[rows, :] = jnp.where(lane_c == i, pos, si_ref[rows, :])
            return 0

        return lax.fori_loop(0, nset * tm // _ROUTE_ROWS, piece, 0, unroll=_ROUTE_UNROLL)

    lax.fori_loop(0, PEER_TOPK, body1, 0)
    si = si_ref[...]
    sv = _take_lanes(s_ref[...], si)
    si = si.astype(F32)

    pair_a = jnp.broadcast_to(pairs_ref[0:1, :], (tm, PEER_KEYS))
    pair_b = jnp.broadcast_to(pairs_ref[1:2, :], (tm, PEER_KEYS))
    pair_ok = jnp.broadcast_to(pairs_ref[2:3, :], (tm, PEER_KEYS)) > 0
    cand, cidx = [], []
    for h in range(PEER_HEADS):
        r0 = slice((2 * h) * tm, (2 * h + 1) * tm)
        r1 = slice((2 * h + 1) * tm, (2 * h + 2) * tm)
        cand.append(jnp.where(pair_ok, _take_lanes(sv[r0], pair_a) + _take_lanes(sv[r1], pair_b), NEG_INF))
        cidx.append(_take_lanes(si[r0], pair_a) * float(PEER_KEYS) + _take_lanes(si[r1], pair_b))
    cand = jnp.concatenate(cand, axis=0)
    cidx = jnp.concatenate(cidx, axis=0)

    lane2 = lax.broadcasted_iota(jnp.int32, cand.shape, 1)
    head_lane0 = (lax.broadcasted_iota(jnp.int32, cand.shape, 0) // tm) * PEER_TOPK
    cleft_ref[...] = cand
    fpos_ref[...] = jnp.zeros(fpos_ref.shape, jnp.int32)

    def body2(i, _):
        def piece(r, _):
            row0 = pl.multiple_of(r * _ROUTE_ROWS, _ROUTE_ROWS)
            rows = pl.ds(row0, _ROUTE_ROWS)
            left = cleft_ref[rows, :]
            pos = jnp.argmax(left, axis=1, keepdims=True).astype(jnp.int32)
            cleft_ref[rows, :] = jnp.where(lane_c == pos, NEG_INF, left)
            out_lane = (row0 // tm) * PEER_TOPK + i
            fpos_ref[rows, :] = jnp.where(lane_c == out_lane, pos, fpos_ref[rows, :])
            return 0

        return lax.fori_loop(0, PEER_HEADS * tm // _ROUTE_ROWS, piece, 0, unroll=_ROUTE_UNROLL)

    lax.fori_loop(0, PEER_TOPK, body2, 0)
    fpos = fpos_ref[...]
    fv = _take_lanes(cand, fpos)
    fe = _take_lanes(cidx, fpos)

    grp = lane2 // PEER_TOPK == head_lane0 // PEER_TOPK
    fe = jnp.where(grp, fe, 0.0)
    mx = jnp.max(jnp.where(grp, fv, NEG_INF), axis=1, keepdims=True)
    ex = jnp.where(grp, jnp.exp(fv - mx), 0.0)
    gate = ex / jnp.sum(ex, axis=1, keepdims=True)
    e_out = fe[0:tm]
    g_out = gate[0:tm]
    for h in range(1, PEER_HEADS):
        e_out = e_out + fe[h * tm:(h + 1) * tm]
        g_out = g_out + gate[h * tm:(h + 1) * tm]
    e_ref[...] = e_out.astype(jnp.int32)
    g_ref[...] = g_out


def _peer_route(qp, sub_keys):
    m, kq = qp.shape
    tm = min(256, m)
    ne = PEER_HEADS * PEER_TOPK
    ospec = pl.BlockSpec((tm, ne), lambda i: (i, 0))
    pairs = [(a, b) for a in range(PEER_TOPK) for b in range(PEER_TOPK) if (a + 1) * (b + 1) <= PEER_TOPK]
    pad = [0] * (PEER_KEYS - len(pairs))
    pair_rows = jnp.array([[a for a, _ in pairs] + pad, [b for _, b in pairs] + pad,
                           [1] * len(pairs) + pad] + [[0] * PEER_KEYS] * 5, jnp.int32)
    return pl.pallas_call(
        _route_kernel,
        grid=(m // tm,),
        in_specs=[pl.BlockSpec((tm, kq), lambda i: (i, 0)),
                  pl.BlockSpec(sub_keys.shape, lambda i: (0, 0, 0, 0)),
                  pl.BlockSpec(pair_rows.shape, lambda i: (0, 0))],
        out_specs=[ospec, ospec],
        out_shape=[jax.ShapeDtypeStruct((m, ne), jnp.int32), jax.ShapeDtypeStruct((m, ne), F32)],
        scratch_shapes=[pltpu.VMEM((2 * PEER_HEADS * tm, PEER_KEYS), F32),
                        pltpu.VMEM((2 * PEER_HEADS * tm, PEER_KEYS), F32),
                        pltpu.VMEM((2 * PEER_HEADS * tm, PEER_KEYS), jnp.int32),
                        pltpu.VMEM((PEER_HEADS * tm, PEER_KEYS), F32),
                        pltpu.VMEM((PEER_HEADS * tm, PEER_KEYS), jnp.int32)],
        compiler_params=_params("parallel"),
    )(qp, sub_keys, pair_rows)


def _gelu_tanh(x):
    return 0.5 * x * (1.0 + jnp.tanh(0.7978845608028654 * (x + 0.044715 * (x * x * x))))


_TOKEN_GROUP = 16


def _expert_kernel(h_ref, gain_ref, e_ref, g_ref, tab_ref, o_ref,
                   xn_ref, hi_ref, lo_ref, act_ref, d_ref, w3_ref, *, nj):
    j = pl.program_id(1)
    tm = h_ref.shape[0]
    te = tab_ref.shape[0]
    nb = te // LANES

    @pl.when(j == 0)
    def _():
        xn_ref[...] = _norm_rows(h_ref[...], gain_ref[...]).astype(BF16)
        o_ref[...] = h_ref[...]
        e = e_ref[...]
        hi_ref[...] = e // LANES
        lo_ref[...] = e % LANES
        act_ref[...] = jnp.zeros(act_ref.shape, F32)

    def score_slab(slot):
        d_ref[slot] = lax.dot_general(xn_ref[...], tab_ref[...], _NT, preferred_element_type=F32)

    def pick_from_slab(slot, slab):
        hi = hi_ref[...]
        lo = lo_ref[...]
        act = act_ref[...]
        for s in range(nb):
            picked = _take_lanes(d_ref[slot, :, s * LANES:(s + 1) * LANES], lo)
            act = jnp.where(hi == slab * nb + s, picked, act)
        act_ref[...] = act

    @pl.when(j == 0)
    def _():
        score_slab(0)

    @pl.when(jnp.logical_and(j >= 1, j < nj))
    def _():
        pick_from_slab((j - 1) % 2, j - 1)
        score_slab(j % 2)

    @pl.when(j == nj)
    def _():
        pick_from_slab((nj - 1) % 2, nj - 1)
        act_ref[...] = g_ref[...] * _gelu_tanh(act_ref[...])
        sub = lax.broadcasted_iota(jnp.int32, (LANES, LANES), 0)

        def group(gi, _):
            t0 = pl.multiple_of(gi * _TOKEN_GROUP, _TOKEN_GROUP)
            grids = []
            for u in range(_TOKEN_GROUP):
                hi_row = hi_ref[pl.ds(t0 + u, 1), :]
                lo_row = lo_ref[pl.ds(t0 + u, 1), :]
                w_row = act_ref[pl.ds(t0 + u, 1), :]
                a_t = jnp.where(hi_row == sub, w_row, 0.0).astype(BF16)
                b_t = jnp.where(lo_row == sub, 1.0, 0.0).astype(BF16)
                grids.append(lax.dot_general(a_t, b_t, _NT, preferred_element_type=F32))
            w3_ref[:, pl.ds(t0, _TOKEN_GROUP), :] = pltpu.einshape(
                "uik->iuk", jnp.stack(grids, axis=0)).astype(BF16)
            return 0

        lax.fori_loop(0, tm // _TOKEN_GROUP, group, 0)

    @pl.when(j >= nj)
    def _():
        jj = j - nj
        w = jnp.concatenate([w3_ref[jj * nb + s] for s in range(nb)], axis=1)
        d = o_ref.shape[1]
        for c0 in range(0, d, _EXPERT_OUT_COLS):
            cols = slice(c0, c0 + _EXPERT_OUT_COLS)
            o_ref[:, cols] += jnp.dot(w, tab_ref[:, cols], preferred_element_type=F32)


_EXPERT_OUT_COLS = 1024


def _peer_experts(h, gain, eidx, gate, uv_tab, layer, *, te=512):
    m, d = h.shape
    ne = uv_tab.shape[2]
    nsel = eidx.shape[1]
    tm = min(512, m)
    nj = ne // te
    once = pl.Buffered(1)
    sel_spec = pl.BlockSpec((tm, nsel), lambda i, j: (i, 0), pipeline_mode=once)
    return pl.pallas_call(
        functools.partial(_expert_kernel, nj=nj),
        grid=(m // tm, 2 * nj),
        in_specs=[pl.BlockSpec((tm, d), lambda i, j: (i, 0)),
                  pl.BlockSpec((1, d), lambda i, j: (0, 0), pipeline_mode=once),
                  sel_spec, sel_spec,
                  pl.BlockSpec((None, None, te, d), lambda i, j: (layer, j // nj, j % nj, 0))],
        out_specs=pl.BlockSpec((tm, d), lambda i, j: (i, 0), pipeline_mode=once),
        out_shape=jax.ShapeDtypeStruct((m, d), F32),
        scratch_shapes=[pltpu.VMEM((tm, d), BF16),
                        pltpu.VMEM((tm, nsel), jnp.int32),
                        pltpu.VMEM((tm, nsel), jnp.int32),
                        pltpu.VMEM((tm, nsel), F32),
                        pltpu.VMEM((2, tm, te), F32),
                        pltpu.VMEM((ne // LANES, tm, LANES), BF16)],
        compiler_params=_params("parallel", "arbitrary"),
    )(h, gain.reshape(1, d), eidx, gate, uv_tab)


def _peer_ffn(h, i, prm):
    qp = _matmul(h, prm["w_q_peer"][i], gain=prm["g_ffn"][i])
    eidx, gate = _peer_route(qp, prm["peer_sub_keys"][i])
    return _peer_experts(h, prm["g_ffn"][i], eidx, gate, prm["peer_uv"], i)


def _ple(h, p, i, prm):
    return _matmul(h, prm["w_ple_gate"][i], gain=prm["g_ple"][i], residual=h,
                   ple=(p, prm["w_ple_proj"][i]))


def _trunk(x, pe, past, prm):
    b, t, d = x.shape
    m = b * t
    h = x.reshape(m, d)
    pe = pe.reshape(pe.shape[0], m, pe.shape[-1])

    q, k, v = _qkv_proj(h, prm["w_qkv_a"], prm["g_mix"][0])
    if past is None:
        o = _sb_prompt(q, k, v, batch=b, seq=t)
    else:
        o = _sb_sample(q, k, v, past[0][0], past[1][0], batch=b, tq=t)
    h = _matmul(o, prm["w_o_a"], residual=h)
    h = _peer_ffn(h, 0, prm)
    h = _ple(h, pe[0], 0, prm)

    kv = _matmul(h, prm["w_kv_b"], gain=prm["g_kv"])
    nkv = N_KV_B * HEAD_DIM
    kb_new = kv[:, :nkv].reshape(b, t, nkv)
    vb_new = kv[:, nkv:].reshape(b, t, nkv)

    qb = _matmul(h, prm["w_q_b"], gain=prm["g_mix"][1], out_dtype=BF16, out_scale=SCALE)
    win = (N_PREV_CHUNKS + 2) * CHUNK
    if past is None:
        front = (N_PREV_CHUNKS + 1) * CHUNK
        kpad = jnp.pad(kb_new.astype(BF16), ((0, 0), (front, 0), (0, 0)))
        vpad = jnp.pad(vb_new.astype(BF16), ((0, 0), (front, 0), (0, 0)))
        bias = _band_bias(prm["rel_bias_b"][0], CHUNK, win)
        ob = _band_attention(qb, kpad, vpad, bias, batch=b, nq=t // CHUNK, cq=CHUNK, win=win,
                             chunked=True, lo_static=0)
    else:
        ck = past[2].reshape(b, -1, nkv)
        cv = past[3].reshape(b, -1, nkv)
        front = win - ck.shape[1] - t
        kpad = jnp.pad(jnp.concatenate([ck, kb_new], axis=1).astype(BF16), ((0, 0), (front, 0), (0, 0)))
        vpad = jnp.pad(jnp.concatenate([cv, vb_new], axis=1).astype(BF16), ((0, 0), (front, 0), (0, 0)))
        bias = _band_bias(prm["rel_bias_b"][0], t, win)
        ob = _band_attention(qb, kpad, vpad, bias, batch=b, nq=1, cq=t, win=win,
                             chunked=False, lo_static=front)
    h = _matmul(ob, prm["w_o_b"], residual=h)
    h = _peer_ffn(h, 1, prm)
    h = _ple(h, pe[1], 1, prm)

    y = _rmsnorm(h, prm["g_final"]).reshape(b, t, d)
    heads = k.shape[1] // HEAD_DIM
    a_k = k.reshape(1, b, t, heads, HEAD_DIM)
    a_v = v.reshape(1, b, t, heads, HEAD_DIM)
    if past is None:
        keep = min(N_PREV_CHUNKS * CHUNK, t)
        b_k, b_v = kb_new[:, t - keep:], vb_new[:, t - keep:]
    else:
        b_k, b_v = kb_new, vb_new
    b_k = b_k.reshape(b, -1, N_KV_B, HEAD_DIM)
    b_v = b_v.reshape(b, -1, N_KV_B, HEAD_DIM)
    return y, a_k, a_v, b_k, b_v


def kernel(x_prompt, x_sample, cache_a_k, cache_a_v, cache_b_k, cache_b_v, p_prompt, p_sample, g_mix, w_qkv_a, w_o_a, g_kv, w_kv_b, w_q_b, rel_bias_b, w_o_b, g_ffn, w_q_peer, peer_sub_keys, peer_u, peer_v, g_ple, w_ple_gate, w_ple_proj, g_final):
    prm = dict(
        g_mix=g_mix, g_kv=g_kv, g_ffn=g_ffn, g_ple=g_ple, g_final=g_final,
        w_qkv_a=_to_bf16(w_qkv_a)[0],
        w_o_a=_to_bf16(w_o_a)[0],
        w_kv_b=_to_bf16(w_kv_b),
        w_q_b=_to_bf16(w_q_b)[0],
        w_o_b=_to_bf16(w_o_b)[0],
        rel_bias_b=rel_bias_b,
        w_q_peer=_to_bf16(w_q_peer),
        peer_sub_keys=peer_sub_keys,
        peer_uv=_peer_tables_bf16(peer_u, peer_v),
        w_ple_gate=_to_bf16(w_ple_gate),
        w_ple_proj=_to_bf16(w_ple_proj),
    )
    y_p, ak_p, av_p, bk_p, bv_p = _trunk(x_prompt, p_prompt, None, prm)
    y_s, ak_s, av_s, bk_s, bv_s = _trunk(x_sample, p_sample,
                                         (cache_a_k, cache_a_v, cache_b_k, cache_b_v), prm)
    return (y_p, y_s, ak_p, av_p, bk_p, bv_p, ak_s, av_s, bk_s, bv_s)
```

```python
import functools

import jax
import jax.numpy as jnp
import numpy as np
from jax import lax
from jax.experimental import pallas as pl
from jax.experimental.pallas import tpu as pltpu

F32 = jnp.float32
BF16 = jnp.bfloat16

EPS = 1e-6
HEAD_DIM = 128
CHUNK = 64
N_PREV_CHUNKS = 8
N_KV_B = 8
G_B = 4
REL_CLIP = 128
PEER_HEADS = 8
PEER_KEYS = 128
PEER_TOPK = 16
SCALE = HEAD_DIM ** -0.5
NEG_INF = float("-inf")

VMEM_LIMIT_BYTES = 60 * 1024 * 1024
LANES = 128

_NT = (((1,), (1,)), ((), ()))


def _params(*sem):
    return pltpu.CompilerParams(dimension_semantics=sem, vmem_limit_bytes=VMEM_LIMIT_BYTES)


def _norm_rows(x, g):
    ms = jnp.mean(x * x, axis=-1, keepdims=True)
    return x * lax.rsqrt(ms + EPS) * g


def _sigmoid(x):
    return 1.0 / (1.0 + jnp.exp(-x))


def _mm_kernel(*refs, norm, res, ple, out_scale):
    it = iter(refs)
    x_ref = next(it)
    g_ref = next(it) if norm else None
    w_ref = next(it)
    res_ref = next(it) if res else None
    p_ref = next(it) if ple else None
    wp_ref = next(it) if ple else None
    o_ref = next(it)
    xn_ref = next(it) if norm else None

    if norm:
        @pl.when(pl.program_id(1) == 0)
        def _():
            xn_ref[...] = _norm_rows(x_ref[...], g_ref[...]).astype(BF16)
        x = xn_ref[...]
    else:
        x = x_ref[...]
    acc = jnp.dot(x, w_ref[...], preferred_element_type=F32)
    if ple:
        proj = jnp.dot(p_ref[...].astype(BF16), wp_ref[...], preferred_element_type=F32)
        acc = _sigmoid(acc) * proj
    if res:
        acc = acc + res_ref[...]
    if out_scale is not None:
        acc = acc * out_scale
    o_ref[...] = acc.astype(o_ref.dtype)


def _matmul(x, w, *, gain=None, residual=None, ple=None, out_dtype=F32, out_scale=None, tn=1024):
    m, k = x.shape
    n = w.shape[1]
    tm = min(512, m)
    tn = min(tn, n)
    norm = gain is not None
    in_specs = [pl.BlockSpec((tm, k), lambda i, j: (i, 0))]
    args = [x]
    if norm:
        in_specs.append(pl.BlockSpec((1, k), lambda i, j: (0, 0)))
        args.append(gain.reshape(1, k))
    in_specs.append(pl.BlockSpec((k, tn), lambda i, j: (0, j)))
    args.append(w)
    if residual is not None:
        in_specs.append(pl.BlockSpec((tm, tn), lambda i, j: (i, j)))
        args.append(residual)
    if ple is not None:
        p, wp = ple
        kp = p.shape[1]
        in_specs.append(pl.BlockSpec((tm, kp), lambda i, j: (i, 0)))
        in_specs.append(pl.BlockSpec((kp, tn), lambda i, j: (0, j)))
        args += [p, wp]
    return pl.pallas_call(
        functools.partial(_mm_kernel, norm=norm, res=residual is not None, ple=ple is not None,
                          out_scale=out_scale),
        grid=(m // tm, n // tn),
        in_specs=in_specs,
        out_specs=pl.BlockSpec((tm, tn), lambda i, j: (i, j)),
        out_shape=jax.ShapeDtypeStruct((m, n), out_dtype),
        scratch_shapes=[pltpu.VMEM((tm, k), BF16)] if norm else [],
        compiler_params=_params("parallel", "arbitrary"),
    )(*args)


def _qkv_kernel(x_ref, g_ref, w_ref, q_ref, k_ref, v_ref, xn_ref, *, nq):
    j = pl.program_id(1)

    @pl.when(j == 0)
    def _():
        xn_ref[...] = _norm_rows(x_ref[...], g_ref[...]).astype(BF16)

    acc = jnp.dot(xn_ref[...], w_ref[...], preferred_element_type=F32)

    @pl.when(j < nq)
    def _():
        q_ref[...] = (acc * SCALE).astype(q_ref.dtype)

    @pl.when(jnp.logical_and(j >= nq, j < 2 * nq))
    def _():
        k_ref[...] = acc

    @pl.when(j >= 2 * nq)
    def _():
        v_ref[...] = acc


def _qkv_proj(x, w, gain, *, tn=1024):
    m, k = x.shape
    n = w.shape[1] // 3
    tm = min(512, m)
    nq = n // tn

    def ospec(first):
        return pl.BlockSpec((tm, tn), lambda i, j: (i, jnp.clip(j - first, 0, nq - 1)))

    return pl.pallas_call(
        functools.partial(_qkv_kernel, nq=nq),
        grid=(m // tm, 3 * nq),
        in_specs=[pl.BlockSpec((tm, k), lambda i, j: (i, 0)),
                  pl.BlockSpec((1, k), lambda i, j: (0, 0)),
                  pl.BlockSpec((k, tn), lambda i, j: (0, j))],
        out_specs=[ospec(0), ospec(nq), ospec(2 * nq)],
        out_shape=[jax.ShapeDtypeStruct((m, n), BF16), jax.ShapeDtypeStruct((m, n), F32),
                   jax.ShapeDtypeStruct((m, n), F32)],
        scratch_shapes=[pltpu.VMEM((tm, k), BF16)],
        compiler_params=_params("parallel", "arbitrary"),
    )(x, gain.reshape(1, k), w)


def _cast_kernel(x_ref, o_ref):
    o_ref[...] = x_ref[...].astype(o_ref.dtype)


_CAST_ROWS = 256
_CAST_COLS = 4096


def _to_bf16(x):
    x2 = x.reshape(-1, x.shape[-1])
    rows, cols = x2.shape
    tr = min(_CAST_ROWS, rows)
    tc = min(_CAST_COLS, cols)
    spec = pl.BlockSpec((tr, tc), lambda i, j: (i, j))
    out = pl.pallas_call(
        _cast_kernel,
        grid=(rows // tr, cols // tc),
        in_specs=[spec],
        out_specs=spec,
        out_shape=jax.ShapeDtypeStruct((rows, cols), BF16),
        compiler_params=_params("parallel", "parallel"),
    )(x2)
    return out.reshape(x.shape)


def _stack_cast_kernel(u_ref, v_ref, o_ref):
    c = pl.program_id(1)

    @pl.when(c == 0)
    def _():
        o_ref[...] = u_ref[...].astype(o_ref.dtype)

    @pl.when(c == 1)
    def _():
        o_ref[...] = v_ref[...].astype(o_ref.dtype)


def _peer_tables_bf16(u, v):
    nl, ne, d = u.shape
    te = _CAST_ROWS
    nblk = ne // te
    uspec = pl.BlockSpec((None, te, d), lambda l, c, e: (l, jnp.where(c == 0, e, nblk - 1), 0))
    vspec = pl.BlockSpec((None, te, d), lambda l, c, e: (l, jnp.where(c == 1, e, 0), 0))
    return pl.pallas_call(
        _stack_cast_kernel,
        grid=(nl, 2, nblk),
        in_specs=[uspec, vspec],
        out_specs=pl.BlockSpec((None, None, te, d), lambda l, c, e: (l, c, e, 0)),
        out_shape=jax.ShapeDtypeStruct((nl, 2, ne, d), BF16),
        compiler_params=_params("parallel", "arbitrary", "arbitrary"),
    )(u, v)


def _rmsnorm_kernel(x_ref, g_ref, o_ref):
    o_ref[...] = _norm_rows(x_ref[...], g_ref[...])


def _rmsnorm(x, gain):
    m, k = x.shape
    tm = min(512, m)
    return pl.pallas_call(
        _rmsnorm_kernel,
        grid=(m // tm,),
        in_specs=[pl.BlockSpec((tm, k), lambda i: (i, 0)), pl.BlockSpec((1, k), lambda i: (0, 0))],
        out_specs=pl.BlockSpec((tm, k), lambda i: (i, 0)),
        out_shape=jax.ShapeDtypeStruct((m, k), F32),
        compiler_params=_params("parallel"),
    )(x, gain.reshape(1, k))


def _strict_lower_ones(n):
    r = lax.broadcasted_iota(jnp.int32, (2 * n, n), 0) % n
    c = lax.broadcasted_iota(jnp.int32, (2 * n, n), 1)
    return jnp.where(r > c, 1.0, 0.0).astype(BF16)


_SB_DEAD_BELOW = -104.0


def _sb_log_terms(q, kb, mask):
    z = lax.dot_general(q, kb, _NT, preferred_element_type=F32)
    nz = -z
    sp = jnp.log(1.0 + jnp.exp(-jnp.maximum(z, nz)))
    log_beta = jnp.minimum(z, 0.0) - sp
    log_keep = jnp.minimum(nz, 0.0) - sp
    if mask is not None:
        log_keep = jnp.where(mask, log_keep, 0.0)
    return log_beta, log_keep


def _sb_block(q, kb, vb, c, acc, tri2, mask):
    log_beta, log_keep = _sb_log_terms(q, kb, mask)
    hi = log_keep.astype(BF16)
    lo = (log_keep - hi.astype(F32)).astype(BF16)
    after = jnp.dot(jnp.concatenate([hi, lo], axis=1), tri2, preferred_element_type=F32)
    w = jnp.exp(log_beta + after + c)
    if mask is not None:
        w = jnp.where(mask, w, 0.0)
    acc = acc + jnp.dot(w.astype(BF16), vb, preferred_element_type=F32)
    c = c + jnp.sum(log_keep, axis=1, keepdims=True)
    return c, acc


_SB_QUERY_GROUP = 8


def _sb_prompt_kernel(q_ref, k_ref, v_ref, o_ref, kb_ref, vb_ref, *, blk):
    seq = q_ref.shape[0]
    kb_ref[pl.ds(0, blk), :] = jnp.zeros((blk, HEAD_DIM), BF16)
    vb_ref[pl.ds(0, blk), :] = jnp.zeros((blk, HEAD_DIM), BF16)
    kb_ref[pl.ds(blk, seq), :] = k_ref[...].astype(BF16)
    vb_ref[pl.ds(blk, seq), :] = v_ref[...].astype(BF16)
    tri = _strict_lower_ones(blk)
    row = lax.broadcasted_iota(jnp.int32, (blk, blk), 0)
    col = lax.broadcasted_iota(jnp.int32, (blk, blk), 1)
    diag_mask = col < row

    def kv_block(j):
        r0 = pl.multiple_of((j + 1) * blk, blk)
        return kb_ref[pl.ds(r0, blk), :], vb_ref[pl.ds(r0, blk), :]

    def older_blocks(i, q, c, acc):
        def k_cond(carry):
            j, cmax, _, _ = carry
            return jnp.logical_and(j >= 0, cmax > _SB_DEAD_BELOW)

        def k_body(carry):
            j, _, c, acc = carry
            c, acc = _sb_block(q, *kv_block(j), c, acc, tri, None)
            return j - 1, jnp.max(c), c, acc

        return lax.while_loop(k_cond, k_body, (i - 2, jnp.max(c), c, acc))[3]

    def group_body(p, _):
        state = []
        for i in [_SB_QUERY_GROUP * p + u for u in range(_SB_QUERY_GROUP)]:
            q = q_ref[pl.ds(pl.multiple_of(i * blk, blk), blk), :]
            c = jnp.zeros((blk, 1), F32)
            acc = jnp.zeros((blk, HEAD_DIM), F32)
            c, acc = _sb_block(q, *kv_block(i), c, acc, tri, diag_mask)
            c, acc = _sb_block(q, *kv_block(i - 1), c, acc, tri, None)
            state.append((i, q, c, acc))
        for i, q, c, acc in state:
            acc = older_blocks(i, q, c, acc)
            o_ref[pl.ds(pl.multiple_of(i * blk, blk), blk), :] = acc.astype(o_ref.dtype)
        return 0

    lax.fori_loop(0, seq // (_SB_QUERY_GROUP * blk), group_body, 0)


def _sb_prompt(q, k, v, *, batch, seq, blk=256):
    t, hd = q.shape
    heads = hd // HEAD_DIM
    blk = min(blk, seq)
    spec = pl.BlockSpec((seq, HEAD_DIM), lambda b, h: (b, h))
    return pl.pallas_call(
        functools.partial(_sb_prompt_kernel, blk=blk),
        grid=(batch, heads),
        in_specs=[spec, spec, spec],
        out_specs=spec,
        out_shape=jax.ShapeDtypeStruct((t, hd), BF16),
        scratch_shapes=[pltpu.VMEM((seq + blk, HEAD_DIM), BF16), pltpu.VMEM((seq + blk, HEAD_DIM), BF16)],
        compiler_params=_params("parallel", "parallel"),
    )(q, k, v)


_SB_HEAD_GROUP = 32


def _sb_probe_kernel(q_ref, kn_ref, kc_ref, alive_ref, *, blk):
    tq = q_ref.shape[0]
    row = lax.broadcasted_iota(jnp.int32, (tq, blk), 0)
    col = lax.broadcasted_iota(jnp.int32, (tq, blk), 1)
    kb = pltpu.einshape("thd->htd", kc_ref[...]).astype(BF16)
    cmax = jnp.full((tq, 1), NEG_INF, F32)
    for g in range(_SB_HEAD_GROUP):
        lanes = slice(g * HEAD_DIM, (g + 1) * HEAD_DIM)
        q = q_ref[:, lanes]
        c = jnp.sum(_sb_log_terms(q, kn_ref[:, lanes], col < row)[1], axis=1, keepdims=True)
        c = c + jnp.sum(_sb_log_terms(q, kb[g], None)[1], axis=1, keepdims=True)
        cmax = jnp.maximum(cmax, c)
    flag = jnp.where(jnp.max(cmax) > _SB_DEAD_BELOW - 1.0, 1, 0)
    alive_ref[...] = jnp.full(alive_ref.shape, flag, jnp.int32)


def _sb_sample_kernel(alive_ref, q_ref, kn_ref, vn_ref, kc_ref, vc_ref, o_ref, c_ref, acc_ref, *, blk):
    del alive_ref
    n = pl.program_id(2)
    tq = q_ref.shape[0]
    tri = _strict_lower_ones(blk)

    @pl.when(n == 0)
    def _():
        row = lax.broadcasted_iota(jnp.int32, (tq, blk), 0)
        col = lax.broadcasted_iota(jnp.int32, (tq, blk), 1)
        for g in range(_SB_HEAD_GROUP):
            lanes = slice(g * HEAD_DIM, (g + 1) * HEAD_DIM)
            c, acc = _sb_block(q_ref[:, lanes], kn_ref[:, lanes], vn_ref[:, lanes],
                               jnp.zeros((tq, 1), F32), jnp.zeros((tq, HEAD_DIM), F32), tri, col < row)
            c_ref[g] = c
            acc_ref[g] = acc

    @pl.when(jnp.max(c_ref[...]) > _SB_DEAD_BELOW)
    def _():
        kb = pltpu.einshape("thd->htd", kc_ref[...]).astype(BF16)
        vb = pltpu.einshape("thd->htd", vc_ref[...]).astype(BF16)
        for g in range(_SB_HEAD_GROUP):
            lanes = slice(g * HEAD_DIM, (g + 1) * HEAD_DIM)
            c, acc = _sb_block(q_ref[:, lanes], kb[g], vb[g], c_ref[g], acc_ref[g], tri, None)
            c_ref[g] = c
            acc_ref[g] = acc

    @pl.when(n == pl.num_programs(2) - 1)
    def _():
        for g in range(_SB_HEAD_GROUP):
            o_ref[:, g * HEAD_DIM:(g + 1) * HEAD_DIM] = acc_ref[g].astype(o_ref.dtype)


def _sb_sample(q, k_new, v_new, cache_k, cache_v, *, batch, tq, blk=256):
    t, hd = q.shape
    heads = hd // HEAD_DIM
    past = cache_k.shape[1]
    nblk = past // blk
    gw = _SB_HEAD_GROUP * HEAD_DIM

    def pad_block(a):
        a = a.reshape(batch, tq, hd)
        a = jnp.pad(a, ((0, 0), (0, blk - tq), (0, 0)))
        return a.reshape(batch * blk, hd).astype(BF16)

    assert heads == _SB_HEAD_GROUP
    kn = pad_block(k_new)
    vn = pad_block(v_new)
    newest = (None, blk, _SB_HEAD_GROUP, HEAD_DIM)
    alive = pl.pallas_call(
        functools.partial(_sb_probe_kernel, blk=blk),
        grid=(batch,),
        in_specs=[pl.BlockSpec((tq, gw), lambda b: (b, 0)),
                  pl.BlockSpec((blk, gw), lambda b: (b, 0)),
                  pl.BlockSpec(newest, lambda b: (b, nblk - 1, 0, 0))],
        out_specs=pl.BlockSpec((None, 8, LANES), lambda b: (b, 0, 0)),
        out_shape=jax.ShapeDtypeStruct((batch, 8, LANES), jnp.int32),
        compiler_params=_params("parallel"),
    )(q, kn, cache_k)[:, 0, 0]

    def cache_block(b, h, n, alive_ref):
        return (b, nblk - 1 - jnp.where(alive_ref[b] > 0, n, 0), h, 0)

    qspec = pl.BlockSpec((tq, gw), lambda b, h, n, alive_ref: (b, h))
    nspec = pl.BlockSpec((blk, gw), lambda b, h, n, alive_ref: (b, h))
    cspec = pl.BlockSpec(newest, cache_block)
    return pl.pallas_call(
        functools.partial(_sb_sample_kernel, blk=blk),
        grid_spec=pltpu.PrefetchScalarGridSpec(
            num_scalar_prefetch=1,
            grid=(batch, heads // _SB_HEAD_GROUP, nblk),
            in_specs=[qspec, nspec, nspec, cspec, cspec],
            out_specs=qspec,
            scratch_shapes=[pltpu.VMEM((_SB_HEAD_GROUP, tq, 1), F32),
                            pltpu.VMEM((_SB_HEAD_GROUP, tq, HEAD_DIM), F32)]),
        out_shape=jax.ShapeDtypeStruct((t, hd), BF16),
        compiler_params=_params("parallel", "parallel", "arbitrary"),
    )(alive, q, kn, vn, cache_k, cache_v)


_BAND_UNROLL = 8


def _band_kernel(q_ref, k_ref, v_ref, bias_ref, o_ref, *, cq, win, nq, chunked, lo_static):
    bias = bias_ref[...].reshape(G_B * cq, win)
    col = lax.broadcasted_iota(jnp.int32, (G_B * cq, win), 1)

    def chunk(c, _):
        r0 = pl.multiple_of(c * cq, cq)
        if chunked:
            lo = jnp.maximum(CHUNK, (N_PREV_CHUNKS + 1 - c) * CHUNK)
        else:
            lo = lo_static
        kb = k_ref[pl.ds(r0, win), :]
        vb = v_ref[pl.ds(r0, win), :]
        qb = q_ref[pl.ds(r0, cq), :]
        q4 = jnp.concatenate([qb[:, g * HEAD_DIM:(g + 1) * HEAD_DIM] for g in range(G_B)], axis=0)
        s = lax.dot_general(q4, kb, _NT, preferred_element_type=F32) + bias
        s = jnp.where(col >= lo, s, NEG_INF)
        m = jnp.max(s, axis=1, keepdims=True)
        e = jnp.exp(s - m)
        o = jnp.dot(e.astype(BF16), vb, preferred_element_type=F32) * (1.0 / jnp.sum(e, axis=1, keepdims=True))
        for g in range(G_B):
            o_ref[pl.ds(r0, cq), g * HEAD_DIM:(g + 1) * HEAD_DIM] = o[g * cq:(g + 1) * cq].astype(o_ref.dtype)
        return 0

    lax.fori_loop(0, nq, chunk, 0, unroll=_BAND_UNROLL if nq % _BAND_UNROLL == 0 else 1)


def _band_bias(rel_table, cq, win):
    lg = win + cq - 1
    idx = np.clip(win - 1 - np.arange(lg), -REL_CLIP, REL_CLIP) + REL_CLIP
    g = rel_table[idx].astype(F32).T
    rows = jnp.tile(g, (1, cq + 1))[:, :cq * (lg + 1)].reshape(-1, cq, lg + 1)
    return rows[:, ::-1, :win]


def _band_attention(q, kpad, vpad, bias, *, batch, nq, cq, win, chunked, lo_static):
    t, hd = q.shape
    rows = kpad.shape[1]
    qw = G_B * HEAD_DIM
    qspec = pl.BlockSpec((nq * cq, qw), lambda b, h: (b, h))
    kspec = pl.BlockSpec((None, rows, HEAD_DIM), lambda b, h: (b, 0, h))
    bspec = pl.BlockSpec((G_B, cq, win), lambda b, h: (h, 0, 0))
    return pl.pallas_call(
        functools.partial(_band_kernel, cq=cq, win=win, nq=nq, chunked=chunked, lo_static=lo_static),
        grid=(batch, N_KV_B),
        in_specs=[qspec, kspec, kspec, bspec],
        out_specs=qspec,
        out_shape=jax.ShapeDtypeStruct((t, hd), BF16),
        compiler_params=_params("parallel", "parallel"),
    )(q, kpad, vpad, bias)


def _split3_dot_nt(a, b):
    a_hi = a.astype(BF16)
    a_lo = (a - a_hi.astype(F32)).astype(BF16)
    b_hi = b.astype(BF16)
    b_lo = (b - b_hi.astype(F32)).astype(BF16)
    return (lax.dot_general(a_hi, b_hi, _NT, preferred_element_type=F32)
            + lax.dot_general(a_lo, b_hi, _NT, preferred_element_type=F32)
            + lax.dot_general(a_hi, b_lo, _NT, preferred_element_type=F32))


def _take_lanes(x, idx):
    return jnp.take_along_axis(x, idx, axis=1, mode="promise_in_bounds")


_ROUTE_ROWS = 64
_ROUTE_UNROLL = 64


def _route_kernel(q_ref, sk_ref, pairs_ref, e_ref, g_ref, s_ref, left_ref, si_ref, cleft_ref, fpos_ref):
    tm = q_ref.shape[0]
    dh = sk_ref.shape[-1]
    nset = 2 * PEER_HEADS
    lane_c = lax.broadcasted_iota(jnp.int32, (_ROUTE_ROWS, PEER_KEYS), 1)

    for n in range(nset):
        sc = _split3_dot_nt(q_ref[:, n * dh:(n + 1) * dh], sk_ref[n // 2, n % 2])
        s_ref[n * tm:(n + 1) * tm, :] = sc
        left_ref[n * tm:(n + 1) * tm, :] = sc
    si_ref[...] = jnp.zeros(si_ref.shape, jnp.int32)

    def body1(i, _):
        def piece(r, _):
            rows = pl.ds(pl.multiple_of(r * _ROUTE_ROWS, _ROUTE_ROWS), _ROUTE_ROWS)
            left = left_ref[rows, :]
            pos = jnp.argmax(left, axis=1, keepdims=True).astype(jnp.int32)
            left_ref[rows, :] = jnp.where(lane_c == pos, NEG_INF, left)
            si_ref[rows, :] = jnp.where(lane_c == i, pos, si_ref[rows, :])
            return 0

        return lax.fori_loop(0, nset * tm // _ROUTE_ROWS, piece, 0, unroll=_ROUTE_UNROLL)

    lax.fori_loop(0, PEER_TOPK, body1, 0)
    si = si_ref[...]
    sv = _take_lanes(s_ref[...], si)
    si = si.astype(F32)

    pair_a = jnp.broadcast_to(pairs_ref[0:1, :], (tm, PEER_KEYS))
    pair_b = jnp.broadcast_to(pairs_ref[1:2, :], (tm, PEER_KEYS))
    pair_ok = jnp.broadcast_to(pairs_ref[2:3, :], (tm, PEER_KEYS)) > 0
    cand, cidx = [], []
    for h in range(PEER_HEADS):
        r0 = slice((2 * h) * tm, (2 * h + 1) * tm)
        r1 = slice((2 * h + 1) * tm, (2 * h + 2) * tm)
        cand.append(jnp.where(pair_ok, _take_lanes(sv[r0], pair_a) + _take_lanes(sv[r1], pair_b), NEG_INF))
        cidx.append(_take_lanes(si[r0], pair_a) * float(PEER_KEYS) + _take_lanes(si[r1], pair_b))
    cand = jnp.concatenate(cand, axis=0)
    cidx = jnp.concatenate(cidx, axis=0)

    lane2 = lax.broadcasted_iota(jnp.int32, cand.shape, 1)
    head_lane0 = (lax.broadcasted_iota(jnp.int32, cand.shape, 0) // tm) * PEER_TOPK
    cleft_ref[...] = cand
    fpos_ref[...] = jnp.zeros(fpos_ref.shape, jnp.int32)

    def body2(i, _):
        def piece(r, _):
            row0 = pl.multiple_of(r * _ROUTE_ROWS, _ROUTE_ROWS)
            rows = pl.ds(row0, _ROUTE_ROWS)
            left = cleft_ref[rows, :]
            pos = jnp.argmax(left, axis=1, keepdims=True).astype(jnp.int32)
            cleft_ref[rows, :] = jnp.where(lane_c == pos, NEG_INF, left)
            out_lane = (row0 // tm) * PEER_TOPK + i
            fpos_ref[rows, :] = jnp.where(lane_c == out_lane, pos, fpos_ref[rows, :])
            return 0

        return lax.fori_loop(0, PEER_HEADS * tm // _ROUTE_ROWS, piece, 0, unroll=_ROUTE_UNROLL)

    lax.fori_loop(0, PEER_TOPK, body2, 0)
    fpos = fpos_ref[...]
    fv = _take_lanes(cand, fpos)
    fe = _take_lanes(cidx, fpos)

    grp = lane2 // PEER_TOPK == head_lane0 // PEER_TOPK
    fe = jnp.where(grp, fe, 0.0)
    mx = jnp.max(jnp.where(grp, fv, NEG_INF), axis=1, keepdims=True)
    ex = jnp.where(grp, jnp.exp(fv - mx), 0.0)
    gate = ex / jnp.sum(ex, axis=1, keepdims=True)
    e_out = fe[0:tm]
    g_out = gate[0:tm]
    for h in range(1, PEER_HEADS):
        e_out = e_out + fe[h * tm:(h + 1) * tm]
        g_out = g_out + gate[h * tm:(h + 1) * tm]
    e_ref[...] = e_out.astype(jnp.int32)
    g_ref[...] = g_out


def _peer_route(qp, sub_keys):
    m, kq = qp.shape
    tm = min(256, m)
    ne = PEER_HEADS * PEER_TOPK
    ospec = pl.BlockSpec((tm, ne), lambda i: (i, 0))
    pairs = [(a, b) for a in range(PEER_TOPK) for b in range(PEER_TOPK) if (a + 1) * (b + 1) <= PEER_TOPK]
    pad = [0] * (PEER_KEYS - len(pairs))
    pair_rows = jnp.array([[a for a, _ in pairs] + pad, [b for _, b in pairs] + pad,
                           [1] * len(pairs) + pad] + [[0] * PEER_KEYS] * 5, jnp.int32)
    return pl.pallas_call(
        _route_kernel,
        grid=(m // tm,),
        in_specs=[pl.BlockSpec((tm, kq), lambda i: (i, 0)),
                  pl.BlockSpec(sub_keys.shape, lambda i: (0, 0, 0, 0)),
                  pl.BlockSpec(pair_rows.shape, lambda i: (0, 0))],
        out_specs=[ospec, ospec],
        out_shape=[jax.ShapeDtypeStruct((m, ne), jnp.int32), jax.ShapeDtypeStruct((m, ne), F32)],
        scratch_shapes=[pltpu.VMEM((2 * PEER_HEADS * tm, PEER_KEYS), F32),
                        pltpu.VMEM((2 * PEER_HEADS * tm, PEER_KEYS), F32),
                        pltpu.VMEM((2 * PEER_HEADS * tm, PEER_KEYS), jnp.int32),
                        pltpu.VMEM((PEER_HEADS * tm, PEER_KEYS), F32),
                        pltpu.VMEM((PEER_HEADS * tm, PEER_KEYS), jnp.int32)],
        compiler_params=_params("parallel"),
    )(qp, sub_keys, pair_rows)


def _gelu_tanh(x):
    return 0.5 * x * (1.0 + jnp.tanh(0.7978845608028654 * (x + 0.044715 * (x * x * x))))


_TOKEN_GROUP = 16


def _expert_kernel(h_ref, gain_ref, e_ref, g_ref, tab_ref, o_ref,
                   xn_ref, hi_ref, lo_ref, act_ref, d_ref, w3_ref, *, nj):
    j = pl.program_id(1)
    tm = h_ref.shape[0]
    te = tab_ref.shape[0]
    nb = te // LANES

    @pl.when(j == 0)
    def _():
        xn_ref[...] = _norm_rows(h_ref[...], gain_ref[...]).astype(BF16)
        o_ref[...] = h_ref[...]
        e = e_ref[...]
        hi_ref[...] = e // LANES
        lo_ref[...] = e % LANES
        act_ref[...] = jnp.zeros(act_ref.shape, F32)

    def score_slab(slot):
        d_ref[slot] = lax.dot_general(xn_ref[...], tab_ref[...], _NT, preferred_element_type=F32)

    def pick_from_slab(slot, slab):
        hi = hi_ref[...]
        lo = lo_ref[...]
        act = act_ref[...]
        for s in range(nb):
            picked = _take_lanes(d_ref[slot, :, s * LANES:(s + 1) * LANES], lo)
            act = jnp.where(hi == slab * nb + s, picked, act)
        act_ref[...] = act

    @pl.when(j == 0)
    def _():
        score_slab(0)

    @pl.when(jnp.logical_and(j >= 1, j < nj))
    def _():
        pick_from_slab((j - 1) % 2, j - 1)
        score_slab(j % 2)

    @pl.when(j == nj)
    def _():
        pick_from_slab((nj - 1) % 2, nj - 1)
        act_ref[...] = g_ref[...] * _gelu_tanh(act_ref[...])
        sub = lax.broadcasted_iota(jnp.int32, (LANES, LANES), 0)

        def group(gi, _):
            t0 = pl.multiple_of(gi * _TOKEN_GROUP, _TOKEN_GROUP)
            grids = []
            for u in range(_TOKEN_GROUP):
                hi_row = hi_ref[pl.ds(t0 + u, 1), :]
                lo_row = lo_ref[pl.ds(t0 + u, 1), :]
                w_row = act_ref[pl.ds(t0 + u, 1), :]
                a_t = jnp.where(hi_row == sub, w_row, 0.0).astype(BF16)
                b_t = jnp.where(lo_row == sub, 1.0, 0.0).astype(BF16)
                grids.append(lax.dot_general(a_t, b_t, _NT, preferred_element_type=F32))
            w3_ref[:, pl.ds(t0, _TOKEN_GROUP), :] = pltpu.einshape(
                "uik->iuk", jnp.stack(grids, axis=0)).astype(BF16)
            return 0

        lax.fori_loop(0, tm // _TOKEN_GROUP, group, 0, unroll=4)

    @pl.when(j >= nj)
    def _():
        jj = j - nj
        w = jnp.concatenate([w3_ref[jj * nb + s] for s in range(nb)], axis=1)
        d = o_ref.shape[1]
        for c0 in range(0, d, _EXPERT_OUT_COLS):
            cols = slice(c0, c0 + _EXPERT_OUT_COLS)
            o_ref[:, cols] += jnp.dot(w, tab_ref[:, cols], preferred_element_type=F32)


_EXPERT_OUT_COLS = 1024


def _peer_experts(h, gain, eidx, gate, uv_tab, layer, *, te=512):
    m, d = h.shape
    ne = uv_tab.shape[2]
    nsel = eidx.shape[1]
    tm = min(512, m)
    nj = ne // te
    once = pl.Buffered(1)
    sel_spec = pl.BlockSpec((tm, nsel), lambda i, j: (i, 0), pipeline_mode=once)
    return pl.pallas_call(
        functools.partial(_expert_kernel, nj=nj),
        grid=(m // tm, 2 * nj),
        in_specs=[pl.BlockSpec((tm, d), lambda i, j: (i, 0)),
                  pl.BlockSpec((1, d), lambda i, j: (0, 0), pipeline_mode=once),
                  sel_spec, sel_spec,
                  pl.BlockSpec((None, None, te, d), lambda i, j: (layer, j // nj, j % nj, 0))],
        out_specs=pl.BlockSpec((tm, d), lambda i, j: (i, 0), pipeline_mode=once),
        out_shape=jax.ShapeDtypeStruct((m, d), F32),
        scratch_shapes=[pltpu.VMEM((tm, d), BF16),
                        pltpu.VMEM((tm, nsel), jnp.int32),
                        pltpu.VMEM((tm, nsel), jnp.int32),
                        pltpu.VMEM((tm, nsel), F32),
                        pltpu.VMEM((2, tm, te), F32),
                        pltpu.VMEM((ne // LANES, tm, LANES), BF16)],
        compiler_params=_params("parallel", "arbitrary"),
    )(h, gain.reshape(1, d), eidx, gate, uv_tab)


def _peer_ffn(h, i, prm):
    qp = _matmul(h, prm["w_q_peer"][i], gain=prm["g_ffn"][i])
    eidx, gate = _peer_route(qp, prm["peer_sub_keys"][i])
    return _peer_experts(h, prm["g_ffn"][i], eidx, gate, prm["peer_uv"], i)


def _ple(h, p, i, prm):
    return _matmul(h, prm["w_ple_gate"][i], gain=prm["g_ple"][i], residual=h,
                   ple=(p, prm["w_ple_proj"][i]))


def _trunk(x, pe, past, prm):
    b, t, d = x.shape
    m = b * t
    h = x.reshape(m, d)
    pe = pe.reshape(pe.shape[0], m, pe.shape[-1])

    q, k, v = _qkv_proj(h, prm["w_qkv_a"], prm["g_mix"][0])
    if past is None:
        o = _sb_prompt(q, k, v, batch=b, seq=t)
    else:
        o = _sb_sample(q, k, v, past[0][0], past[1][0], batch=b, tq=t)
    h = _matmul(o, prm["w_o_a"], residual=h)
    h = _peer_ffn(h, 0, prm)
    h = _ple(h, pe[0], 0, prm)

    kv = _matmul(h, prm["w_kv_b"], gain=prm["g_kv"])
    nkv = N_KV_B * HEAD_DIM
    kb_new = kv[:, :nkv].reshape(b, t, nkv)
    vb_new = kv[:, nkv:].reshape(b, t, nkv)

    qb = _matmul(h, prm["w_q_b"], gain=prm["g_mix"][1], out_dtype=BF16, out_scale=SCALE)
    win = (N_PREV_CHUNKS + 2) * CHUNK
    if past is None:
        front = (N_PREV_CHUNKS + 1) * CHUNK
        kpad = jnp.pad(kb_new.astype(BF16), ((0, 0), (front, 0), (0, 0)))
        vpad = jnp.pad(vb_new.astype(BF16), ((0, 0), (front, 0), (0, 0)))
        bias = _band_bias(prm["rel_bias_b"][0], CHUNK, win)
        ob = _band_attention(qb, kpad, vpad, bias, batch=b, nq=t // CHUNK, cq=CHUNK, win=win,
                             chunked=True, lo_static=0)
    else:
        ck = past[2].reshape(b, -1, nkv)
        cv = past[3].reshape(b, -1, nkv)
        front = win - ck.shape[1] - t
        kpad = jnp.pad(jnp.concatenate([ck, kb_new], axis=1).astype(BF16), ((0, 0), (front, 0), (0, 0)))
        vpad = jnp.pad(jnp.concatenate([cv, vb_new], axis=1).astype(BF16), ((0, 0), (front, 0), (0, 0)))
        bias = _band_bias(prm["rel_bias_b"][0], t, win)
        ob = _band_attention(qb, kpad, vpad, bias, batch=b, nq=1, cq=t, win=win,
                             chunked=False, lo_static=front)
    h = _matmul(ob, prm["w_o_b"], residual=h)
    h = _peer_ffn(h, 1, prm)
    h = _ple(h, pe[1], 1, prm)

    y = _rmsnorm(h, prm["g_final"]).reshape(b, t, d)
    heads = k.shape[1] // HEAD_DIM
    a_k = k.reshape(1, b, t, heads, HEAD_DIM)
    a_v = v.reshape(1, b, t, heads, HEAD_DIM)
    if past is None:
        keep = min(N_PREV_CHUNKS * CHUNK, t)
        b_k, b_v = kb_new[:, t - keep:], vb_new[:, t - keep:]
    else:
        b_k, b_v = kb_new, vb_new
    b_k = b_k.reshape(b, -1, N_KV_B, HEAD_DIM)
    b_v = b_v.reshape(b, -1, N_KV_B, HEAD_DIM)
    return y, a_k, a_v, b_k, b_v


def kernel(x_prompt, x_sample, cache_a_k, cache_a_v, cache_b_k, cache_b_v, p_prompt, p_sample, g_mix, w_qkv_a, w_o_a, g_kv, w_kv_b, w_q_b, rel_bias_b, w_o_b, g_ffn, w_q_peer, peer_sub_keys, peer_u, peer_v, g_ple, w_ple_gate, w_ple_proj, g_final):
    prm = dict(
        g_mix=g_mix, g_kv=g_kv, g_ffn=g_ffn, g_ple=g_ple, g_final=g_final,
        w_qkv_a=_to_bf16(w_qkv_a)[0],
        w_o_a=_to_bf16(w_o_a)[0],
        w_kv_b=_to_bf16(w_kv_b),
        w_q_b=_to_bf16(w_q_b)[0],
        w_o_b=_to_bf16(w_o_b)[0],
        rel_bias_b=rel_bias_b,
        w_q_peer=_to_bf16(w_q_peer),
        peer_sub_keys=peer_sub_keys,
        peer_uv=_peer_tables_bf16(peer_u, peer_v),
        w_ple_gate=_to_bf16(w_ple_gate),
        w_ple_proj=_to_bf16(w_ple_proj),
    )
    y_p, ak_p, av_p, bk_p, bv_p = _trunk(x_prompt, p_prompt, None, prm)
    y_s, ak_s, av_s, bk_s, bv_s = _trunk(x_sample, p_sample,
                                         (cache_a_k, cache_a_v, cache_b_k, cache_b_v), prm)
    return (y_p, y_s, ak_p, av_p, bk_p, bv_p, ak_s, av_s, bk_s, bv_s)
```

```python
import functools

import jax
import jax.numpy as jnp
import numpy as np
from jax import lax
from jax.experimental import pallas as pl
from jax.experimental.pallas import tpu as pltpu

F32 = jnp.float32
BF16 = jnp.bfloat16

EPS = 1e-6
HEAD_DIM = 128
CHUNK = 64
N_PREV_CHUNKS = 8
N_KV_B = 8
G_B = 4
REL_CLIP = 128
PEER_HEADS = 8
PEER_KEYS = 128
PEER_TOPK = 16
SCALE = HEAD_DIM ** -0.5
NEG_INF = float("-inf")

VMEM_LIMIT_BYTES = 60 * 1024 * 1024
LANES = 128

_NT = (((1,), (1,)), ((), ()))


def _params(*sem):
    return pltpu.CompilerParams(dimension_semantics=sem, vmem_limit_bytes=VMEM_LIMIT_BYTES)


def _norm_rows(x, g):
    ms = jnp.mean(x * x, axis=-1, keepdims=True)
    return x * lax.rsqrt(ms + EPS) * g


def _sigmoid(x):
    return 1.0 / (1.0 + jnp.exp(-x))


def _normed_rows_pipeline(x_ref, g_ref, xn_ref, step):
    i = pl.program_id(0)
    j = pl.program_id(1)
    last = pl.num_programs(1) - 1
    slot = i % 2

    def normalise(dst):
        xn_ref[dst] = _norm_rows(x_ref[...], g_ref[...]).astype(BF16)

    @pl.when(jnp.logical_and(i == 0, j == 0))
    def _():
        normalise(0)

    @pl.when(j < last)
    def _():
        step(slot, lambda: None)

    @pl.when(j == last)
    def _():
        step(slot, lambda: normalise(1 - slot))


def _next_tile_rows(ni, nj):
    return lambda i, j: (jnp.minimum(i + jnp.where(j == nj - 1, 1, 0), ni - 1), 0)


def _mm_kernel(*refs, norm, pipelined, res, ple, out_scale):
    it = iter(refs)
    x_ref = next(it)
    g_ref = next(it) if norm else None
    w_ref = next(it)
    res_ref = next(it) if res else None
    p_ref = next(it) if ple else None
    wp_ref = next(it) if ple else None
    o_ref = next(it)
    xn_ref = next(it) if norm else None

    def step(x):
        acc = jnp.dot(x, w_ref[...], preferred_element_type=F32)
        if ple:
            proj = jnp.dot(p_ref[...].astype(BF16), wp_ref[...], preferred_element_type=F32)
            acc = _sigmoid(acc) * proj
        if res:
            acc = acc + res_ref[...]
        if out_scale is not None:
            acc = acc * out_scale
        o_ref[...] = acc.astype(o_ref.dtype)

    if pipelined:
        def pipelined_step(slot, after):
            step(xn_ref[slot])
            after()

        _normed_rows_pipeline(x_ref, g_ref, xn_ref, pipelined_step)
    elif norm:
        @pl.when(pl.program_id(1) == 0)
        def _():
            xn_ref[...] = _norm_rows(x_ref[...], g_ref[...]).astype(BF16)
        step(xn_ref[...])
    else:
        step(x_ref[...])


def _matmul(x, w, *, gain=None, residual=None, ple=None, out_dtype=F32, out_scale=None, tn=1024):
    m, k = x.shape
    n = w.shape[1]
    tm = min(512, m)
    tn = min(tn, n)
    norm = gain is not None
    pipelined = norm and ple is None and n // tn >= 2
    x_map = _next_tile_rows(m // tm, n // tn) if pipelined else (lambda i, j: (i, 0))
    in_specs = [pl.BlockSpec((tm, k), x_map)]
    args = [x]
    if norm:
        in_specs.append(pl.BlockSpec((1, k), lambda i, j: (0, 0)))
        args.append(gain.reshape(1, k))
    in_specs.append(pl.BlockSpec((k, tn), lambda i, j: (0, j)))
    args.append(w)
    if residual is not None:
        in_specs.append(pl.BlockSpec((tm, tn), lambda i, j: (i, j)))
        args.append(residual)
    if ple is not None:
        p, wp = ple
        kp = p.shape[1]
        in_specs.append(pl.BlockSpec((tm, kp), lambda i, j: (i, 0)))
        in_specs.append(pl.BlockSpec((kp, tn), lambda i, j: (0, j)))
        args += [p, wp]
    return pl.pallas_call(
        functools.partial(_mm_kernel, norm=norm, pipelined=pipelined, res=residual is not None,
                          ple=ple is not None, out_scale=out_scale),
        grid=(m // tm, n // tn),
        in_specs=in_specs,
        out_specs=pl.BlockSpec((tm, tn), lambda i, j: (i, j)),
        out_shape=jax.ShapeDtypeStruct((m, n), out_dtype),
        scratch_shapes=([pltpu.VMEM((2, tm, k), BF16)] if pipelined else
                        [pltpu.VMEM((tm, k), BF16)] if norm else []),
        compiler_params=_params("arbitrary" if pipelined else "parallel", "arbitrary"),
    )(*args)


def _qkv_kernel(x_ref, g_ref, w_ref, q_ref, k_ref, v_ref, xn_ref, *, nq):
    j = pl.program_id(1)

    def step(slot, after):
        acc = jnp.dot(xn_ref[slot], w_ref[...], preferred_element_type=F32)
        after()

        @pl.when(j < nq)
        def _():
            q_ref[...] = (acc * SCALE).astype(q_ref.dtype)

        @pl.when(jnp.logical_and(j >= nq, j < 2 * nq))
        def _():
            k_ref[...] = acc

        @pl.when(j >= 2 * nq)
        def _():
            v_ref[...] = acc

    _normed_rows_pipeline(x_ref, g_ref, xn_ref, step)


def _qkv_proj(x, w, gain, *, tn=1024):
    m, k = x.shape
    n = w.shape[1] // 3
    tm = min(512, m)
    nq = n // tn

    def ospec(first):
        return pl.BlockSpec((tm, tn), lambda i, j: (i, jnp.clip(j - first, 0, nq - 1)))

    return pl.pallas_call(
        functools.partial(_qkv_kernel, nq=nq),
        grid=(m // tm, 3 * nq),
        in_specs=[pl.BlockSpec((tm, k), _next_tile_rows(m // tm, 3 * nq)),
                  pl.BlockSpec((1, k), lambda i, j: (0, 0)),
                  pl.BlockSpec((k, tn), lambda i, j: (0, j))],
        out_specs=[ospec(0), ospec(nq), ospec(2 * nq)],
        out_shape=[jax.ShapeDtypeStruct((m, n), BF16), jax.ShapeDtypeStruct((m, n), F32),
                   jax.ShapeDtypeStruct((m, n), F32)],
        scratch_shapes=[pltpu.VMEM((2, tm, k), BF16)],
        compiler_params=_params("arbitrary", "arbitrary"),
    )(x, gain.reshape(1, k), w)


def _cast_kernel(x_ref, o_ref):
    o_ref[...] = x_ref[...].astype(o_ref.dtype)


_CAST_ROWS = 256
_CAST_COLS = 4096


def _to_bf16(x):
    x2 = x.reshape(-1, x.shape[-1])
    rows, cols = x2.shape
    tr = min(_CAST_ROWS, rows)
    tc = min(_CAST_COLS, cols)
    spec = pl.BlockSpec((tr, tc), lambda i, j: (i, j))
    out = pl.pallas_call(
        _cast_kernel,
        grid=(rows // tr, cols // tc),
        in_specs=[spec],
        out_specs=spec,
        out_shape=jax.ShapeDtypeStruct((rows, cols), BF16),
        compiler_params=_params("parallel", "parallel"),
    )(x2)
    return out.reshape(x.shape)


def _stack_cast_kernel(u_ref, v_ref, o_ref):
    c = pl.program_id(1)

    @pl.when(c == 0)
    def _():
        o_ref[...] = u_ref[...].astype(o_ref.dtype)

    @pl.when(c == 1)
    def _():
        o_ref[...] = v_ref[...].astype(o_ref.dtype)


def _peer_tables_bf16(u, v):
    nl, ne, d = u.shape
    te = _CAST_ROWS
    nblk = ne // te
    uspec = pl.BlockSpec((None, te, d), lambda l, c, e: (l, jnp.where(c == 0, e, nblk - 1), 0))
    vspec = pl.BlockSpec((None, te, d), lambda l, c, e: (l, jnp.where(c == 1, e, 0), 0))
    return pl.pallas_call(
        _stack_cast_kernel,
        grid=(nl, 2, nblk),
        in_specs=[uspec, vspec],
        out_specs=pl.BlockSpec((None, None, te, d), lambda l, c, e: (l, c, e, 0)),
        out_shape=jax.ShapeDtypeStruct((nl, 2, ne, d), BF16),
        compiler_params=_params("parallel", "arbitrary", "arbitrary"),
    )(u, v)


def _rmsnorm_kernel(x_ref, g_ref, o_ref):
    o_ref[...] = _norm_rows(x_ref[...], g_ref[...])


def _rmsnorm(x, gain):
    m, k = x.shape
    tm = min(512, m)
    return pl.pallas_call(
        _rmsnorm_kernel,
        grid=(m // tm,),
        in_specs=[pl.BlockSpec((tm, k), lambda i: (i, 0)), pl.BlockSpec((1, k), lambda i: (0, 0))],
        out_specs=pl.BlockSpec((tm, k), lambda i: (i, 0)),
        out_shape=jax.ShapeDtypeStruct((m, k), F32),
        compiler_params=_params("parallel"),
    )(x, gain.reshape(1, k))


def _strict_lower_ones(n):
    r = lax.broadcasted_iota(jnp.int32, (2 * n, n), 0) % n
    c = lax.broadcasted_iota(jnp.int32, (2 * n, n), 1)
    return jnp.where(r > c, 1.0, 0.0).astype(BF16)


_SB_DEAD_BELOW = -104.0


def _sb_log_terms(q, kb, mask):
    z = lax.dot_general(q, kb, _NT, preferred_element_type=F32)
    nz = -z
    sp = jnp.log(1.0 + jnp.exp(-jnp.maximum(z, nz)))
    log_beta = jnp.minimum(z, 0.0) - sp
    log_keep = jnp.minimum(nz, 0.0) - sp
    if mask is not None:
        log_keep = jnp.where(mask, log_keep, 0.0)
    return log_beta, log_keep


def _sb_block(q, kb, vb, c, acc, tri2, mask):
    log_beta, log_keep = _sb_log_terms(q, kb, mask)
    hi = log_keep.astype(BF16)
    lo = (log_keep - hi.astype(F32)).astype(BF16)
    after = jnp.dot(jnp.concatenate([hi, lo], axis=1), tri2, preferred_element_type=F32)
    w = jnp.exp(log_beta + after + c)
    if mask is not None:
        w = jnp.where(mask, w, 0.0)
    acc = acc + jnp.dot(w.astype(BF16), vb, preferred_element_type=F32)
    c = c + jnp.sum(log_keep, axis=1, keepdims=True)
    return c, acc


_SB_QUERY_GROUP = 8


def _sb_prompt_kernel(q_ref, k_ref, v_ref, o_ref, kb_ref, vb_ref, *, blk):
    seq = q_ref.shape[0]
    kb_ref[pl.ds(0, blk), :] = jnp.zeros((blk, HEAD_DIM), BF16)
    vb_ref[pl.ds(0, blk), :] = jnp.zeros((blk, HEAD_DIM), BF16)
    kb_ref[pl.ds(blk, seq), :] = k_ref[...].astype(BF16)
    vb_ref[pl.ds(blk, seq), :] = v_ref[...].astype(BF16)
    tri = _strict_lower_ones(blk)
    row = lax.broadcasted_iota(jnp.int32, (blk, blk), 0)
    col = lax.broadcasted_iota(jnp.int32, (blk, blk), 1)
    diag_mask = col < row

    def kv_block(j):
        r0 = pl.multiple_of((j + 1) * blk, blk)
        return kb_ref[pl.ds(r0, blk), :], vb_ref[pl.ds(r0, blk), :]

    def older_blocks(i, q, c, acc):
        def k_cond(carry):
            j, cmax, _, _ = carry
            return jnp.logical_and(j >= 0, cmax > _SB_DEAD_BELOW)

        def k_body(carry):
            j, _, c, acc = carry
            c, acc = _sb_block(q, *kv_block(j), c, acc, tri, None)
            return j - 1, jnp.max(c), c, acc

        return lax.while_loop(k_cond, k_body, (i - 2, jnp.max(c), c, acc))[3]

    def group_body(p, _):
        state = []
        for i in [_SB_QUERY_GROUP * p + u for u in range(_SB_QUERY_GROUP)]:
            q = q_ref[pl.ds(pl.multiple_of(i * blk, blk), blk), :]
            c = jnp.zeros((blk, 1), F32)
            acc = jnp.zeros((blk, HEAD_DIM), F32)
            c, acc = _sb_block(q, *kv_block(i), c, acc, tri, diag_mask)
            c, acc = _sb_block(q, *kv_block(i - 1), c, acc, tri, None)
            state.append((i, q, c, acc))
        for i, q, c, acc in state:
            acc = older_blocks(i, q, c, acc)
            o_ref[pl.ds(pl.multiple_of(i * blk, blk), blk), :] = acc.astype(o_ref.dtype)
        return 0

    lax.fori_loop(0, seq // (_SB_QUERY_GROUP * blk), group_body, 0)


def _sb_prompt(q, k, v, *, batch, seq, blk=256):
    t, hd = q.shape
    heads = hd // HEAD_DIM
    blk = min(blk, seq)
    spec = pl.BlockSpec((seq, HEAD_DIM), lambda b, h: (b, h))
    return pl.pallas_call(
        functools.partial(_sb_prompt_kernel, blk=blk),
        grid=(batch, heads),
        in_specs=[spec, spec, spec],
        out_specs=spec,
        out_shape=jax.ShapeDtypeStruct((t, hd), BF16),
        scratch_shapes=[pltpu.VMEM((seq + blk, HEAD_DIM), BF16), pltpu.VMEM((seq + blk, HEAD_DIM), BF16)],
        compiler_params=_params("parallel", "parallel"),
    )(q, k, v)


_SB_HEAD_GROUP = 32


def _sb_probe_kernel(q_ref, kn_ref, kc_ref, alive_ref, *, blk):
    tq = q_ref.shape[0]
    row = lax.broadcasted_iota(jnp.int32, (tq, blk), 0)
    col = lax.broadcasted_iota(jnp.int32, (tq, blk), 1)
    kb = pltpu.einshape("thd->htd", kc_ref[...]).astype(BF16)
    cmax = jnp.full((tq, 1), NEG_INF, F32)
    for g in range(_SB_HEAD_GROUP):
        lanes = slice(g * HEAD_DIM, (g + 1) * HEAD_DIM)
        q = q_ref[:, lanes]
        c = jnp.sum(_sb_log_terms(q, kn_ref[:, lanes], col < row)[1], axis=1, keepdims=True)
        c = c + jnp.sum(_sb_log_terms(q, kb[g], None)[1], axis=1, keepdims=True)
        cmax = jnp.maximum(cmax, c)
    flag = jnp.where(jnp.max(cmax) > _SB_DEAD_BELOW - 1.0, 1, 0)
    alive_ref[...] = jnp.full(alive_ref.shape, flag, jnp.int32)


def _sb_sample_kernel(alive_ref, q_ref, kn_ref, vn_ref, kc_ref, vc_ref, o_ref, c_ref, acc_ref, *, blk):
    del alive_ref
    n = pl.program_id(2)
    tq = q_ref.shape[0]
    tri = _strict_lower_ones(blk)

    @pl.when(n == 0)
    def _():
        row = lax.broadcasted_iota(jnp.int32, (tq, blk), 0)
        col = lax.broadcasted_iota(jnp.int32, (tq, blk), 1)
        for g in range(_SB_HEAD_GROUP):
            lanes = slice(g * HEAD_DIM, (g + 1) * HEAD_DIM)
            c, acc = _sb_block(q_ref[:, lanes], kn_ref[:, lanes], vn_ref[:, lanes],
                               jnp.zeros((tq, 1), F32), jnp.zeros((tq, HEAD_DIM), F32), tri, col < row)
            c_ref[g] = c
            acc_ref[g] = acc

    @pl.when(jnp.max(c_ref[...]) > _SB_DEAD_BELOW)
    def _():
        kb = pltpu.einshape("thd->htd", kc_ref[...]).astype(BF16)
        vb = pltpu.einshape("thd->htd", vc_ref[...]).astype(BF16)
        for g in range(_SB_HEAD_GROUP):
            lanes = slice(g * HEAD_DIM, (g + 1) * HEAD_DIM)
            c, acc = _sb_block(q_ref[:, lanes], kb[g], vb[g], c_ref[g], acc_ref[g], tri, None)
            c_ref[g] = c
            acc_ref[g] = acc

    @pl.when(n == pl.num_programs(2) - 1)
    def _():
        for g in range(_SB_HEAD_GROUP):
            o_ref[:, g * HEAD_DIM:(g + 1) * HEAD_DIM] = acc_ref[g].astype(o_ref.dtype)


def _sb_sample(q, k_new, v_new, cache_k, cache_v, *, batch, tq, blk=256):
    t, hd = q.shape
    heads = hd // HEAD_DIM
    past = cache_k.shape[1]
    nblk = past // blk
    gw = _SB_HEAD_GROUP * HEAD_DIM

    def pad_block(a):
        a = a.reshape(batch, tq, hd)
        a = jnp.pad(a, ((0, 0), (0, blk - tq), (0, 0)))
        return a.reshape(batch * blk, hd).astype(BF16)

    assert heads == _SB_HEAD_GROUP
    kn = pad_block(k_new)
    vn = pad_block(v_new)
    newest = (None, blk, _SB_HEAD_GROUP, HEAD_DIM)
    alive = pl.pallas_call(
        functools.partial(_sb_probe_kernel, blk=blk),
        grid=(batch,),
        in_specs=[pl.BlockSpec((tq, gw), lambda b: (b, 0)),
                  pl.BlockSpec((blk, gw), lambda b: (b, 0)),
                  pl.BlockSpec(newest, lambda b: (b, nblk - 1, 0, 0))],
        out_specs=pl.BlockSpec((None, 8, LANES), lambda b: (b, 0, 0)),
        out_shape=jax.ShapeDtypeStruct((batch, 8, LANES), jnp.int32),
        compiler_params=_params("parallel"),
    )(q, kn, cache_k)[:, 0, 0]

    def cache_block(b, h, n, alive_ref):
        return (b, nblk - 1 - jnp.where(alive_ref[b] > 0, n, 0), h, 0)

    qspec = pl.BlockSpec((tq, gw), lambda b, h, n, alive_ref: (b, h))
    nspec = pl.BlockSpec((blk, gw), lambda b, h, n, alive_ref: (b, h))
    cspec = pl.BlockSpec(newest, cache_block)
    return pl.pallas_call(
        functools.partial(_sb_sample_kernel, blk=blk),
        grid_spec=pltpu.PrefetchScalarGridSpec(
            num_scalar_prefetch=1,
            grid=(batch, heads // _SB_HEAD_GROUP, nblk),
            in_specs=[qspec, nspec, nspec, cspec, cspec],
            out_specs=qspec,
            scratch_shapes=[pltpu.VMEM((_SB_HEAD_GROUP, tq, 1), F32),
                            pltpu.VMEM((_SB_HEAD_GROUP, tq, HEAD_DIM), F32)]),
        out_shape=jax.ShapeDtypeStruct((t, hd), BF16),
        compiler_params=_params("parallel", "parallel", "arbitrary"),
    )(alive, q, kn, vn, cache_k, cache_v)


_BAND_UNROLL = 8


def _band_kernel(q_ref, k_ref, v_ref, bias_ref, o_ref, *, cq, win, nq, chunked, lo_static):
    bias = bias_ref[...].reshape(G_B * cq, win)
    col = lax.broadcasted_iota(jnp.int32, (G_B * cq, win), 1)

    def chunk(c, _):
        r0 = pl.multiple_of(c * cq, cq)
        if chunked:
            lo = jnp.maximum(CHUNK, (N_PREV_CHUNKS + 1 - c) * CHUNK)
        else:
            lo = lo_static
        kb = k_ref[pl.ds(r0, win), :]
        vb = v_ref[pl.ds(r0, win), :]
        qb = q_ref[pl.ds(r0, cq), :]
        q4 = jnp.concatenate([qb[:, g * HEAD_DIM:(g + 1) * HEAD_DIM] for g in range(G_B)], axis=0)
        s = lax.dot_general(q4, kb, _NT, preferred_element_type=F32) + bias
        s = jnp.where(col >= lo, s, NEG_INF)
        m = jnp.max(s, axis=1, keepdims=True)
        e = jnp.exp(s - m)
        o = jnp.dot(e.astype(BF16), vb, preferred_element_type=F32) * (1.0 / jnp.sum(e, axis=1, keepdims=True))
        for g in range(G_B):
            o_ref[pl.ds(r0, cq), g * HEAD_DIM:(g + 1) * HEAD_DIM] = o[g * cq:(g + 1) * cq].astype(o_ref.dtype)
        return 0

    lax.fori_loop(0, nq, chunk, 0, unroll=_BAND_UNROLL if nq % _BAND_UNROLL == 0 else 1)


def _band_bias(rel_table, cq, win):
    lg = win + cq - 1
    idx = np.clip(win - 1 - np.arange(lg), -REL_CLIP, REL_CLIP) + REL_CLIP
    g = rel_table[idx].astype(F32).T
    rows = jnp.tile(g, (1, cq + 1))[:, :cq * (lg + 1)].reshape(-1, cq, lg + 1)
    return rows[:, ::-1, :win]


def _band_attention(q, kpad, vpad, bias, *, batch, nq, cq, win, chunked, lo_static):
    t, hd = q.shape
    rows = kpad.shape[1]
    qw = G_B * HEAD_DIM
    qspec = pl.BlockSpec((nq * cq, qw), lambda b, h: (b, h))
    kspec = pl.BlockSpec((None, rows, HEAD_DIM), lambda b, h: (b, 0, h))
    bspec = pl.BlockSpec((G_B, cq, win), lambda b, h: (h, 0, 0))
    return pl.pallas_call(
        functools.partial(_band_kernel, cq=cq, win=win, nq=nq, chunked=chunked, lo_static=lo_static),
        grid=(batch, N_KV_B),
        in_specs=[qspec, kspec, kspec, bspec],
        out_specs=qspec,
        out_shape=jax.ShapeDtypeStruct((t, hd), BF16),
        compiler_params=_params("parallel", "parallel"),
    )(q, kpad, vpad, bias)


def _split3_dot_nt(a, b):
    a_hi = a.astype(BF16)
    a_lo = (a - a_hi.astype(F32)).astype(BF16)
    b_hi = b.astype(BF16)
    b_lo = (b - b_hi.astype(F32)).astype(BF16)
    return (lax.dot_general(a_hi, b_hi, _NT, preferred_element_type=F32)
            + lax.dot_general(a_lo, b_hi, _NT, preferred_element_type=F32)
            + lax.dot_general(a_hi, b_lo, _NT, preferred_element_type=F32))


def _take_lanes(x, idx):
    return jnp.take_along_axis(x, idx, axis=1, mode="promise_in_bounds")


_ROUTE_ROWS = 64
_ROUTE_UNROLL = 64


def _route_kernel(q_ref, sk_ref, pairs_ref, e_ref, g_ref, s_ref, left_ref, si_ref, cleft_ref, fpos_ref):
    tm = q_ref.shape[0]
    dh = sk_ref.shape[-1]
    nset = 2 * PEER_HEADS
    lane_c = lax.broadcasted_iota(jnp.int32, (_ROUTE_ROWS, PEER_KEYS), 1)

    for n in range(nset):
        sc = _split3_dot_nt(q_ref[:, n * dh:(n + 1) * dh], sk_ref[n // 2, n % 2])
        s_ref[n * tm:(n + 1) * tm, :] = sc
        left_ref[n * tm:(n + 1) * tm, :] = sc
    si_ref[...] = jnp.zeros(si_ref.shape, jnp.int32)

    def body1(i, _):
        def piece(r, _):
            rows = pl.ds(pl.multiple_of(r * _ROUTE_ROWS, _ROUTE_ROWS), _ROUTE_ROWS)
            left = left_ref[rows, :]
            pos = jnp.argmax(left, axis=1, keepdims=True).astype(jnp.int32)
            left_ref[rows, :] = jnp.where(lane_c == pos, NEG_INF, left)
            si_ref[rows, :] = jnp.where(lane_c == i, pos, si_ref[rows, :])
            return 0

        return lax.fori_loop(0, nset * tm // _ROUTE_ROWS, piece, 0, unroll=_ROUTE_UNROLL)

    lax.fori_loop(0, PEER_TOPK, body1, 0)
    si = si_ref[...]
    sv = _take_lanes(s_ref[...], si)
    si = si.astype(F32)

    pair_a = jnp.broadcast_to(pairs_ref[0:1, :], (tm, PEER_KEYS))
    pair_b = jnp.broadcast_to(pairs_ref[1:2, :], (tm, PEER_KEYS))
    pair_ok = jnp.broadcast_to(pairs_ref[2:3, :], (tm, PEER_KEYS)) > 0
    cand, cidx = [], []
    for h in range(PEER_HEADS):
        r0 = slice((2 * h) * tm, (2 * h + 1) * tm)
        r1 = slice((2 * h + 1) * tm, (2 * h + 2) * tm)
        cand.append(jnp.where(pair_ok, _take_lanes(sv[r0], pair_a) + _take_lanes(sv[r1], pair_b), NEG_INF))
        cidx.append(_take_lanes(si[r0], pair_a) * float(PEER_KEYS) + _take_lanes(si[r1], pair_b))
    cand = jnp.concatenate(cand, axis=0)
    cidx = jnp.concatenate(cidx, axis=0)

    lane2 = lax.broadcasted_iota(jnp.int32, cand.shape, 1)
    head_lane0 = (lax.broadcasted_iota(jnp.int32, cand.shape, 0) // tm) * PEER_TOPK
    cleft_ref[...] = cand
    fpos_ref[...] = jnp.zeros(fpos_ref.shape, jnp.int32)

    def body2(i, _):
        def piece(r, _):
            row0 = pl.multiple_of(r * _ROUTE_ROWS, _ROUTE_ROWS)
            rows = pl.ds(row0, _ROUTE_ROWS)
            left = cleft_ref[rows, :]
            pos = jnp.argmax(left, axis=1, keepdims=True).astype(jnp.int32)
            cleft_ref[rows, :] = jnp.where(lane_c == pos, NEG_INF, left)
            out_lane = (row0 // tm) * PEER_TOPK + i
            fpos_ref[rows, :] = jnp.where(lane_c == out_lane, pos, fpos_ref[rows, :])
            return 0

        return lax.fori_loop(0, PEER_HEADS * tm // _ROUTE_ROWS, piece, 0, unroll=_ROUTE_UNROLL)

    lax.fori_loop(0, PEER_TOPK, body2, 0)
    fpos = fpos_ref[...]
    fv = _take_lanes(cand, fpos)
    fe = _take_lanes(cidx, fpos)

    grp = lane2 // PEER_TOPK == head_lane0 // PEER_TOPK
    fe = jnp.where(grp, fe, 0.0)
    mx = jnp.max(jnp.where(grp, fv, NEG_INF), axis=1, keepdims=True)
    ex = jnp.where(grp, jnp.exp(fv - mx), 0.0)
    gate = ex / jnp.sum(ex, axis=1, keepdims=True)
    e_out = fe[0:tm]
    g_out = gate[0:tm]
    for h in range(1, PEER_HEADS):
        e_out = e_out + fe[h * tm:(h + 1) * tm]
        g_out = g_out + gate[h * tm:(h + 1) * tm]
    e_ref[...] = e_out.astype(jnp.int32)
    g_ref[...] = g_out


def _peer_route(qp, sub_keys):
    m, kq = qp.shape
    tm = min(256, m)
    ne = PEER_HEADS * PEER_TOPK
    ospec = pl.BlockSpec((tm, ne), lambda i: (i, 0))
    pairs = [(a, b) for a in range(PEER_TOPK) for b in range(PEER_TOPK) if (a + 1) * (b + 1) <= PEER_TOPK]
    pad = [0] * (PEER_KEYS - len(pairs))
    pair_rows = jnp.array([[a for a, _ in pairs] + pad, [b for _, b in pairs] + pad,
                           [1] * len(pairs) + pad] + [[0] * PEER_KEYS] * 5, jnp.int32)
    return pl.pallas_call(
        _route_kernel,
        grid=(m // tm,),
        in_specs=[pl.BlockSpec((tm, kq), lambda i: (i, 0)),
                  pl.BlockSpec(sub_keys.shape, lambda i: (0, 0, 0, 0)),
                  pl.BlockSpec(pair_rows.shape, lambda i: (0, 0))],
        out_specs=[ospec, ospec],
        out_shape=[jax.ShapeDtypeStruct((m, ne), jnp.int32), jax.ShapeDtypeStruct((m, ne), F32)],
        scratch_shapes=[pltpu.VMEM((2 * PEER_HEADS * tm, PEER_KEYS), F32),
                        pltpu.VMEM((2 * PEER_HEADS * tm, PEER_KEYS), F32),
                        pltpu.VMEM((2 * PEER_HEADS * tm, PEER_KEYS), jnp.int32),
                        pltpu.VMEM((PEER_HEADS * tm, PEER_KEYS), F32),
                        pltpu.VMEM((PEER_HEADS * tm, PEER_KEYS), jnp.int32)],
        compiler_params=_params("parallel"),
    )(qp, sub_keys, pair_rows)


def _gelu_tanh(x):
    return 0.5 * x * (1.0 + jnp.tanh(0.7978845608028654 * (x + 0.044715 * (x * x * x))))


_TOKEN_GROUP = 16


def _expert_kernel(h_ref, gain_ref, e_ref, g_ref, tab_ref, o_ref,
                   xn_ref, hi_ref, lo_ref, act_ref, d_ref, w3_ref, *, nj):
    j = pl.program_id(1)
    tm = h_ref.shape[0]
    te = tab_ref.shape[0]
    nb = te // LANES

    @pl.when(j == 0)
    def _():
        xn_ref[...] = _norm_rows(h_ref[...], gain_ref[...]).astype(BF16)
        o_ref[...] = h_ref[...]
        e = e_ref[...]
        hi_ref[...] = e // LANES
        lo_ref[...] = e % LANES
        act_ref[...] = jnp.zeros(act_ref.shape, F32)

    def score_slab(slot):
        d_ref[slot] = lax.dot_general(xn_ref[...], tab_ref[...], _NT, preferred_element_type=F32)

    def pick_from_slab(slot, slab):
        hi = hi_ref[...]
        lo = lo_ref[...]
        act = act_ref[...]
        for s in range(nb):
            picked = _take_lanes(d_ref[slot, :, s * LANES:(s + 1) * LANES], lo)
            act = jnp.where(hi == slab * nb + s, picked, act)
        act_ref[...] = act

    @pl.when(j == 0)
    def _():
        score_slab(0)

    @pl.when(jnp.logical_and(j >= 1, j < nj))
    def _():
        pick_from_slab((j - 1) % 2, j - 1)
        score_slab(j % 2)

    @pl.when(j == nj)
    def _():
        pick_from_slab((nj - 1) % 2, nj - 1)
        act_ref[...] = g_ref[...] * _gelu_tanh(act_ref[...])
        sub = lax.broadcasted_iota(jnp.int32, (LANES, LANES), 0)

        def group(gi, _):
            t0 = pl.multiple_of(gi * _TOKEN_GROUP, _TOKEN_GROUP)
            grids = []
            for u in range(_TOKEN_GROUP):
                hi_row = hi_ref[pl.ds(t0 + u, 1), :]
                lo_row = lo_ref[pl.ds(t0 + u, 1), :]
                w_row = act_ref[pl.ds(t0 + u, 1), :]
                a_t = jnp.where(hi_row == sub, w_row, 0.0).astype(BF16)
                b_t = jnp.where(lo_row == sub, 1.0, 0.0).astype(BF16)
                grids.append(lax.dot_general(a_t, b_t, _NT, preferred_element_type=F32))
            w3_ref[:, pl.ds(t0, _TOKEN_GROUP), :] = pltpu.einshape(
                "uik->iuk", jnp.stack(grids, axis=0)).astype(BF16)
            return 0

        lax.fori_loop(0, tm // _TOKEN_GROUP, group, 0, unroll=4)

    @pl.when(j >= nj)
    def _():
        jj = j - nj
        w = jnp.concatenate([w3_ref[jj * nb + s] for s in range(nb)], axis=1)
        d = o_ref.shape[1]
        for c0 in range(0, d, _EXPERT_OUT_COLS):
            cols = slice(c0, c0 + _EXPERT_OUT_COLS)
            o_ref[:, cols] += jnp.dot(w, tab_ref[:, cols], preferred_element_type=F32)


_EXPERT_OUT_COLS = 1024


def _peer_experts(h, gain, eidx, gate, uv_tab, layer, *, te=512):
    m, d = h.shape
    ne = uv_tab.shape[2]
    nsel = eidx.shape[1]
    tm = min(512, m)
    nj = ne // te
    once = pl.Buffered(1)
    sel_spec = pl.BlockSpec((tm, nsel), lambda i, j: (i, 0), pipeline_mode=once)
    return pl.pallas_call(
        functools.partial(_expert_kernel, nj=nj),
        grid=(m // tm, 2 * nj),
        in_specs=[pl.BlockSpec((tm, d), lambda i, j: (i, 0)),
                  pl.BlockSpec((1, d), lambda i, j: (0, 0), pipeline_mode=once),
                  sel_spec, sel_spec,
                  pl.BlockSpec((None, None, te, d), lambda i, j: (layer, j // nj, j % nj, 0))],
        out_specs=pl.BlockSpec((tm, d), lambda i, j: (i, 0), pipeline_mode=once),
        out_shape=jax.ShapeDtypeStruct((m, d), F32),
        scratch_shapes=[pltpu.VMEM((tm, d), BF16),
                        pltpu.VMEM((tm, nsel), jnp.int32),
                        pltpu.VMEM((tm, nsel), jnp.int32),
                        pltpu.VMEM((tm, nsel), F32),
                        pltpu.VMEM((2, tm, te), F32),
                        pltpu.VMEM((ne // LANES, tm, LANES), BF16)],
        compiler_params=_params("parallel", "arbitrary"),
    )(h, gain.reshape(1, d), eidx, gate, uv_tab)


def _peer_ffn(h, i, prm):
    qp = _matmul(h, prm["w_q_peer"][i], gain=prm["g_ffn"][i])
    eidx, gate = _peer_route(qp, prm["peer_sub_keys"][i])
    return _peer_experts(h, prm["g_ffn"][i], eidx, gate, prm["peer_uv"], i)


def _ple(h, p, i, prm):
    return _matmul(h, prm["w_ple_gate"][i], gain=prm["g_ple"][i], residual=h,
                   ple=(p, prm["w_ple_proj"][i]))


def _trunk(x, pe, past, prm):
    b, t, d = x.shape
    m = b * t
    h = x.reshape(m, d)
    pe = pe.reshape(pe.shape[0], m, pe.shape[-1])

    q, k, v = _qkv_proj(h, prm["w_qkv_a"], prm["g_mix"][0])
    if past is None:
        o = _sb_prompt(q, k, v, batch=b, seq=t)
    else:
        o = _sb_sample(q, k, v, past[0][0], past[1][0], batch=b, tq=t)
    h = _matmul(o, prm["w_o_a"], residual=h)
    h = _peer_ffn(h, 0, prm)
    h = _ple(h, pe[0], 0, prm)

    kv = _matmul(h, prm["w_kv_b"], gain=prm["g_kv"])
    nkv = N_KV_B * HEAD_DIM
    kb_new = kv[:, :nkv].reshape(b, t, nkv)
    vb_new = kv[:, nkv:].reshape(b, t, nkv)

    qb = _matmul(h, prm["w_q_b"], gain=prm["g_mix"][1], out_dtype=BF16, out_scale=SCALE)
    win = (N_PREV_CHUNKS + 2) * CHUNK
    if past is None:
        front = (N_PREV_CHUNKS + 1) * CHUNK
        kpad = jnp.pad(kb_new.astype(BF16), ((0, 0), (front, 0), (0, 0)))
        vpad = jnp.pad(vb_new.astype(BF16), ((0, 0), (front, 0), (0, 0)))
        bias = _band_bias(prm["rel_bias_b"][0], CHUNK, win)
        ob = _band_attention(qb, kpad, vpad, bias, batch=b, nq=t // CHUNK, cq=CHUNK, win=win,
                             chunked=True, lo_static=0)
    else:
        ck = past[2].reshape(b, -1, nkv)
        cv = past[3].reshape(b, -1, nkv)
        front = win - ck.shape[1] - t
        kpad = jnp.pad(jnp.concatenate([ck, kb_new], axis=1).astype(BF16), ((0, 0), (front, 0), (0, 0)))
        vpad = jnp.pad(jnp.concatenate([cv, vb_new], axis=1).astype(BF16), ((0, 0), (front, 0), (0, 0)))
        bias = _band_bias(prm["rel_bias_b"][0], t, win)
        ob = _band_attention(qb, kpad, vpad, bias, batch=b, nq=1, cq=t, win=win,
                             chunked=False, lo_static=front)
    h = _matmul(ob, prm["w_o_b"], residual=h)
    h = _peer_ffn(h, 1, prm)
    h = _ple(h, pe[1], 1, prm)

    y = _rmsnorm(h, prm["g_final"]).reshape(b, t, d)
    heads = k.shape[1] // HEAD_DIM
    a_k = k.reshape(1, b, t, heads, HEAD_DIM)
    a_v = v.reshape(1, b, t, heads, HEAD_DIM)
    if past is None:
        keep = min(N_PREV_CHUNKS * CHUNK, t)
        b_k, b_v = kb_new[:, t - keep:], vb_new[:, t - keep:]
    else:
        b_k, b_v = kb_new, vb_new
    b_k = b_k.reshape(b, -1, N_KV_B, HEAD_DIM)
    b_v = b_v.reshape(b, -1, N_KV_B, HEAD_DIM)
    return y, a_k, a_v, b_k, b_v


def kernel(x_prompt, x_sample, cache_a_k, cache_a_v, cache_b_k, cache_b_v, p_prompt, p_sample, g_mix, w_qkv_a, w_o_a, g_kv, w_kv_b, w_q_b, rel_bias_b, w_o_b, g_ffn, w_q_peer, peer_sub_keys, peer_u, peer_v, g_ple, w_ple_gate, w_ple_proj, g_final):
    prm = dict(
        g_mix=g_mix, g_kv=g_kv, g_ffn=g_ffn, g_ple=g_ple, g_final=g_final,
        w_qkv_a=_to_bf16(w_qkv_a)[0],
        w_o_a=_to_bf16(w_o_a)[0],
        w_kv_b=_to_bf16(w_kv_b),
        w_q_b=_to_bf16(w_q_b)[0],
        w_o_b=_to_bf16(w_o_b)[0],
        rel_bias_b=rel_bias_b,
        w_q_peer=_to_bf16(w_q_peer),
        peer_sub_keys=peer_sub_keys,
        peer_uv=_peer_tables_bf16(peer_u, peer_v),
        w_ple_gate=_to_bf16(w_ple_gate),
        w_ple_proj=_to_bf16(w_ple_proj),
    )
    y_p, ak_p, av_p, bk_p, bv_p = _trunk(x_prompt, p_prompt, None, prm)
    y_s, ak_s, av_s, bk_s, bv_s = _trunk(x_sample, p_sample,
                                         (cache_a_k, cache_a_v, cache_b_k, cache_b_v), prm)
    return (y_p, y_s, ak_p, av_p, bk_p, bv_p, ak_s, av_s, bk_s, bv_s)
```

```python
import functools

import jax
import jax.numpy as jnp
import numpy as np
from jax import lax
from jax.experimental import pallas as pl
from jax.experimental.pallas import tpu as pltpu

F32 = jnp.float32
BF16 = jnp.bfloat16

EPS = 1e-6
HEAD_DIM = 128
CHUNK = 64
N_PREV_CHUNKS = 8
N_KV_B = 8
G_B = 4
REL_CLIP = 128
PEER_HEADS = 8
PEER_KEYS = 128
PEER_TOPK = 16
SCALE = HEAD_DIM ** -0.5
NEG_INF = float("-inf")

VMEM_LIMIT_BYTES = 60 * 1024 * 1024
LANES = 128

ROW_TILE = 512
COL_TILE = 1024
ROUTE_ROW_TILE = 256
EXPERT_SLAB = 512
SB_BLOCK = 256

_NT = (((1,), (1,)), ((), ()))


def _params(*sem):
    return pltpu.CompilerParams(dimension_semantics=sem, vmem_limit_bytes=VMEM_LIMIT_BYTES)


def _norm_rows(x, g):
    ms = jnp.mean(x * x, axis=-1, keepdims=True)
    return x * lax.rsqrt(ms + EPS) * g


def _sigmoid(x):
    return 1.0 / (1.0 + jnp.exp(-x))


def _normed_rows_pipeline(x_ref, g_ref, xn_ref, step):
    i = pl.program_id(0)
    j = pl.program_id(1)
    last = pl.num_programs(1) - 1
    slot = i % 2

    def normalise(dst):
        xn_ref[dst] = _norm_rows(x_ref[...], g_ref[...]).astype(BF16)

    @pl.when(jnp.logical_and(i == 0, j == 0))
    def _():
        normalise(0)

    @pl.when(j < last)
    def _():
        step(slot, lambda: None)

    @pl.when(j == last)
    def _():
        step(slot, lambda: normalise(1 - slot))


def _next_tile_rows(ni, nj):
    return lambda i, j: (jnp.minimum(i + jnp.where(j == nj - 1, 1, 0), ni - 1), 0)


def _mm_kernel(*refs, norm, pipelined, res, ple, out_scale):
    it = iter(refs)
    x_ref = next(it)
    g_ref = next(it) if norm else None
    w_ref = next(it)
    res_ref = next(it) if res else None
    p_ref = next(it) if ple else None
    wp_ref = next(it) if ple else None
    o_ref = next(it)
    xn_ref = next(it) if norm else None

    def step(x):
        acc = jnp.dot(x, w_ref[...], preferred_element_type=F32)
        if ple:
            proj = jnp.dot(p_ref[...].astype(BF16), wp_ref[...], preferred_element_type=F32)
            acc = _sigmoid(acc) * proj
        if res:
            acc = acc + res_ref[...]
        if out_scale is not None:
            acc = acc * out_scale
        o_ref[...] = acc.astype(o_ref.dtype)

    if pipelined:
        def pipelined_step(slot, after):
            step(xn_ref[slot])
            after()

        _normed_rows_pipeline(x_ref, g_ref, xn_ref, pipelined_step)
    elif norm:
        @pl.when(pl.program_id(1) == 0)
        def _():
            xn_ref[...] = _norm_rows(x_ref[...], g_ref[...]).astype(BF16)
        step(xn_ref[...])
    else:
        step(x_ref[...])


def _matmul(x, w, *, gain=None, residual=None, ple=None, out_dtype=F32, out_scale=None, tn=COL_TILE):
    m, k = x.shape
    n = w.shape[1]
    norm = gain is not None
    tm = min(ROW_TILE if norm else 2 * ROW_TILE, m)
    tn = min(tn, n)
    pipelined = norm and ple is None and n // tn >= 2
    x_map = _next_tile_rows(m // tm, n // tn) if pipelined else (lambda i, j: (i, 0))
    in_specs = [pl.BlockSpec((tm, k), x_map)]
    args = [x]
    if norm:
        in_specs.append(pl.BlockSpec((1, k), lambda i, j: (0, 0)))
        args.append(gain.reshape(1, k))
    in_specs.append(pl.BlockSpec((k, tn), lambda i, j: (0, j)))
    args.append(w)
    if residual is not None:
        in_specs.append(pl.BlockSpec((tm, tn), lambda i, j: (i, j)))
        args.append(residual)
    if ple is not None:
        p, wp = ple
        kp = p.shape[1]
        in_specs.append(pl.BlockSpec((tm, kp), lambda i, j: (i, 0)))
        in_specs.append(pl.BlockSpec((kp, tn), lambda i, j: (0, j)))
        args += [p, wp]
    return pl.pallas_call(
        functools.partial(_mm_kernel, norm=norm, pipelined=pipelined, res=residual is not None,
                          ple=ple is not None, out_scale=out_scale),
        grid=(m // tm, n // tn),
        in_specs=in_specs,
        out_specs=pl.BlockSpec((tm, tn), lambda i, j: (i, j)),
        out_shape=jax.ShapeDtypeStruct((m, n), out_dtype),
        scratch_shapes=([pltpu.VMEM((2, tm, k), BF16)] if pipelined else
                        [pltpu.VMEM((tm, k), BF16)] if norm else []),
        compiler_params=_params("arbitrary" if pipelined else "parallel", "arbitrary"),
    )(*args)


def _qkv_kernel(x_ref, g_ref, w_ref, q_ref, k_ref, v_ref, xn_ref, *, nq):
    j = pl.program_id(1)

    def step(slot, after):
        acc = jnp.dot(xn_ref[slot], w_ref[...], preferred_element_type=F32)
        after()

        @pl.when(j < nq)
        def _():
            q_ref[...] = (acc * SCALE).astype(q_ref.dtype)

        @pl.when(jnp.logical_and(j >= nq, j < 2 * nq))
        def _():
            k_ref[...] = acc

        @pl.when(j >= 2 * nq)
        def _():
            v_ref[...] = acc

    _normed_rows_pipeline(x_ref, g_ref, xn_ref, step)


def _qkv_proj(x, w, gain, *, tn=COL_TILE):
    m, k = x.shape
    n = w.shape[1] // 3
    tm = min(ROW_TILE, m)
    nq = n // tn

    def ospec(first):
        return pl.BlockSpec((tm, tn), lambda i, j: (i, jnp.clip(j - first, 0, nq - 1)))

    return pl.pallas_call(
        functools.partial(_qkv_kernel, nq=nq),
        grid=(m // tm, 3 * nq),
        in_specs=[pl.BlockSpec((tm, k), _next_tile_rows(m // tm, 3 * nq)),
                  pl.BlockSpec((1, k), lambda i, j: (0, 0)),
                  pl.BlockSpec((k, tn), lambda i, j: (0, j))],
        out_specs=[ospec(0), ospec(nq), ospec(2 * nq)],
        out_shape=[jax.ShapeDtypeStruct((m, n), BF16), jax.ShapeDtypeStruct((m, n), F32),
                   jax.ShapeDtypeStruct((m, n), F32)],
        scratch_shapes=[pltpu.VMEM((2, tm, k), BF16)],
        compiler_params=_params("arbitrary", "arbitrary"),
    )(x, gain.reshape(1, k), w)


def _cast_kernel(x_ref, o_ref):
    o_ref[...] = x_ref[...].astype(o_ref.dtype)


_CAST_ROWS = 256
_CAST_COLS = 4096


def _to_bf16(x):
    x2 = x.reshape(-1, x.shape[-1])
    rows, cols = x2.shape
    tr = min(_CAST_ROWS, rows)
    tc = min(_CAST_COLS, cols)
    spec = pl.BlockSpec((tr, tc), lambda i, j: (i, j))
    out = pl.pallas_call(
        _cast_kernel,
        grid=(rows // tr, cols // tc),
        in_specs=[spec],
        out_specs=spec,
        out_shape=jax.ShapeDtypeStruct((rows, cols), BF16),
        compiler_params=_params("parallel", "parallel"),
    )(x2)
    return out.reshape(x.shape)


def _stack_cast_kernel(u_ref, v_ref, o_ref):
    c = pl.program_id(1)

    @pl.when(c == 0)
    def _():
        o_ref[...] = u_ref[...].astype(o_ref.dtype)

    @pl.when(c == 1)
    def _():
        o_ref[...] = v_ref[...].astype(o_ref.dtype)


def _peer_tables_bf16(u, v):
    nl, ne, d = u.shape
    te = _CAST_ROWS
    nblk = ne // te
    uspec = pl.BlockSpec((None, te, d), lambda l, c, e: (l, jnp.where(c == 0, e, nblk - 1), 0))
    vspec = pl.BlockSpec((None, te, d), lambda l, c, e: (l, jnp.where(c == 1, e, 0), 0))
    return pl.pallas_call(
        _stack_cast_kernel,
        grid=(nl, 2, nblk),
        in_specs=[uspec, vspec],
        out_specs=pl.BlockSpec((None, None, te, d), lambda l, c, e: (l, c, e, 0)),
        out_shape=jax.ShapeDtypeStruct((nl, 2, ne, d), BF16),
        compiler_params=_params("parallel", "arbitrary", "arbitrary"),
    )(u, v)


def _rmsnorm_kernel(x_ref, g_ref, o_ref):
    o_ref[...] = _norm_rows(x_ref[...], g_ref[...])


def _rmsnorm(x, gain):
    m, k = x.shape
    tm = min(ROW_TILE, m)
    return pl.pallas_call(
        _rmsnorm_kernel,
        grid=(m // tm,),
        in_specs=[pl.BlockSpec((tm, k), lambda i: (i, 0)), pl.BlockSpec((1, k), lambda i: (0, 0))],
        out_specs=pl.BlockSpec((tm, k), lambda i: (i, 0)),
        out_shape=jax.ShapeDtypeStruct((m, k), F32),
        compiler_params=_params("parallel"),
    )(x, gain.reshape(1, k))


def _strict_lower_ones(n):
    r = lax.broadcasted_iota(jnp.int32, (2 * n, n), 0) % n
    c = lax.broadcasted_iota(jnp.int32, (2 * n, n), 1)
    return jnp.where(r > c, 1.0, 0.0).astype(BF16)


_SB_DEAD_BELOW = -104.0


def _sb_log_terms(q, kb, mask):
    z = lax.dot_general(q, kb, _NT, preferred_element_type=F32)
    nz = -z
    sp = jnp.log(1.0 + jnp.exp(-jnp.maximum(z, nz)))
    log_beta = jnp.minimum(z, 0.0) - sp
    log_keep = jnp.minimum(nz, 0.0) - sp
    if mask is not None:
        log_keep = jnp.where(mask, log_keep, 0.0)
    return log_beta, log_keep


def _sb_block(q, kb, vb, c, acc, tri2, mask):
    log_beta, log_keep = _sb_log_terms(q, kb, mask)
    hi = log_keep.astype(BF16)
    lo = (log_keep - hi.astype(F32)).astype(BF16)
    after = jnp.dot(jnp.concatenate([hi, lo], axis=1), tri2, preferred_element_type=F32)
    w = jnp.exp(log_beta + after + c)
    if mask is not None:
        w = jnp.where(mask, w, 0.0)
    acc = acc + jnp.dot(w.astype(BF16), vb, preferred_element_type=F32)
    c = c + jnp.sum(log_keep, axis=1, keepdims=True)
    return c, acc


_SB_QUERY_GROUP = 8


def _sb_prompt_kernel(q_ref, k_ref, v_ref, o_ref, kb_ref, vb_ref, *, blk):
    seq = q_ref.shape[0]
    kb_ref[pl.ds(0, blk), :] = jnp.zeros((blk, HEAD_DIM), BF16)
    vb_ref[pl.ds(0, blk), :] = jnp.zeros((blk, HEAD_DIM), BF16)
    kb_ref[pl.ds(blk, seq), :] = k_ref[...].astype(BF16)
    vb_ref[pl.ds(blk, seq), :] = v_ref[...].astype(BF16)
    tri = _strict_lower_ones(blk)
    row = lax.broadcasted_iota(jnp.int32, (blk, blk), 0)
    col = lax.broadcasted_iota(jnp.int32, (blk, blk), 1)
    diag_mask = col < row

    def kv_block(j):
        r0 = pl.multiple_of((j + 1) * blk, blk)
        return kb_ref[pl.ds(r0, blk), :], vb_ref[pl.ds(r0, blk), :]

    def older_blocks(i, q, c, acc):
        def k_cond(carry):
            j, cmax, _, _ = carry
            return jnp.logical_and(j >= 0, cmax > _SB_DEAD_BELOW)

        def k_body(carry):
            j, _, c, acc = carry
            c, acc = _sb_block(q, *kv_block(j), c, acc, tri, None)
            return j - 1, jnp.max(c), c, acc

        return lax.while_loop(k_cond, k_body, (i - 2, jnp.max(c), c, acc))[3]

    def group_body(p, _):
        state = []
        for i in [_SB_QUERY_GROUP * p + u for u in range(_SB_QUERY_GROUP)]:
            q = q_ref[pl.ds(pl.multiple_of(i * blk, blk), blk), :]
            c = jnp.zeros((blk, 1), F32)
            acc = jnp.zeros((blk, HEAD_DIM), F32)
            c, acc = _sb_block(q, *kv_block(i), c, acc, tri, diag_mask)
            c, acc = _sb_block(q, *kv_block(i - 1), c, acc, tri, None)
            state.append((i, q, c, acc))
        for i, q, c, acc in state:
            acc = older_blocks(i, q, c, acc)
            o_ref[pl.ds(pl.multiple_of(i * blk, blk), blk), :] = acc.astype(o_ref.dtype)
        return 0

    lax.fori_loop(0, seq // (_SB_QUERY_GROUP * blk), group_body, 0)


def _sb_prompt(q, k, v, *, batch, seq, blk=SB_BLOCK):
    t, hd = q.shape
    heads = hd // HEAD_DIM
    blk = min(blk, seq)
    spec = pl.BlockSpec((seq, HEAD_DIM), lambda b, h: (b, h))
    return pl.pallas_call(
        functools.partial(_sb_prompt_kernel, blk=blk),
        grid=(batch, heads),
        in_specs=[spec, spec, spec],
        out_specs=spec,
        out_shape=jax.ShapeDtypeStruct((t, hd), BF16),
        scratch_shapes=[pltpu.VMEM((seq + blk, HEAD_DIM), BF16), pltpu.VMEM((seq + blk, HEAD_DIM), BF16)],
        compiler_params=_params("parallel", "parallel"),
    )(q, k, v)


_SB_HEAD_GROUP = 32


def _sb_probe_kernel(q_ref, kn_ref, kc_ref, alive_ref, *, blk):
    tq = q_ref.shape[0]
    row = lax.broadcasted_iota(jnp.int32, (tq, blk), 0)
    col = lax.broadcasted_iota(jnp.int32, (tq, blk), 1)
    kb = pltpu.einshape("thd->htd", kc_ref[...]).astype(BF16)
    cmax = jnp.full((tq, 1), NEG_INF, F32)
    for g in range(_SB_HEAD_GROUP):
        lanes = slice(g * HEAD_DIM, (g + 1) * HEAD_DIM)
        q = q_ref[:, lanes]
        c = jnp.sum(_sb_log_terms(q, kn_ref[:, lanes], col < row)[1], axis=1, keepdims=True)
        c = c + jnp.sum(_sb_log_terms(q, kb[g], None)[1], axis=1, keepdims=True)
        cmax = jnp.maximum(cmax, c)
    flag = jnp.where(jnp.max(cmax) > _SB_DEAD_BELOW - 1.0, 1, 0)
    alive_ref[...] = jnp.full(alive_ref.shape, flag, jnp.int32)


def _sb_sample_kernel(alive_ref, q_ref, kn_ref, vn_ref, kc_ref, vc_ref, o_ref, c_ref, acc_ref, *, blk):
    del alive_ref
    n = pl.program_id(2)
    tq = q_ref.shape[0]
    tri = _strict_lower_ones(blk)

    @pl.when(n == 0)
    def _():
        row = lax.broadcasted_iota(jnp.int32, (tq, blk), 0)
        col = lax.broadcasted_iota(jnp.int32, (tq, blk), 1)
        for g in range(_SB_HEAD_GROUP):
            lanes = slice(g * HEAD_DIM, (g + 1) * HEAD_DIM)
            c, acc = _sb_block(q_ref[:, lanes], kn_ref[:, lanes], vn_ref[:, lanes],
                               jnp.zeros((tq, 1), F32), jnp.zeros((tq, HEAD_DIM), F32), tri, col < row)
            c_ref[g] = c
            acc_ref[g] = acc

    @pl.when(jnp.max(c_ref[...]) > _SB_DEAD_BELOW)
    def _():
        kb = pltpu.einshape("thd->htd", kc_ref[...]).astype(BF16)
        vb = pltpu.einshape("thd->htd", vc_ref[...]).astype(BF16)
        for g in range(_SB_HEAD_GROUP):
            lanes = slice(g * HEAD_DIM, (g + 1) * HEAD_DIM)
            c, acc = _sb_block(q_ref[:, lanes], kb[g], vb[g], c_ref[g], acc_ref[g], tri, None)
            c_ref[g] = c
            acc_ref[g] = acc

    @pl.when(n == pl.num_programs(2) - 1)
    def _():
        for g in range(_SB_HEAD_GROUP):
            o_ref[:, g * HEAD_DIM:(g + 1) * HEAD_DIM] = acc_ref[g].astype(o_ref.dtype)


def _sb_sample(q, k_new, v_new, cache_k, cache_v, *, batch, tq, blk=SB_BLOCK):
    t, hd = q.shape
    heads = hd // HEAD_DIM
    past = cache_k.shape[1]
    nblk = past // blk
    gw = _SB_HEAD_GROUP * HEAD_DIM

    def pad_block(a):
        a = a.reshape(batch, tq, hd)
        a = jnp.pad(a, ((0, 0), (0, blk - tq), (0, 0)))
        return a.reshape(batch * blk, hd).astype(BF16)

    assert heads == _SB_HEAD_GROUP
    kn = pad_block(k_new)
    vn = pad_block(v_new)
    newest = (None, blk, _SB_HEAD_GROUP, HEAD_DIM)
    alive = pl.pallas_call(
        functools.partial(_sb_probe_kernel, blk=blk),
        grid=(batch,),
        in_specs=[pl.BlockSpec((tq, gw), lambda b: (b, 0)),
                  pl.BlockSpec((blk, gw), lambda b: (b, 0)),
                  pl.BlockSpec(newest, lambda b: (b, nblk - 1, 0, 0))],
        out_specs=pl.BlockSpec((None, 8, LANES), lambda b: (b, 0, 0)),
        out_shape=jax.ShapeDtypeStruct((batch, 8, LANES), jnp.int32),
        compiler_params=_params("parallel"),
    )(q, kn, cache_k)[:, 0, 0]

    def cache_block(b, h, n, alive_ref):
        return (b, nblk - 1 - jnp.where(alive_ref[b] > 0, n, 0), h, 0)

    qspec = pl.BlockSpec((tq, gw), lambda b, h, n, alive_ref: (b, h))
    nspec = pl.BlockSpec((blk, gw), lambda b, h, n, alive_ref: (b, h))
    cspec = pl.BlockSpec(newest, cache_block)
    return pl.pallas_call(
        functools.partial(_sb_sample_kernel, blk=blk),
        grid_spec=pltpu.PrefetchScalarGridSpec(
            num_scalar_prefetch=1,
            grid=(batch, heads // _SB_HEAD_GROUP, nblk),
            in_specs=[qspec, nspec, nspec, cspec, cspec],
            out_specs=qspec,
            scratch_shapes=[pltpu.VMEM((_SB_HEAD_GROUP, tq, 1), F32),
                            pltpu.VMEM((_SB_HEAD_GROUP, tq, HEAD_DIM), F32)]),
        out_shape=jax.ShapeDtypeStruct((t, hd), BF16),
        compiler_params=_params("parallel", "parallel", "arbitrary"),
    )(alive, q, kn, vn, cache_k, cache_v)


_BAND_UNROLL = 8


def _band_kernel(q_ref, k_ref, v_ref, bias_ref, o_ref, *, cq, win, nq, chunked, lo_static):
    bias = bias_ref[...].reshape(G_B * cq, win)
    col = lax.broadcasted_iota(jnp.int32, (G_B * cq, win), 1)

    def chunk(c, _):
        r0 = pl.multiple_of(c * cq, cq)
        if chunked:
            lo = jnp.maximum(CHUNK, (N_PREV_CHUNKS + 1 - c) * CHUNK)
        else:
            lo = lo_static
        kb = k_ref[pl.ds(r0, win), :]
        vb = v_ref[pl.ds(r0, win), :]
        qb = q_ref[pl.ds(r0, cq), :]
        q4 = jnp.concatenate([qb[:, g * HEAD_DIM:(g + 1) * HEAD_DIM] for g in range(G_B)], axis=0)
        s = lax.dot_general(q4, kb, _NT, preferred_element_type=F32) + bias
        s = jnp.where(col >= lo, s, NEG_INF)
        m = jnp.max(s, axis=1, keepdims=True)
        e = jnp.exp(s - m)
        o = jnp.dot(e.astype(BF16), vb, preferred_element_type=F32) * (1.0 / jnp.sum(e, axis=1, keepdims=True))
        for g in range(G_B):
            o_ref[pl.ds(r0, cq), g * HEAD_DIM:(g + 1) * HEAD_DIM] = o[g * cq:(g + 1) * cq].astype(o_ref.dtype)
        return 0

    lax.fori_loop(0, nq, chunk, 0, unroll=_BAND_UNROLL if nq % _BAND_UNROLL == 0 else 1)


def _band_bias(rel_table, cq, win):
    lg = win + cq - 1
    idx = np.clip(win - 1 - np.arange(lg), -REL_CLIP, REL_CLIP) + REL_CLIP
    g = rel_table[idx].astype(F32).T
    rows = jnp.tile(g, (1, cq + 1))[:, :cq * (lg + 1)].reshape(-1, cq, lg + 1)
    return rows[:, ::-1, :win]


def _band_attention(q, kpad, vpad, bias, *, batch, nq, cq, win, chunked, lo_static):
    t, hd = q.shape
    rows = kpad.shape[1]
    qw = G_B * HEAD_DIM
    qspec = pl.BlockSpec((nq * cq, qw), lambda b, h: (b, h))
    kspec = pl.BlockSpec((None, rows, HEAD_DIM), lambda b, h: (b, 0, h))
    bspec = pl.BlockSpec((G_B, cq, win), lambda b, h: (h, 0, 0))
    return pl.pallas_call(
        functools.partial(_band_kernel, cq=cq, win=win, nq=nq, chunked=chunked, lo_static=lo_static),
        grid=(batch, N_KV_B),
        in_specs=[qspec, kspec, kspec, bspec],
        out_specs=qspec,
        out_shape=jax.ShapeDtypeStruct((t, hd), BF16),
        compiler_params=_params("parallel", "parallel"),
    )(q, kpad, vpad, bias)


def _split3_dot_nt(a, b):
    a_hi = a.astype(BF16)
    a_lo = (a - a_hi.astype(F32)).astype(BF16)
    b_hi = b.astype(BF16)
    b_lo = (b - b_hi.astype(F32)).astype(BF16)
    return (lax.dot_general(a_hi, b_hi, _NT, preferred_element_type=F32)
            + lax.dot_general(a_lo, b_hi, _NT, preferred_element_type=F32)
            + lax.dot_general(a_hi, b_lo, _NT, preferred_element_type=F32))


def _take_lanes(x, idx):
    return jnp.take_along_axis(x, idx, axis=1, mode="promise_in_bounds")


_ROUTE_ROWS = 64
_ROUTE_UNROLL = 64


def _route_kernel(q_ref, sk_ref, pairs_ref, e_ref, g_ref, s_ref, left_ref, si_ref, cleft_ref, fpos_ref):
    tm = q_ref.shape[0]
    dh = sk_ref.shape[-1]
    nset = 2 * PEER_HEADS
    lane_c = lax.broadcasted_iota(jnp.int32, (_ROUTE_ROWS, PEER_KEYS), 1)

    for n in range(nset):
        sc = _split3_dot_nt(q_ref[:, n * dh:(n + 1) * dh], sk_ref[n // 2, n % 2])
        s_ref[n * tm:(n + 1) * tm, :] = sc
        left_ref[n * tm:(n + 1) * tm, :] = sc
    si_ref[...] = jnp.zeros(si_ref.shape, jnp.int32)

    def body1(i, _):
        def piece(r, _):
            rows = pl.ds(pl.multiple_of(r * _ROUTE_ROWS, _ROUTE_ROWS), _ROUTE_ROWS)
            left = left_ref[rows, :]
            pos = jnp.argmax(left, axis=1, keepdims=True).astype(jnp.int32)
            left_ref[rows, :] = jnp.where(lane_c == pos, NEG_INF, left)
            si_ref[rows, :] = jnp.where(lane_c == i, pos, si_ref[rows, :])
            return 0

        return lax.fori_loop(0, nset * tm // _ROUTE_ROWS, piece, 0, unroll=_ROUTE_UNROLL)

    lax.fori_loop(0, PEER_TOPK, body1, 0)
    si = si_ref[...]
    sv = _take_lanes(s_ref[...], si)
    si = si.astype(F32)

    pair_a = jnp.broadcast_to(pairs_ref[0:1, :], (tm, PEER_KEYS))
    pair_b = jnp.broadcast_to(pairs_ref[1:2, :], (tm, PEER_KEYS))
    pair_ok = jnp.broadcast_to(pairs_ref[2:3, :], (tm, PEER_KEYS)) > 0
    cand, cidx = [], []
    for h in range(PEER_HEADS):
        r0 = slice((2 * h) * tm, (2 * h + 1) * tm)
        r1 = slice((2 * h + 1) * tm, (2 * h + 2) * tm)
        cand.append(jnp.where(pair_ok, _take_lanes(sv[r0], pair_a) + _take_lanes(sv[r1], pair_b), NEG_INF))
        cidx.append(_take_lanes(si[r0], pair_a) * float(PEER_KEYS) + _take_lanes(si[r1], pair_b))
    cand = jnp.concatenate(cand, axis=0)
    cidx = jnp.concatenate(cidx, axis=0)

    lane2 = lax.broadcasted_iota(jnp.int32, cand.shape, 1)
    head_lane0 = (lax.broadcasted_iota(jnp.int32, cand.shape, 0) // tm) * PEER_TOPK
    cleft_ref[...] = cand
    fpos_ref[...] = jnp.zeros(fpos_ref.shape, jnp.int32)

    def body2(i, _):
        def piece(r, _):
            row0 = pl.multiple_of(r * _ROUTE_ROWS, _ROUTE_ROWS)
            rows = pl.ds(row0, _ROUTE_ROWS)
            left = cleft_ref[rows, :]
            pos = jnp.argmax(left, axis=1, keepdims=True).astype(jnp.int32)
            cleft_ref[rows, :] = jnp.where(lane_c == pos, NEG_INF, left)
            out_lane = (row0 // tm) * PEER_TOPK + i
            fpos_ref[rows, :] = jnp.where(lane_c == out_lane, pos, fpos_ref[rows, :])
            return 0

        return lax.fori_loop(0, PEER_HEADS * tm // _ROUTE_ROWS, piece, 0, unroll=_ROUTE_UNROLL)

    lax.fori_loop(0, PEER_TOPK, body2, 0)
    fpos = fpos_ref[...]
    fv = _take_lanes(cand, fpos)
    fe = _take_lanes(cidx, fpos)

    grp = lane2 // PEER_TOPK == head_lane0 // PEER_TOPK
    fe = jnp.where(grp, fe, 0.0)
    mx = jnp.max(jnp.where(grp, fv, NEG_INF), axis=1, keepdims=True)
    ex = jnp.where(grp, jnp.exp(fv - mx), 0.0)
    gate = ex / jnp.sum(ex, axis=1, keepdims=True)
    e_out = fe[0:tm]
    g_out = gate[0:tm]
    for h in range(1, PEER_HEADS):
        e_out = e_out + fe[h * tm:(h + 1) * tm]
        g_out = g_out + gate[h * tm:(h + 1) * tm]
    e_ref[...] = e_out.astype(jnp.int32)
    g_ref[...] = g_out


def _peer_route(qp, sub_keys):
    m, kq = qp.shape
    tm = min(ROUTE_ROW_TILE, m)
    ne = PEER_HEADS * PEER_TOPK
    ospec = pl.BlockSpec((tm, ne), lambda i: (i, 0))
    pairs = [(a, b) for a in range(PEER_TOPK) for b in range(PEER_TOPK) if (a + 1) * (b + 1) <= PEER_TOPK]
    pad = [0] * (PEER_KEYS - len(pairs))
    pair_rows = jnp.array([[a for a, _ in pairs] + pad, [b for _, b in pairs] + pad,
                           [1] * len(pairs) + pad] + [[0] * PEER_KEYS] * 5, jnp.int32)
    return pl.pallas_call(
        _route_kernel,
        grid=(m // tm,),
        in_specs=[pl.BlockSpec((tm, kq), lambda i: (i, 0)),
                  pl.BlockSpec(sub_keys.shape, lambda i: (0, 0, 0, 0)),
                  pl.BlockSpec(pair_rows.shape, lambda i: (0, 0))],
        out_specs=[ospec, ospec],
        out_shape=[jax.ShapeDtypeStruct((m, ne), jnp.int32), jax.ShapeDtypeStruct((m, ne), F32)],
        scratch_shapes=[pltpu.VMEM((2 * PEER_HEADS * tm, PEER_KEYS), F32),
                        pltpu.VMEM((2 * PEER_HEADS * tm, PEER_KEYS), F32),
                        pltpu.VMEM((2 * PEER_HEADS * tm, PEER_KEYS), jnp.int32),
                        pltpu.VMEM((PEER_HEADS * tm, PEER_KEYS), F32),
                        pltpu.VMEM((PEER_HEADS * tm, PEER_KEYS), jnp.int32)],
        compiler_params=_params("parallel"),
    )(qp, sub_keys, pair_rows)


def _gelu_tanh(x):
    return 0.5 * x * (1.0 + jnp.tanh(0.7978845608028654 * (x + 0.044715 * (x * x * x))))


_TOKEN_GROUP = 16


def _expert_kernel(h_ref, gain_ref, e_ref, g_ref, tab_ref, o_ref,
                   xn_ref, hi_ref, lo_ref, act_ref, d_ref, w3_ref, *, nj):
    j = pl.program_id(1)
    tm = h_ref.shape[0]
    te = tab_ref.shape[0]
    nb = te // LANES

    @pl.when(j == 0)
    def _():
        xn_ref[...] = _norm_rows(h_ref[...], gain_ref[...]).astype(BF16)
        o_ref[...] = h_ref[...]
        e = e_ref[...]
        hi_ref[...] = e // LANES
        lo_ref[...] = e % LANES
        act_ref[...] = jnp.zeros(act_ref.shape, F32)

    def score_slab(slot):
        d_ref[slot] = lax.dot_general(xn_ref[...], tab_ref[...], _NT, preferred_element_type=F32)

    def pick_from_slab(slot, slab):
        hi = hi_ref[...]
        lo = lo_ref[...]
        act = act_ref[...]
        for s in range(nb):
            picked = _take_lanes(d_ref[slot, :, s * LANES:(s + 1) * LANES], lo)
            act = jnp.where(hi == slab * nb + s, picked, act)
        act_ref[...] = act

    @pl.when(j == 0)
    def _():
        score_slab(0)

    @pl.when(jnp.logical_and(j >= 1, j < nj))
    def _():
        pick_from_slab((j - 1) % 2, j - 1)
        score_slab(j % 2)

    @pl.when(j == nj)
    def _():
        pick_from_slab((nj - 1) % 2, nj - 1)
        act_ref[...] = g_ref[...] * _gelu_tanh(act_ref[...])
        sub = lax.broadcasted_iota(jnp.int32, (LANES, LANES), 0)

        def group(gi, _):
            t0 = pl.multiple_of(gi * _TOKEN_GROUP, _TOKEN_GROUP)
            grids = []
            for u in range(_TOKEN_GROUP):
                hi_row = hi_ref[pl.ds(t0 + u, 1), :]
                lo_row = lo_ref[pl.ds(t0 + u, 1), :]
                w_row = act_ref[pl.ds(t0 + u, 1), :]
                a_t = jnp.where(hi_row == sub, w_row, 0.0).astype(BF16)
                b_t = jnp.where(lo_row == sub, 1.0, 0.0).astype(BF16)
                grids.append(lax.dot_general(a_t, b_t, _NT, preferred_element_type=F32))
            w3_ref[:, pl.ds(t0, _TOKEN_GROUP), :] = pltpu.einshape(
                "uik->iuk", jnp.stack(grids, axis=0)).astype(BF16)
            return 0

        lax.fori_loop(0, tm // _TOKEN_GROUP, group, 0, unroll=4)

    @pl.when(j >= nj)
    def _():
        jj = j - nj
        w = jnp.concatenate([w3_ref[jj * nb + s] for s in range(nb)], axis=1)
        d = o_ref.shape[1]
        for c0 in range(0, d, _EXPERT_OUT_COLS):
            cols = slice(c0, c0 + _EXPERT_OUT_COLS)
            o_ref[:, cols] += jnp.dot(w, tab_ref[:, cols], preferred_element_type=F32)


_EXPERT_OUT_COLS = 1024


def _peer_experts(h, gain, eidx, gate, uv_tab, layer, *, te=EXPERT_SLAB):
    m, d = h.shape
    ne = uv_tab.shape[2]
    nsel = eidx.shape[1]
    tm = min(ROW_TILE, m)
    nj = ne // te
    once = pl.Buffered(1)
    sel_spec = pl.BlockSpec((tm, nsel), lambda i, j: (i, 0), pipeline_mode=once)
    return pl.pallas_call(
        functools.partial(_expert_kernel, nj=nj),
        grid=(m // tm, 2 * nj),
        in_specs=[pl.BlockSpec((tm, d), lambda i, j: (i, 0)),
                  pl.BlockSpec((1, d), lambda i, j: (0, 0), pipeline_mode=once),
                  sel_spec, sel_spec,
                  pl.BlockSpec((None, None, te, d), lambda i, j: (layer, j // nj, j % nj, 0))],
        out_specs=pl.BlockSpec((tm, d), lambda i, j: (i, 0), pipeline_mode=once),
        out_shape=jax.ShapeDtypeStruct((m, d), F32),
        scratch_shapes=[pltpu.VMEM((tm, d), BF16),
                        pltpu.VMEM((tm, nsel), jnp.int32),
                        pltpu.VMEM((tm, nsel), jnp.int32),
                        pltpu.VMEM((tm, nsel), F32),
                        pltpu.VMEM((2, tm, te), F32),
                        pltpu.VMEM((ne // LANES, tm, LANES), BF16)],
        compiler_params=_params("parallel", "arbitrary"),
    )(h, gain.reshape(1, d), eidx, gate, uv_tab)


def _peer_ffn(h, i, prm):
    qp = _matmul(h, prm["w_q_peer"][i], gain=prm["g_ffn"][i])
    eidx, gate = _peer_route(qp, prm["peer_sub_keys"][i])
    return _peer_experts(h, prm["g_ffn"][i], eidx, gate, prm["peer_uv"], i)


def _ple(h, p, i, prm):
    return _matmul(h, prm["w_ple_gate"][i], gain=prm["g_ple"][i], residual=h,
                   ple=(p, prm["w_ple_proj"][i]))


def _trunk(x, pe, past, prm):
    b, t, d = x.shape
    m = b * t
    h = x.reshape(m, d)
    pe = pe.reshape(pe.shape[0], m, pe.shape[-1])

    q, k, v = _qkv_proj(h, prm["w_qkv_a"], prm["g_mix"][0])
    if past is None:
        o = _sb_prompt(q, k, v, batch=b, seq=t)
    else:
        o = _sb_sample(q, k, v, past[0][0], past[1][0], batch=b, tq=t)
    h = _matmul(o, prm["w_o_a"], residual=h)
    h = _peer_ffn(h, 0, prm)
    h = _ple(h, pe[0], 0, prm)

    kv = _matmul(h, prm["w_kv_b"], gain=prm["g_kv"])
    nkv = N_KV_B * HEAD_DIM
    kb_new = kv[:, :nkv].reshape(b, t, nkv)
    vb_new = kv[:, nkv:].reshape(b, t, nkv)

    qb = _matmul(h, prm["w_q_b"], gain=prm["g_mix"][1], out_dtype=BF16, out_scale=SCALE)
    win = (N_PREV_CHUNKS + 2) * CHUNK
    if past is None:
        front = (N_PREV_CHUNKS + 1) * CHUNK
        kpad = jnp.pad(kb_new.astype(BF16), ((0, 0), (front, 0), (0, 0)))
        vpad = jnp.pad(vb_new.astype(BF16), ((0, 0), (front, 0), (0, 0)))
        bias = _band_bias(prm["rel_bias_b"][0], CHUNK, win)
        ob = _band_attention(qb, kpad, vpad, bias, batch=b, nq=t // CHUNK, cq=CHUNK, win=win,
                             chunked=True, lo_static=0)
    else:
        ck = past[2].reshape(b, -1, nkv)
        cv = past[3].reshape(b, -1, nkv)
        front = win - ck.shape[1] - t
        kpad = jnp.pad(jnp.concatenate([ck, kb_new], axis=1).astype(BF16), ((0, 0), (front, 0), (0, 0)))
        vpad = jnp.pad(jnp.concatenate([cv, vb_new], axis=1).astype(BF16), ((0, 0), (front, 0), (0, 0)))
        bias = _band_bias(prm["rel_bias_b"][0], t, win)
        ob = _band_attention(qb, kpad, vpad, bias, batch=b, nq=1, cq=t, win=win,
                             chunked=False, lo_static=front)
    h = _matmul(ob, prm["w_o_b"], residual=h)
    h = _peer_ffn(h, 1, prm)
    h = _ple(h, pe[1], 1, prm)

    y = _rmsnorm(h, prm["g_final"]).reshape(b, t, d)
    heads = k.shape[1] // HEAD_DIM
    a_k = k.reshape(1, b, t, heads, HEAD_DIM)
    a_v = v.reshape(1, b, t, heads, HEAD_DIM)
    if past is None:
        keep = min(N_PREV_CHUNKS * CHUNK, t)
        b_k, b_v = kb_new[:, t - keep:], vb_new[:, t - keep:]
    else:
        b_k, b_v = kb_new, vb_new
    b_k = b_k.reshape(b, -1, N_KV_B, HEAD_DIM)
    b_v = b_v.reshape(b, -1, N_KV_B, HEAD_DIM)
    return y, a_k, a_v, b_k, b_v


def kernel(x_prompt, x_sample, cache_a_k, cache_a_v, cache_b_k, cache_b_v, p_prompt, p_sample, g_mix, w_qkv_a, w_o_a, g_kv, w_kv_b, w_q_b, rel_bias_b, w_o_b, g_ffn, w_q_peer, peer_sub_keys, peer_u, peer_v, g_ple, w_ple_gate, w_ple_proj, g_final):
    prm = dict(
        g_mix=g_mix, g_kv=g_kv, g_ffn=g_ffn, g_ple=g_ple, g_final=g_final,
        w_qkv_a=_to_bf16(w_qkv_a)[0],
        w_o_a=_to_bf16(w_o_a)[0],
        w_kv_b=_to_bf16(w_kv_b),
        w_q_b=_to_bf16(w_q_b)[0],
        w_o_b=_to_bf16(w_o_b)[0],
        rel_bias_b=rel_bias_b,
        w_q_peer=_to_bf16(w_q_peer),
        peer_sub_keys=peer_sub_keys,
        peer_uv=_peer_tables_bf16(peer_u, peer_v),
        w_ple_gate=_to_bf16(w_ple_gate),
        w_ple_proj=_to_bf16(w_ple_proj),
    )
    y_p, ak_p, av_p, bk_p, bv_p = _trunk(x_prompt, p_prompt, None, prm)
    y_s, ak_s, av_s, bk_s, bv_s = _trunk(x_sample, p_sample,
                                         (cache_a_k, cache_a_v, cache_b_k, cache_b_v), prm)
    return (y_p, y_s, ak_p, av_p, bk_p, bv_p, ak_s, av_s, bk_s, bv_s)
```

```python
import functools

import jax
import jax.numpy as jnp
import numpy as np
from jax import lax
from jax.experimental import pallas as pl
from jax.experimental.pallas import tpu as pltpu

F32 = jnp.float32
BF16 = jnp.bfloat16

EPS = 1e-6
HEAD_DIM = 128
CHUNK = 64
N_PREV_CHUNKS = 8
N_KV_B = 8
G_B = 4
REL_CLIP = 128
PEER_HEADS = 8
PEER_KEYS = 128
PEER_TOPK = 16
SCALE = HEAD_DIM ** -0.5
NEG_INF = float("-inf")

VMEM_LIMIT_BYTES = 60 * 1024 * 1024
LANES = 128

ROW_TILE = 512
COL_TILE = 1024
ROUTE_ROW_TILE = 256
EXPERT_SLAB = 512
SB_BLOCK = 256

_NT = (((1,), (1,)), ((), ()))


def _params(*sem):
    return pltpu.CompilerParams(dimension_semantics=sem, vmem_limit_bytes=VMEM_LIMIT_BYTES)


def _norm_rows(x, g):
    ms = jnp.mean(x * x, axis=-1, keepdims=True)
    return x * lax.rsqrt(ms + EPS) * g


def _sigmoid(x):
    return 1.0 / (1.0 + jnp.exp(-x))


def _normed_rows_pipeline(x_ref, g_ref, xn_ref, step):
    i = pl.program_id(0)
    j = pl.program_id(1)
    last = pl.num_programs(1) - 1
    slot = i % 2

    def normalise(dst):
        xn_ref[dst] = _norm_rows(x_ref[...], g_ref[...]).astype(BF16)

    @pl.when(jnp.logical_and(i == 0, j == 0))
    def _():
        normalise(0)

    @pl.when(j < last)
    def _():
        step(slot, lambda: None)

    @pl.when(j == last)
    def _():
        step(slot, lambda: normalise(1 - slot))


def _next_tile_rows(ni, nj):
    return lambda i, j: (jnp.minimum(i + jnp.where(j == nj - 1, 1, 0), ni - 1), 0)


def _mm_kernel(*refs, norm, pipelined, res, ple, out_scale):
    it = iter(refs)
    x_ref = next(it)
    g_ref = next(it) if norm else None
    w_ref = next(it)
    res_ref = next(it) if res else None
    p_ref = next(it) if ple else None
    wp_ref = next(it) if ple else None
    o_ref = next(it)
    xn_ref = next(it) if norm else None

    def step(x):
        acc = jnp.dot(x, w_ref[...], preferred_element_type=F32)
        if ple:
            proj = jnp.dot(p_ref[...].astype(BF16), wp_ref[...], preferred_element_type=F32)
            acc = _sigmoid(acc) * proj
        if res:
            acc = acc + res_ref[...]
        if out_scale is not None:
            acc = acc * out_scale
        o_ref[...] = acc.astype(o_ref.dtype)

    if pipelined:
        def pipelined_step(slot, after):
            step(xn_ref[slot])
            after()

        _normed_rows_pipeline(x_ref, g_ref, xn_ref, pipelined_step)
    elif norm:
        @pl.when(pl.program_id(1) == 0)
        def _():
            xn_ref[...] = _norm_rows(x_ref[...], g_ref[...]).astype(BF16)
        step(xn_ref[...])
    else:
        step(x_ref[...])


def _matmul(x, w, *, gain=None, residual=None, ple=None, out_dtype=F32, out_scale=None, tn=COL_TILE):
    m, k = x.shape
    n = w.shape[1]
    norm = gain is not None
    tm = min(ROW_TILE if norm else 2 * ROW_TILE, m)
    tn = min(tn, n)
    pipelined = norm and ple is None and n // tn >= 2
    x_map = _next_tile_rows(m // tm, n // tn) if pipelined else (lambda i, j: (i, 0))
    in_specs = [pl.BlockSpec((tm, k), x_map)]
    args = [x]
    if norm:
        in_specs.append(pl.BlockSpec((1, k), lambda i, j: (0, 0)))
        args.append(gain.reshape(1, k))
    in_specs.append(pl.BlockSpec((k, tn), lambda i, j: (0, j)))
    args.append(w)
    if residual is not None:
        in_specs.append(pl.BlockSpec((tm, tn), lambda i, j: (i, j)))
        args.append(residual)
    if ple is not None:
        p, wp = ple
        kp = p.shape[1]
        in_specs.append(pl.BlockSpec((tm, kp), lambda i, j: (i, 0)))
        in_specs.append(pl.BlockSpec((kp, tn), lambda i, j: (0, j)))
        args += [p, wp]
    return pl.pallas_call(
        functools.partial(_mm_kernel, norm=norm, pipelined=pipelined, res=residual is not None,
                          ple=ple is not None, out_scale=out_scale),
        grid=(m // tm, n // tn),
        in_specs=in_specs,
        out_specs=pl.BlockSpec((tm, tn), lambda i, j: (i, j)),
        out_shape=jax.ShapeDtypeStruct((m, n), out_dtype),
        scratch_shapes=([pltpu.VMEM((2, tm, k), BF16)] if pipelined else
                        [pltpu.VMEM((tm, k), BF16)] if norm else []),
        compiler_params=_params("arbitrary" if pipelined else "parallel", "arbitrary"),
    )(*args)


def _qkv_kernel(x_ref, g_ref, w_ref, q_ref, k_ref, v_ref, xn_ref, *, nq):
    j = pl.program_id(1)

    def step(slot, after):
        acc = jnp.dot(xn_ref[slot], w_ref[...], preferred_element_type=F32)
        after()

        @pl.when(j < nq)
        def _():
            q_ref[...] = (acc * SCALE).astype(q_ref.dtype)

        @pl.when(jnp.logical_and(j >= nq, j < 2 * nq))
        def _():
            k_ref[...] = acc

        @pl.when(j >= 2 * nq)
        def _():
            v_ref[...] = acc

    _normed_rows_pipeline(x_ref, g_ref, xn_ref, step)


def _qkv_proj(x, w, gain, *, tn=COL_TILE):
    m, k = x.shape
    n = w.shape[1] // 3
    tm = min(ROW_TILE, m)
    nq = n // tn

    def ospec(first):
        return pl.BlockSpec((tm, tn), lambda i, j: (i, jnp.clip(j - first, 0, nq - 1)))

    return pl.pallas_call(
        functools.partial(_qkv_kernel, nq=nq),
        grid=(m // tm, 3 * nq),
        in_specs=[pl.BlockSpec((tm, k), _next_tile_rows(m // tm, 3 * nq)),
                  pl.BlockSpec((1, k), lambda i, j: (0, 0)),
                  pl.BlockSpec((k, tn), lambda i, j: (0, j))],
        out_specs=[ospec(0), ospec(nq), ospec(2 * nq)],
        out_shape=[jax.ShapeDtypeStruct((m, n), BF16), jax.ShapeDtypeStruct((m, n), F32),
                   jax.ShapeDtypeStruct((m, n), F32)],
        scratch_shapes=[pltpu.VMEM((2, tm, k), BF16)],
        compiler_params=_params("arbitrary", "arbitrary"),
    )(x, gain.reshape(1, k), w)


def _cast_kernel(x_ref, o_ref):
    o_ref[...] = x_ref[...].astype(o_ref.dtype)


_CAST_ROWS = 256
_CAST_COLS = 4096


def _to_bf16(x):
    x2 = x.reshape(-1, x.shape[-1])
    rows, cols = x2.shape
    tr = min(_CAST_ROWS, rows)
    tc = min(_CAST_COLS, cols)
    spec = pl.BlockSpec((tr, tc), lambda i, j: (i, j))
    out = pl.pallas_call(
        _cast_kernel,
        grid=(rows // tr, cols // tc),
        in_specs=[spec],
        out_specs=spec,
        out_shape=jax.ShapeDtypeStruct((rows, cols), BF16),
        compiler_params=_params("parallel", "parallel"),
    )(x2)
    return out.reshape(x.shape)


def _stack_cast_kernel(u_ref, v_ref, o_ref):
    c = pl.program_id(1)

    @pl.when(c == 0)
    def _():
        o_ref[...] = u_ref[...].astype(o_ref.dtype)

    @pl.when(c == 1)
    def _():
        o_ref[...] = v_ref[...].astype(o_ref.dtype)


def _peer_tables_bf16(u, v):
    nl, ne, d = u.shape
    te = _CAST_ROWS
    nblk = ne // te
    uspec = pl.BlockSpec((None, te, d), lambda l, c, e: (l, jnp.where(c == 0, e, nblk - 1), 0))
    vspec = pl.BlockSpec((None, te, d), lambda l, c, e: (l, jnp.where(c == 1, e, 0), 0))
    return pl.pallas_call(
        _stack_cast_kernel,
        grid=(nl, 2, nblk),
        in_specs=[uspec, vspec],
        out_specs=pl.BlockSpec((None, None, te, d), lambda l, c, e: (l, c, e, 0)),
        out_shape=jax.ShapeDtypeStruct((nl, 2, ne, d), BF16),
        compiler_params=_params("parallel", "arbitrary", "arbitrary"),
    )(u, v)


def _rmsnorm_kernel(x_ref, g_ref, o_ref):
    o_ref[...] = _norm_rows(x_ref[...], g_ref[...])


def _rmsnorm(x, gain):
    m, k = x.shape
    tm = min(ROW_TILE, m)
    return pl.pallas_call(
        _rmsnorm_kernel,
        grid=(m // tm,),
        in_specs=[pl.BlockSpec((tm, k), lambda i: (i, 0)), pl.BlockSpec((1, k), lambda i: (0, 0))],
        out_specs=pl.BlockSpec((tm, k), lambda i: (i, 0)),
        out_shape=jax.ShapeDtypeStruct((m, k), F32),
        compiler_params=_params("parallel"),
    )(x, gain.reshape(1, k))


def _strict_lower_ones(n):
    r = lax.broadcasted_iota(jnp.int32, (2 * n, n), 0) % n
    c = lax.broadcasted_iota(jnp.int32, (2 * n, n), 1)
    return jnp.where(r > c, 1.0, 0.0).astype(BF16)


_SB_DEAD_BELOW = -104.0


def _sb_log_terms(q, kb, mask):
    z = lax.dot_general(q, kb, _NT, preferred_element_type=F32)
    nz = -z
    sp = jnp.log(1.0 + jnp.exp(-jnp.maximum(z, nz)))
    log_beta = jnp.minimum(z, 0.0) - sp
    log_keep = jnp.minimum(nz, 0.0) - sp
    if mask is not None:
        log_keep = jnp.where(mask, log_keep, 0.0)
    return log_beta, log_keep


def _sb_block(q, kb, vb, c, acc, tri2, mask):
    log_beta, log_keep = _sb_log_terms(q, kb, mask)
    hi = log_keep.astype(BF16)
    lo = (log_keep - hi.astype(F32)).astype(BF16)
    after = jnp.dot(jnp.concatenate([hi, lo], axis=1), tri2, preferred_element_type=F32)
    w = jnp.exp(log_beta + after + c)
    if mask is not None:
        w = jnp.where(mask, w, 0.0)
    acc = acc + jnp.dot(w.astype(BF16), vb, preferred_element_type=F32)
    c = c + jnp.sum(log_keep, axis=1, keepdims=True)
    return c, acc


_SB_QUERY_GROUP = 8


def _sb_prompt_kernel(q_ref, k_ref, v_ref, o_ref, kb_ref, vb_ref, *, blk):
    seq = q_ref.shape[0]
    kb_ref[pl.ds(0, blk), :] = jnp.zeros((blk, HEAD_DIM), BF16)
    vb_ref[pl.ds(0, blk), :] = jnp.zeros((blk, HEAD_DIM), BF16)
    kb_ref[pl.ds(blk, seq), :] = k_ref[...].astype(BF16)
    vb_ref[pl.ds(blk, seq), :] = v_ref[...].astype(BF16)
    tri = _strict_lower_ones(blk)
    row = lax.broadcasted_iota(jnp.int32, (blk, blk), 0)
    col = lax.broadcasted_iota(jnp.int32, (blk, blk), 1)
    diag_mask = col < row

    def kv_block(j):
        r0 = pl.multiple_of((j + 1) * blk, blk)
        return kb_ref[pl.ds(r0, blk), :], vb_ref[pl.ds(r0, blk), :]

    def older_blocks(i, q, c, acc):
        def k_cond(carry):
            j, cmax, _, _ = carry
            return jnp.logical_and(j >= 0, cmax > _SB_DEAD_BELOW)

        def k_body(carry):
            j, _, c, acc = carry
            c, acc = _sb_block(q, *kv_block(j), c, acc, tri, None)
            return j - 1, jnp.max(c), c, acc

        return lax.while_loop(k_cond, k_body, (i - 2, jnp.max(c), c, acc))[3]

    def group_body(p, _):
        state = []
        for i in [_SB_QUERY_GROUP * p + u for u in range(_SB_QUERY_GROUP)]:
            q = q_ref[pl.ds(pl.multiple_of(i * blk, blk), blk), :]
            c = jnp.zeros((blk, 1), F32)
            acc = jnp.zeros((blk, HEAD_DIM), F32)
            c, acc = _sb_block(q, *kv_block(i), c, acc, tri, diag_mask)
            c, acc = _sb_block(q, *kv_block(i - 1), c, acc, tri, None)
            state.append((i, q, c, acc))
        for i, q, c, acc in state:
            acc = older_blocks(i, q, c, acc)
            o_ref[pl.ds(pl.multiple_of(i * blk, blk), blk), :] = acc.astype(o_ref.dtype)
        return 0

    lax.fori_loop(0, seq // (_SB_QUERY_GROUP * blk), group_body, 0)


def _sb_prompt(q, k, v, *, batch, seq, blk=SB_BLOCK):
    t, hd = q.shape
    heads = hd // HEAD_DIM
    blk = min(blk, seq)
    spec = pl.BlockSpec((seq, HEAD_DIM), lambda b, h: (b, h))
    return pl.pallas_call(
        functools.partial(_sb_prompt_kernel, blk=blk),
        grid=(batch, heads),
        in_specs=[spec, spec, spec],
        out_specs=spec,
        out_shape=jax.ShapeDtypeStruct((t, hd), BF16),
        scratch_shapes=[pltpu.VMEM((seq + blk, HEAD_DIM), BF16), pltpu.VMEM((seq + blk, HEAD_DIM), BF16)],
        compiler_params=_params("parallel", "parallel"),
    )(q, k, v)


_SB_HEAD_GROUP = 32


def _sb_probe_kernel(q_ref, kn_ref, kc_ref, alive_ref, *, blk):
    tq = q_ref.shape[0]
    row = lax.broadcasted_iota(jnp.int32, (tq, blk), 0)
    col = lax.broadcasted_iota(jnp.int32, (tq, blk), 1)
    kb = pltpu.einshape("thd->htd", kc_ref[...]).astype(BF16)
    cmax = jnp.full((tq, 1), NEG_INF, F32)
    for g in range(_SB_HEAD_GROUP):
        lanes = slice(g * HEAD_DIM, (g + 1) * HEAD_DIM)
        q = q_ref[:, lanes]
        c = jnp.sum(_sb_log_terms(q, kn_ref[:, lanes], col < row)[1], axis=1, keepdims=True)
        c = c + jnp.sum(_sb_log_terms(q, kb[g], None)[1], axis=1, keepdims=True)
        cmax = jnp.maximum(cmax, c)
    flag = jnp.where(jnp.max(cmax) > _SB_DEAD_BELOW - 1.0, 1, 0)
    alive_ref[...] = jnp.full(alive_ref.shape, flag, jnp.int32)


def _sb_sample_kernel(alive_ref, q_ref, kn_ref, vn_ref, kc_ref, vc_ref, o_ref, c_ref, acc_ref, *, blk):
    del alive_ref
    n = pl.program_id(2)
    tq = q_ref.shape[0]
    tri = _strict_lower_ones(blk)

    @pl.when(n == 0)
    def _():
        row = lax.broadcasted_iota(jnp.int32, (tq, blk), 0)
        col = lax.broadcasted_iota(jnp.int32, (tq, blk), 1)
        for g in range(_SB_HEAD_GROUP):
            lanes = slice(g * HEAD_DIM, (g + 1) * HEAD_DIM)
            c, acc = _sb_block(q_ref[:, lanes], kn_ref[:, lanes], vn_ref[:, lanes],
                               jnp.zeros((tq, 1), F32), jnp.zeros((tq, HEAD_DIM), F32), tri, col < row)
            c_ref[g] = c
            acc_ref[g] = acc

    @pl.when(jnp.max(c_ref[...]) > _SB_DEAD_BELOW)
    def _():
        kb = pltpu.einshape("thd->htd", kc_ref[...]).astype(BF16)
        vb = pltpu.einshape("thd->htd", vc_ref[...]).astype(BF16)
        for g in range(_SB_HEAD_GROUP):
            lanes = slice(g * HEAD_DIM, (g + 1) * HEAD_DIM)
            c, acc = _sb_block(q_ref[:, lanes], kb[g], vb[g], c_ref[g], acc_ref[g], tri, None)
            c_ref[g] = c
            acc_ref[g] = acc

    @pl.when(n == pl.num_programs(2) - 1)
    def _():
        for g in range(_SB_HEAD_GROUP):
            o_ref[:, g * HEAD_DIM:(g + 1) * HEAD_DIM] = acc_ref[g].astype(o_ref.dtype)


def _sb_sample(q, k_new, v_new, cache_k, cache_v, *, batch, tq, blk=SB_BLOCK):
    t, hd = q.shape
    heads = hd // HEAD_DIM
    past = cache_k.shape[1]
    nblk = past // blk
    gw = _SB_HEAD_GROUP * HEAD_DIM

    def pad_block(a):
        a = a.reshape(batch, tq, hd)
        a = jnp.pad(a, ((0, 0), (0, blk - tq), (0, 0)))
        return a.reshape(batch * blk, hd).astype(BF16)

    assert heads == _SB_HEAD_GROUP
    kn = pad_block(k_new)
    vn = pad_block(v_new)
    newest = (None, blk, _SB_HEAD_GROUP, HEAD_DIM)
    alive = pl.pallas_call(
        functools.partial(_sb_probe_kernel, blk=blk),
        grid=(batch,),
        in_specs=[pl.BlockSpec((tq, gw), lambda b: (b, 0)),
                  pl.BlockSpec((blk, gw), lambda b: (b, 0)),
                  pl.BlockSpec(newest, lambda b: (b, nblk - 1, 0, 0))],
        out_specs=pl.BlockSpec((None, 8, LANES), lambda b: (b, 0, 0)),
        out_shape=jax.ShapeDtypeStruct((batch, 8, LANES), jnp.int32),
        compiler_params=_params("parallel"),
    )(q, kn, cache_k)[:, 0, 0]

    def cache_block(b, h, n, alive_ref):
        return (b, nblk - 1 - jnp.where(alive_ref[b] > 0, n, 0), h, 0)

    qspec = pl.BlockSpec((tq, gw), lambda b, h, n, alive_ref: (b, h))
    nspec = pl.BlockSpec((blk, gw), lambda b, h, n, alive_ref: (b, h))
    cspec = pl.BlockSpec(newest, cache_block)
    return pl.pallas_call(
        functools.partial(_sb_sample_kernel, blk=blk),
        grid_spec=pltpu.PrefetchScalarGridSpec(
            num_scalar_prefetch=1,
            grid=(batch, heads // _SB_HEAD_GROUP, nblk),
            in_specs=[qspec, nspec, nspec, cspec, cspec],
            out_specs=qspec,
            scratch_shapes=[pltpu.VMEM((_SB_HEAD_GROUP, tq, 1), F32),
                            pltpu.VMEM((_SB_HEAD_GROUP, tq, HEAD_DIM), F32)]),
        out_shape=jax.ShapeDtypeStruct((t, hd), BF16),
        compiler_params=_params("parallel", "parallel", "arbitrary"),
    )(alive, q, kn, vn, cache_k, cache_v)


_BAND_UNROLL = 8


def _band_kernel(q_ref, k_ref, v_ref, bias_ref, o_ref, *, cq, win, nq, chunked, lo_static):
    bias = bias_ref[...].reshape(G_B * cq, win)
    col = lax.broadcasted_iota(jnp.int32, (G_B * cq, win), 1)

    def chunk(c, _):
        r0 = pl.multiple_of(c * cq, cq)
        if chunked:
            lo = jnp.maximum(CHUNK, (N_PREV_CHUNKS + 1 - c) * CHUNK)
        else:
            lo = lo_static
        kb = k_ref[pl.ds(r0, win), :]
        vb = v_ref[pl.ds(r0, win), :]
        qb = q_ref[pl.ds(r0, cq), :]
        q4 = jnp.concatenate([qb[:, g * HEAD_DIM:(g + 1) * HEAD_DIM] for g in range(G_B)], axis=0)
        s = lax.dot_general(q4, kb, _NT, preferred_element_type=F32) + bias
        s = jnp.where(col >= lo, s, NEG_INF)
        m = jnp.max(s, axis=1, keepdims=True)
        e = jnp.exp(s - m)
        o = jnp.dot(e.astype(BF16), vb, preferred_element_type=F32) * (1.0 / jnp.sum(e, axis=1, keepdims=True))
        for g in range(G_B):
            o_ref[pl.ds(r0, cq), g * HEAD_DIM:(g + 1) * HEAD_DIM] = o[g * cq:(g + 1) * cq].astype(o_ref.dtype)
        return 0

    lax.fori_loop(0, nq, chunk, 0, unroll=_BAND_UNROLL if nq % _BAND_UNROLL == 0 else 1)


def _band_bias(rel_table, cq, win):
    lg = win + cq - 1
    idx = np.clip(win - 1 - np.arange(lg), -REL_CLIP, REL_CLIP) + REL_CLIP
    g = rel_table[idx].astype(F32).T
    rows = jnp.tile(g, (1, cq + 1))[:, :cq * (lg + 1)].reshape(-1, cq, lg + 1)
    return rows[:, ::-1, :win]


def _band_attention(q, kpad, vpad, bias, *, batch, nq, cq, win, chunked, lo_static):
    t, hd = q.shape
    rows = kpad.shape[1]
    qw = G_B * HEAD_DIM
    qspec = pl.BlockSpec((nq * cq, qw), lambda b, h: (b, h))
    kspec = pl.BlockSpec((None, rows, HEAD_DIM), lambda b, h: (b, 0, h))
    bspec = pl.BlockSpec((G_B, cq, win), lambda b, h: (h, 0, 0))
    return pl.pallas_call(
        functools.partial(_band_kernel, cq=cq, win=win, nq=nq, chunked=chunked, lo_static=lo_static),
        grid=(batch, N_KV_B),
        in_specs=[qspec, kspec, kspec, bspec],
        out_specs=qspec,
        out_shape=jax.ShapeDtypeStruct((t, hd), BF16),
        compiler_params=_params("parallel", "parallel"),
    )(q, kpad, vpad, bias)


def _split3_dot_nt(a, b):
    a_hi = a.astype(BF16)
    a_lo = (a - a_hi.astype(F32)).astype(BF16)
    b_hi = b.astype(BF16)
    b_lo = (b - b_hi.astype(F32)).astype(BF16)
    return (lax.dot_general(a_hi, b_hi, _NT, preferred_element_type=F32)
            + lax.dot_general(a_lo, b_hi, _NT, preferred_element_type=F32)
            + lax.dot_general(a_hi, b_lo, _NT, preferred_element_type=F32))


def _take_lanes(x, idx):
    return jnp.take_along_axis(x, idx, axis=1, mode="promise_in_bounds")


_ROUTE_ROWS = 64
_ROUTE_UNROLL = 64


def _route_kernel(q_ref, sk_ref, pairs_ref, e_ref, g_ref, s_ref, left_ref, si_ref, cleft_ref, fpos_ref):
    tm = q_ref.shape[0]
    dh = sk_ref.shape[-1]
    nset = 2 * PEER_HEADS
    lane_c = lax.broadcasted_iota(jnp.int32, (_ROUTE_ROWS, PEER_KEYS), 1)

    for n in range(nset):
        sc = _split3_dot_nt(q_ref[:, n * dh:(n + 1) * dh], sk_ref[n // 2, n % 2])
        s_ref[n * tm:(n + 1) * tm, :] = sc
        left_ref[n * tm:(n + 1) * tm, :] = sc
    si_ref[...] = jnp.zeros(si_ref.shape, jnp.int32)

    def body1(i, _):
        def piece(r, _):
            rows = pl.ds(pl.multiple_of(r * _ROUTE_ROWS, _ROUTE_ROWS), _ROUTE_ROWS)
            left = left_ref[rows, :]
            pos = jnp.argmax(left, axis=1, keepdims=True).astype(jnp.int32)
            left_ref[rows, :] = jnp.where(lane_c == pos, NEG_INF, left)
            si_ref[rows, :] = jnp.where(lane_c == i, pos, si_ref[rows, :])
            return 0

        return lax.fori_loop(0, nset * tm // _ROUTE_ROWS, piece, 0, unroll=_ROUTE_UNROLL)

    lax.fori_loop(0, PEER_TOPK, body1, 0)
    si = si_ref[...]
    sv = _take_lanes(s_ref[...], si)
    si = si.astype(F32)

    pair_a = jnp.broadcast_to(pairs_ref[0:1, :], (tm, PEER_KEYS))
    pair_b = jnp.broadcast_to(pairs_ref[1:2, :], (tm, PEER_KEYS))
    pair_ok = jnp.broadcast_to(pairs_ref[2:3, :], (tm, PEER_KEYS)) > 0
    cand, cidx = [], []
    for h in range(PEER_HEADS):
        r0 = slice((2 * h) * tm, (2 * h + 1) * tm)
        r1 = slice((2 * h + 1) * tm, (2 * h + 2) * tm)
        cand.append(jnp.where(pair_ok, _take_lanes(sv[r0], pair_a) + _take_lanes(sv[r1], pair_b), NEG_INF))
        cidx.append(_take_lanes(si[r0], pair_a) * float(PEER_KEYS) + _take_lanes(si[r1], pair_b))
    cand = jnp.concatenate(cand, axis=0)
    cidx = jnp.concatenate(cidx, axis=0)

    lane2 = lax.broadcasted_iota(jnp.int32, cand.shape, 1)
    head_lane0 = (lax.broadcasted_iota(jnp.int32, cand.shape, 0) // tm) * PEER_TOPK
    cleft_ref[...] = cand
    fpos_ref[...] = jnp.zeros(fpos_ref.shape, jnp.int32)

    def body2(i, _):
        def piece(r, _):
            row0 = pl.multiple_of(r * _ROUTE_ROWS, _ROUTE_ROWS)
            rows = pl.ds(row0, _ROUTE_ROWS)
            left = cleft_ref[rows, :]
            pos = jnp.argmax(left, axis=1, keepdims=True).astype(jnp.int32)
            cleft_ref[rows, :] = jnp.where(lane_c == pos, NEG_INF, left)
            out_lane = (row0 // tm) * PEER_TOPK + i
            fpos_ref[rows, :] = jnp.where(lane_c == out_lane, pos, fpos_ref[rows, :])
            return 0

        return lax.fori_loop(0, PEER_HEADS * tm // _ROUTE_ROWS, piece, 0, unroll=_ROUTE_UNROLL)

    lax.fori_loop(0, PEER_TOPK, body2, 0)
    fpos = fpos_ref[...]
    fv = _take_lanes(cand, fpos)
    fe = _take_lanes(cidx, fpos)

    grp = lane2 // PEER_TOPK == head_lane0 // PEER_TOPK
    fe = jnp.where(grp, fe, 0.0)
    mx = jnp.max(jnp.where(grp, fv, NEG_INF), axis=1, keepdims=True)
    ex = jnp.where(grp, jnp.exp(fv - mx), 0.0)
    gate = ex / jnp.sum(ex, axis=1, keepdims=True)
    e_out = fe[0:tm]
    g_out = gate[0:tm]
    for h in range(1, PEER_HEADS):
        e_out = e_out + fe[h * tm:(h + 1) * tm]
        g_out = g_out + gate[h * tm:(h + 1) * tm]
    e_ref[...] = e_out.astype(jnp.int32)
    g_ref[...] = g_out


def _peer_route(qp, sub_keys):
    m, kq = qp.shape
    tm = min(ROUTE_ROW_TILE, m)
    ne = PEER_HEADS * PEER_TOPK
    ospec = pl.BlockSpec((tm, ne), lambda i: (i, 0))
    pairs = [(a, b) for a in range(PEER_TOPK) for b in range(PEER_TOPK) if (a + 1) * (b + 1) <= PEER_TOPK]
    pad = [0] * (PEER_KEYS - len(pairs))
    pair_rows = jnp.array([[a for a, _ in pairs] + pad, [b for _, b in pairs] + pad,
                           [1] * len(pairs) + pad] + [[0] * PEER_KEYS] * 5, jnp.int32)
    return pl.pallas_call(
        _route_kernel,
        grid=(m // tm,),
        in_specs=[pl.BlockSpec((tm, kq), lambda i: (i, 0)),
                  pl.BlockSpec(sub_keys.shape, lambda i: (0, 0, 0, 0)),
                  pl.BlockSpec(pair_rows.shape, lambda i: (0, 0))],
        out_specs=[ospec, ospec],
        out_shape=[jax.ShapeDtypeStruct((m, ne), jnp.int32), jax.ShapeDtypeStruct((m, ne), F32)],
        scratch_shapes=[pltpu.VMEM((2 * PEER_HEADS * tm, PEER_KEYS), F32),
                        pltpu.VMEM((2 * PEER_HEADS * tm, PEER_KEYS), F32),
                        pltpu.VMEM((2 * PEER_HEADS * tm, PEER_KEYS), jnp.int32),
                        pltpu.VMEM((PEER_HEADS * tm, PEER_KEYS), F32),
                        pltpu.VMEM((PEER_HEADS * tm, PEER_KEYS), jnp.int32)],
        compiler_params=_params("parallel"),
    )(qp, sub_keys, pair_rows)


def _gelu_tanh(x):
    return 0.5 * x * (1.0 + jnp.tanh(0.7978845608028654 * (x + 0.044715 * (x * x * x))))


_TOKEN_GROUP = 16


def _expert_kernel(h_ref, gain_ref, e_ref, g_ref, tab_ref, next_gain_ref, o_ref, on_ref,
                   hi_ref, lo_ref, act_ref, d_ref, w3_ref, *, nj):
    j = pl.program_id(1)
    xn_ref = on_ref
    tm = h_ref.shape[0]
    te = tab_ref.shape[0]
    nb = te // LANES

    @pl.when(j == 0)
    def _():
        xn_ref[...] = _norm_rows(h_ref[...], gain_ref[...]).astype(BF16)
        o_ref[...] = h_ref[...]
        e = e_ref[...]
        hi_ref[...] = e // LANES
        lo_ref[...] = e % LANES
        act_ref[...] = jnp.zeros(act_ref.shape, F32)

    def score_slab(slot):
        d_ref[slot] = lax.dot_general(xn_ref[...], tab_ref[...], _NT, preferred_element_type=F32)

    def pick_from_slab(slot, slab):
        hi = hi_ref[...]
        lo = lo_ref[...]
        act = act_ref[...]
        for s in range(nb):
            picked = _take_lanes(d_ref[slot, :, s * LANES:(s + 1) * LANES], lo)
            act = jnp.where(hi == slab * nb + s, picked, act)
        act_ref[...] = act

    @pl.when(j == 0)
    def _():
        score_slab(0)

    @pl.when(jnp.logical_and(j >= 1, j < nj))
    def _():
        pick_from_slab((j - 1) % 2, j - 1)
        score_slab(j % 2)

    @pl.when(j == nj)
    def _():
        pick_from_slab((nj - 1) % 2, nj - 1)
        act_ref[...] = g_ref[...] * _gelu_tanh(act_ref[...])
        sub = lax.broadcasted_iota(jnp.int32, (LANES, LANES), 0)

        def group(gi, _):
            t0 = pl.multiple_of(gi * _TOKEN_GROUP, _TOKEN_GROUP)
            grids = []
            for u in range(_TOKEN_GROUP):
                hi_row = hi_ref[pl.ds(t0 + u, 1), :]
                lo_row = lo_ref[pl.ds(t0 + u, 1), :]
                w_row = act_ref[pl.ds(t0 + u, 1), :]
                a_t = jnp.where(hi_row == sub, w_row, 0.0).astype(BF16)
                b_t = jnp.where(lo_row == sub, 1.0, 0.0).astype(BF16)
                grids.append(lax.dot_general(a_t, b_t, _NT, preferred_element_type=F32))
            w3_ref[:, pl.ds(t0, _TOKEN_GROUP), :] = pltpu.einshape(
                "uik->iuk", jnp.stack(grids, axis=0)).astype(BF16)
            return 0

        lax.fori_loop(0, tm // _TOKEN_GROUP, group, 0, unroll=4)

    @pl.when(j >= nj)
    def _():
        jj = j - nj
        w = jnp.concatenate([w3_ref[jj * nb + s] for s in range(nb)], axis=1)
        d = o_ref.shape[1]
        for c0 in range(0, d, _EXPERT_OUT_COLS):
            cols = slice(c0, c0 + _EXPERT_OUT_COLS)
            o_ref[:, cols] += jnp.dot(w, tab_ref[:, cols], preferred_element_type=F32)

    @pl.when(j == 2 * nj - 1)
    def _():
        for r0 in range(0, tm, LANES):
            rows = slice(r0, min(r0 + LANES, tm))
            on_ref[rows, :] = _norm_rows(o_ref[rows, :], next_gain_ref[...]).astype(BF16)


_EXPERT_OUT_COLS = 1024


def _peer_experts(h, gain, eidx, gate, uv_tab, layer, next_gain, *, te=EXPERT_SLAB):
    m, d = h.shape
    ne = uv_tab.shape[2]
    nsel = eidx.shape[1]
    tm = min(ROW_TILE, m)
    nj = ne // te
    once = pl.Buffered(1)
    sel_spec = pl.BlockSpec((tm, nsel), lambda i, j: (i, 0), pipeline_mode=once)
    return pl.pallas_call(
        functools.partial(_expert_kernel, nj=nj),
        grid=(m // tm, 2 * nj),
        in_specs=[pl.BlockSpec((tm, d), lambda i, j: (i, 0)),
                  pl.BlockSpec((1, d), lambda i, j: (0, 0), pipeline_mode=once),
                  sel_spec, sel_spec,
                  pl.BlockSpec((None, None, te, d), lambda i, j: (layer, j // nj, j % nj, 0)),
                  pl.BlockSpec((1, d), lambda i, j: (0, 0), pipeline_mode=once)],
        out_specs=[pl.BlockSpec((tm, d), lambda i, j: (i, 0), pipeline_mode=once),
                   pl.BlockSpec((tm, d), lambda i, j: (i, 0), pipeline_mode=once)],
        out_shape=[jax.ShapeDtypeStruct((m, d), F32), jax.ShapeDtypeStruct((m, d), BF16)],
        scratch_shapes=[pltpu.VMEM((tm, nsel), jnp.int32),
                        pltpu.VMEM((tm, nsel), jnp.int32),
                        pltpu.VMEM((tm, nsel), F32),
                        pltpu.VMEM((2, tm, te), F32),
                        pltpu.VMEM((ne // LANES, tm, LANES), BF16)],
        compiler_params=_params("parallel", "arbitrary"),
    )(h, gain.reshape(1, d), eidx, gate, uv_tab, next_gain.reshape(1, d))


def _peer_ffn(h, i, prm):
    qp = _matmul(h, prm["w_q_peer"][i], gain=prm["g_ffn"][i])
    eidx, gate = _peer_route(qp, prm["peer_sub_keys"][i])
    return _peer_experts(h, prm["g_ffn"][i], eidx, gate, prm["peer_uv"], i, prm["g_ple"][i])


def _ple(h, hn, p, i, prm):
    return _matmul(hn, prm["w_ple_gate"][i], residual=h, ple=(p, prm["w_ple_proj"][i]), tn=COL_TILE // 2)


def _trunk(x, pe, past, prm):
    b, t, d = x.shape
    m = b * t
    h = x.reshape(m, d)
    pe = pe.reshape(pe.shape[0], m, pe.shape[-1])

    q, k, v = _qkv_proj(h, prm["w_qkv_a"], prm["g_mix"][0])
    if past is None:
        o = _sb_prompt(q, k, v, batch=b, seq=t)
    else:
        o = _sb_sample(q, k, v, past[0][0], past[1][0], batch=b, tq=t)
    h = _matmul(o, prm["w_o_a"], residual=h)
    h, hn = _peer_ffn(h, 0, prm)
    h = _ple(h, hn, pe[0], 0, prm)

    kv = _matmul(h, prm["w_kv_b"], gain=prm["g_kv"])
    nkv = N_KV_B * HEAD_DIM
    kb_new = kv[:, :nkv].reshape(b, t, nkv)
    vb_new = kv[:, nkv:].reshape(b, t, nkv)

    qb = _matmul(h, prm["w_q_b"], gain=prm["g_mix"][1], out_dtype=BF16, out_scale=SCALE)
    win = (N_PREV_CHUNKS + 2) * CHUNK
    if past is None:
        front = (N_PREV_CHUNKS + 1) * CHUNK
        kpad = jnp.pad(kb_new.astype(BF16), ((0, 0), (front, 0), (0, 0)))
        vpad = jnp.pad(vb_new.astype(BF16), ((0, 0), (front, 0), (0, 0)))
        bias = _band_bias(prm["rel_bias_b"][0], CHUNK, win)
        ob = _band_attention(qb, kpad, vpad, bias, batch=b, nq=t // CHUNK, cq=CHUNK, win=win,
                             chunked=True, lo_static=0)
    else:
        ck = past[2].reshape(b, -1, nkv)
        cv = past[3].reshape(b, -1, nkv)
        front = win - ck.shape[1] - t
        kpad = jnp.pad(jnp.concatenate([ck, kb_new], axis=1).astype(BF16), ((0, 0), (front, 0), (0, 0)))
        vpad = jnp.pad(jnp.concatenate([cv, vb_new], axis=1).astype(BF16), ((0, 0), (front, 0), (0, 0)))
        bias = _band_bias(prm["rel_bias_b"][0], t, win)
        ob = _band_attention(qb, kpad, vpad, bias, batch=b, nq=1, cq=t, win=win,
                             chunked=False, lo_static=front)
    h = _matmul(ob, prm["w_o_b"], residual=h)
    h, hn = _peer_ffn(h, 1, prm)
    h = _ple(h, hn, pe[1], 1, prm)

    y = _rmsnorm(h, prm["g_final"]).reshape(b, t, d)
    heads = k.shape[1] // HEAD_DIM
    a_k = k.reshape(1, b, t, heads, HEAD_DIM)
    a_v = v.reshape(1, b, t, heads, HEAD_DIM)
    if past is None:
        keep = min(N_PREV_CHUNKS * CHUNK, t)
        b_k, b_v = kb_new[:, t - keep:], vb_new[:, t - keep:]
    else:
        b_k, b_v = kb_new, vb_new
    b_k = b_k.reshape(b, -1, N_KV_B, HEAD_DIM)
    b_v = b_v.reshape(b, -1, N_KV_B, HEAD_DIM)
    return y, a_k, a_v, b_k, b_v


def kernel(x_prompt, x_sample, cache_a_k, cache_a_v, cache_b_k, cache_b_v, p_prompt, p_sample, g_mix, w_qkv_a, w_o_a, g_kv, w_kv_b, w_q_b, rel_bias_b, w_o_b, g_ffn, w_q_peer, peer_sub_keys, peer_u, peer_v, g_ple, w_ple_gate, w_ple_proj, g_final):
    prm = dict(
        g_mix=g_mix, g_kv=g_kv, g_ffn=g_ffn, g_ple=g_ple, g_final=g_final,
        w_qkv_a=_to_bf16(w_qkv_a)[0],
        w_o_a=_to_bf16(w_o_a)[0],
        w_kv_b=_to_bf16(w_kv_b),
        w_q_b=_to_bf16(w_q_b)[0],
        w_o_b=_to_bf16(w_o_b)[0],
        rel_bias_b=rel_bias_b,
        w_q_peer=_to_bf16(w_q_peer),
        peer_sub_keys=peer_sub_keys,
        peer_uv=_peer_tables_bf16(peer_u, peer_v),
        w_ple_gate=_to_bf16(w_ple_gate),
        w_ple_proj=_to_bf16(w_ple_proj),
    )
    y_p, ak_p, av_p, bk_p, bv_p = _trunk(x_prompt, p_prompt, None, prm)
    y_s, ak_s, av_s, bk_s, bv_s = _trunk(x_sample, p_sample,
                                         (cache_a_k, cache_a_v, cache_b_k, cache_b_v), prm)
    return (y_p, y_s, ak_p, av_p, bk_p, bv_p, ak_s, av_s, bk_s, bv_s)
```

```python
import functools

import jax
import jax.numpy as jnp
import numpy as np
from jax import lax
from jax.experimental import pallas as pl
from jax.experimental.pallas import tpu as pltpu

F32 = jnp.float32
BF16 = jnp.bfloat16

EPS = 1e-6
HEAD_DIM = 128
CHUNK = 64
N_PREV_CHUNKS = 8
N_KV_B = 8
G_B = 4
REL_CLIP = 128
PEER_HEADS = 8
PEER_KEYS = 128
PEER_TOPK = 16
SCALE = HEAD_DIM ** -0.5
NEG_INF = float("-inf")

VMEM_LIMIT_BYTES = 60 * 1024 * 1024
LANES = 128

ROW_TILE = 512
COL_TILE = 1024
ROUTE_ROW_TILE = 256
EXPERT_SLAB = 512
SB_BLOCK = 256

_NT = (((1,), (1,)), ((), ()))


def _params(*sem):
    return pltpu.CompilerParams(dimension_semantics=sem, vmem_limit_bytes=VMEM_LIMIT_BYTES)


def _norm_rows(x, g):
    ms = jnp.mean(x * x, axis=-1, keepdims=True)
    return x * lax.rsqrt(ms + EPS) * g


def _sigmoid(x):
    return 1.0 / (1.0 + jnp.exp(-x))


def _normed_rows_pipeline(x_ref, g_ref, xn_ref, step):
    i = pl.program_id(0)
    j = pl.program_id(1)
    last = pl.num_programs(1) - 1
    slot = i % 2

    def normalise(dst):
        xn_ref[dst] = _norm_rows(x_ref[...], g_ref[...]).astype(BF16)

    @pl.when(jnp.logical_and(i == 0, j == 0))
    def _():
        normalise(0)

    @pl.when(j < last)
    def _():
        step(slot, lambda: None)

    @pl.when(j == last)
    def _():
        step(slot, lambda: normalise(1 - slot))


def _next_tile_rows(ni, nj):
    return lambda i, j: (jnp.minimum(i + jnp.where(j == nj - 1, 1, 0), ni - 1), 0)


def _mm_kernel(*refs, norm, pipelined, res, ple, out_scale):
    it = iter(refs)
    x_ref = next(it)
    g_ref = next(it) if norm else None
    w_ref = next(it)
    res_ref = next(it) if res else None
    p_ref = next(it) if ple else None
    wp_ref = next(it) if ple else None
    o_ref = next(it)
    xn_ref = next(it) if norm else None

    def step(x):
        acc = jnp.dot(x, w_ref[...], preferred_element_type=F32)
        if ple:
            proj = jnp.dot(p_ref[...].astype(BF16), wp_ref[...], preferred_element_type=F32)
            acc = _sigmoid(acc) * proj
        if res:
            acc = acc + res_ref[...]
        if out_scale is not None:
            acc = acc * out_scale
        o_ref[...] = acc.astype(o_ref.dtype)

    if pipelined:
        def pipelined_step(slot, after):
            step(xn_ref[slot])
            after()

        _normed_rows_pipeline(x_ref, g_ref, xn_ref, pipelined_step)
    elif norm:
        @pl.when(pl.program_id(1) == 0)
        def _():
            xn_ref[...] = _norm_rows(x_ref[...], g_ref[...]).astype(BF16)
        step(xn_ref[...])
    else:
        step(x_ref[...])


def _matmul(x, w, *, gain=None, residual=None, ple=None, out_dtype=F32, out_scale=None, tn=COL_TILE):
    m, k = x.shape
    n = w.shape[1]
    norm = gain is not None
    tm = min(ROW_TILE if norm else 2 * ROW_TILE, m)
    tn = min(tn, n)
    pipelined = norm and ple is None and n // tn >= 2
    x_map = _next_tile_rows(m // tm, n // tn) if pipelined else (lambda i, j: (i, 0))
    in_specs = [pl.BlockSpec((tm, k), x_map)]
    args = [x]
    if norm:
        in_specs.append(pl.BlockSpec((1, k), lambda i, j: (0, 0)))
        args.append(gain.reshape(1, k))
    in_specs.append(pl.BlockSpec((k, tn), lambda i, j: (0, j)))
    args.append(w)
    if residual is not None:
        in_specs.append(pl.BlockSpec((tm, tn), lambda i, j: (i, j)))
        args.append(residual)
    if ple is not None:
        p, wp = ple
        kp = p.shape[1]
        in_specs.append(pl.BlockSpec((tm, kp), lambda i, j: (i, 0)))
        in_specs.append(pl.BlockSpec((kp, tn), lambda i, j: (0, j)))
        args += [p, wp]
    return pl.pallas_call(
        functools.partial(_mm_kernel, norm=norm, pipelined=pipelined, res=residual is not None,
                          ple=ple is not None, out_scale=out_scale),
        grid=(m // tm, n // tn),
        in_specs=in_specs,
        out_specs=pl.BlockSpec((tm, tn), lambda i, j: (i, j)),
        out_shape=jax.ShapeDtypeStruct((m, n), out_dtype),
        scratch_shapes=([pltpu.VMEM((2, tm, k), BF16)] if pipelined else
                        [pltpu.VMEM((tm, k), BF16)] if norm else []),
        compiler_params=_params("arbitrary" if pipelined else "parallel", "arbitrary"),
    )(*args)


def _qkv_kernel(x_ref, g_ref, w_ref, q_ref, k_ref, v_ref, xn_ref, *, nq):
    j = pl.program_id(1)

    def step(slot, after):
        acc = jnp.dot(xn_ref[slot], w_ref[...], preferred_element_type=F32)
        after()

        @pl.when(j < nq)
        def _():
            q_ref[...] = (acc * SCALE).astype(q_ref.dtype)

        @pl.when(jnp.logical_and(j >= nq, j < 2 * nq))
        def _():
            k_ref[...] = acc

        @pl.when(j >= 2 * nq)
        def _():
            v_ref[...] = acc

    _normed_rows_pipeline(x_ref, g_ref, xn_ref, step)


def _qkv_proj(x, w, gain, *, tn=COL_TILE):
    m, k = x.shape
    n = w.shape[1] // 3
    tm = min(ROW_TILE, m)
    nq = n // tn

    def ospec(first):
        return pl.BlockSpec((tm, tn), lambda i, j: (i, jnp.clip(j - first, 0, nq - 1)))

    return pl.pallas_call(
        functools.partial(_qkv_kernel, nq=nq),
        grid=(m // tm, 3 * nq),
        in_specs=[pl.BlockSpec((tm, k), _next_tile_rows(m // tm, 3 * nq)),
                  pl.BlockSpec((1, k), lambda i, j: (0, 0)),
                  pl.BlockSpec((k, tn), lambda i, j: (0, j))],
        out_specs=[ospec(0), ospec(nq), ospec(2 * nq)],
        out_shape=[jax.ShapeDtypeStruct((m, n), BF16), jax.ShapeDtypeStruct((m, n), F32),
                   jax.ShapeDtypeStruct((m, n), F32)],
        scratch_shapes=[pltpu.VMEM((2, tm, k), BF16)],
        compiler_params=_params("arbitrary", "arbitrary"),
    )(x, gain.reshape(1, k), w)


def _cast_kernel(x_ref, o_ref):
    o_ref[...] = x_ref[...].astype(o_ref.dtype)


_CAST_ROWS = 256
_CAST_COLS = 4096


def _to_bf16(x):
    x2 = x.reshape(-1, x.shape[-1])
    rows, cols = x2.shape
    tr = min(_CAST_ROWS, rows)
    tc = min(_CAST_COLS, cols)
    spec = pl.BlockSpec((tr, tc), lambda i, j: (i, j))
    out = pl.pallas_call(
        _cast_kernel,
        grid=(rows // tr, cols // tc),
        in_specs=[spec],
        out_specs=spec,
        out_shape=jax.ShapeDtypeStruct((rows, cols), BF16),
        compiler_params=_params("parallel", "parallel"),
    )(x2)
    return out.reshape(x.shape)


def _stack_cast_kernel(u_ref, v_ref, o_ref):
    c = pl.program_id(1)

    @pl.when(c == 0)
    def _():
        o_ref[...] = u_ref[...].astype(o_ref.dtype)

    @pl.when(c == 1)
    def _():
        o_ref[...] = v_ref[...].astype(o_ref.dtype)


def _peer_tables_bf16(u, v):
    nl, ne, d = u.shape
    te = _CAST_ROWS
    nblk = ne // te
    uspec = pl.BlockSpec((None, te, d), lambda l, c, e: (l, jnp.where(c == 0, e, nblk - 1), 0))
    vspec = pl.BlockSpec((None, te, d), lambda l, c, e: (l, jnp.where(c == 1, e, 0), 0))
    return pl.pallas_call(
        _stack_cast_kernel,
        grid=(nl, 2, nblk),
        in_specs=[uspec, vspec],
        out_specs=pl.BlockSpec((None, None, te, d), lambda l, c, e: (l, c, e, 0)),
        out_shape=jax.ShapeDtypeStruct((nl, 2, ne, d), BF16),
        compiler_params=_params("parallel", "arbitrary", "arbitrary"),
    )(u, v)


def _rmsnorm_kernel(x_ref, g_ref, o_ref):
    o_ref[...] = _norm_rows(x_ref[...], g_ref[...])


def _rmsnorm(x, gain):
    m, k = x.shape
    tm = min(ROW_TILE, m)
    return pl.pallas_call(
        _rmsnorm_kernel,
        grid=(m // tm,),
        in_specs=[pl.BlockSpec((tm, k), lambda i: (i, 0)), pl.BlockSpec((1, k), lambda i: (0, 0))],
        out_specs=pl.BlockSpec((tm, k), lambda i: (i, 0)),
        out_shape=jax.ShapeDtypeStruct((m, k), F32),
        compiler_params=_params("parallel"),
    )(x, gain.reshape(1, k))


def _strict_lower_ones(n):
    r = lax.broadcasted_iota(jnp.int32, (2 * n, n), 0) % n
    c = lax.broadcasted_iota(jnp.int32, (2 * n, n), 1)
    return jnp.where(r > c, 1.0, 0.0).astype(BF16)


_SB_DEAD_BELOW = -104.0


def _sb_log_terms(q, kb, mask):
    z = lax.dot_general(q, kb, _NT, preferred_element_type=F32)
    nz = -z
    sp = jnp.log(1.0 + jnp.exp(-jnp.maximum(z, nz)))
    log_beta = jnp.minimum(z, 0.0) - sp
    log_keep = jnp.minimum(nz, 0.0) - sp
    if mask is not None:
        log_keep = jnp.where(mask, log_keep, 0.0)
    return log_beta, log_keep


def _sb_block(q, kb, vb, c, acc, tri2, mask):
    log_beta, log_keep = _sb_log_terms(q, kb, mask)
    hi = log_keep.astype(BF16)
    lo = (log_keep - hi.astype(F32)).astype(BF16)
    after = jnp.dot(jnp.concatenate([hi, lo], axis=1), tri2, preferred_element_type=F32)
    w = jnp.exp(log_beta + after + c)
    if mask is not None:
        w = jnp.where(mask, w, 0.0)
    acc = acc + jnp.dot(w.astype(BF16), vb, preferred_element_type=F32)
    c = c + jnp.sum(log_keep, axis=1, keepdims=True)
    return c, acc


_SB_QUERY_GROUP = 16


def _sb_prompt_kernel(q_ref, k_ref, v_ref, o_ref, kb_ref, vb_ref, *, blk):
    seq = q_ref.shape[0]
    kb_ref[pl.ds(0, blk), :] = jnp.zeros((blk, HEAD_DIM), BF16)
    vb_ref[pl.ds(0, blk), :] = jnp.zeros((blk, HEAD_DIM), BF16)
    kb_ref[pl.ds(blk, seq), :] = k_ref[...].astype(BF16)
    vb_ref[pl.ds(blk, seq), :] = v_ref[...].astype(BF16)
    tri = _strict_lower_ones(blk)
    row = lax.broadcasted_iota(jnp.int32, (blk, blk), 0)
    col = lax.broadcasted_iota(jnp.int32, (blk, blk), 1)
    diag_mask = col < row

    def kv_block(j):
        r0 = pl.multiple_of((j + 1) * blk, blk)
        return kb_ref[pl.ds(r0, blk), :], vb_ref[pl.ds(r0, blk), :]

    def older_blocks(i, q, c, acc):
        def k_cond(carry):
            j, cmax, _, _ = carry
            return jnp.logical_and(j >= 0, cmax > _SB_DEAD_BELOW)

        def k_body(carry):
            j, _, c, acc = carry
            c, acc = _sb_block(q, *kv_block(j), c, acc, tri, None)
            return j - 1, jnp.max(c), c, acc

        return lax.while_loop(k_cond, k_body, (i - 2, jnp.max(c), c, acc))[3]

    def group_body(p, _):
        state = []
        for i in [_SB_QUERY_GROUP * p + u for u in range(_SB_QUERY_GROUP)]:
            q = q_ref[pl.ds(pl.multiple_of(i * blk, blk), blk), :]
            c = jnp.zeros((blk, 1), F32)
            acc = jnp.zeros((blk, HEAD_DIM), F32)
            c, acc = _sb_block(q, *kv_block(i), c, acc, tri, diag_mask)
            c, acc = _sb_block(q, *kv_block(i - 1), c, acc, tri, None)
            state.append((i, q, c, acc))
        for i, q, c, acc in state:
            acc = older_blocks(i, q, c, acc)
            o_ref[pl.ds(pl.multiple_of(i * blk, blk), blk), :] = acc.astype(o_ref.dtype)
        return 0

    lax.fori_loop(0, seq // (_SB_QUERY_GROUP * blk), group_body, 0)


def _sb_prompt(q, k, v, *, batch, seq, blk=SB_BLOCK):
    t, hd = q.shape
    heads = hd // HEAD_DIM
    blk = min(blk, seq)
    spec = pl.BlockSpec((seq, HEAD_DIM), lambda b, h: (b, h))
    return pl.pallas_call(
        functools.partial(_sb_prompt_kernel, blk=blk),
        grid=(batch, heads),
        in_specs=[spec, spec, spec],
        out_specs=spec,
        out_shape=jax.ShapeDtypeStruct((t, hd), BF16),
        scratch_shapes=[pltpu.VMEM((seq + blk, HEAD_DIM), BF16), pltpu.VMEM((seq + blk, HEAD_DIM), BF16)],
        compiler_params=_params("parallel", "parallel"),
    )(q, k, v)


_SB_HEAD_GROUP = 32


def _sb_probe_kernel(q_ref, kn_ref, kc_ref, alive_ref, *, blk):
    tq = q_ref.shape[0]
    row = lax.broadcasted_iota(jnp.int32, (tq, blk), 0)
    col = lax.broadcasted_iota(jnp.int32, (tq, blk), 1)
    kb = pltpu.einshape("thd->htd", kc_ref[...]).astype(BF16)
    cmax = jnp.full((tq, 1), NEG_INF, F32)
    for g in range(_SB_HEAD_GROUP):
        lanes = slice(g * HEAD_DIM, (g + 1) * HEAD_DIM)
        q = q_ref[:, lanes]
        c = jnp.sum(_sb_log_terms(q, kn_ref[:, lanes], col < row)[1], axis=1, keepdims=True)
        c = c + jnp.sum(_sb_log_terms(q, kb[g], None)[1], axis=1, keepdims=True)
        cmax = jnp.maximum(cmax, c)
    flag = jnp.where(jnp.max(cmax) > _SB_DEAD_BELOW - 1.0, 1, 0)
    alive_ref[...] = jnp.full(alive_ref.shape, flag, jnp.int32)


def _sb_sample_kernel(alive_ref, q_ref, kn_ref, vn_ref, kc_ref, vc_ref, o_ref, c_ref, acc_ref, *, blk):
    del alive_ref
    n = pl.program_id(2)
    tq = q_ref.shape[0]
    tri = _strict_lower_ones(blk)

    @pl.when(n == 0)
    def _():
        row = lax.broadcasted_iota(jnp.int32, (tq, blk), 0)
        col = lax.broadcasted_iota(jnp.int32, (tq, blk), 1)
        for g in range(_SB_HEAD_GROUP):
            lanes = slice(g * HEAD_DIM, (g + 1) * HEAD_DIM)
            c, acc = _sb_block(q_ref[:, lanes], kn_ref[:, lanes], vn_ref[:, lanes],
                               jnp.zeros((tq, 1), F32), jnp.zeros((tq, HEAD_DIM), F32), tri, col < row)
            c_ref[g] = c
            acc_ref[g] = acc

    @pl.when(jnp.max(c_ref[...]) > _SB_DEAD_BELOW)
    def _():
        kb = pltpu.einshape("thd->htd", kc_ref[...]).astype(BF16)
        vb = pltpu.einshape("thd->htd", vc_ref[...]).astype(BF16)
        for g in range(_SB_HEAD_GROUP):
            lanes = slice(g * HEAD_DIM, (g + 1) * HEAD_DIM)
            c, acc = _sb_block(q_ref[:, lanes], kb[g], vb[g], c_ref[g], acc_ref[g], tri, None)
            c_ref[g] = c
            acc_ref[g] = acc

    @pl.when(n == pl.num_programs(2) - 1)
    def _():
        for g in range(_SB_HEAD_GROUP):
            o_ref[:, g * HEAD_DIM:(g + 1) * HEAD_DIM] = acc_ref[g].astype(o_ref.dtype)


def _sb_sample(q, k_new, v_new, cache_k, cache_v, *, batch, tq, blk=SB_BLOCK):
    t, hd = q.shape
    heads = hd // HEAD_DIM
    past = cache_k.shape[1]
    nblk = past // blk
    gw = _SB_HEAD_GROUP * HEAD_DIM

    def pad_block(a):
        a = a.reshape(batch, tq, hd)
        a = jnp.pad(a, ((0, 0), (0, blk - tq), (0, 0)))
        return a.reshape(batch * blk, hd).astype(BF16)

    assert heads == _SB_HEAD_GROUP
    kn = pad_block(k_new)
    vn = pad_block(v_new)
    newest = (None, blk, _SB_HEAD_GROUP, HEAD_DIM)
    alive = pl.pallas_call(
        functools.partial(_sb_probe_kernel, blk=blk),
        grid=(batch,),
        in_specs=[pl.BlockSpec((tq, gw), lambda b: (b, 0)),
                  pl.BlockSpec((blk, gw), lambda b: (b, 0)),
                  pl.BlockSpec(newest, lambda b: (b, nblk - 1, 0, 0))],
        out_specs=pl.BlockSpec((None, 8, LANES), lambda b: (b, 0, 0)),
        out_shape=jax.ShapeDtypeStruct((batch, 8, LANES), jnp.int32),
        compiler_params=_params("parallel"),
    )(q, kn, cache_k)[:, 0, 0]

    def cache_block(b, h, n, alive_ref):
        return (b, nblk - 1 - jnp.where(alive_ref[b] > 0, n, 0), h, 0)

    qspec = pl.BlockSpec((tq, gw), lambda b, h, n, alive_ref: (b, h))
    nspec = pl.BlockSpec((blk, gw), lambda b, h, n, alive_ref: (b, h))
    cspec = pl.BlockSpec(newest, cache_block)
    return pl.pallas_call(
        functools.partial(_sb_sample_kernel, blk=blk),
        grid_spec=pltpu.PrefetchScalarGridSpec(
            num_scalar_prefetch=1,
            grid=(batch, heads // _SB_HEAD_GROUP, nblk),
            in_specs=[qspec, nspec, nspec, cspec, cspec],
            out_specs=qspec,
            scratch_shapes=[pltpu.VMEM((_SB_HEAD_GROUP, tq, 1), F32),
                            pltpu.VMEM((_SB_HEAD_GROUP, tq, HEAD_DIM), F32)]),
        out_shape=jax.ShapeDtypeStruct((t, hd), BF16),
        compiler_params=_params("parallel", "parallel", "arbitrary"),
    )(alive, q, kn, vn, cache_k, cache_v)


_BAND_UNROLL = 16


def _band_kernel(q_ref, k_ref, v_ref, bias_ref, o_ref, *, cq, win, nq, chunked, lo_static):
    bias = bias_ref[...].reshape(G_B * cq, win)
    col = lax.broadcasted_iota(jnp.int32, (G_B * cq, win), 1)

    def chunk(c, _):
        r0 = pl.multiple_of(c * cq, cq)
        if chunked:
            lo = jnp.maximum(CHUNK, (N_PREV_CHUNKS + 1 - c) * CHUNK)
        else:
            lo = lo_static
        kb = k_ref[pl.ds(r0, win), :]
        vb = v_ref[pl.ds(r0, win), :]
        qb = q_ref[pl.ds(r0, cq), :]
        q4 = jnp.concatenate([qb[:, g * HEAD_DIM:(g + 1) * HEAD_DIM] for g in range(G_B)], axis=0)
        s = lax.dot_general(q4, kb, _NT, preferred_element_type=F32) + bias
        s = jnp.where(col >= lo, s, NEG_INF)
        m = jnp.max(s, axis=1, keepdims=True)
        e = jnp.exp(s - m)
        o = jnp.dot(e.astype(BF16), vb, preferred_element_type=F32) * (1.0 / jnp.sum(e, axis=1, keepdims=True))
        for g in range(G_B):
            o_ref[pl.ds(r0, cq), g * HEAD_DIM:(g + 1) * HEAD_DIM] = o[g * cq:(g + 1) * cq].astype(o_ref.dtype)
        return 0

    lax.fori_loop(0, nq, chunk, 0, unroll=_BAND_UNROLL if nq % _BAND_UNROLL == 0 else 1)


def _band_bias(rel_table, cq, win):
    lg = win + cq - 1
    idx = np.clip(win - 1 - np.arange(lg), -REL_CLIP, REL_CLIP) + REL_CLIP
    g = rel_table[idx].astype(F32).T
    rows = jnp.tile(g, (1, cq + 1))[:, :cq * (lg + 1)].reshape(-1, cq, lg + 1)
    return rows[:, ::-1, :win]


def _band_attention(q, kpad, vpad, bias, *, batch, nq, cq, win, chunked, lo_static):
    t, hd = q.shape
    rows = kpad.shape[1]
    qw = G_B * HEAD_DIM
    qspec = pl.BlockSpec((nq * cq, qw), lambda b, h: (b, h))
    kspec = pl.BlockSpec((None, rows, HEAD_DIM), lambda b, h: (b, 0, h))
    bspec = pl.BlockSpec((G_B, cq, win), lambda b, h: (h, 0, 0))
    return pl.pallas_call(
        functools.partial(_band_kernel, cq=cq, win=win, nq=nq, chunked=chunked, lo_static=lo_static),
        grid=(batch, N_KV_B),
        in_specs=[qspec, kspec, kspec, bspec],
        out_specs=qspec,
        out_shape=jax.ShapeDtypeStruct((t, hd), BF16),
        compiler_params=_params("parallel", "parallel"),
    )(q, kpad, vpad, bias)


def _split3_dot_nt(a, b):
    a_hi = a.astype(BF16)
    a_lo = (a - a_hi.astype(F32)).astype(BF16)
    b_hi = b.astype(BF16)
    b_lo = (b - b_hi.astype(F32)).astype(BF16)
    return lax.dot_general(jnp.concatenate([a_hi, a_lo, a_hi], axis=1),
                           jnp.concatenate([b_hi, b_hi, b_lo], axis=1), _NT, preferred_element_type=F32)


def _take_lanes(x, idx):
    return jnp.take_along_axis(x, idx, axis=1, mode="promise_in_bounds")


_ROUTE_ROWS = 64
_ROUTE_UNROLL = 64


def _route_kernel(q_ref, sk_ref, pairs_ref, e_ref, g_ref, s_ref, left_ref, si_ref, cleft_ref, fpos_ref):
    tm = q_ref.shape[0]
    dh = sk_ref.shape[-1]
    nset = 2 * PEER_HEADS
    lane_c = lax.broadcasted_iota(jnp.int32, (_ROUTE_ROWS, PEER_KEYS), 1)

    for n in range(nset):
        sc = _split3_dot_nt(q_ref[:, n * dh:(n + 1) * dh], sk_ref[n // 2, n % 2])
        s_ref[n * tm:(n + 1) * tm, :] = sc
        left_ref[n * tm:(n + 1) * tm, :] = sc
    si_ref[...] = jnp.zeros(si_ref.shape, jnp.int32)

    def body1(i, _):
        def piece(r, _):
            rows = pl.ds(pl.multiple_of(r * _ROUTE_ROWS, _ROUTE_ROWS), _ROUTE_ROWS)
            left = left_ref[rows, :]
            pos = jnp.argmax(left, axis=1, keepdims=True).astype(jnp.int32)
            left_ref[rows, :] = jnp.where(lane_c == pos, NEG_INF, left)
            si_ref[rows, :] = jnp.where(lane_c == i, pos, si_ref[rows, :])
            return 0

        return lax.fori_loop(0, nset * tm // _ROUTE_ROWS, piece, 0, unroll=_ROUTE_UNROLL)

    lax.fori_loop(0, PEER_TOPK, body1, 0)
    si = si_ref[...]
    sv = _take_lanes(s_ref[...], si)
    si = si.astype(F32)

    pair_a = jnp.broadcast_to(pairs_ref[0:1, :], (tm, PEER_KEYS))
    pair_b = jnp.broadcast_to(pairs_ref[1:2, :], (tm, PEER_KEYS))
    pair_ok = jnp.broadcast_to(pairs_ref[2:3, :], (tm, PEER_KEYS)) > 0
    cand, cidx = [], []
    for h in range(PEER_HEADS):
        r0 = slice((2 * h) * tm, (2 * h + 1) * tm)
        r1 = slice((2 * h + 1) * tm, (2 * h + 2) * tm)
        cand.append(jnp.where(pair_ok, _take_lanes(sv[r0], pair_a) + _take_lanes(sv[r1], pair_b), NEG_INF))
        cidx.append(_take_lanes(si[r0], pair_a) * float(PEER_KEYS) + _take_lanes(si[r1], pair_b))
    cand = jnp.concatenate(cand, axis=0)
    cidx = jnp.concatenate(cidx, axis=0)

    lane2 = lax.broadcasted_iota(jnp.int32, cand.shape, 1)
    head_lane0 = (lax.broadcasted_iota(jnp.int32, cand.shape, 0) // tm) * PEER_TOPK
    cleft_ref[...] = cand
    fpos_ref[...] = jnp.zeros(fpos_ref.shape, jnp.int32)

    def body2(i, _):
        def piece(r, _):
            row0 = pl.multiple_of(r * _ROUTE_ROWS, _ROUTE_ROWS)
            rows = pl.ds(row0, _ROUTE_ROWS)
            left = cleft_ref[rows, :]
            pos = jnp.argmax(left, axis=1, keepdims=True).astype(jnp.int32)
            cleft_ref[rows, :] = jnp.where(lane_c == pos, NEG_INF, left)
            out_lane = (row0 // tm) * PEER_TOPK + i
            fpos_ref[rows, :] = jnp.where(lane_c == out_lane, pos, fpos_ref[rows, :])
            return 0

        return lax.fori_loop(0, PEER_HEADS * tm // _ROUTE_ROWS, piece, 0, unroll=_ROUTE_UNROLL)

    lax.fori_loop(0, PEER_TOPK, body2, 0)
    fpos = fpos_ref[...]
    fv = _take_lanes(cand, fpos)
    fe = _take_lanes(cidx, fpos)

    grp = lane2 // PEER_TOPK == head_lane0 // PEER_TOPK
    fe = jnp.where(grp, fe, 0.0)
    mx = jnp.max(jnp.where(grp, fv, NEG_INF), axis=1, keepdims=True)
    ex = jnp.where(grp, jnp.exp(fv - mx), 0.0)
    gate = ex / jnp.sum(ex, axis=1, keepdims=True)
    e_out = fe[0:tm]
    g_out = gate[0:tm]
    for h in range(1, PEER_HEADS):
        e_out = e_out + fe[h * tm:(h + 1) * tm]
        g_out = g_out + gate[h * tm:(h + 1) * tm]
    e_ref[...] = e_out.astype(jnp.int32)
    g_ref[...] = g_out


def _peer_route(qp, sub_keys):
    m, kq = qp.shape
    tm = min(ROUTE_ROW_TILE, m)
    ne = PEER_HEADS * PEER_TOPK
    ospec = pl.BlockSpec((tm, ne), lambda i: (i, 0))
    pairs = [(a, b) for a in range(PEER_TOPK) for b in range(PEER_TOPK) if (a + 1) * (b + 1) <= PEER_TOPK]
    pad = [0] * (PEER_KEYS - len(pairs))
    pair_rows = jnp.array([[a for a, _ in pairs] + pad, [b for _, b in pairs] + pad,
                           [1] * len(pairs) + pad] + [[0] * PEER_KEYS] * 5, jnp.int32)
    return pl.pallas_call(
        _route_kernel,
        grid=(m // tm,),
        in_specs=[pl.BlockSpec((tm, kq), lambda i: (i, 0)),
                  pl.BlockSpec(sub_keys.shape, lambda i: (0, 0, 0, 0)),
                  pl.BlockSpec(pair_rows.shape, lambda i: (0, 0))],
        out_specs=[ospec, ospec],
        out_shape=[jax.ShapeDtypeStruct((m, ne), jnp.int32), jax.ShapeDtypeStruct((m, ne), F32)],
        scratch_shapes=[pltpu.VMEM((2 * PEER_HEADS * tm, PEER_KEYS), F32),
                        pltpu.VMEM((2 * PEER_HEADS * tm, PEER_KEYS), F32),
                        pltpu.VMEM((2 * PEER_HEADS * tm, PEER_KEYS), jnp.int32),
                        pltpu.VMEM((PEER_HEADS * tm, PEER_KEYS), F32),
                        pltpu.VMEM((PEER_HEADS * tm, PEER_KEYS), jnp.int32)],
        compiler_params=_params("parallel"),
    )(qp, sub_keys, pair_rows)


def _gelu_tanh(x):
    return 0.5 * x * (1.0 + jnp.tanh(0.7978845608028654 * (x + 0.044715 * (x * x * x))))


_TOKEN_GROUP = 16


def _expert_kernel(h_ref, gain_ref, e_ref, g_ref, tab_ref, next_gain_ref, o_ref, on_ref,
                   hi_ref, lo_ref, act_ref, d_ref, w3_ref, *, nj):
    j = pl.program_id(1)
    xn_ref = on_ref
    tm = h_ref.shape[0]
    te = tab_ref.shape[0]
    nb = te // LANES

    @pl.when(j == 0)
    def _():
        xn_ref[...] = _norm_rows(h_ref[...], gain_ref[...]).astype(BF16)
        o_ref[...] = h_ref[...]
        e = e_ref[...]
        hi_ref[...] = e // LANES
        lo_ref[...] = e % LANES
        act_ref[...] = jnp.zeros(act_ref.shape, F32)

    def score_slab(slot):
        d_ref[slot] = lax.dot_general(xn_ref[...], tab_ref[...], _NT, preferred_element_type=F32)

    def pick_from_slab(slot, slab):
        hi = hi_ref[...]
        lo = lo_ref[...]
        act = act_ref[...]
        for s in range(nb):
            picked = _take_lanes(d_ref[slot, :, s * LANES:(s + 1) * LANES], lo)
            act = jnp.where(hi == slab * nb + s, picked, act)
        act_ref[...] = act

    @pl.when(j == 0)
    def _():
        score_slab(0)

    @pl.when(jnp.logical_and(j >= 1, j < nj))
    def _():
        pick_from_slab((j - 1) % 2, j - 1)
        score_slab(j % 2)

    @pl.when(j == nj)
    def _():
        pick_from_slab((nj - 1) % 2, nj - 1)
        act_ref[...] = g_ref[...] * _gelu_tanh(act_ref[...])
        sub = lax.broadcasted_iota(jnp.int32, (LANES, LANES), 0)

        def group(gi, _):
            t0 = pl.multiple_of(gi * _TOKEN_GROUP, _TOKEN_GROUP)
            grids = []
            for u in range(_TOKEN_GROUP):
                hi_row = hi_ref[pl.ds(t0 + u, 1), :]
                lo_row = lo_ref[pl.ds(t0 + u, 1), :]
                w_row = act_ref[pl.ds(t0 + u, 1), :]
                a_t = jnp.where(hi_row == sub, w_row, 0.0).astype(BF16)
                b_t = jnp.where(lo_row == sub, 1.0, 0.0).astype(BF16)
                grids.append(lax.dot_general(a_t, b_t, _NT, preferred_element_type=F32))
            w3_ref[:, pl.ds(t0, _TOKEN_GROUP), :] = pltpu.einshape(
                "uik->iuk", jnp.stack(grids, axis=0)).astype(BF16)
            return 0

        lax.fori_loop(0, tm // _TOKEN_GROUP, group, 0, unroll=4)

    @pl.when(j >= nj)
    def _():
        jj = j - nj
        w = jnp.concatenate([w3_ref[jj * nb + s] for s in range(nb)], axis=1)
        d = o_ref.shape[1]
        for c0 in range(0, d, _EXPERT_OUT_COLS):
            cols = slice(c0, c0 + _EXPERT_OUT_COLS)
            o_ref[:, cols] += jnp.dot(w, tab_ref[:, cols], preferred_element_type=F32)

    @pl.when(j == 2 * nj - 1)
    def _():
        for r0 in range(0, tm, LANES):
            rows = slice(r0, min(r0 + LANES, tm))
            on_ref[rows, :] = _norm_rows(o_ref[rows, :], next_gain_ref[...]).astype(BF16)


_EXPERT_OUT_COLS = 1024


def _peer_experts(h, gain, eidx, gate, uv_tab, layer, next_gain, *, te=EXPERT_SLAB):
    m, d = h.shape
    ne = uv_tab.shape[2]
    nsel = eidx.shape[1]
    tm = min(ROW_TILE, m)
    nj = ne // te
    once = pl.Buffered(1)
    sel_spec = pl.BlockSpec((tm, nsel), lambda i, j: (i, 0), pipeline_mode=once)
    return pl.pallas_call(
        functools.partial(_expert_kernel, nj=nj),
        grid=(m // tm, 2 * nj),
        in_specs=[pl.BlockSpec((tm, d), lambda i, j: (i, 0)),
                  pl.BlockSpec((1, d), lambda i, j: (0, 0), pipeline_mode=once),
                  sel_spec, sel_spec,
                  pl.BlockSpec((None, None, te, d), lambda i, j: (layer, j // nj, j % nj, 0)),
                  pl.BlockSpec((1, d), lambda i, j: (0, 0), pipeline_mode=once)],
        out_specs=[pl.BlockSpec((tm, d), lambda i, j: (i, 0), pipeline_mode=once),
                   pl.BlockSpec((tm, d), lambda i, j: (i, 0), pipeline_mode=once)],
        out_shape=[jax.ShapeDtypeStruct((m, d), F32), jax.ShapeDtypeStruct((m, d), BF16)],
        scratch_shapes=[pltpu.VMEM((tm, nsel), jnp.int32),
                        pltpu.VMEM((tm, nsel), jnp.int32),
                        pltpu.VMEM((tm, nsel), F32),
                        pltpu.VMEM((2, tm, te), F32),
                        pltpu.VMEM((ne // LANES, tm, LANES), BF16)],
        compiler_params=_params("parallel", "arbitrary"),
    )(h, gain.reshape(1, d), eidx, gate, uv_tab, next_gain.reshape(1, d))


def _peer_ffn(h, i, prm):
    qp = _matmul(h, prm["w_q_peer"][i], gain=prm["g_ffn"][i])
    eidx, gate = _peer_route(qp, prm["peer_sub_keys"][i])
    return _peer_experts(h, prm["g_ffn"][i], eidx, gate, prm["peer_uv"], i, prm["g_ple"][i])


def _ple(h, hn, p, i, prm):
    return _matmul(hn, prm["w_ple_gate"][i], residual=h, ple=(p, prm["w_ple_proj"][i]), tn=COL_TILE // 2)


def _trunk(x, pe, past, prm):
    b, t, d = x.shape
    m = b * t
    h = x.reshape(m, d)
    pe = pe.reshape(pe.shape[0], m, pe.shape[-1])

    q, k, v = _qkv_proj(h, prm["w_qkv_a"], prm["g_mix"][0])
    if past is None:
        o = _sb_prompt(q, k, v, batch=b, seq=t)
    else:
        o = _sb_sample(q, k, v, past[0][0], past[1][0], batch=b, tq=t)
    h = _matmul(o, prm["w_o_a"], residual=h)
    h, hn = _peer_ffn(h, 0, prm)
    h = _ple(h, hn, pe[0], 0, prm)

    kv = _matmul(h, prm["w_kv_b"], gain=prm["g_kv"])
    nkv = N_KV_B * HEAD_DIM
    kb_new = kv[:, :nkv].reshape(b, t, nkv)
    vb_new = kv[:, nkv:].reshape(b, t, nkv)

    qb = _matmul(h, prm["w_q_b"], gain=prm["g_mix"][1], out_dtype=BF16, out_scale=SCALE)
    win = (N_PREV_CHUNKS + 2) * CHUNK
    if past is None:
        front = (N_PREV_CHUNKS + 1) * CHUNK
        kpad = jnp.pad(kb_new.astype(BF16), ((0, 0), (front, 0), (0, 0)))
        vpad = jnp.pad(vb_new.astype(BF16), ((0, 0), (front, 0), (0, 0)))
        bias = _band_bias(prm["rel_bias_b"][0], CHUNK, win)
        ob = _band_attention(qb, kpad, vpad, bias, batch=b, nq=t // CHUNK, cq=CHUNK, win=win,
                             chunked=True, lo_static=0)
    else:
        ck = past[2].reshape(b, -1, nkv)
        cv = past[3].reshape(b, -1, nkv)
        front = win - ck.shape[1] - t
        kpad = jnp.pad(jnp.concatenate([ck, kb_new], axis=1).astype(BF16), ((0, 0), (front, 0), (0, 0)))
        vpad = jnp.pad(jnp.concatenate([cv, vb_new], axis=1).astype(BF16), ((0, 0), (front, 0), (0, 0)))
        bias = _band_bias(prm["rel_bias_b"][0], t, win)
        ob = _band_attention(qb, kpad, vpad, bias, batch=b, nq=1, cq=t, win=win,
                             chunked=False, lo_static=front)
    h = _matmul(ob, prm["w_o_b"], residual=h)
    h, hn = _peer_ffn(h, 1, prm)
    h = _ple(h, hn, pe[1], 1, prm)

    y = _rmsnorm(h, prm["g_final"]).reshape(b, t, d)
    heads = k.shape[1] // HEAD_DIM
    a_k = k.reshape(1, b, t, heads, HEAD_DIM)
    a_v = v.reshape(1, b, t, heads, HEAD_DIM)
    if past is None:
        keep = min(N_PREV_CHUNKS * CHUNK, t)
        b_k, b_v = kb_new[:, t - keep:], vb_new[:, t - keep:]
    else:
        b_k, b_v = kb_new, vb_new
    b_k = b_k.reshape(b, -1, N_KV_B, HEAD_DIM)
    b_v = b_v.reshape(b, -1, N_KV_B, HEAD_DIM)
    return y, a_k, a_v, b_k, b_v


def kernel(x_prompt, x_sample, cache_a_k, cache_a_v, cache_b_k, cache_b_v, p_prompt, p_sample, g_mix, w_qkv_a, w_o_a, g_kv, w_kv_b, w_q_b, rel_bias_b, w_o_b, g_ffn, w_q_peer, peer_sub_keys, peer_u, peer_v, g_ple, w_ple_gate, w_ple_proj, g_final):
    prm = dict(
        g_mix=g_mix, g_kv=g_kv, g_ffn=g_ffn, g_ple=g_ple, g_final=g_final,
        w_qkv_a=_to_bf16(w_qkv_a)[0],
        w_o_a=_to_bf16(w_o_a)[0],
        w_kv_b=_to_bf16(w_kv_b),
        w_q_b=_to_bf16(w_q_b)[0],
        w_o_b=_to_bf16(w_o_b)[0],
        rel_bias_b=rel_bias_b,
        w_q_peer=_to_bf16(w_q_peer),
        peer_sub_keys=peer_sub_keys,
        peer_uv=_peer_tables_bf16(peer_u, peer_v),
        w_ple_gate=_to_bf16(w_ple_gate),
        w_ple_proj=_to_bf16(w_ple_proj),
    )
    y_p, ak_p, av_p, bk_p, bv_p = _trunk(x_prompt, p_prompt, None, prm)
    y_s, ak_s, av_s, bk_s, bv_s = _trunk(x_sample, p_sample,
                                         (cache_a_k, cache_a_v, cache_b_k, cache_b_v), prm)
    return (y_p, y_s, ak_p, av_p, bk_p, bv_p, ak_s, av_s, bk_s, bv_s)
```

```python
import functools

import jax
import jax.numpy as jnp
import numpy as np
from jax import lax
from jax.experimental import pallas as pl
from jax.experimental.pallas import tpu as pltpu

F32 = jnp.float32
BF16 = jnp.bfloat16

EPS = 1e-6
HEAD_DIM = 128
CHUNK = 64
N_PREV_CHUNKS = 8
N_KV_B = 8
G_B = 4
REL_CLIP = 128
PEER_HEADS = 8
PEER_KEYS = 128
PEER_TOPK = 16
SCALE = HEAD_DIM ** -0.5
NEG_INF = float("-inf")

VMEM_LIMIT_BYTES = 60 * 1024 * 1024
LANES = 128

ROW_TILE = 512
COL_TILE = 1024
ROUTE_ROW_TILE = 256
EXPERT_SLAB = 512
SB_BLOCK = 256

_NT = (((1,), (1,)), ((), ()))


def _params(*sem):
    return pltpu.CompilerParams(dimension_semantics=sem, vmem_limit_bytes=VMEM_LIMIT_BYTES)


def _norm_rows(x, g):
    ms = jnp.mean(x * x, axis=-1, keepdims=True)
    return x * lax.rsqrt(ms + EPS) * g


def _sigmoid(x):
    return 1.0 / (1.0 + jnp.exp(-x))


def _normed_rows_pipeline(x_ref, g_ref, xn_ref, step):
    i = pl.program_id(0)
    j = pl.program_id(1)
    last = pl.num_programs(1) - 1
    slot = i % 2

    def normalise(dst):
        xn_ref[dst] = _norm_rows(x_ref[...], g_ref[...]).astype(BF16)

    @pl.when(jnp.logical_and(i == 0, j == 0))
    def _():
        normalise(0)

    @pl.when(j < last)
    def _():
        step(slot, lambda: None)

    @pl.when(j == last)
    def _():
        step(slot, lambda: normalise(1 - slot))


def _next_tile_rows(ni, nj):
    return lambda i, j: (jnp.minimum(i + jnp.where(j == nj - 1, 1, 0), ni - 1), 0)


def _mm_kernel(*refs, norm, pipelined, res, ple, out_scale):
    it = iter(refs)
    x_ref = next(it)
    g_ref = next(it) if norm else None
    w_ref = next(it)
    res_ref = next(it) if res else None
    p_ref = next(it) if ple else None
    wp_ref = next(it) if ple else None
    o_ref = next(it)
    xn_ref = next(it) if norm else None

    def step(x):
        acc = jnp.dot(x, w_ref[...], preferred_element_type=F32)
        if ple:
            proj = jnp.dot(p_ref[...].astype(BF16), wp_ref[...], preferred_element_type=F32)
            acc = _sigmoid(acc) * proj
        if res:
            acc = acc + res_ref[...]
        if out_scale is not None:
            acc = acc * out_scale
        o_ref[...] = acc.astype(o_ref.dtype)

    if pipelined:
        def pipelined_step(slot, after):
            step(xn_ref[slot])
            after()

        _normed_rows_pipeline(x_ref, g_ref, xn_ref, pipelined_step)
    elif norm:
        @pl.when(pl.program_id(1) == 0)
        def _():
            xn_ref[...] = _norm_rows(x_ref[...], g_ref[...]).astype(BF16)
        step(xn_ref[...])
    else:
        step(x_ref[...])


def _matmul(x, w, *, gain=None, residual=None, ple=None, out_dtype=F32, out_scale=None, tn=COL_TILE):
    m, k = x.shape
    n = w.shape[1]
    norm = gain is not None
    tm = min(ROW_TILE if norm else 2 * ROW_TILE, m)
    tn = min(tn, n)
    assert m % tm == 0 and n % tn == 0
    pipelined = norm and ple is None and n // tn >= 2
    x_map = _next_tile_rows(m // tm, n // tn) if pipelined else (lambda i, j: (i, 0))
    in_specs = [pl.BlockSpec((tm, k), x_map)]
    args = [x]
    if norm:
        in_specs.append(pl.BlockSpec((1, k), lambda i, j: (0, 0)))
        args.append(gain.reshape(1, k))
    in_specs.append(pl.BlockSpec((k, tn), lambda i, j: (0, j)))
    args.append(w)
    if residual is not None:
        in_specs.append(pl.BlockSpec((tm, tn), lambda i, j: (i, j)))
        args.append(residual)
    if ple is not None:
        p, wp = ple
        kp = p.shape[1]
        in_specs.append(pl.BlockSpec((tm, kp), lambda i, j: (i, 0)))
        in_specs.append(pl.BlockSpec((kp, tn), lambda i, j: (0, j)))
        args += [p, wp]
    return pl.pallas_call(
        functools.partial(_mm_kernel, norm=norm, pipelined=pipelined, res=residual is not None,
                          ple=ple is not None, out_scale=out_scale),
        grid=(m // tm, n // tn),
        in_specs=in_specs,
        out_specs=pl.BlockSpec((tm, tn), lambda i, j: (i, j)),
        out_shape=jax.ShapeDtypeStruct((m, n), out_dtype),
        scratch_shapes=([pltpu.VMEM((2, tm, k), BF16)] if pipelined else
                        [pltpu.VMEM((tm, k), BF16)] if norm else []),
        compiler_params=_params("arbitrary" if pipelined else "parallel", "arbitrary"),
    )(*args)


def _qkv_kernel(x_ref, g_ref, w_ref, q_ref, k_ref, v_ref, xn_ref, *, nq):
    j = pl.program_id(1)

    def step(slot, after):
        acc = jnp.dot(xn_ref[slot], w_ref[...], preferred_element_type=F32)
        after()

        @pl.when(j < nq)
        def _():
            q_ref[...] = (acc * SCALE).astype(q_ref.dtype)

        @pl.when(jnp.logical_and(j >= nq, j < 2 * nq))
        def _():
            k_ref[...] = acc

        @pl.when(j >= 2 * nq)
        def _():
            v_ref[...] = acc

    _normed_rows_pipeline(x_ref, g_ref, xn_ref, step)


def _qkv_proj(x, w, gain, *, tn=COL_TILE):
    m, k = x.shape
    n = w.shape[1] // 3
    tm = min(ROW_TILE, m)
    nq = n // tn
    assert m % tm == 0 and n % tn == 0

    def ospec(first):
        return pl.BlockSpec((tm, tn), lambda i, j: (i, jnp.clip(j - first, 0, nq - 1)))

    return pl.pallas_call(
        functools.partial(_qkv_kernel, nq=nq),
        grid=(m // tm, 3 * nq),
        in_specs=[pl.BlockSpec((tm, k), _next_tile_rows(m // tm, 3 * nq)),
                  pl.BlockSpec((1, k), lambda i, j: (0, 0)),
                  pl.BlockSpec((k, tn), lambda i, j: (0, j))],
        out_specs=[ospec(0), ospec(nq), ospec(2 * nq)],
        out_shape=[jax.ShapeDtypeStruct((m, n), BF16), jax.ShapeDtypeStruct((m, n), F32),
                   jax.ShapeDtypeStruct((m, n), F32)],
        scratch_shapes=[pltpu.VMEM((2, tm, k), BF16)],
        compiler_params=_params("arbitrary", "arbitrary"),
    )(x, gain.reshape(1, k), w)


def _cast_kernel(x_ref, o_ref):
    o_ref[...] = x_ref[...].astype(o_ref.dtype)


_CAST_ROWS = 256
_CAST_COLS = 4096


def _to_bf16(x):
    x2 = x.reshape(-1, x.shape[-1])
    rows, cols = x2.shape
    tr = min(_CAST_ROWS, rows)
    tc = min(_CAST_COLS, cols)
    spec = pl.BlockSpec((tr, tc), lambda i, j: (i, j))
    out = pl.pallas_call(
        _cast_kernel,
        grid=(rows // tr, cols // tc),
        in_specs=[spec],
        out_specs=spec,
        out_shape=jax.ShapeDtypeStruct((rows, cols), BF16),
        compiler_params=_params("parallel", "parallel"),
    )(x2)
    return out.reshape(x.shape)


def _stack_cast_kernel(u_ref, v_ref, o_ref):
    c = pl.program_id(1)

    @pl.when(c == 0)
    def _():
        o_ref[...] = u_ref[...].astype(o_ref.dtype)

    @pl.when(c == 1)
    def _():
        o_ref[...] = v_ref[...].astype(o_ref.dtype)


def _peer_tables_bf16(u, v):
    nl, ne, d = u.shape
    te = _CAST_ROWS
    nblk = ne // te
    uspec = pl.BlockSpec((None, te, d), lambda l, c, e: (l, jnp.where(c == 0, e, nblk - 1), 0))
    vspec = pl.BlockSpec((None, te, d), lambda l, c, e: (l, jnp.where(c == 1, e, 0), 0))
    return pl.pallas_call(
        _stack_cast_kernel,
        grid=(nl, 2, nblk),
        in_specs=[uspec, vspec],
        out_specs=pl.BlockSpec((None, None, te, d), lambda l, c, e: (l, c, e, 0)),
        out_shape=jax.ShapeDtypeStruct((nl, 2, ne, d), BF16),
        compiler_params=_params("parallel", "arbitrary", "arbitrary"),
    )(u, v)


def _rmsnorm_kernel(x_ref, g_ref, o_ref):
    o_ref[...] = _norm_rows(x_ref[...], g_ref[...])


def _rmsnorm(x, gain):
    m, k = x.shape
    tm = min(ROW_TILE, m)
    return pl.pallas_call(
        _rmsnorm_kernel,
        grid=(m // tm,),
        in_specs=[pl.BlockSpec((tm, k), lambda i: (i, 0)), pl.BlockSpec((1, k), lambda i: (0, 0))],
        out_specs=pl.BlockSpec((tm, k), lambda i: (i, 0)),
        out_shape=jax.ShapeDtypeStruct((m, k), F32),
        compiler_params=_params("parallel"),
    )(x, gain.reshape(1, k))


def _strict_lower_ones(n):
    r = lax.broadcasted_iota(jnp.int32, (2 * n, n), 0) % n
    c = lax.broadcasted_iota(jnp.int32, (2 * n, n), 1)
    return jnp.where(r > c, 1.0, 0.0).astype(BF16)


_SB_DEAD_BELOW = -104.0


def _sb_log_terms(q, kb, mask):
    z = lax.dot_general(q, kb, _NT, preferred_element_type=F32)
    nz = -z
    sp = jnp.log(1.0 + jnp.exp(-jnp.maximum(z, nz)))
    log_beta = jnp.minimum(z, 0.0) - sp
    log_keep = jnp.minimum(nz, 0.0) - sp
    if mask is not None:
        log_keep = jnp.where(mask, log_keep, 0.0)
    return log_beta, log_keep


def _sb_block(q, kb, vb, c, acc, tri2, mask):
    log_beta, log_keep = _sb_log_terms(q, kb, mask)
    hi = log_keep.astype(BF16)
    lo = (log_keep - hi.astype(F32)).astype(BF16)
    after = jnp.dot(jnp.concatenate([hi, lo], axis=1), tri2, preferred_element_type=F32)
    w = jnp.exp(log_beta + after + c)
    if mask is not None:
        w = jnp.where(mask, w, 0.0)
    acc = acc + jnp.dot(w.astype(BF16), vb, preferred_element_type=F32)
    c = c + jnp.sum(log_keep, axis=1, keepdims=True)
    return c, acc


_SB_QUERY_GROUP = 16


def _sb_prompt_kernel(q_ref, k_ref, v_ref, o_ref, kb_ref, vb_ref, *, blk, group):
    seq = q_ref.shape[0]
    kb_ref[pl.ds(0, blk), :] = jnp.zeros((blk, HEAD_DIM), BF16)
    vb_ref[pl.ds(0, blk), :] = jnp.zeros((blk, HEAD_DIM), BF16)
    kb_ref[pl.ds(blk, seq), :] = k_ref[...].astype(BF16)
    vb_ref[pl.ds(blk, seq), :] = v_ref[...].astype(BF16)
    tri = _strict_lower_ones(blk)
    row = lax.broadcasted_iota(jnp.int32, (blk, blk), 0)
    col = lax.broadcasted_iota(jnp.int32, (blk, blk), 1)
    diag_mask = col < row

    def kv_block(j):
        r0 = pl.multiple_of((j + 1) * blk, blk)
        return kb_ref[pl.ds(r0, blk), :], vb_ref[pl.ds(r0, blk), :]

    def older_blocks(i, q, c, acc):
        def k_cond(carry):
            j, cmax, _, _ = carry
            return jnp.logical_and(j >= 0, cmax > _SB_DEAD_BELOW)

        def k_body(carry):
            j, _, c, acc = carry
            c, acc = _sb_block(q, *kv_block(j), c, acc, tri, None)
            return j - 1, jnp.max(c), c, acc

        return lax.while_loop(k_cond, k_body, (i - 2, jnp.max(c), c, acc))[3]

    def group_body(p, _):
        state = []
        for i in [group * p + u for u in range(group)]:
            q = q_ref[pl.ds(pl.multiple_of(i * blk, blk), blk), :]
            c = jnp.zeros((blk, 1), F32)
            acc = jnp.zeros((blk, HEAD_DIM), F32)
            c, acc = _sb_block(q, *kv_block(i), c, acc, tri, diag_mask)
            c, acc = _sb_block(q, *kv_block(i - 1), c, acc, tri, None)
            state.append((i, q, c, acc))
        for i, q, c, acc in state:
            acc = older_blocks(i, q, c, acc)
            o_ref[pl.ds(pl.multiple_of(i * blk, blk), blk), :] = acc.astype(o_ref.dtype)
        return 0

    lax.fori_loop(0, seq // (group * blk), group_body, 0)


def _sb_prompt(q, k, v, *, batch, seq, blk=SB_BLOCK):
    t, hd = q.shape
    heads = hd // HEAD_DIM
    blk = min(blk, seq)
    group = min(_SB_QUERY_GROUP, seq // blk)
    assert seq % (group * blk) == 0 and t == batch * seq
    spec = pl.BlockSpec((seq, HEAD_DIM), lambda b, h: (b, h))
    return pl.pallas_call(
        functools.partial(_sb_prompt_kernel, blk=blk, group=group),
        grid=(batch, heads),
        in_specs=[spec, spec, spec],
        out_specs=spec,
        out_shape=jax.ShapeDtypeStruct((t, hd), BF16),
        scratch_shapes=[pltpu.VMEM((seq + blk, HEAD_DIM), BF16), pltpu.VMEM((seq + blk, HEAD_DIM), BF16)],
        compiler_params=_params("parallel", "parallel"),
    )(q, k, v)


_SB_HEAD_GROUP = 32


def _sb_probe_kernel(q_ref, kn_ref, kc_ref, alive_ref, *, blk):
    tq = q_ref.shape[0]
    row = lax.broadcasted_iota(jnp.int32, (tq, blk), 0)
    col = lax.broadcasted_iota(jnp.int32, (tq, blk), 1)
    kb = pltpu.einshape("thd->htd", kc_ref[...]).astype(BF16)
    cmax = jnp.full((tq, 1), NEG_INF, F32)
    for g in range(_SB_HEAD_GROUP):
        lanes = slice(g * HEAD_DIM, (g + 1) * HEAD_DIM)
        q = q_ref[:, lanes]
        c = jnp.sum(_sb_log_terms(q, kn_ref[:, lanes], col < row)[1], axis=1, keepdims=True)
        c = c + jnp.sum(_sb_log_terms(q, kb[g], None)[1], axis=1, keepdims=True)
        cmax = jnp.maximum(cmax, c)
    flag = jnp.where(jnp.max(cmax) > _SB_DEAD_BELOW - 1.0, 1, 0)
    alive_ref[...] = jnp.full(alive_ref.shape, flag, jnp.int32)


def _sb_sample_kernel(alive_ref, q_ref, kn_ref, vn_ref, kc_ref, vc_ref, o_ref, c_ref, acc_ref, *, blk):
    del alive_ref
    n = pl.program_id(2)
    tq = q_ref.shape[0]
    tri = _strict_lower_ones(blk)

    @pl.when(n == 0)
    def _():
        row = lax.broadcasted_iota(jnp.int32, (tq, blk), 0)
        col = lax.broadcasted_iota(jnp.int32, (tq, blk), 1)
        for g in range(_SB_HEAD_GROUP):
            lanes = slice(g * HEAD_DIM, (g + 1) * HEAD_DIM)
            c, acc = _sb_block(q_ref[:, lanes], kn_ref[:, lanes], vn_ref[:, lanes],
                               jnp.zeros((tq, 1), F32), jnp.zeros((tq, HEAD_DIM), F32), tri, col < row)
            c_ref[g] = c
            acc_ref[g] = acc

    @pl.when(jnp.max(c_ref[...]) > _SB_DEAD_BELOW)
    def _():
        kb = pltpu.einshape("thd->htd", kc_ref[...]).astype(BF16)
        vb = pltpu.einshape("thd->htd", vc_ref[...]).astype(BF16)
        for g in range(_SB_HEAD_GROUP):
            lanes = slice(g * HEAD_DIM, (g + 1) * HEAD_DIM)
            c, acc = _sb_block(q_ref[:, lanes], kb[g], vb[g], c_ref[g], acc_ref[g], tri, None)
            c_ref[g] = c
            acc_ref[g] = acc

    @pl.when(n == pl.num_programs(2) - 1)
    def _():
        for g in range(_SB_HEAD_GROUP):
            o_ref[:, g * HEAD_DIM:(g + 1) * HEAD_DIM] = acc_ref[g].astype(o_ref.dtype)


def _sb_sample(q, k_new, v_new, cache_k, cache_v, *, batch, tq, blk=SB_BLOCK):
    t, hd = q.shape
    heads = hd // HEAD_DIM
    past = cache_k.shape[1]
    nblk = past // blk
    assert past % blk == 0 and tq <= blk
    gw = _SB_HEAD_GROUP * HEAD_DIM

    def pad_block(a):
        a = a.reshape(batch, tq, hd)
        a = jnp.pad(a, ((0, 0), (0, blk - tq), (0, 0)))
        return a.reshape(batch * blk, hd).astype(BF16)

    assert heads == _SB_HEAD_GROUP
    kn = pad_block(k_new)
    vn = pad_block(v_new)
    newest = (None, blk, _SB_HEAD_GROUP, HEAD_DIM)
    alive = pl.pallas_call(
        functools.partial(_sb_probe_kernel, blk=blk),
        grid=(batch,),
        in_specs=[pl.BlockSpec((tq, gw), lambda b: (b, 0)),
                  pl.BlockSpec((blk, gw), lambda b: (b, 0)),
                  pl.BlockSpec(newest, lambda b: (b, nblk - 1, 0, 0))],
        out_specs=pl.BlockSpec((None, 8, LANES), lambda b: (b, 0, 0)),
        out_shape=jax.ShapeDtypeStruct((batch, 8, LANES), jnp.int32),
        compiler_params=_params("parallel"),
    )(q, kn, cache_k)[:, 0, 0]

    def cache_block(b, h, n, alive_ref):
        return (b, nblk - 1 - jnp.where(alive_ref[b] > 0, n, 0), h, 0)

    qspec = pl.BlockSpec((tq, gw), lambda b, h, n, alive_ref: (b, h))
    nspec = pl.BlockSpec((blk, gw), lambda b, h, n, alive_ref: (b, h))
    cspec = pl.BlockSpec(newest, cache_block)
    return pl.pallas_call(
        functools.partial(_sb_sample_kernel, blk=blk),
        grid_spec=pltpu.PrefetchScalarGridSpec(
            num_scalar_prefetch=1,
            grid=(batch, heads // _SB_HEAD_GROUP, nblk),
            in_specs=[qspec, nspec, nspec, cspec, cspec],
            out_specs=qspec,
            scratch_shapes=[pltpu.VMEM((_SB_HEAD_GROUP, tq, 1), F32),
                            pltpu.VMEM((_SB_HEAD_GROUP, tq, HEAD_DIM), F32)]),
        out_shape=jax.ShapeDtypeStruct((t, hd), BF16),
        compiler_params=_params("parallel", "parallel", "arbitrary"),
    )(alive, q, kn, vn, cache_k, cache_v)


_BAND_UNROLL = 16


def _band_kernel(q_ref, k_ref, v_ref, bias_ref, o_ref, *, cq, win, nq, chunked, lo_static):
    bias = bias_ref[...].reshape(G_B * cq, win)
    col = lax.broadcasted_iota(jnp.int32, (G_B * cq, win), 1)

    def chunk(c, _):
        r0 = pl.multiple_of(c * cq, cq)
        if chunked:
            lo = jnp.maximum(CHUNK, (N_PREV_CHUNKS + 1 - c) * CHUNK)
        else:
            lo = lo_static
        kb = k_ref[pl.ds(r0, win), :]
        vb = v_ref[pl.ds(r0, win), :]
        qb = q_ref[pl.ds(r0, cq), :]
        q4 = jnp.concatenate([qb[:, g * HEAD_DIM:(g + 1) * HEAD_DIM] for g in range(G_B)], axis=0)
        s = lax.dot_general(q4, kb, _NT, preferred_element_type=F32) + bias
        s = jnp.where(col >= lo, s, NEG_INF)
        m = jnp.max(s, axis=1, keepdims=True)
        e = jnp.exp(s - m)
        o = jnp.dot(e.astype(BF16), vb, preferred_element_type=F32) * (1.0 / jnp.sum(e, axis=1, keepdims=True))
        for g in range(G_B):
            o_ref[pl.ds(r0, cq), g * HEAD_DIM:(g + 1) * HEAD_DIM] = o[g * cq:(g + 1) * cq].astype(o_ref.dtype)
        return 0

    lax.fori_loop(0, nq, chunk, 0, unroll=_BAND_UNROLL if nq % _BAND_UNROLL == 0 else 1)


def _band_bias(rel_table, cq, win):
    lg = win + cq - 1
    idx = np.clip(win - 1 - np.arange(lg), -REL_CLIP, REL_CLIP) + REL_CLIP
    g = rel_table[idx].astype(F32).T
    rows = jnp.tile(g, (1, cq + 1))[:, :cq * (lg + 1)].reshape(-1, cq, lg + 1)
    return rows[:, ::-1, :win]


def _band_attention(q, kpad, vpad, bias, *, batch, nq, cq, win, chunked, lo_static):
    t, hd = q.shape
    rows = kpad.shape[1]
    qw = G_B * HEAD_DIM
    qspec = pl.BlockSpec((nq * cq, qw), lambda b, h: (b, h))
    kspec = pl.BlockSpec((None, rows, HEAD_DIM), lambda b, h: (b, 0, h))
    bspec = pl.BlockSpec((G_B, cq, win), lambda b, h: (h, 0, 0))
    return pl.pallas_call(
        functools.partial(_band_kernel, cq=cq, win=win, nq=nq, chunked=chunked, lo_static=lo_static),
        grid=(batch, N_KV_B),
        in_specs=[qspec, kspec, kspec, bspec],
        out_specs=qspec,
        out_shape=jax.ShapeDtypeStruct((t, hd), BF16),
        compiler_params=_params("parallel", "parallel"),
    )(q, kpad, vpad, bias)


def _split3_dot_nt(a, b):
    a_hi = a.astype(BF16)
    a_lo = (a - a_hi.astype(F32)).astype(BF16)
    b_hi = b.astype(BF16)
    b_lo = (b - b_hi.astype(F32)).astype(BF16)
    return lax.dot_general(jnp.concatenate([a_hi, a_lo, a_hi], axis=1),
                           jnp.concatenate([b_hi, b_hi, b_lo], axis=1), _NT, preferred_element_type=F32)


def _take_lanes(x, idx):
    return jnp.take_along_axis(x, idx, axis=1, mode="promise_in_bounds")


_ROUTE_ROWS = 64
_ROUTE_UNROLL = 64


def _route_kernel(q_ref, sk_ref, pairs_ref, e_ref, g_ref, s_ref, left_ref, si_ref, cleft_ref, fpos_ref):
    tm = q_ref.shape[0]
    dh = sk_ref.shape[-1]
    nset = 2 * PEER_HEADS
    lane_c = lax.broadcasted_iota(jnp.int32, (_ROUTE_ROWS, PEER_KEYS), 1)

    for n in range(nset):
        sc = _split3_dot_nt(q_ref[:, n * dh:(n + 1) * dh], sk_ref[n // 2, n % 2])
        s_ref[n * tm:(n + 1) * tm, :] = sc
        left_ref[n * tm:(n + 1) * tm, :] = sc
    si_ref[...] = jnp.zeros(si_ref.shape, jnp.int32)

    def body1(i, _):
        def piece(r, _):
            rows = pl.ds(pl.multiple_of(r * _ROUTE_ROWS, _ROUTE_ROWS), _ROUTE_ROWS)
            left = left_ref[rows, :]
            pos = jnp.argmax(left, axis=1, keepdims=True).astype(jnp.int32)
            left_ref[rows, :] = jnp.where(lane_c == pos, NEG_INF, left)
            si_ref[rows, :] = jnp.where(lane_c == i, pos, si_ref[rows, :])
            return 0

        return lax.fori_loop(0, nset * tm // _ROUTE_ROWS, piece, 0, unroll=_ROUTE_UNROLL)

    lax.fori_loop(0, PEER_TOPK, body1, 0)
    si = si_ref[...]
    sv = _take_lanes(s_ref[...], si)
    si = si.astype(F32)

    pair_a = jnp.broadcast_to(pairs_ref[0:1, :], (tm, PEER_KEYS))
    pair_b = jnp.broadcast_to(pairs_ref[1:2, :], (tm, PEER_KEYS))
    pair_ok = jnp.broadcast_to(pairs_ref[2:3, :], (tm, PEER_KEYS)) > 0
    cand, cidx = [], []
    for h in range(PEER_HEADS):
        r0 = slice((2 * h) * tm, (2 * h + 1) * tm)
        r1 = slice((2 * h + 1) * tm, (2 * h + 2) * tm)
        cand.append(jnp.where(pair_ok, _take_lanes(sv[r0], pair_a) + _take_lanes(sv[r1], pair_b), NEG_INF))
        cidx.append(_take_lanes(si[r0], pair_a) * float(PEER_KEYS) + _take_lanes(si[r1], pair_b))
    cand = jnp.concatenate(cand, axis=0)
    cidx = jnp.concatenate(cidx, axis=0)

    lane2 = lax.broadcasted_iota(jnp.int32, cand.shape, 1)
    head_lane0 = (lax.broadcasted_iota(jnp.int32, cand.shape, 0) // tm) * PEER_TOPK
    cleft_ref[...] = cand
    fpos_ref[...] = jnp.zeros(fpos_ref.shape, jnp.int32)

    def body2(i, _):
        def piece(r, _):
            row0 = pl.multiple_of(r * _ROUTE_ROWS, _ROUTE_ROWS)
            rows = pl.ds(row0, _ROUTE_ROWS)
            left = cleft_ref[rows, :]
            pos = jnp.argmax(left, axis=1, keepdims=True).astype(jnp.int32)
            cleft_ref[rows, :] = jnp.where(lane_c == pos, NEG_INF, left)
            out_lane = (row0 // tm) * PEER_TOPK + i
            fpos_ref[rows, :] = jnp.where(lane_c == out_lane, pos, fpos_ref[rows, :])
            return 0

        return lax.fori_loop(0, PEER_HEADS * tm // _ROUTE_ROWS, piece, 0, unroll=_ROUTE_UNROLL)

    lax.fori_loop(0, PEER_TOPK, body2, 0)
    fpos = fpos_ref[...]
    fv = _take_lanes(cand, fpos)
    fe = _take_lanes(cidx, fpos)

    grp = lane2 // PEER_TOPK == head_lane0 // PEER_TOPK
    fe = jnp.where(grp, fe, 0.0)
    mx = jnp.max(jnp.where(grp, fv, NEG_INF), axis=1, keepdims=True)
    ex = jnp.where(grp, jnp.exp(fv - mx), 0.0)
    gate = ex / jnp.sum(ex, axis=1, keepdims=True)
    e_out = fe[0:tm]
    g_out = gate[0:tm]
    for h in range(1, PEER_HEADS):
        e_out = e_out + fe[h * tm:(h + 1) * tm]
        g_out = g_out + gate[h * tm:(h + 1) * tm]
    e_ref[...] = e_out.astype(jnp.int32)
    g_ref[...] = g_out


def _peer_route(qp, sub_keys):
    m, kq = qp.shape
    tm = min(ROUTE_ROW_TILE, m)
    assert m % tm == 0 and tm % _ROUTE_ROWS == 0
    ne = PEER_HEADS * PEER_TOPK
    ospec = pl.BlockSpec((tm, ne), lambda i: (i, 0))
    pairs = [(a, b) for a in range(PEER_TOPK) for b in range(PEER_TOPK) if (a + 1) * (b + 1) <= PEER_TOPK]
    pad = [0] * (PEER_KEYS - len(pairs))
    pair_rows = jnp.array([[a for a, _ in pairs] + pad, [b for _, b in pairs] + pad,
                           [1] * len(pairs) + pad] + [[0] * PEER_KEYS] * 5, jnp.int32)
    return pl.pallas_call(
        _route_kernel,
        grid=(m // tm,),
        in_specs=[pl.BlockSpec((tm, kq), lambda i: (i, 0)),
                  pl.BlockSpec(sub_keys.shape, lambda i: (0, 0, 0, 0)),
                  pl.BlockSpec(pair_rows.shape, lambda i: (0, 0))],
        out_specs=[ospec, ospec],
        out_shape=[jax.ShapeDtypeStruct((m, ne), jnp.int32), jax.ShapeDtypeStruct((m, ne), F32)],
        scratch_shapes=[pltpu.VMEM((2 * PEER_HEADS * tm, PEER_KEYS), F32),
                        pltpu.VMEM((2 * PEER_HEADS * tm, PEER_KEYS), F32),
                        pltpu.VMEM((2 * PEER_HEADS * tm, PEER_KEYS), jnp.int32),
                        pltpu.VMEM((PEER_HEADS * tm, PEER_KEYS), F32),
                        pltpu.VMEM((PEER_HEADS * tm, PEER_KEYS), jnp.int32)],
        compiler_params=_params("parallel"),
    )(qp, sub_keys, pair_rows)


def _gelu_tanh(x):
    return 0.5 * x * (1.0 + jnp.tanh(0.7978845608028654 * (x + 0.044715 * (x * x * x))))


_TOKEN_GROUP = 16


def _expert_kernel(h_ref, gain_ref, e_ref, g_ref, tab_ref, next_gain_ref, o_ref, on_ref,
                   hi_ref, lo_ref, act_ref, d_ref, w3_ref, *, nj):
    j = pl.program_id(1)
    xn_ref = on_ref
    tm = h_ref.shape[0]
    te = tab_ref.shape[0]
    nb = te // LANES

    @pl.when(j == 0)
    def _():
        xn_ref[...] = _norm_rows(h_ref[...], gain_ref[...]).astype(BF16)
        o_ref[...] = h_ref[...]
        e = e_ref[...]
        hi_ref[...] = e // LANES
        lo_ref[...] = e % LANES
        act_ref[...] = jnp.zeros(act_ref.shape, F32)

    def score_slab(slot):
        d_ref[slot] = lax.dot_general(xn_ref[...], tab_ref[...], _NT, preferred_element_type=F32)

    def pick_from_slab(slot, slab):
        hi = hi_ref[...]
        lo = lo_ref[...]
        act = act_ref[...]
        for s in range(nb):
            picked = _take_lanes(d_ref[slot, :, s * LANES:(s + 1) * LANES], lo)
            act = jnp.where(hi == slab * nb + s, picked, act)
        act_ref[...] = act

    @pl.when(j == 0)
    def _():
        score_slab(0)

    @pl.when(jnp.logical_and(j >= 1, j < nj))
    def _():
        pick_from_slab((j - 1) % 2, j - 1)
        score_slab(j % 2)

    @pl.when(j == nj)
    def _():
        pick_from_slab((nj - 1) % 2, nj - 1)
        act_ref[...] = g_ref[...] * _gelu_tanh(act_ref[...])
        sub = lax.broadcasted_iota(jnp.int32, (LANES, LANES), 0)

        def group(gi, _):
            t0 = pl.multiple_of(gi * _TOKEN_GROUP, _TOKEN_GROUP)
            grids = []
            for u in range(_TOKEN_GROUP):
                hi_row = hi_ref[pl.ds(t0 + u, 1), :]
                lo_row = lo_ref[pl.ds(t0 + u, 1), :]
                w_row = act_ref[pl.ds(t0 + u, 1), :]
                a_t = jnp.where(hi_row == sub, w_row, 0.0).astype(BF16)
                b_t = jnp.where(lo_row == sub, 1.0, 0.0).astype(BF16)
                grids.append(lax.dot_general(a_t, b_t, _NT, preferred_element_type=F32))
            w3_ref[:, pl.ds(t0, _TOKEN_GROUP), :] = pltpu.einshape(
                "uik->iuk", jnp.stack(grids, axis=0)).astype(BF16)
            return 0

        lax.fori_loop(0, tm // _TOKEN_GROUP, group, 0, unroll=4)

    @pl.when(j >= nj)
    def _():
        jj = j - nj
        w = jnp.concatenate([w3_ref[jj * nb + s] for s in range(nb)], axis=1)
        d = o_ref.shape[1]
        for c0 in range(0, d, _EXPERT_OUT_COLS):
            cols = slice(c0, c0 + _EXPERT_OUT_COLS)
            o_ref[:, cols] += jnp.dot(w, tab_ref[:, cols], preferred_element_type=F32)

    @pl.when(j == 2 * nj - 1)
    def _():
        for r0 in range(0, tm, LANES):
            rows = slice(r0, min(r0 + LANES, tm))
            on_ref[rows, :] = _norm_rows(o_ref[rows, :], next_gain_ref[...]).astype(BF16)


_EXPERT_OUT_COLS = 1024


def _peer_experts(h, gain, eidx, gate, uv_tab, layer, next_gain, *, te=EXPERT_SLAB):
    m, d = h.shape
    ne = uv_tab.shape[2]
    nsel = eidx.shape[1]
    tm = min(ROW_TILE, m)
    nj = ne // te
    assert m % tm == 0 and ne % te == 0 and tm % _TOKEN_GROUP == 0
    once = pl.Buffered(1)
    sel_spec = pl.BlockSpec((tm, nsel), lambda i, j: (i, 0), pipeline_mode=once)
    return pl.pallas_call(
        functools.partial(_expert_kernel, nj=nj),
        grid=(m // tm, 2 * nj),
        in_specs=[pl.BlockSpec((tm, d), lambda i, j: (i, 0)),
                  pl.BlockSpec((1, d), lambda i, j: (0, 0), pipeline_mode=once),
                  sel_spec, sel_spec,
                  pl.BlockSpec((None, None, te, d), lambda i, j: (layer, j // nj, j % nj, 0)),
                  pl.BlockSpec((1, d), lambda i, j: (0, 0), pipeline_mode=once)],
        out_specs=[pl.BlockSpec((tm, d), lambda i, j: (i, 0), pipeline_mode=once),
                   pl.BlockSpec((tm, d), lambda i, j: (i, 0), pipeline_mode=once)],
        out_shape=[jax.ShapeDtypeStruct((m, d), F32), jax.ShapeDtypeStruct((m, d), BF16)],
        scratch_shapes=[pltpu.VMEM((tm, nsel), jnp.int32),
                        pltpu.VMEM((tm, nsel), jnp.int32),
                        pltpu.VMEM((tm, nsel), F32),
                        pltpu.VMEM((2, tm, te), F32),
                        pltpu.VMEM((ne // LANES, tm, LANES), BF16)],
        compiler_params=_params("parallel", "arbitrary"),
    )(h, gain.reshape(1, d), eidx, gate, uv_tab, next_gain.reshape(1, d))


def _peer_ffn(h, i, prm):
    qp = _matmul(h, prm["w_q_peer"][i], gain=prm["g_ffn"][i])
    eidx, gate = _peer_route(qp, prm["peer_sub_keys"][i])
    return _peer_experts(h, prm["g_ffn"][i], eidx, gate, prm["peer_uv"], i, prm["g_ple"][i])


def _ple(h, hn, p, i, prm):
    return _matmul(hn, prm["w_ple_gate"][i], residual=h, ple=(p, prm["w_ple_proj"][i]), tn=COL_TILE // 2)


def _trunk(x, pe, past, prm):
    b, t, d = x.shape
    m = b * t
    h = x.reshape(m, d)
    pe = pe.reshape(pe.shape[0], m, pe.shape[-1])

    q, k, v = _qkv_proj(h, prm["w_qkv_a"], prm["g_mix"][0])
    if past is None:
        o = _sb_prompt(q, k, v, batch=b, seq=t)
    else:
        o = _sb_sample(q, k, v, past[0][0], past[1][0], batch=b, tq=t)
    h = _matmul(o, prm["w_o_a"], residual=h)
    h, hn = _peer_ffn(h, 0, prm)
    h = _ple(h, hn, pe[0], 0, prm)

    kv = _matmul(h, prm["w_kv_b"], gain=prm["g_kv"])
    nkv = N_KV_B * HEAD_DIM
    kb_new = kv[:, :nkv].reshape(b, t, nkv)
    vb_new = kv[:, nkv:].reshape(b, t, nkv)

    qb = _matmul(h, prm["w_q_b"], gain=prm["g_mix"][1], out_dtype=BF16, out_scale=SCALE)
    win = (N_PREV_CHUNKS + 2) * CHUNK
    if past is None:
        front = (N_PREV_CHUNKS + 1) * CHUNK
        kpad = jnp.pad(kb_new.astype(BF16), ((0, 0), (front, 0), (0, 0)))
        vpad = jnp.pad(vb_new.astype(BF16), ((0, 0), (front, 0), (0, 0)))
        bias = _band_bias(prm["rel_bias_b"][0], CHUNK, win)
        ob = _band_attention(qb, kpad, vpad, bias, batch=b, nq=t // CHUNK, cq=CHUNK, win=win,
                             chunked=True, lo_static=0)
    else:
        ck = past[2].reshape(b, -1, nkv)
        cv = past[3].reshape(b, -1, nkv)
        front = win - ck.shape[1] - t
        kpad = jnp.pad(jnp.concatenate([ck, kb_new], axis=1).astype(BF16), ((0, 0), (front, 0), (0, 0)))
        vpad = jnp.pad(jnp.concatenate([cv, vb_new], axis=1).astype(BF16), ((0, 0), (front, 0), (0, 0)))
        bias = _band_bias(prm["rel_bias_b"][0], t, win)
        ob = _band_attention(qb, kpad, vpad, bias, batch=b, nq=1, cq=t, win=win,
                             chunked=False, lo_static=front)
    h = _matmul(ob, prm["w_o_b"], residual=h)
    h, hn = _peer_ffn(h, 1, prm)
    h = _ple(h, hn, pe[1], 1, prm)

    y = _rmsnorm(h, prm["g_final"]).reshape(b, t, d)
    heads = k.shape[1] // HEAD_DIM
    a_k = k.reshape(1, b, t, heads, HEAD_DIM)
    a_v = v.reshape(1, b, t, heads, HEAD_DIM)
    if past is None:
        keep = min(N_PREV_CHUNKS * CHUNK, t)
        b_k, b_v = kb_new[:, t - keep:], vb_new[:, t - keep:]
    else:
        b_k, b_v = kb_new, vb_new
    b_k = b_k.reshape(b, -1, N_KV_B, HEAD_DIM)
    b_v = b_v.reshape(b, -1, N_KV_B, HEAD_DIM)
    return y, a_k, a_v, b_k, b_v


def kernel(x_prompt, x_sample, cache_a_k, cache_a_v, cache_b_k, cache_b_v, p_prompt, p_sample, g_mix, w_qkv_a, w_o_a, g_kv, w_kv_b, w_q_b, rel_bias_b, w_o_b, g_ffn, w_q_peer, peer_sub_keys, peer_u, peer_v, g_ple, w_ple_gate, w_ple_proj, g_final):
    prm = dict(
        g_mix=g_mix, g_kv=g_kv, g_ffn=g_ffn, g_ple=g_ple, g_final=g_final,
        w_qkv_a=_to_bf16(w_qkv_a)[0],
        w_o_a=_to_bf16(w_o_a)[0],
        w_kv_b=_to_bf16(w_kv_b),
        w_q_b=_to_bf16(w_q_b)[0],
        w_o_b=_to_bf16(w_o_b)[0],
        rel_bias_b=rel_bias_b,
        w_q_peer=_to_bf16(w_q_peer),
        peer_sub_keys=peer_sub_keys,
        peer_uv=_peer_tables_bf16(peer_u, peer_v),
        w_ple_gate=_to_bf16(w_ple_gate),
        w_ple_proj=_to_bf16(w_ple_proj),
    )
    y_p, ak_p, av_p, bk_p, bv_p = _trunk(x_prompt, p_prompt, None, prm)
    y_s, ak_s, av_s, bk_s, bv_s = _trunk(x_sample, p_sample,
                                         (cache_a_k, cache_a_v, cache_b_k, cache_b_v), prm)
    return (y_p, y_s, ak_p, av_p, bk_p, bv_p, ak_s, av_s, bk_s, bv_s)
```

```python
import functools

import jax
import jax.numpy as jnp
import numpy as np
from jax import lax
from jax.experimental import pallas as pl
from jax.experimental.pallas import tpu as pltpu

F32 = jnp.float32
BF16 = jnp.bfloat16

EPS = 1e-6
HEAD_DIM = 128
CHUNK = 64
N_PREV_CHUNKS = 8
N_KV_B = 8
G_B = 4
REL_CLIP = 128
PEER_HEADS = 8
PEER_KEYS = 128
PEER_TOPK = 16
SCALE = HEAD_DIM ** -0.5
NEG_INF = float("-inf")

VMEM_LIMIT_BYTES = 60 * 1024 * 1024
LANES = 128

ROW_TILE = 512
COL_TILE = 1024
ROUTE_ROW_TILE = 256
EXPERT_SLAB = 512
SB_BLOCK = 256

_NT = (((1,), (1,)), ((), ()))


def _params(*sem):
    return pltpu.CompilerParams(dimension_semantics=sem, vmem_limit_bytes=VMEM_LIMIT_BYTES)


def _norm_rows(x, g):
    ms = jnp.mean(x * x, axis=-1, keepdims=True)
    return x * lax.rsqrt(ms + EPS) * g


def _sigmoid(x):
    return 1.0 / (1.0 + jnp.exp(-x))


def _normed_rows_pipeline(x_ref, g_ref, xn_ref, step):
    i = pl.program_id(0)
    j = pl.program_id(1)
    last = pl.num_programs(1) - 1
    slot = i % 2

    def normalise(dst):
        xn_ref[dst] = _norm_rows(x_ref[...], g_ref[...]).astype(BF16)

    @pl.when(jnp.logical_and(i == 0, j == 0))
    def _():
        normalise(0)

    @pl.when(j < last)
    def _():
        step(slot, lambda: None)

    @pl.when(j == last)
    def _():
        step(slot, lambda: normalise(1 - slot))


def _next_tile_rows(ni, nj):
    return lambda i, j: (jnp.minimum(i + jnp.where(j == nj - 1, 1, 0), ni - 1), 0)


def _mm_kernel(*refs, norm, pipelined, res, ple, out_scale):
    it = iter(refs)
    x_ref = next(it)
    g_ref = next(it) if norm else None
    w_ref = next(it)
    res_ref = next(it) if res else None
    p_ref = next(it) if ple else None
    wp_ref = next(it) if ple else None
    o_ref = next(it)
    xn_ref = next(it) if norm else None

    def step(x):
        acc = jnp.dot(x, w_ref[...], preferred_element_type=F32)
        if ple:
            proj = jnp.dot(p_ref[...].astype(BF16), wp_ref[...], preferred_element_type=F32)
            acc = _sigmoid(acc) * proj
        if res:
            acc = acc + res_ref[...]
        if out_scale is not None:
            acc = acc * out_scale
        o_ref[...] = acc.astype(o_ref.dtype)

    if pipelined:
        def pipelined_step(slot, after):
            step(xn_ref[slot])
            after()

        _normed_rows_pipeline(x_ref, g_ref, xn_ref, pipelined_step)
    elif norm:
        @pl.when(pl.program_id(1) == 0)
        def _():
            xn_ref[...] = _norm_rows(x_ref[...], g_ref[...]).astype(BF16)
        step(xn_ref[...])
    else:
        step(x_ref[...])


def _matmul(x, w, *, gain=None, residual=None, ple=None, out_dtype=F32, out_scale=None, tn=COL_TILE):
    m, k = x.shape
    n = w.shape[1]
    norm = gain is not None
    tm = min(ROW_TILE if norm else 2 * ROW_TILE, m)
    tn = min(tn, n)
    assert m % tm == 0 and n % tn == 0
    pipelined = norm and ple is None and n // tn >= 2
    x_map = _next_tile_rows(m // tm, n // tn) if pipelined else (lambda i, j: (i, 0))
    in_specs = [pl.BlockSpec((tm, k), x_map)]
    args = [x]
    if norm:
        in_specs.append(pl.BlockSpec((1, k), lambda i, j: (0, 0)))
        args.append(gain.reshape(1, k))
    in_specs.append(pl.BlockSpec((k, tn), lambda i, j: (0, j)))
    args.append(w)
    if residual is not None:
        in_specs.append(pl.BlockSpec((tm, tn), lambda i, j: (i, j)))
        args.append(residual)
    if ple is not None:
        p, wp = ple
        kp = p.shape[1]
        in_specs.append(pl.BlockSpec((tm, kp), lambda i, j: (i, 0)))
        in_specs.append(pl.BlockSpec((kp, tn), lambda i, j: (0, j)))
        args += [p, wp]
    return pl.pallas_call(
        functools.partial(_mm_kernel, norm=norm, pipelined=pipelined, res=residual is not None,
                          ple=ple is not None, out_scale=out_scale),
        grid=(m // tm, n // tn),
        in_specs=in_specs,
        out_specs=pl.BlockSpec((tm, tn), lambda i, j: (i, j)),
        out_shape=jax.ShapeDtypeStruct((m, n), out_dtype),
        scratch_shapes=([pltpu.VMEM((2, tm, k), BF16)] if pipelined else
                        [pltpu.VMEM((tm, k), BF16)] if norm else []),
        compiler_params=_params("arbitrary" if pipelined else "parallel", "arbitrary"),
    )(*args)


def _qkv_kernel(x_ref, g_ref, w_ref, q_ref, k_ref, v_ref, xn_ref, *, nq):
    j = pl.program_id(1)

    def step(slot, after):
        acc = jnp.dot(xn_ref[slot], w_ref[...], preferred_element_type=F32)
        after()

        @pl.when(j < nq)
        def _():
            q_ref[...] = (acc * SCALE).astype(q_ref.dtype)

        @pl.when(jnp.logical_and(j >= nq, j < 2 * nq))
        def _():
            k_ref[...] = acc

        @pl.when(j >= 2 * nq)
        def _():
            v_ref[...] = acc

    _normed_rows_pipeline(x_ref, g_ref, xn_ref, step)


def _qkv_proj(x, w, gain, *, tn=COL_TILE):
    m, k = x.shape
    n = w.shape[1] // 3
    tm = min(ROW_TILE, m)
    nq = n // tn
    assert m % tm == 0 and n % tn == 0

    def ospec(first):
        return pl.BlockSpec((tm, tn), lambda i, j: (i, jnp.clip(j - first, 0, nq - 1)))

    return pl.pallas_call(
        functools.partial(_qkv_kernel, nq=nq),
        grid=(m // tm, 3 * nq),
        in_specs=[pl.BlockSpec((tm, k), _next_tile_rows(m // tm, 3 * nq)),
                  pl.BlockSpec((1, k), lambda i, j: (0, 0)),
                  pl.BlockSpec((k, tn), lambda i, j: (0, j))],
        out_specs=[ospec(0), ospec(nq), ospec(2 * nq)],
        out_shape=[jax.ShapeDtypeStruct((m, n), BF16), jax.ShapeDtypeStruct((m, n), F32),
                   jax.ShapeDtypeStruct((m, n), F32)],
        scratch_shapes=[pltpu.VMEM((2, tm, k), BF16)],
        compiler_params=_params("arbitrary", "arbitrary"),
    )(x, gain.reshape(1, k), w)


def _kvq_kernel(x_ref, gk_ref, gq_ref, w_ref, kv_ref, qb_ref, xn_ref, *, nkv):
    i = pl.program_id(0)
    j = pl.program_id(1)
    last = pl.num_programs(1) - 1
    slot = i % 2

    def normalise(dst):
        for r0 in range(0, x_ref.shape[0], LANES):
            rows = slice(r0, min(r0 + LANES, x_ref.shape[0]))
            x = x_ref[rows, :]
            r = x * lax.rsqrt(jnp.mean(x * x, axis=-1, keepdims=True) + EPS)
            xn_ref[0, dst, rows, :] = (r * gk_ref[...]).astype(BF16)
            xn_ref[1, dst, rows, :] = (r * gq_ref[...]).astype(BF16)

    @pl.when(jnp.logical_and(i == 0, j == 0))
    def _():
        normalise(0)

    @pl.when(j < nkv)
    def _():
        kv_ref[...] = jnp.dot(xn_ref[0, slot], w_ref[...], preferred_element_type=F32)

    @pl.when(jnp.logical_and(j >= nkv, j < last))
    def _():
        qb_ref[...] = (jnp.dot(xn_ref[1, slot], w_ref[...], preferred_element_type=F32) * SCALE).astype(qb_ref.dtype)

    @pl.when(j == last)
    def _():
        acc = jnp.dot(xn_ref[1, slot], w_ref[...], preferred_element_type=F32)
        normalise(1 - slot)
        qb_ref[...] = (acc * SCALE).astype(qb_ref.dtype)


def _kvq_proj(x, w, gain_kv, gain_q, n_kv, *, tn=COL_TILE):
    m, k = x.shape
    n_q = w.shape[1] - n_kv
    tm = min(ROW_TILE, m)
    assert m % tm == 0 and n_kv % tn == 0 and n_q % tn == 0
    nkv, nqb = n_kv // tn, n_q // tn
    steps = nkv + nqb
    gspec = pl.BlockSpec((1, k), lambda i, j: (0, 0))
    return pl.pallas_call(
        functools.partial(_kvq_kernel, nkv=nkv),
        grid=(m // tm, steps),
        in_specs=[pl.BlockSpec((tm, k), _next_tile_rows(m // tm, steps)), gspec, gspec,
                  pl.BlockSpec((k, tn), lambda i, j: (0, j))],
        out_specs=[pl.BlockSpec((tm, tn), lambda i, j: (i, jnp.minimum(j, nkv - 1))),
                   pl.BlockSpec((tm, tn), lambda i, j: (i, jnp.clip(j - nkv, 0, nqb - 1)))],
        out_shape=[jax.ShapeDtypeStruct((m, n_kv), F32), jax.ShapeDtypeStruct((m, n_q), BF16)],
        scratch_shapes=[pltpu.VMEM((2, 2, tm, k), BF16)],
        compiler_params=_params("arbitrary", "arbitrary"),
    )(x, gain_kv.reshape(1, k), gain_q.reshape(1, k), w)


def _cast_kernel(x_ref, o_ref):
    o_ref[...] = x_ref[...].astype(o_ref.dtype)


_CAST_ROWS = 256
_CAST_COLS = 4096


def _to_bf16(x):
    x2 = x.reshape(-1, x.shape[-1])
    rows, cols = x2.shape
    tr = min(_CAST_ROWS, rows)
    tc = min(_CAST_COLS, cols)
    spec = pl.BlockSpec((tr, tc), lambda i, j: (i, j))
    out = pl.pallas_call(
        _cast_kernel,
        grid=(rows // tr, cols // tc),
        in_specs=[spec],
        out_specs=spec,
        out_shape=jax.ShapeDtypeStruct((rows, cols), BF16),
        compiler_params=_params("parallel", "parallel"),
    )(x2)
    return out.reshape(x.shape)


def _stack_cast_kernel(u_ref, v_ref, o_ref):
    c = pl.program_id(1)

    @pl.when(c == 0)
    def _():
        o_ref[...] = u_ref[...].astype(o_ref.dtype)

    @pl.when(c == 1)
    def _():
        o_ref[...] = v_ref[...].astype(o_ref.dtype)


def _peer_tables_bf16(u, v):
    nl, ne, d = u.shape
    te = _CAST_ROWS
    nblk = ne // te
    uspec = pl.BlockSpec((None, te, d), lambda l, c, e: (l, jnp.where(c == 0, e, nblk - 1), 0))
    vspec = pl.BlockSpec((None, te, d), lambda l, c, e: (l, jnp.where(c == 1, e, 0), 0))
    return pl.pallas_call(
        _stack_cast_kernel,
        grid=(nl, 2, nblk),
        in_specs=[uspec, vspec],
        out_specs=pl.BlockSpec((None, None, te, d), lambda l, c, e: (l, c, e, 0)),
        out_shape=jax.ShapeDtypeStruct((nl, 2, ne, d), BF16),
        compiler_params=_params("parallel", "arbitrary", "arbitrary"),
    )(u, v)


def _rmsnorm_kernel(x_ref, g_ref, o_ref):
    o_ref[...] = _norm_rows(x_ref[...], g_ref[...])


def _rmsnorm(x, gain):
    m, k = x.shape
    tm = min(ROW_TILE, m)
    return pl.pallas_call(
        _rmsnorm_kernel,
        grid=(m // tm,),
        in_specs=[pl.BlockSpec((tm, k), lambda i: (i, 0)), pl.BlockSpec((1, k), lambda i: (0, 0))],
        out_specs=pl.BlockSpec((tm, k), lambda i: (i, 0)),
        out_shape=jax.ShapeDtypeStruct((m, k), F32),
        compiler_params=_params("parallel"),
    )(x, gain.reshape(1, k))


def _strict_lower_ones(n):
    r = lax.broadcasted_iota(jnp.int32, (2 * n, n), 0) % n
    c = lax.broadcasted_iota(jnp.int32, (2 * n, n), 1)
    return jnp.where(r > c, 1.0, 0.0).astype(BF16)


_SB_DEAD_BELOW = -104.0


def _sb_log_terms(q, kb, mask):
    z = lax.dot_general(q, kb, _NT, preferred_element_type=F32)
    nz = -z
    sp = jnp.log(1.0 + jnp.exp(-jnp.maximum(z, nz)))
    log_beta = jnp.minimum(z, 0.0) - sp
    log_keep = jnp.minimum(nz, 0.0) - sp
    if mask is not None:
        log_keep = jnp.where(mask, log_keep, 0.0)
    return log_beta, log_keep


def _sb_block(q, kb, vb, c, acc, tri2, mask):
    log_beta, log_keep = _sb_log_terms(q, kb, mask)
    hi = log_keep.astype(BF16)
    lo = (log_keep - hi.astype(F32)).astype(BF16)
    after = jnp.dot(jnp.concatenate([hi, lo], axis=1), tri2, preferred_element_type=F32)
    w = jnp.exp(log_beta + after + c)
    if mask is not None:
        w = jnp.where(mask, w, 0.0)
    acc = acc + jnp.dot(w.astype(BF16), vb, preferred_element_type=F32)
    c = c + jnp.sum(log_keep, axis=1, keepdims=True)
    return c, acc


_SB_QUERY_GROUP = 16


def _sb_prompt_kernel(q_ref, k_ref, v_ref, o_ref, kb_ref, vb_ref, *, blk, group):
    seq = q_ref.shape[0]
    kb_ref[pl.ds(0, blk), :] = jnp.zeros((blk, HEAD_DIM), BF16)
    vb_ref[pl.ds(0, blk), :] = jnp.zeros((blk, HEAD_DIM), BF16)
    kb_ref[pl.ds(blk, seq), :] = k_ref[...].astype(BF16)
    vb_ref[pl.ds(blk, seq), :] = v_ref[...].astype(BF16)
    tri = _strict_lower_ones(blk)
    row = lax.broadcasted_iota(jnp.int32, (blk, blk), 0)
    col = lax.broadcasted_iota(jnp.int32, (blk, blk), 1)
    diag_mask = col < row

    def kv_block(j):
        r0 = pl.multiple_of((j + 1) * blk, blk)
        return kb_ref[pl.ds(r0, blk), :], vb_ref[pl.ds(r0, blk), :]

    def older_blocks(i, q, c, acc):
        def k_cond(carry):
            j, cmax, _, _ = carry
            return jnp.logical_and(j >= 0, cmax > _SB_DEAD_BELOW)

        def k_body(carry):
            j, _, c, acc = carry
            c, acc = _sb_block(q, *kv_block(j), c, acc, tri, None)
            return j - 1, jnp.max(c), c, acc

        return lax.while_loop(k_cond, k_body, (i - 2, jnp.max(c), c, acc))[3]

    def group_body(p, _):
        state = []
        for i in [group * p + u for u in range(group)]:
            q = q_ref[pl.ds(pl.multiple_of(i * blk, blk), blk), :]
            c = jnp.zeros((blk, 1), F32)
            acc = jnp.zeros((blk, HEAD_DIM), F32)
            c, acc = _sb_block(q, *kv_block(i), c, acc, tri, diag_mask)
            c, acc = _sb_block(q, *kv_block(i - 1), c, acc, tri, None)
            state.append((i, q, c, acc))
        for i, q, c, acc in state:
            acc = older_blocks(i, q, c, acc)
            o_ref[pl.ds(pl.multiple_of(i * blk, blk), blk), :] = acc.astype(o_ref.dtype)
        return 0

    lax.fori_loop(0, seq // (group * blk), group_body, 0)


def _sb_prompt(q, k, v, *, batch, seq, blk=SB_BLOCK):
    t, hd = q.shape
    heads = hd // HEAD_DIM
    blk = min(blk, seq)
    group = min(_SB_QUERY_GROUP, seq // blk)
    assert seq % (group * blk) == 0 and t == batch * seq
    spec = pl.BlockSpec((seq, HEAD_DIM), lambda b, h: (b, h))
    return pl.pallas_call(
        functools.partial(_sb_prompt_kernel, blk=blk, group=group),
        grid=(batch, heads),
        in_specs=[spec, spec, spec],
        out_specs=spec,
        out_shape=jax.ShapeDtypeStruct((t, hd), BF16),
        scratch_shapes=[pltpu.VMEM((seq + blk, HEAD_DIM), BF16), pltpu.VMEM((seq + blk, HEAD_DIM), BF16)],
        compiler_params=_params("parallel", "parallel"),
    )(q, k, v)


_SB_HEAD_GROUP = 32


def _sb_probe_kernel(q_ref, kn_ref, kc_ref, alive_ref, *, blk):
    tq = q_ref.shape[0]
    row = lax.broadcasted_iota(jnp.int32, (tq, blk), 0)
    col = lax.broadcasted_iota(jnp.int32, (tq, blk), 1)
    kb = pltpu.einshape("thd->htd", kc_ref[...]).astype(BF16)
    cmax = jnp.full((tq, 1), NEG_INF, F32)
    for g in range(_SB_HEAD_GROUP):
        lanes = slice(g * HEAD_DIM, (g + 1) * HEAD_DIM)
        q = q_ref[:, lanes]
        c = jnp.sum(_sb_log_terms(q, kn_ref[:, lanes], col < row)[1], axis=1, keepdims=True)
        c = c + jnp.sum(_sb_log_terms(q, kb[g], None)[1], axis=1, keepdims=True)
        cmax = jnp.maximum(cmax, c)
    flag = jnp.where(jnp.max(cmax) > _SB_DEAD_BELOW - 1.0, 1, 0)
    alive_ref[...] = jnp.full(alive_ref.shape, flag, jnp.int32)


def _sb_sample_kernel(alive_ref, q_ref, kn_ref, vn_ref, kc_ref, vc_ref, o_ref, c_ref, acc_ref, *, blk):
    del alive_ref
    n = pl.program_id(2)
    tq = q_ref.shape[0]
    tri = _strict_lower_ones(blk)

    @pl.when(n == 0)
    def _():
        row = lax.broadcasted_iota(jnp.int32, (tq, blk), 0)
        col = lax.broadcasted_iota(jnp.int32, (tq, blk), 1)
        for g in range(_SB_HEAD_GROUP):
            lanes = slice(g * HEAD_DIM, (g + 1) * HEAD_DIM)
            c, acc = _sb_block(q_ref[:, lanes], kn_ref[:, lanes], vn_ref[:, lanes],
                               jnp.zeros((tq, 1), F32), jnp.zeros((tq, HEAD_DIM), F32), tri, col < row)
            c_ref[g] = c
            acc_ref[g] = acc

    @pl.when(jnp.max(c_ref[...]) > _SB_DEAD_BELOW)
    def _():
        kb = pltpu.einshape("thd->htd", kc_ref[...]).astype(BF16)
        vb = pltpu.einshape("thd->htd", vc_ref[...]).astype(BF16)
        for g in range(_SB_HEAD_GROUP):
            lanes = slice(g * HEAD_DIM, (g + 1) * HEAD_DIM)
            c, acc = _sb_block(q_ref[:, lanes], kb[g], vb[g], c_ref[g], acc_ref[g], tri, None)
            c_ref[g] = c
            acc_ref[g] = acc

    @pl.when(n == pl.num_programs(2) - 1)
    def _():
        for g in range(_SB_HEAD_GROUP):
            o_ref[:, g * HEAD_DIM:(g + 1) * HEAD_DIM] = acc_ref[g].astype(o_ref.dtype)


def _sb_sample(q, k_new, v_new, cache_k, cache_v, *, batch, tq, blk=SB_BLOCK):
    t, hd = q.shape
    heads = hd // HEAD_DIM
    past = cache_k.shape[1]
    nblk = past // blk
    assert past % blk == 0 and tq <= blk
    gw = _SB_HEAD_GROUP * HEAD_DIM

    def pad_block(a):
        a = a.reshape(batch, tq, hd)
        a = jnp.pad(a, ((0, 0), (0, blk - tq), (0, 0)))
        return a.reshape(batch * blk, hd).astype(BF16)

    assert heads == _SB_HEAD_GROUP
    kn = pad_block(k_new)
    vn = pad_block(v_new)
    newest = (None, blk, _SB_HEAD_GROUP, HEAD_DIM)
    alive = pl.pallas_call(
        functools.partial(_sb_probe_kernel, blk=blk),
        grid=(batch,),
        in_specs=[pl.BlockSpec((tq, gw), lambda b: (b, 0)),
                  pl.BlockSpec((blk, gw), lambda b: (b, 0)),
                  pl.BlockSpec(newest, lambda b: (b, nblk - 1, 0, 0))],
        out_specs=pl.BlockSpec((None, 8, LANES), lambda b: (b, 0, 0)),
        out_shape=jax.ShapeDtypeStruct((batch, 8, LANES), jnp.int32),
        compiler_params=_params("parallel"),
    )(q, kn, cache_k)[:, 0, 0]

    def cache_block(b, h, n, alive_ref):
        return (b, nblk - 1 - jnp.where(alive_ref[b] > 0, n, 0), h, 0)

    qspec = pl.BlockSpec((tq, gw), lambda b, h, n, alive_ref: (b, h))
    nspec = pl.BlockSpec((blk, gw), lambda b, h, n, alive_ref: (b, h))
    cspec = pl.BlockSpec(newest, cache_block)
    return pl.pallas_call(
        functools.partial(_sb_sample_kernel, blk=blk),
        grid_spec=pltpu.PrefetchScalarGridSpec(
            num_scalar_prefetch=1,
            grid=(batch, heads // _SB_HEAD_GROUP, nblk),
            in_specs=[qspec, nspec, nspec, cspec, cspec],
            out_specs=qspec,
            scratch_shapes=[pltpu.VMEM((_SB_HEAD_GROUP, tq, 1), F32),
                            pltpu.VMEM((_SB_HEAD_GROUP, tq, HEAD_DIM), F32)]),
        out_shape=jax.ShapeDtypeStruct((t, hd), BF16),
        compiler_params=_params("parallel", "parallel", "arbitrary"),
    )(alive, q, kn, vn, cache_k, cache_v)


_BAND_UNROLL = 16


def _band_kernel(q_ref, k_ref, v_ref, bias_ref, o_ref, *, cq, win, nq, chunked, lo_static):
    bias = bias_ref[...].reshape(G_B * cq, win)
    col = lax.broadcasted_iota(jnp.int32, (G_B * cq, win), 1)

    def chunk(c, _):
        r0 = pl.multiple_of(c * cq, cq)
        if chunked:
            lo = jnp.maximum(CHUNK, (N_PREV_CHUNKS + 1 - c) * CHUNK)
        else:
            lo = lo_static
        kb = k_ref[pl.ds(r0, win), :]
        vb = v_ref[pl.ds(r0, win), :]
        qb = q_ref[pl.ds(r0, cq), :]
        q4 = jnp.concatenate([qb[:, g * HEAD_DIM:(g + 1) * HEAD_DIM] for g in range(G_B)], axis=0)
        s = lax.dot_general(q4, kb, _NT, preferred_element_type=F32) + bias
        s = jnp.where(col >= lo, s, NEG_INF)
        m = jnp.max(s, axis=1, keepdims=True)
        e = jnp.exp(s - m)
        o = jnp.dot(e.astype(BF16), vb, preferred_element_type=F32) * (1.0 / jnp.sum(e, axis=1, keepdims=True))
        for g in range(G_B):
            o_ref[pl.ds(r0, cq), g * HEAD_DIM:(g + 1) * HEAD_DIM] = o[g * cq:(g + 1) * cq].astype(o_ref.dtype)
        return 0

    lax.fori_loop(0, nq, chunk, 0, unroll=_BAND_UNROLL if nq % _BAND_UNROLL == 0 else 1)


def _band_bias(rel_table, cq, win):
    lg = win + cq - 1
    idx = np.clip(win - 1 - np.arange(lg), -REL_CLIP, REL_CLIP) + REL_CLIP
    g = rel_table[idx].astype(F32).T
    rows = jnp.tile(g, (1, cq + 1))[:, :cq * (lg + 1)].reshape(-1, cq, lg + 1)
    return rows[:, ::-1, :win]


def _band_attention(q, kpad, vpad, bias, *, batch, nq, cq, win, chunked, lo_static):
    t, hd = q.shape
    rows = kpad.shape[1]
    qw = G_B * HEAD_DIM
    qspec = pl.BlockSpec((nq * cq, qw), lambda b, h: (b, h))
    kspec = pl.BlockSpec((None, rows, HEAD_DIM), lambda b, h: (b, 0, h))
    bspec = pl.BlockSpec((G_B, cq, win), lambda b, h: (h, 0, 0))
    return pl.pallas_call(
        functools.partial(_band_kernel, cq=cq, win=win, nq=nq, chunked=chunked, lo_static=lo_static),
        grid=(batch, N_KV_B),
        in_specs=[qspec, kspec, kspec, bspec],
        out_specs=qspec,
        out_shape=jax.ShapeDtypeStruct((t, hd), BF16),
        compiler_params=_params("parallel", "parallel"),
    )(q, kpad, vpad, bias)


def _split3_dot_nt(a, b):
    a_hi = a.astype(BF16)
    a_lo = (a - a_hi.astype(F32)).astype(BF16)
    b_hi = b.astype(BF16)
    b_lo = (b - b_hi.astype(F32)).astype(BF16)
    return lax.dot_general(jnp.concatenate([a_hi, a_lo, a_hi], axis=1),
                           jnp.concatenate([b_hi, b_hi, b_lo], axis=1), _NT, preferred_element_type=F32)


def _take_lanes(x, idx):
    return jnp.take_along_axis(x, idx, axis=1, mode="promise_in_bounds")


_ROUTE_ROWS = 64
_ROUTE_UNROLL = 64


def _route_kernel(q_ref, sk_ref, pairs_ref, e_ref, g_ref, s_ref, left_ref, si_ref, cleft_ref, fpos_ref):
    tm = q_ref.shape[0]
    dh = sk_ref.shape[-1]
    nset = 2 * PEER_HEADS
    lane_c = lax.broadcasted_iota(jnp.int32, (_ROUTE_ROWS, PEER_KEYS), 1)

    for n in range(nset):
        sc = _split3_dot_nt(q_ref[:, n * dh:(n + 1) * dh], sk_ref[n // 2, n % 2])
        s_ref[n * tm:(n + 1) * tm, :] = sc
        left_ref[n * tm:(n + 1) * tm, :] = sc
    si_ref[...] = jnp.zeros(si_ref.shape, jnp.int32)

    def body1(i, _):
        def piece(r, _):
            rows = pl.ds(pl.multiple_of(r * _ROUTE_ROWS, _ROUTE_ROWS), _ROUTE_ROWS)
            left = left_ref[rows, :]
            pos = jnp.argmax(left, axis=1, keepdims=True).astype(jnp.int32)
            left_ref[rows, :] = jnp.where(lane_c == pos, NEG_INF, left)
            si_ref[rows, :] = jnp.where(lane_c == i, pos, si_ref[rows, :])
            return 0

        return lax.fori_loop(0, nset * tm // _ROUTE_ROWS, piece, 0, unroll=_ROUTE_UNROLL)

    lax.fori_loop(0, PEER_TOPK, body1, 0)
    si = si_ref[...]
    sv = _take_lanes(s_ref[...], si)
    si = si.astype(F32)

    pair_a = jnp.broadcast_to(pairs_ref[0:1, :], (tm, PEER_KEYS))
    pair_b = jnp.broadcast_to(pairs_ref[1:2, :], (tm, PEER_KEYS))
    pair_ok = jnp.broadcast_to(pairs_ref[2:3, :], (tm, PEER_KEYS)) > 0
    cand, cidx = [], []
    for h in range(PEER_HEADS):
        r0 = slice((2 * h) * tm, (2 * h + 1) * tm)
        r1 = slice((2 * h + 1) * tm, (2 * h + 2) * tm)
        cand.append(jnp.where(pair_ok, _take_lanes(sv[r0], pair_a) + _take_lanes(sv[r1], pair_b), NEG_INF))
        cidx.append(_take_lanes(si[r0], pair_a) * float(PEER_KEYS) + _take_lanes(si[r1], pair_b))
    cand = jnp.concatenate(cand, axis=0)
    cidx = jnp.concatenate(cidx, axis=0)

    lane2 = lax.broadcasted_iota(jnp.int32, cand.shape, 1)
    head_lane0 = (lax.broadcasted_iota(jnp.int32, cand.shape, 0) // tm) * PEER_TOPK
    cleft_ref[...] = cand
    fpos_ref[...] = jnp.zeros(fpos_ref.shape, jnp.int32)

    def body2(i, _):
        def piece(r, _):
            row0 = pl.multiple_of(r * _ROUTE_ROWS, _ROUTE_ROWS)
            rows = pl.ds(row0, _ROUTE_ROWS)
            left = cleft_ref[rows, :]
            pos = jnp.argmax(left, axis=1, keepdims=True).astype(jnp.int32)
            cleft_ref[rows, :] = jnp.where(lane_c == pos, NEG_INF, left)
            out_lane = (row0 // tm) * PEER_TOPK + i
            fpos_ref[rows, :] = jnp.where(lane_c == out_lane, pos, fpos_ref[rows, :])
            return 0

        return lax.fori_loop(0, PEER_HEADS * tm // _ROUTE_ROWS, piece, 0, unroll=_ROUTE_UNROLL)

    lax.fori_loop(0, PEER_TOPK, body2, 0)
    fpos = fpos_ref[...]
    fv = _take_lanes(cand, fpos)
    fe = _take_lanes(cidx, fpos)

    grp = lane2 // PEER_TOPK == head_lane0 // PEER_TOPK
    fe = jnp.where(grp, fe, 0.0)
    mx = jnp.max(jnp.where(grp, fv, NEG_INF), axis=1, keepdims=True)
    ex = jnp.where(grp, jnp.exp(fv - mx), 0.0)
    gate = ex / jnp.sum(ex, axis=1, keepdims=True)
    e_out = fe[0:tm]
    g_out = gate[0:tm]
    for h in range(1, PEER_HEADS):
        e_out = e_out + fe[h * tm:(h + 1) * tm]
        g_out = g_out + gate[h * tm:(h + 1) * tm]
    e_ref[...] = e_out.astype(jnp.int32)
    g_ref[...] = g_out


def _peer_route(qp, sub_keys):
    m, kq = qp.shape
    tm = min(ROUTE_ROW_TILE, m)
    assert m % tm == 0 and tm % _ROUTE_ROWS == 0
    ne = PEER_HEADS * PEER_TOPK
    ospec = pl.BlockSpec((tm, ne), lambda i: (i, 0))
    pairs = [(a, b) for a in range(PEER_TOPK) for b in range(PEER_TOPK) if (a + 1) * (b + 1) <= PEER_TOPK]
    pad = [0] * (PEER_KEYS - len(pairs))
    pair_rows = jnp.array([[a for a, _ in pairs] + pad, [b for _, b in pairs] + pad,
                           [1] * len(pairs) + pad] + [[0] * PEER_KEYS] * 5, jnp.int32)
    return pl.pallas_call(
        _route_kernel,
        grid=(m // tm,),
        in_specs=[pl.BlockSpec((tm, kq), lambda i: (i, 0)),
                  pl.BlockSpec(sub_keys.shape, lambda i: (0, 0, 0, 0)),
                  pl.BlockSpec(pair_rows.shape, lambda i: (0, 0))],
        out_specs=[ospec, ospec],
        out_shape=[jax.ShapeDtypeStruct((m, ne), jnp.int32), jax.ShapeDtypeStruct((m, ne), F32)],
        scratch_shapes=[pltpu.VMEM((2 * PEER_HEADS * tm, PEER_KEYS), F32),
                        pltpu.VMEM((2 * PEER_HEADS * tm, PEER_KEYS), F32),
                        pltpu.VMEM((2 * PEER_HEADS * tm, PEER_KEYS), jnp.int32),
                        pltpu.VMEM((PEER_HEADS * tm, PEER_KEYS), F32),
                        pltpu.VMEM((PEER_HEADS * tm, PEER_KEYS), jnp.int32)],
        compiler_params=_params("parallel"),
    )(qp, sub_keys, pair_rows)


def _gelu_tanh(x):
    return 0.5 * x * (1.0 + jnp.tanh(0.7978845608028654 * (x + 0.044715 * (x * x * x))))


_TOKEN_GROUP = 16


def _expert_kernel(h_ref, gain_ref, e_ref, g_ref, tab_ref, next_gain_ref, o_ref, on_ref,
                   hi_ref, lo_ref, act_ref, d_ref, w3_ref, *, nj):
    j = pl.program_id(1)
    xn_ref = on_ref
    tm = h_ref.shape[0]
    te = tab_ref.shape[0]
    nb = te // LANES

    @pl.when(j == 0)
    def _():
        xn_ref[...] = _norm_rows(h_ref[...], gain_ref[...]).astype(BF16)
        o_ref[...] = h_ref[...]
        e = e_ref[...]
        hi_ref[...] = e // LANES
        lo_ref[...] = e % LANES
        act_ref[...] = jnp.zeros(act_ref.shape, F32)

    def score_slab(slot):
        d_ref[slot] = lax.dot_general(xn_ref[...], tab_ref[...], _NT, preferred_element_type=F32)

    def pick_from_slab(slot, slab):
        hi = hi_ref[...]
        lo = lo_ref[...]
        act = act_ref[...]
        for s in range(nb):
            picked = _take_lanes(d_ref[slot, :, s * LANES:(s + 1) * LANES], lo)
            act = jnp.where(hi == slab * nb + s, picked, act)
        act_ref[...] = act

    @pl.when(j == 0)
    def _():
        score_slab(0)

    @pl.when(jnp.logical_and(j >= 1, j < nj))
    def _():
        pick_from_slab((j - 1) % 2, j - 1)
        score_slab(j % 2)

    @pl.when(j == nj)
    def _():
        pick_from_slab((nj - 1) % 2, nj - 1)
        act_ref[...] = g_ref[...] * _gelu_tanh(act_ref[...])
        sub = lax.broadcasted_iota(jnp.int32, (LANES, LANES), 0)

        def group(gi, _):
            t0 = pl.multiple_of(gi * _TOKEN_GROUP, _TOKEN_GROUP)
            grids = []
            for u in range(_TOKEN_GROUP):
                hi_row = hi_ref[pl.ds(t0 + u, 1), :]
                lo_row = lo_ref[pl.ds(t0 + u, 1), :]
                w_row = act_ref[pl.ds(t0 + u, 1), :]
                a_t = jnp.where(hi_row == sub, w_row, 0.0).astype(BF16)
                b_t = jnp.where(lo_row == sub, 1.0, 0.0).astype(BF16)
                grids.append(lax.dot_general(a_t, b_t, _NT, preferred_element_type=F32))
            w3_ref[:, pl.ds(t0, _TOKEN_GROUP), :] = pltpu.einshape(
                "uik->iuk", jnp.stack(grids, axis=0)).astype(BF16)
            return 0

        lax.fori_loop(0, tm // _TOKEN_GROUP, group, 0, unroll=4)

    @pl.when(j >= nj)
    def _():
        jj = j - nj
        w = jnp.concatenate([w3_ref[jj * nb + s] for s in range(nb)], axis=1)
        d = o_ref.shape[1]
        for c0 in range(0, d, _EXPERT_OUT_COLS):
            cols = slice(c0, c0 + _EXPERT_OUT_COLS)
            o_ref[:, cols] += jnp.dot(w, tab_ref[:, cols], preferred_element_type=F32)

    @pl.when(j == 2 * nj - 1)
    def _():
        for r0 in range(0, tm, LANES):
            rows = slice(r0, min(r0 + LANES, tm))
            on_ref[rows, :] = _norm_rows(o_ref[rows, :], next_gain_ref[...]).astype(BF16)


_EXPERT_OUT_COLS = 1024


def _peer_experts(h, gain, eidx, gate, uv_tab, layer, next_gain, *, te=EXPERT_SLAB):
    m, d = h.shape
    ne = uv_tab.shape[2]
    nsel = eidx.shape[1]
    tm = min(ROW_TILE, m)
    nj = ne // te
    assert m % tm == 0 and ne % te == 0 and tm % _TOKEN_GROUP == 0
    once = pl.Buffered(1)
    sel_spec = pl.BlockSpec((tm, nsel), lambda i, j: (i, 0), pipeline_mode=once)
    return pl.pallas_call(
        functools.partial(_expert_kernel, nj=nj),
        grid=(m // tm, 2 * nj),
        in_specs=[pl.BlockSpec((tm, d), lambda i, j: (i, 0)),
                  pl.BlockSpec((1, d), lambda i, j: (0, 0), pipeline_mode=once),
                  sel_spec, sel_spec,
                  pl.BlockSpec((None, None, te, d), lambda i, j: (layer, j // nj, j % nj, 0)),
                  pl.BlockSpec((1, d), lambda i, j: (0, 0), pipeline_mode=once)],
        out_specs=[pl.BlockSpec((tm, d), lambda i, j: (i, 0), pipeline_mode=once),
                   pl.BlockSpec((tm, d), lambda i, j: (i, 0), pipeline_mode=once)],
        out_shape=[jax.ShapeDtypeStruct((m, d), F32), jax.ShapeDtypeStruct((m, d), BF16)],
        scratch_shapes=[pltpu.VMEM((tm, nsel), jnp.int32),
                        pltpu.VMEM((tm, nsel), jnp.int32),
                        pltpu.VMEM((tm, nsel), F32),
                        pltpu.VMEM((2, tm, te), F32),
                        pltpu.VMEM((ne // LANES, tm, LANES), BF16)],
        compiler_params=_params("parallel", "arbitrary"),
    )(h, gain.reshape(1, d), eidx, gate, uv_tab, next_gain.reshape(1, d))


def _peer_ffn(h, i, prm):
    qp = _matmul(h, prm["w_q_peer"][i], gain=prm["g_ffn"][i])
    eidx, gate = _peer_route(qp, prm["peer_sub_keys"][i])
    return _peer_experts(h, prm["g_ffn"][i], eidx, gate, prm["peer_uv"], i, prm["g_ple"][i])


def _ple(h, hn, p, i, prm):
    return _matmul(hn, prm["w_ple_gate"][i], residual=h, ple=(p, prm["w_ple_proj"][i]), tn=COL_TILE // 2)


def _trunk(x, pe, past, prm):
    b, t, d = x.shape
    m = b * t
    h = x.reshape(m, d)
    pe = pe.reshape(pe.shape[0], m, pe.shape[-1])

    q, k, v = _qkv_proj(h, prm["w_qkv_a"], prm["g_mix"][0])
    if past is None:
        o = _sb_prompt(q, k, v, batch=b, seq=t)
    else:
        o = _sb_sample(q, k, v, past[0][0], past[1][0], batch=b, tq=t)
    h = _matmul(o, prm["w_o_a"], residual=h)
    h, hn = _peer_ffn(h, 0, prm)
    h = _ple(h, hn, pe[0], 0, prm)

    nkv = N_KV_B * HEAD_DIM
    kv, qb = _kvq_proj(h, prm["w_kvq_b"], prm["g_kv"], prm["g_mix"][1], 2 * nkv)
    kb_new = kv[:, :nkv].reshape(b, t, nkv)
    vb_new = kv[:, nkv:].reshape(b, t, nkv)

    win = (N_PREV_CHUNKS + 2) * CHUNK
    if past is None:
        front = (N_PREV_CHUNKS + 1) * CHUNK
        kpad = jnp.pad(kb_new.astype(BF16), ((0, 0), (front, 0), (0, 0)))
        vpad = jnp.pad(vb_new.astype(BF16), ((0, 0), (front, 0), (0, 0)))
        bias = _band_bias(prm["rel_bias_b"][0], CHUNK, win)
        ob = _band_attention(qb, kpad, vpad, bias, batch=b, nq=t // CHUNK, cq=CHUNK, win=win,
                             chunked=True, lo_static=0)
    else:
        ck = past[2].reshape(b, -1, nkv)
        cv = past[3].reshape(b, -1, nkv)
        front = win - ck.shape[1] - t
        kpad = jnp.pad(jnp.concatenate([ck, kb_new], axis=1).astype(BF16), ((0, 0), (front, 0), (0, 0)))
        vpad = jnp.pad(jnp.concatenate([cv, vb_new], axis=1).astype(BF16), ((0, 0), (front, 0), (0, 0)))
        bias = _band_bias(prm["rel_bias_b"][0], t, win)
        ob = _band_attention(qb, kpad, vpad, bias, batch=b, nq=1, cq=t, win=win,
                             chunked=False, lo_static=front)
    h = _matmul(ob, prm["w_o_b"], residual=h)
    h, hn = _peer_ffn(h, 1, prm)
    h = _ple(h, hn, pe[1], 1, prm)

    y = _rmsnorm(h, prm["g_final"]).reshape(b, t, d)
    heads = k.shape[1] // HEAD_DIM
    a_k = k.reshape(1, b, t, heads, HEAD_DIM)
    a_v = v.reshape(1, b, t, heads, HEAD_DIM)
    if past is None:
        keep = min(N_PREV_CHUNKS * CHUNK, t)
        b_k, b_v = kb_new[:, t - keep:], vb_new[:, t - keep:]
    else:
        b_k, b_v = kb_new, vb_new
    b_k = b_k.reshape(b, -1, N_KV_B, HEAD_DIM)
    b_v = b_v.reshape(b, -1, N_KV_B, HEAD_DIM)
    return y, a_k, a_v, b_k, b_v


def kernel(x_prompt, x_sample, cache_a_k, cache_a_v, cache_b_k, cache_b_v, p_prompt, p_sample, g_mix, w_qkv_a, w_o_a, g_kv, w_kv_b, w_q_b, rel_bias_b, w_o_b, g_ffn, w_q_peer, peer_sub_keys, peer_u, peer_v, g_ple, w_ple_gate, w_ple_proj, g_final):
    prm = dict(
        g_mix=g_mix, g_kv=g_kv, g_ffn=g_ffn, g_ple=g_ple, g_final=g_final,
        w_qkv_a=_to_bf16(w_qkv_a)[0],
        w_o_a=_to_bf16(w_o_a)[0],
        w_kvq_b=jnp.concatenate([_to_bf16(w_kv_b), _to_bf16(w_q_b)[0]], axis=1),
        w_o_b=_to_bf16(w_o_b)[0],
        rel_bias_b=rel_bias_b,
        w_q_peer=_to_bf16(w_q_peer),
        peer_sub_keys=peer_sub_keys,
        peer_uv=_peer_tables_bf16(peer_u, peer_v),
        w_ple_gate=_to_bf16(w_ple_gate),
        w_ple_proj=_to_bf16(w_ple_proj),
    )
    y_p, ak_p, av_p, bk_p, bv_p = _trunk(x_prompt, p_prompt, None, prm)
    y_s, ak_s, av_s, bk_s, bv_s = _trunk(x_sample, p_sample,
                                         (cache_a_k, cache_a_v, cache_b_k, cache_b_v), prm)
    return (y_p, y_s, ak_p, av_p, bk_p, bv_p, ak_s, av_s, bk_s, bv_s)
```

```python
import functools

import jax
import jax.numpy as jnp
import numpy as np
from jax import lax
from jax.experimental import pallas as pl
from jax.experimental.pallas import tpu as pltpu

F32 = jnp.float32
BF16 = jnp.bfloat16

EPS = 1e-6
HEAD_DIM = 128
CHUNK = 64
N_PREV_CHUNKS = 8
N_KV_B = 8
G_B = 4
REL_CLIP = 128
PEER_HEADS = 8
PEER_KEYS = 128
PEER_TOPK = 16
SCALE = HEAD_DIM ** -0.5
NEG_INF = float("-inf")

VMEM_LIMIT_BYTES = 60 * 1024 * 1024
LANES = 128

ROW_TILE = 512
COL_TILE = 1024
ROUTE_ROW_TILE = 256
EXPERT_SLAB = 512
SB_BLOCK = 256

_NT = (((1,), (1,)), ((), ()))


def _params(*sem):
    return pltpu.CompilerParams(dimension_semantics=sem, vmem_limit_bytes=VMEM_LIMIT_BYTES)


def _norm_rows(x, g):
    ms = jnp.mean(x * x, axis=-1, keepdims=True)
    return x * lax.rsqrt(ms + EPS) * g


def _sigmoid(x):
    return 1.0 / (1.0 + jnp.exp(-x))


def _normed_rows_pipeline(x_ref, g_ref, xn_ref, step):
    i = pl.program_id(0)
    j = pl.program_id(1)
    last = pl.num_programs(1) - 1
    slot = i % 2

    def normalise(dst):
        xn_ref[dst] = _norm_rows(x_ref[...], g_ref[...]).astype(BF16)

    @pl.when(jnp.logical_and(i == 0, j == 0))
    def _():
        normalise(0)

    @pl.when(j < last)
    def _():
        step(slot, lambda: None)

    @pl.when(j == last)
    def _():
        step(slot, lambda: normalise(1 - slot))


def _next_tile_rows(ni, nj):
    return lambda i, j: (jnp.minimum(i + jnp.where(j == nj - 1, 1, 0), ni - 1), 0)


def _mm_kernel(*refs, norm, pipelined, res, ple, out_scale):
    it = iter(refs)
    x_ref = next(it)
    g_ref = next(it) if norm else None
    w_ref = next(it)
    res_ref = next(it) if res else None
    p_ref = next(it) if ple else None
    wp_ref = next(it) if ple else None
    o_ref = next(it)
    xn_ref = next(it) if norm else None

    def step(x):
        acc = jnp.dot(x, w_ref[...], preferred_element_type=F32)
        if ple:
            proj = jnp.dot(p_ref[...].astype(BF16), wp_ref[...], preferred_element_type=F32)
            acc = _sigmoid(acc) * proj
        if res:
            acc = acc + res_ref[...]
        if out_scale is not None:
            acc = acc * out_scale
        o_ref[...] = acc.astype(o_ref.dtype)

    if pipelined:
        def pipelined_step(slot, after):
            step(xn_ref[slot])
            after()

        _normed_rows_pipeline(x_ref, g_ref, xn_ref, pipelined_step)
    elif norm:
        @pl.when(pl.program_id(1) == 0)
        def _():
            xn_ref[...] = _norm_rows(x_ref[...], g_ref[...]).astype(BF16)
        step(xn_ref[...])
    else:
        step(x_ref[...])


def _matmul(x, w, *, gain=None, residual=None, ple=None, out_dtype=F32, out_scale=None, tn=COL_TILE):
    m, k = x.shape
    n = w.shape[1]
    norm = gain is not None
    tm = min(ROW_TILE if norm else 2 * ROW_TILE, m)
    tn = min(tn, n)
    assert m % tm == 0 and n % tn == 0
    pipelined = norm and ple is None and n // tn >= 2
    x_map = _next_tile_rows(m // tm, n // tn) if pipelined else (lambda i, j: (i, 0))
    in_specs = [pl.BlockSpec((tm, k), x_map)]
    args = [x]
    if norm:
        in_specs.append(pl.BlockSpec((1, k), lambda i, j: (0, 0)))
        args.append(gain.reshape(1, k))
    in_specs.append(pl.BlockSpec((k, tn), lambda i, j: (0, j)))
    args.append(w)
    if residual is not None:
        in_specs.append(pl.BlockSpec((tm, tn), lambda i, j: (i, j)))
        args.append(residual)
    if ple is not None:
        p, wp = ple
        kp = p.shape[1]
        in_specs.append(pl.BlockSpec((tm, kp), lambda i, j: (i, 0)))
        in_specs.append(pl.BlockSpec((kp, tn), lambda i, j: (0, j)))
        args += [p, wp]
    return pl.pallas_call(
        functools.partial(_mm_kernel, norm=norm, pipelined=pipelined, res=residual is not None,
                          ple=ple is not None, out_scale=out_scale),
        grid=(m // tm, n // tn),
        in_specs=in_specs,
        out_specs=pl.BlockSpec((tm, tn), lambda i, j: (i, j)),
        out_shape=jax.ShapeDtypeStruct((m, n), out_dtype),
        scratch_shapes=([pltpu.VMEM((2, tm, k), BF16)] if pipelined else
                        [pltpu.VMEM((tm, k), BF16)] if norm else []),
        compiler_params=_params("arbitrary" if pipelined else "parallel", "arbitrary"),
    )(*args)


def _qkv_kernel(x_ref, g_ref, w_ref, q_ref, k_ref, v_ref, xn_ref, *, nq):
    j = pl.program_id(1)

    def step(slot, after):
        def dot():
            return jnp.dot(xn_ref[slot], w_ref[...], preferred_element_type=F32)

        @pl.when(j < nq)
        def _():
            q_ref[...] = (dot() * SCALE).astype(q_ref.dtype)

        @pl.when(jnp.logical_and(j >= nq, j < 2 * nq))
        def _():
            k_ref[...] = dot()

        @pl.when(j >= 2 * nq)
        def _():
            v_ref[...] = dot()
            after()

    _normed_rows_pipeline(x_ref, g_ref, xn_ref, step)


def _qkv_proj(x, w, gain, *, tn=COL_TILE):
    m, k = x.shape
    n = w.shape[1] // 3
    tm = min(ROW_TILE, m)
    nq = n // tn
    assert m % tm == 0 and n % tn == 0

    def ospec(first):
        return pl.BlockSpec((tm, tn), lambda i, j: (i, jnp.clip(j - first, 0, nq - 1)))

    return pl.pallas_call(
        functools.partial(_qkv_kernel, nq=nq),
        grid=(m // tm, 3 * nq),
        in_specs=[pl.BlockSpec((tm, k), _next_tile_rows(m // tm, 3 * nq)),
                  pl.BlockSpec((1, k), lambda i, j: (0, 0)),
                  pl.BlockSpec((k, tn), lambda i, j: (0, j))],
        out_specs=[ospec(0), ospec(nq), ospec(2 * nq)],
        out_shape=[jax.ShapeDtypeStruct((m, n), BF16), jax.ShapeDtypeStruct((m, n), F32),
                   jax.ShapeDtypeStruct((m, n), F32)],
        scratch_shapes=[pltpu.VMEM((2, tm, k), BF16)],
        compiler_params=_params("arbitrary", "arbitrary"),
    )(x, gain.reshape(1, k), w)


def _kvq_kernel(x_ref, gk_ref, gq_ref, w_ref, kv_ref, qb_ref, xn_ref, *, nkv):
    i = pl.program_id(0)
    j = pl.program_id(1)
    last = pl.num_programs(1) - 1
    slot = i % 2

    def normalise(dst):
        for r0 in range(0, x_ref.shape[0], LANES):
            rows = slice(r0, min(r0 + LANES, x_ref.shape[0]))
            x = x_ref[rows, :]
            r = x * lax.rsqrt(jnp.mean(x * x, axis=-1, keepdims=True) + EPS)
            xn_ref[0, dst, rows, :] = (r * gk_ref[...]).astype(BF16)
            xn_ref[1, dst, rows, :] = (r * gq_ref[...]).astype(BF16)

    @pl.when(jnp.logical_and(i == 0, j == 0))
    def _():
        normalise(0)

    @pl.when(j < nkv)
    def _():
        kv_ref[...] = jnp.dot(xn_ref[0, slot], w_ref[...], preferred_element_type=F32)

    @pl.when(jnp.logical_and(j >= nkv, j < last))
    def _():
        qb_ref[...] = (jnp.dot(xn_ref[1, slot], w_ref[...], preferred_element_type=F32) * SCALE).astype(qb_ref.dtype)

    @pl.when(j == last)
    def _():
        acc = jnp.dot(xn_ref[1, slot], w_ref[...], preferred_element_type=F32)
        normalise(1 - slot)
        qb_ref[...] = (acc * SCALE).astype(qb_ref.dtype)


def _kvq_proj(x, w, gain_kv, gain_q, n_kv, *, tn=COL_TILE):
    m, k = x.shape
    n_q = w.shape[1] - n_kv
    tm = min(ROW_TILE, m)
    assert m % tm == 0 and n_kv % tn == 0 and n_q % tn == 0
    nkv, nqb = n_kv // tn, n_q // tn
    steps = nkv + nqb
    gspec = pl.BlockSpec((1, k), lambda i, j: (0, 0))
    return pl.pallas_call(
        functools.partial(_kvq_kernel, nkv=nkv),
        grid=(m // tm, steps),
        in_specs=[pl.BlockSpec((tm, k), _next_tile_rows(m // tm, steps)), gspec, gspec,
                  pl.BlockSpec((k, tn), lambda i, j: (0, j))],
        out_specs=[pl.BlockSpec((tm, tn), lambda i, j: (i, jnp.minimum(j, nkv - 1))),
                   pl.BlockSpec((tm, tn), lambda i, j: (i, jnp.clip(j - nkv, 0, nqb - 1)))],
        out_shape=[jax.ShapeDtypeStruct((m, n_kv), F32), jax.ShapeDtypeStruct((m, n_q), BF16)],
        scratch_shapes=[pltpu.VMEM((2, 2, tm, k), BF16)],
        compiler_params=_params("arbitrary", "arbitrary"),
    )(x, gain_kv.reshape(1, k), gain_q.reshape(1, k), w)


def _cast_kernel(x_ref, o_ref):
    o_ref[...] = x_ref[...].astype(o_ref.dtype)


_CAST_ROWS = 256
_CAST_COLS = 4096


def _to_bf16(x):
    x2 = x.reshape(-1, x.shape[-1])
    rows, cols = x2.shape
    tr = min(_CAST_ROWS, rows)
    tc = min(_CAST_COLS, cols)
    spec = pl.BlockSpec((tr, tc), lambda i, j: (i, j))
    out = pl.pallas_call(
        _cast_kernel,
        grid=(rows // tr, cols // tc),
        in_specs=[spec],
        out_specs=spec,
        out_shape=jax.ShapeDtypeStruct((rows, cols), BF16),
        compiler_params=_params("parallel", "parallel"),
    )(x2)
    return out.reshape(x.shape)


def _stack_cast_kernel(u_ref, v_ref, o_ref):
    c = pl.program_id(1)

    @pl.when(c == 0)
    def _():
        o_ref[...] = u_ref[...].astype(o_ref.dtype)

    @pl.when(c == 1)
    def _():
        o_ref[...] = v_ref[...].astype(o_ref.dtype)


def _peer_tables_bf16(u, v):
    nl, ne, d = u.shape
    te = _CAST_ROWS
    nblk = ne // te
    uspec = pl.BlockSpec((None, te, d), lambda l, c, e: (l, jnp.where(c == 0, e, nblk - 1), 0))
    vspec = pl.BlockSpec((None, te, d), lambda l, c, e: (l, jnp.where(c == 1, e, 0), 0))
    return pl.pallas_call(
        _stack_cast_kernel,
        grid=(nl, 2, nblk),
        in_specs=[uspec, vspec],
        out_specs=pl.BlockSpec((None, None, te, d), lambda l, c, e: (l, c, e, 0)),
        out_shape=jax.ShapeDtypeStruct((nl, 2, ne, d), BF16),
        compiler_params=_params("parallel", "arbitrary", "arbitrary"),
    )(u, v)


def _rmsnorm_kernel(x_ref, g_ref, o_ref):
    o_ref[...] = _norm_rows(x_ref[...], g_ref[...])


def _rmsnorm(x, gain):
    m, k = x.shape
    tm = min(ROW_TILE, m)
    return pl.pallas_call(
        _rmsnorm_kernel,
        grid=(m // tm,),
        in_specs=[pl.BlockSpec((tm, k), lambda i: (i, 0)), pl.BlockSpec((1, k), lambda i: (0, 0))],
        out_specs=pl.BlockSpec((tm, k), lambda i: (i, 0)),
        out_shape=jax.ShapeDtypeStruct((m, k), F32),
        compiler_params=_params("parallel"),
    )(x, gain.reshape(1, k))


def _strict_lower_ones(n):
    r = lax.broadcasted_iota(jnp.int32, (2 * n, n), 0) % n
    c = lax.broadcasted_iota(jnp.int32, (2 * n, n), 1)
    return jnp.where(r > c, 1.0, 0.0).astype(BF16)


_SB_DEAD_BELOW = -104.0


def _sb_log_terms(q, kb, mask):
    z = lax.dot_general(q, kb, _NT, preferred_element_type=F32)
    nz = -z
    sp = jnp.log(1.0 + jnp.exp(-jnp.maximum(z, nz)))
    log_beta = jnp.minimum(z, 0.0) - sp
    log_keep = jnp.minimum(nz, 0.0) - sp
    if mask is not None:
        log_keep = jnp.where(mask, log_keep, 0.0)
    return log_beta, log_keep


def _sb_block(q, kb, vb, c, acc, tri2, mask):
    log_beta, log_keep = _sb_log_terms(q, kb, mask)
    hi = log_keep.astype(BF16)
    lo = (log_keep - hi.astype(F32)).astype(BF16)
    after = jnp.dot(jnp.concatenate([hi, lo], axis=1), tri2, preferred_element_type=F32)
    w = jnp.exp(log_beta + after + c)
    if mask is not None:
        w = jnp.where(mask, w, 0.0)
    acc = acc + jnp.dot(w.astype(BF16), vb, preferred_element_type=F32)
    c = c + jnp.sum(log_keep, axis=1, keepdims=True)
    return c, acc


_SB_QUERY_GROUP = 16


def _sb_prompt_kernel(q_ref, k_ref, v_ref, o_ref, kb_ref, vb_ref, *, blk, group):
    seq = q_ref.shape[0]
    kb_ref[pl.ds(0, blk), :] = jnp.zeros((blk, HEAD_DIM), BF16)
    vb_ref[pl.ds(0, blk), :] = jnp.zeros((blk, HEAD_DIM), BF16)
    kb_ref[pl.ds(blk, seq), :] = k_ref[...].astype(BF16)
    vb_ref[pl.ds(blk, seq), :] = v_ref[...].astype(BF16)
    tri = _strict_lower_ones(blk)
    row = lax.broadcasted_iota(jnp.int32, (blk, blk), 0)
    col = lax.broadcasted_iota(jnp.int32, (blk, blk), 1)
    diag_mask = col < row

    def kv_block(j):
        r0 = pl.multiple_of((j + 1) * blk, blk)
        return kb_ref[pl.ds(r0, blk), :], vb_ref[pl.ds(r0, blk), :]

    def older_blocks(i, q, c, acc):
        def k_cond(carry):
            j, cmax, _, _ = carry
            return jnp.logical_and(j >= 0, cmax > _SB_DEAD_BELOW)

        def k_body(carry):
            j, _, c, acc = carry
            c, acc = _sb_block(q, *kv_block(j), c, acc, tri, None)
            return j - 1, jnp.max(c), c, acc

        return lax.while_loop(k_cond, k_body, (i - 2, jnp.max(c), c, acc))[3]

    def group_body(p, _):
        state = []
        for i in [group * p + u for u in range(group)]:
            q = q_ref[pl.ds(pl.multiple_of(i * blk, blk), blk), :]
            c = jnp.zeros((blk, 1), F32)
            acc = jnp.zeros((blk, HEAD_DIM), F32)
            c, acc = _sb_block(q, *kv_block(i), c, acc, tri, diag_mask)
            c, acc = _sb_block(q, *kv_block(i - 1), c, acc, tri, None)
            state.append((i, q, c, acc))
        for i, q, c, acc in state:
            acc = older_blocks(i, q, c, acc)
            o_ref[pl.ds(pl.multiple_of(i * blk, blk), blk), :] = acc.astype(o_ref.dtype)
        return 0

    lax.fori_loop(0, seq // (group * blk), group_body, 0)


def _sb_prompt(q, k, v, *, batch, seq, blk=SB_BLOCK):
    t, hd = q.shape
    heads = hd // HEAD_DIM
    blk = min(blk, seq)
    group = min(_SB_QUERY_GROUP, seq // blk)
    assert seq % (group * blk) == 0 and t == batch * seq
    spec = pl.BlockSpec((seq, HEAD_DIM), lambda b, h: (b, h))
    return pl.pallas_call(
        functools.partial(_sb_prompt_kernel, blk=blk, group=group),
        grid=(batch, heads),
        in_specs=[spec, spec, spec],
        out_specs=spec,
        out_shape=jax.ShapeDtypeStruct((t, hd), BF16),
        scratch_shapes=[pltpu.VMEM((seq + blk, HEAD_DIM), BF16), pltpu.VMEM((seq + blk, HEAD_DIM), BF16)],
        compiler_params=_params("parallel", "parallel"),
    )(q, k, v)


_SB_HEAD_GROUP = 32


def _sb_probe_kernel(q_ref, kn_ref, kc_ref, alive_ref, *, blk):
    tq = q_ref.shape[0]
    row = lax.broadcasted_iota(jnp.int32, (tq, blk), 0)
    col = lax.broadcasted_iota(jnp.int32, (tq, blk), 1)
    kb = pltpu.einshape("thd->htd", kc_ref[...]).astype(BF16)
    cmax = jnp.full((tq, 1), NEG_INF, F32)
    for g in range(_SB_HEAD_GROUP):
        lanes = slice(g * HEAD_DIM, (g + 1) * HEAD_DIM)
        q = q_ref[:, lanes]
        c = jnp.sum(_sb_log_terms(q, kn_ref[:, lanes], col < row)[1], axis=1, keepdims=True)
        c = c + jnp.sum(_sb_log_terms(q, kb[g], None)[1], axis=1, keepdims=True)
        cmax = jnp.maximum(cmax, c)
    flag = jnp.where(jnp.max(cmax) > _SB_DEAD_BELOW - 1.0, 1, 0)
    alive_ref[...] = jnp.full(alive_ref.shape, flag, jnp.int32)


def _sb_sample_kernel(alive_ref, q_ref, kn_ref, vn_ref, kc_ref, vc_ref, o_ref, c_ref, acc_ref, *, blk):
    del alive_ref
    n = pl.program_id(2)
    tq = q_ref.shape[0]
    tri = _strict_lower_ones(blk)

    @pl.when(n == 0)
    def _():
        row = lax.broadcasted_iota(jnp.int32, (tq, blk), 0)
        col = lax.broadcasted_iota(jnp.int32, (tq, blk), 1)
        for g in range(_SB_HEAD_GROUP):
            lanes = slice(g * HEAD_DIM, (g + 1) * HEAD_DIM)
            c, acc = _sb_block(q_ref[:, lanes], kn_ref[:, lanes], vn_ref[:, lanes],
                               jnp.zeros((tq, 1), F32), jnp.zeros((tq, HEAD_DIM), F32), tri, col < row)
            c_ref[g] = c
            acc_ref[g] = acc

    @pl.when(jnp.max(c_ref[...]) > _SB_DEAD_BELOW)
    def _():
        kb = pltpu.einshape("thd->htd", kc_ref[...]).astype(BF16)
        vb = pltpu.einshape("thd->htd", vc_ref[...]).astype(BF16)
        for g in range(_SB_HEAD_GROUP):
            lanes = slice(g * HEAD_DIM, (g + 1) * HEAD_DIM)
            c, acc = _sb_block(q_ref[:, lanes], kb[g], vb[g], c_ref[g], acc_ref[g], tri, None)
            c_ref[g] = c
            acc_ref[g] = acc

    @pl.when(n == pl.num_programs(2) - 1)
    def _():
        for g in range(_SB_HEAD_GROUP):
            o_ref[:, g * HEAD_DIM:(g + 1) * HEAD_DIM] = acc_ref[g].astype(o_ref.dtype)


def _sb_sample(q, k_new, v_new, cache_k, cache_v, *, batch, tq, blk=SB_BLOCK):
    t, hd = q.shape
    heads = hd // HEAD_DIM
    past = cache_k.shape[1]
    nblk = past // blk
    assert past % blk == 0 and tq <= blk
    gw = _SB_HEAD_GROUP * HEAD_DIM

    def pad_block(a):
        a = a.reshape(batch, tq, hd)
        a = jnp.pad(a, ((0, 0), (0, blk - tq), (0, 0)))
        return a.reshape(batch * blk, hd).astype(BF16)

    assert heads == _SB_HEAD_GROUP
    kn = pad_block(k_new)
    vn = pad_block(v_new)
    newest = (None, blk, _SB_HEAD_GROUP, HEAD_DIM)
    alive = pl.pallas_call(
        functools.partial(_sb_probe_kernel, blk=blk),
        grid=(batch,),
        in_specs=[pl.BlockSpec((tq, gw), lambda b: (b, 0)),
                  pl.BlockSpec((blk, gw), lambda b: (b, 0)),
                  pl.BlockSpec(newest, lambda b: (b, nblk - 1, 0, 0))],
        out_specs=pl.BlockSpec((None, 8, LANES), lambda b: (b, 0, 0)),
        out_shape=jax.ShapeDtypeStruct((batch, 8, LANES), jnp.int32),
        compiler_params=_params("parallel"),
    )(q, kn, cache_k)[:, 0, 0]

    def cache_block(b, h, n, alive_ref):
        return (b, nblk - 1 - jnp.where(alive_ref[b] > 0, n, 0), h, 0)

    qspec = pl.BlockSpec((tq, gw), lambda b, h, n, alive_ref: (b, h))
    nspec = pl.BlockSpec((blk, gw), lambda b, h, n, alive_ref: (b, h))
    cspec = pl.BlockSpec(newest, cache_block)
    return pl.pallas_call(
        functools.partial(_sb_sample_kernel, blk=blk),
        grid_spec=pltpu.PrefetchScalarGridSpec(
            num_scalar_prefetch=1,
            grid=(batch, heads // _SB_HEAD_GROUP, nblk),
            in_specs=[qspec, nspec, nspec, cspec, cspec],
            out_specs=qspec,
            scratch_shapes=[pltpu.VMEM((_SB_HEAD_GROUP, tq, 1), F32),
                            pltpu.VMEM((_SB_HEAD_GROUP, tq, HEAD_DIM), F32)]),
        out_shape=jax.ShapeDtypeStruct((t, hd), BF16),
        compiler_params=_params("parallel", "parallel", "arbitrary"),
    )(alive, q, kn, vn, cache_k, cache_v)


_BAND_UNROLL = 16


def _band_kernel(q_ref, k_ref, v_ref, bias_ref, o_ref, *, cq, win, nq, chunked, lo_static):
    bias = bias_ref[...].reshape(G_B * cq, win)
    col = lax.broadcasted_iota(jnp.int32, (G_B * cq, win), 1)

    def chunk(c, _):
        r0 = pl.multiple_of(c * cq, cq)
        if chunked:
            lo = jnp.maximum(CHUNK, (N_PREV_CHUNKS + 1 - c) * CHUNK)
        else:
            lo = lo_static
        kb = k_ref[pl.ds(r0, win), :]
        vb = v_ref[pl.ds(r0, win), :]
        qb = q_ref[pl.ds(r0, cq), :]
        q4 = jnp.concatenate([qb[:, g * HEAD_DIM:(g + 1) * HEAD_DIM] for g in range(G_B)], axis=0)
        s = lax.dot_general(q4, kb, _NT, preferred_element_type=F32) + bias
        s = jnp.where(col >= lo, s, NEG_INF)
        m = jnp.max(s, axis=1, keepdims=True)
        e = jnp.exp(s - m)
        o = jnp.dot(e.astype(BF16), vb, preferred_element_type=F32) * (1.0 / jnp.sum(e, axis=1, keepdims=True))
        for g in range(G_B):
            o_ref[pl.ds(r0, cq), g * HEAD_DIM:(g + 1) * HEAD_DIM] = o[g * cq:(g + 1) * cq].astype(o_ref.dtype)
        return 0

    lax.fori_loop(0, nq, chunk, 0, unroll=_BAND_UNROLL if nq % _BAND_UNROLL == 0 else 1)


def _band_bias(rel_table, cq, win):
    lg = win + cq - 1
    idx = np.clip(win - 1 - np.arange(lg), -REL_CLIP, REL_CLIP) + REL_CLIP
    g = rel_table[idx].astype(F32).T
    rows = jnp.tile(g, (1, cq + 1))[:, :cq * (lg + 1)].reshape(-1, cq, lg + 1)
    return rows[:, ::-1, :win]


def _band_attention(q, kpad, vpad, bias, *, batch, nq, cq, win, chunked, lo_static):
    t, hd = q.shape
    rows = kpad.shape[1]
    qw = G_B * HEAD_DIM
    qspec = pl.BlockSpec((nq * cq, qw), lambda b, h: (b, h))
    kspec = pl.BlockSpec((None, rows, HEAD_DIM), lambda b, h: (b, 0, h))
    bspec = pl.BlockSpec((G_B, cq, win), lambda b, h: (h, 0, 0))
    return pl.pallas_call(
        functools.partial(_band_kernel, cq=cq, win=win, nq=nq, chunked=chunked, lo_static=lo_static),
        grid=(batch, N_KV_B),
        in_specs=[qspec, kspec, kspec, bspec],
        out_specs=qspec,
        out_shape=jax.ShapeDtypeStruct((t, hd), BF16),
        compiler_params=_params("parallel", "parallel"),
    )(q, kpad, vpad, bias)


def _split3_dot_nt(a, b):
    a_hi = a.astype(BF16)
    a_lo = (a - a_hi.astype(F32)).astype(BF16)
    b_hi = b.astype(BF16)
    b_lo = (b - b_hi.astype(F32)).astype(BF16)
    return lax.dot_general(jnp.concatenate([a_hi, a_lo, a_hi], axis=1),
                           jnp.concatenate([b_hi, b_hi, b_lo], axis=1), _NT, preferred_element_type=F32)


def _take_lanes(x, idx):
    return jnp.take_along_axis(x, idx, axis=1, mode="promise_in_bounds")


_ROUTE_ROWS = 64
_ROUTE_UNROLL = 64


def _route_kernel(q_ref, sk_ref, pairs_ref, e_ref, g_ref, s_ref, left_ref, si_ref, cleft_ref, fpos_ref):
    tm = q_ref.shape[0]
    dh = sk_ref.shape[-1]
    nset = 2 * PEER_HEADS
    lane_c = lax.broadcasted_iota(jnp.int32, (_ROUTE_ROWS, PEER_KEYS), 1)

    for n in range(nset):
        sc = _split3_dot_nt(q_ref[:, n * dh:(n + 1) * dh], sk_ref[n // 2, n % 2])
        s_ref[n * tm:(n + 1) * tm, :] = sc
        left_ref[n * tm:(n + 1) * tm, :] = sc
    si_ref[...] = jnp.zeros(si_ref.shape, jnp.int32)

    def body1(i, _):
        def piece(r, _):
            rows = pl.ds(pl.multiple_of(r * _ROUTE_ROWS, _ROUTE_ROWS), _ROUTE_ROWS)
            left = left_ref[rows, :]
            pos = jnp.argmax(left, axis=1, keepdims=True).astype(jnp.int32)
            left_ref[rows, :] = jnp.where(lane_c == pos, NEG_INF, left)
            si_ref[rows, :] = jnp.where(lane_c == i, pos, si_ref[rows, :])
            return 0

        return lax.fori_loop(0, nset * tm // _ROUTE_ROWS, piece, 0, unroll=_ROUTE_UNROLL)

    lax.fori_loop(0, PEER_TOPK, body1, 0)
    si = si_ref[...]
    sv = _take_lanes(s_ref[...], si)
    si = si.astype(F32)

    pair_a = jnp.broadcast_to(pairs_ref[0:1, :], (tm, PEER_KEYS))
    pair_b = jnp.broadcast_to(pairs_ref[1:2, :], (tm, PEER_KEYS))
    pair_ok = jnp.broadcast_to(pairs_ref[2:3, :], (tm, PEER_KEYS)) > 0
    cand, cidx = [], []
    for h in range(PEER_HEADS):
        r0 = slice((2 * h) * tm, (2 * h + 1) * tm)
        r1 = slice((2 * h + 1) * tm, (2 * h + 2) * tm)
        cand.append(jnp.where(pair_ok, _take_lanes(sv[r0], pair_a) + _take_lanes(sv[r1], pair_b), NEG_INF))
        cidx.append(_take_lanes(si[r0], pair_a) * float(PEER_KEYS) + _take_lanes(si[r1], pair_b))
    cand = jnp.concatenate(cand, axis=0)
    cidx = jnp.concatenate(cidx, axis=0)

    lane2 = lax.broadcasted_iota(jnp.int32, cand.shape, 1)
    head_lane0 = (lax.broadcasted_iota(jnp.int32, cand.shape, 0) // tm) * PEER_TOPK
    cleft_ref[...] = cand
    fpos_ref[...] = jnp.zeros(fpos_ref.shape, jnp.int32)

    def body2(i, _):
        def piece(r, _):
            row0 = pl.multiple_of(r * _ROUTE_ROWS, _ROUTE_ROWS)
            rows = pl.ds(row0, _ROUTE_ROWS)
            left = cleft_ref[rows, :]
            pos = jnp.argmax(left, axis=1, keepdims=True).astype(jnp.int32)
            cleft_ref[rows, :] = jnp.where(lane_c == pos, NEG_INF, left)
            out_lane = (row0 // tm) * PEER_TOPK + i
            fpos_ref[rows, :] = jnp.where(lane_c == out_lane, pos, fpos_ref[rows, :])
            return 0

        return lax.fori_loop(0, PEER_HEADS * tm // _ROUTE_ROWS, piece, 0, unroll=_ROUTE_UNROLL)

    lax.fori_loop(0, PEER_TOPK, body2, 0)
    fpos = fpos_ref[...]
    fv = _take_lanes(cand, fpos)
    fe = _take_lanes(cidx, fpos)

    grp = lane2 // PEER_TOPK == head_lane0 // PEER_TOPK
    fe = jnp.where(grp, fe, 0.0)
    mx = jnp.max(jnp.where(grp, fv, NEG_INF), axis=1, keepdims=True)
    ex = jnp.where(grp, jnp.exp(fv - mx), 0.0)
    gate = ex / jnp.sum(ex, axis=1, keepdims=True)
    e_out = fe[0:tm]
    g_out = gate[0:tm]
    for h in range(1, PEER_HEADS):
        e_out = e_out + fe[h * tm:(h + 1) * tm]
        g_out = g_out + gate[h * tm:(h + 1) * tm]
    e_ref[...] = e_out.astype(jnp.int32)
    g_ref[...] = g_out


def _peer_route(qp, sub_keys):
    m, kq = qp.shape
    tm = min(ROUTE_ROW_TILE, m)
    assert m % tm == 0 and tm % _ROUTE_ROWS == 0
    ne = PEER_HEADS * PEER_TOPK
    ospec = pl.BlockSpec((tm, ne), lambda i: (i, 0))
    pairs = [(a, b) for a in range(PEER_TOPK) for b in range(PEER_TOPK) if (a + 1) * (b + 1) <= PEER_TOPK]
    pad = [0] * (PEER_KEYS - len(pairs))
    pair_rows = jnp.array([[a for a, _ in pairs] + pad, [b for _, b in pairs] + pad,
                           [1] * len(pairs) + pad] + [[0] * PEER_KEYS] * 5, jnp.int32)
    return pl.pallas_call(
        _route_kernel,
        grid=(m // tm,),
        in_specs=[pl.BlockSpec((tm, kq), lambda i: (i, 0)),
                  pl.BlockSpec(sub_keys.shape, lambda i: (0, 0, 0, 0)),
                  pl.BlockSpec(pair_rows.shape, lambda i: (0, 0))],
        out_specs=[ospec, ospec],
        out_shape=[jax.ShapeDtypeStruct((m, ne), jnp.int32), jax.ShapeDtypeStruct((m, ne), F32)],
        scratch_shapes=[pltpu.VMEM((2 * PEER_HEADS * tm, PEER_KEYS), F32),
                        pltpu.VMEM((2 * PEER_HEADS * tm, PEER_KEYS), F32),
                        pltpu.VMEM((2 * PEER_HEADS * tm, PEER_KEYS), jnp.int32),
                        pltpu.VMEM((PEER_HEADS * tm, PEER_KEYS), F32),
                        pltpu.VMEM((PEER_HEADS * tm, PEER_KEYS), jnp.int32)],
        compiler_params=_params("parallel"),
    )(qp, sub_keys, pair_rows)


def _gelu_tanh(x):
    return 0.5 * x * (1.0 + jnp.tanh(0.7978845608028654 * (x + 0.044715 * (x * x * x))))


_TOKEN_GROUP = 16


def _expert_kernel(h_ref, gain_ref, e_ref, g_ref, tab_ref, next_gain_ref, o_ref, on_ref,
                   hi_ref, lo_ref, act_ref, d_ref, w3_ref, *, nj):
    j = pl.program_id(1)
    xn_ref = on_ref
    tm = h_ref.shape[0]
    te = tab_ref.shape[0]
    nb = te // LANES

    @pl.when(j == 0)
    def _():
        xn_ref[...] = _norm_rows(h_ref[...], gain_ref[...]).astype(BF16)
        o_ref[...] = h_ref[...]
        e = e_ref[...]
        hi_ref[...] = e // LANES
        lo_ref[...] = e % LANES
        act_ref[...] = jnp.zeros(act_ref.shape, F32)

    def score_slab(slot):
        d_ref[slot] = lax.dot_general(xn_ref[...], tab_ref[...], _NT, preferred_element_type=F32)

    def pick_from_slab(slot, slab):
        hi = hi_ref[...]
        lo = lo_ref[...]
        act = act_ref[...]
        for s in range(nb):
            picked = _take_lanes(d_ref[slot, :, s * LANES:(s + 1) * LANES], lo)
            act = jnp.where(hi == slab * nb + s, picked, act)
        act_ref[...] = act

    @pl.when(j == 0)
    def _():
        score_slab(0)

    @pl.when(jnp.logical_and(j >= 1, j < nj))
    def _():
        pick_from_slab((j - 1) % 2, j - 1)
        score_slab(j % 2)

    @pl.when(j == nj)
    def _():
        pick_from_slab((nj - 1) % 2, nj - 1)
        act_ref[...] = g_ref[...] * _gelu_tanh(act_ref[...])
        sub = lax.broadcasted_iota(jnp.int32, (LANES, LANES), 0)

        def group(gi, _):
            t0 = pl.multiple_of(gi * _TOKEN_GROUP, _TOKEN_GROUP)
            grids = []
            for u in range(_TOKEN_GROUP):
                hi_row = hi_ref[pl.ds(t0 + u, 1), :]
                lo_row = lo_ref[pl.ds(t0 + u, 1), :]
                w_row = act_ref[pl.ds(t0 + u, 1), :]
                a_t = jnp.where(hi_row == sub, w_row, 0.0).astype(BF16)
                b_t = jnp.where(lo_row == sub, 1.0, 0.0).astype(BF16)
                grids.append(lax.dot_general(a_t, b_t, _NT, preferred_element_type=F32))
            w3_ref[:, pl.ds(t0, _TOKEN_GROUP), :] = pltpu.einshape(
                "uik->iuk", jnp.stack(grids, axis=0)).astype(BF16)
            return 0

        lax.fori_loop(0, tm // _TOKEN_GROUP, group, 0, unroll=4)

    @pl.when(j >= nj)
    def _():
        jj = j - nj
        w = jnp.concatenate([w3_ref[jj * nb + s] for s in range(nb)], axis=1)
        d = o_ref.shape[1]
        for c0 in range(0, d, _EXPERT_OUT_COLS):
            cols = slice(c0, c0 + _EXPERT_OUT_COLS)
            o_ref[:, cols] += jnp.dot(w, tab_ref[:, cols], preferred_element_type=F32)

    @pl.when(j == 2 * nj - 1)
    def _():
        for r0 in range(0, tm, LANES):
            rows = slice(r0, min(r0 + LANES, tm))
            on_ref[rows, :] = _norm_rows(o_ref[rows, :], next_gain_ref[...]).astype(BF16)


_EXPERT_OUT_COLS = 1024


def _peer_experts(h, gain, eidx, gate, uv_tab, layer, next_gain, *, te=EXPERT_SLAB):
    m, d = h.shape
    ne = uv_tab.shape[2]
    nsel = eidx.shape[1]
    tm = min(ROW_TILE, m)
    nj = ne // te
    assert m % tm == 0 and ne % te == 0 and tm % _TOKEN_GROUP == 0
    once = pl.Buffered(1)
    sel_spec = pl.BlockSpec((tm, nsel), lambda i, j: (i, 0), pipeline_mode=once)
    return pl.pallas_call(
        functools.partial(_expert_kernel, nj=nj),
        grid=(m // tm, 2 * nj),
        in_specs=[pl.BlockSpec((tm, d), lambda i, j: (i, 0)),
                  pl.BlockSpec((1, d), lambda i, j: (0, 0), pipeline_mode=once),
                  sel_spec, sel_spec,
                  pl.BlockSpec((None, None, te, d), lambda i, j: (layer, j // nj, j % nj, 0)),
                  pl.BlockSpec((1, d), lambda i, j: (0, 0), pipeline_mode=once)],
        out_specs=[pl.BlockSpec((tm, d), lambda i, j: (i, 0), pipeline_mode=once),
                   pl.BlockSpec((tm, d), lambda i, j: (i, 0), pipeline_mode=once)],
        out_shape=[jax.ShapeDtypeStruct((m, d), F32), jax.ShapeDtypeStruct((m, d), BF16)],
        scratch_shapes=[pltpu.VMEM((tm, nsel), jnp.int32),
                        pltpu.VMEM((tm, nsel), jnp.int32),
                        pltpu.VMEM((tm, nsel), F32),
                        pltpu.VMEM((2, tm, te), F32),
                        pltpu.VMEM((ne // LANES, tm, LANES), BF16)],
        compiler_params=_params("parallel", "arbitrary"),
    )(h, gain.reshape(1, d), eidx, gate, uv_tab, next_gain.reshape(1, d))


def _peer_ffn(h, i, prm):
    qp = _matmul(h, prm["w_q_peer"][i], gain=prm["g_ffn"][i])
    eidx, gate = _peer_route(qp, prm["peer_sub_keys"][i])
    return _peer_experts(h, prm["g_ffn"][i], eidx, gate, prm["peer_uv"], i, prm["g_ple"][i])


def _ple(h, hn, p, i, prm):
    return _matmul(hn, prm["w_ple_gate"][i], residual=h, ple=(p, prm["w_ple_proj"][i]), tn=COL_TILE // 2)


def _trunk(x, pe, past, prm):
    b, t, d = x.shape
    m = b * t
    h = x.reshape(m, d)
    pe = pe.reshape(pe.shape[0], m, pe.shape[-1])

    q, k, v = _qkv_proj(h, prm["w_qkv_a"], prm["g_mix"][0])
    if past is None:
        o = _sb_prompt(q, k, v, batch=b, seq=t)
    else:
        o = _sb_sample(q, k, v, past[0][0], past[1][0], batch=b, tq=t)
    h = _matmul(o, prm["w_o_a"], residual=h)
    h, hn = _peer_ffn(h, 0, prm)
    h = _ple(h, hn, pe[0], 0, prm)

    nkv = N_KV_B * HEAD_DIM
    kv, qb = _kvq_proj(h, prm["w_kvq_b"], prm["g_kv"], prm["g_mix"][1], 2 * nkv)
    kb_new = kv[:, :nkv].reshape(b, t, nkv)
    vb_new = kv[:, nkv:].reshape(b, t, nkv)

    win = (N_PREV_CHUNKS + 2) * CHUNK
    if past is None:
        front = (N_PREV_CHUNKS + 1) * CHUNK
        kpad = jnp.pad(kb_new.astype(BF16), ((0, 0), (front, 0), (0, 0)))
        vpad = jnp.pad(vb_new.astype(BF16), ((0, 0), (front, 0), (0, 0)))
        bias = _band_bias(prm["rel_bias_b"][0], CHUNK, win)
        ob = _band_attention(qb, kpad, vpad, bias, batch=b, nq=t // CHUNK, cq=CHUNK, win=win,
                             chunked=True, lo_static=0)
    else:
        ck = past[2].reshape(b, -1, nkv)
        cv = past[3].reshape(b, -1, nkv)
        front = win - ck.shape[1] - t
        kpad = jnp.pad(jnp.concatenate([ck, kb_new], axis=1).astype(BF16), ((0, 0), (front, 0), (0, 0)))
        vpad = jnp.pad(jnp.concatenate([cv, vb_new], axis=1).astype(BF16), ((0, 0), (front, 0), (0, 0)))
        bias = _band_bias(prm["rel_bias_b"][0], t, win)
        ob = _band_attention(qb, kpad, vpad, bias, batch=b, nq=1, cq=t, win=win,
                             chunked=False, lo_static=front)
    h = _matmul(ob, prm["w_o_b"], residual=h)
    h, hn = _peer_ffn(h, 1, prm)
    h = _ple(h, hn, pe[1], 1, prm)

    y = _rmsnorm(h, prm["g_final"]).reshape(b, t, d)
    heads = k.shape[1] // HEAD_DIM
    a_k = k.reshape(1, b, t, heads, HEAD_DIM)
    a_v = v.reshape(1, b, t, heads, HEAD_DIM)
    if past is None:
        keep = min(N_PREV_CHUNKS * CHUNK, t)
        b_k, b_v = kb_new[:, t - keep:], vb_new[:, t - keep:]
    else:
        b_k, b_v = kb_new, vb_new
    b_k = b_k.reshape(b, -1, N_KV_B, HEAD_DIM)
    b_v = b_v.reshape(b, -1, N_KV_B, HEAD_DIM)
    return y, a_k, a_v, b_k, b_v


def kernel(x_prompt, x_sample, cache_a_k, cache_a_v, cache_b_k, cache_b_v, p_prompt, p_sample, g_mix, w_qkv_a, w_o_a, g_kv, w_kv_b, w_q_b, rel_bias_b, w_o_b, g_ffn, w_q_peer, peer_sub_keys, peer_u, peer_v, g_ple, w_ple_gate, w_ple_proj, g_final):
    prm = dict(
        g_mix=g_mix, g_kv=g_kv, g_ffn=g_ffn, g_ple=g_ple, g_final=g_final,
        w_qkv_a=_to_bf16(w_qkv_a)[0],
        w_o_a=_to_bf16(w_o_a)[0],
        w_kvq_b=jnp.concatenate([_to_bf16(w_kv_b), _to_bf16(w_q_b)[0]], axis=1),
        w_o_b=_to_bf16(w_o_b)[0],
        rel_bias_b=rel_bias_b,
        w_q_peer=_to_bf16(w_q_peer),
        peer_sub_keys=peer_sub_keys,
        peer_uv=_peer_tables_bf16(peer_u, peer_v),
        w_ple_gate=_to_bf16(w_ple_gate),
        w_ple_proj=_to_bf16(w_ple_proj),
    )
    y_p, ak_p, av_p, bk_p, bv_p = _trunk(x_prompt, p_prompt, None, prm)
    y_s, ak_s, av_s, bk_s, bv_s = _trunk(x_sample, p_sample,
                                         (cache_a_k, cache_a_v, cache_b_k, cache_b_v), prm)
    return (y_p, y_s, ak_p, av_p, bk_p, bv_p, ak_s, av_s, bk_s, bv_s)
```
